```python
import math
import jax, jax.numpy as jnp
from jax import lax
import numpy as np

D_MODEL = 2048
BATCH = 4
SEQ = 8192
DEPTH = 2

N_MIXERS = 2
N_META = 16
GLA_HEADS = 4
GLA_DK = D_MODEL // 2 // GLA_HEADS
GLA_DV = D_MODEL // GLA_HEADS
GLA_GATE_RANK = 16
GLA_GATE_TAU = 16.0
GLA_CHUNK = 64
GLA_SPLITS = (GLA_HEADS * GLA_DK, 2 * GLA_HEADS * GLA_DK,
              2 * GLA_HEADS * GLA_DK + GLA_HEADS * GLA_DV,
              2 * GLA_HEADS * GLA_DK + 2 * GLA_HEADS * GLA_DV)
GLA_IN_COLS = 2 * GLA_HEADS * GLA_DK + 2 * GLA_HEADS * GLA_DV + GLA_GATE_RANK
CONV_WIDTH = 3
N_EXPERTS = 64
TOP_K = 8
N_GROUPS = 8
TOPK_GROUPS = 4
D_EXPERT = 512
ROUTED_SCALE = 2.5
MOE_BLOCK = 128
LN_EPS = 1e-5
RMS_EPS = 1e-6
DN_ALPHA = (2 * DEPTH) ** 0.25
DN_BETA = (8 * DEPTH) ** -0.25
N_GLA_LAYERS = (DEPTH + 1) // 2
N_CONV_LAYERS = DEPTH // 2

kernel_name = 'hybrid_gla_shortconv_moe_deepnorm'


def layer_norm(x, g, b):
    xf = x.astype(jnp.float32)
    mu = xf.mean(-1, keepdims=True)
    var = jnp.square(xf - mu).mean(-1, keepdims=True)
    y = (xf - mu) * lax.rsqrt(var + LN_EPS) * g.astype(jnp.float32) + b.astype(jnp.float32)
    return y.astype(x.dtype)


def gla_mixer(x, w_in, w_gate_up, b_gate, norm_g, w_out):
    Bsz, L, _ = x.shape
    H, DK, DV, C = GLA_HEADS, GLA_DK, GLA_DV, GLA_CHUNK
    proj = x @ w_in
    q, k, v, r, g_low = jnp.split(proj, GLA_SPLITS, axis=-1)
    log_a = jax.nn.log_sigmoid((g_low @ w_gate_up + b_gate).astype(jnp.float32)) / GLA_GATE_TAU
    q = q * (DK ** -0.5)
    n_pad = (-L) % C
    NC = (L + n_pad) // C

    def to_chunks(t, d):
        t = jnp.pad(t, ((0, 0), (n_pad, 0), (0, 0)))
        return t.reshape(Bsz, NC, C, H, d).transpose(1, 0, 3, 2, 4)

    qc, kc, la = to_chunks(q, DK), to_chunks(k, DK), to_chunks(log_a, DK)
    vc = to_chunks(v, DV)
    causal = jnp.tril(jnp.ones((C, C), dtype=bool))[:, :, None]

    def step(S, inp):
        qi, ki, vi, lai = inp
        b = jnp.cumsum(lai, axis=2)
        diff = b[:, :, :, None, :] - b[:, :, None, :, :]
        decay = jnp.exp(jnp.where(causal, diff, -jnp.inf))
        att = jnp.einsum('bhid,bhjd,bhijd->bhij', qi, ki, decay)
        o = att @ vi + (qi * jnp.exp(b)) @ S
        b_last = b[:, :, -1:, :]
        S_new = jnp.exp(b_last[:, :, 0, :])[..., None] * S + jnp.einsum(
            'bhjd,bhje->bhde', ki * jnp.exp(b_last - b), vi)
        return S_new, o

    S0 = jnp.zeros((Bsz, H, DK, DV), jnp.float32)
    _, o = lax.scan(step, S0, (qc, kc, vc, la))
    o = o.transpose(1, 0, 3, 2, 4).reshape(Bsz, NC * C, H, DV)[:, n_pad:]
    of = o.astype(jnp.float32)
    of = of * lax.rsqrt(jnp.mean(of * of, axis=-1, keepdims=True) + RMS_EPS)
    of = of * norm_g.astype(jnp.float32).reshape(H, DV)
    o = of.reshape(Bsz, L, H * DV) * jax.nn.silu(r.astype(jnp.float32))
    return o.astype(x.dtype) @ w_out


def short_conv_mixer(x, w_in, conv_w, w_out):
    bg, cg, h = jnp.split(x @ w_in, 3, axis=-1)
    u = cg * h
    u = lax.conv_general_dilated(
        u, conv_w.astype(u.dtype)[:, None, :], window_strides=(1,),
        padding=((CONV_WIDTH - 1, 0),), dimension_numbers=('NWC', 'WIO', 'NWC'),
        feature_group_count=D_MODEL)
    return (bg * u) @ w_out


def moe(x, router_w, router_bias, w1, w3, w2, ws1, ws3, ws2):
    Bsz, L, D = x.shape
    xt = x.reshape(-1, D)
    N = xt.shape[0]
    E, G = N_EXPERTS, N_GROUPS
    scores = jax.nn.sigmoid((xt @ router_w).astype(jnp.float32))
    choice = scores + router_bias.astype(jnp.float32)
    grp_score = lax.top_k(choice.reshape(N, G, E // G), 2)[0].sum(-1)
    _, grp_idx = lax.top_k(grp_score, TOPK_GROUPS)
    grp_mask = jnp.any(grp_idx[..., None] == jnp.arange(G), axis=-2)
    masked = jnp.where(jnp.repeat(grp_mask, E // G, axis=-1), choice, -jnp.inf)
    _, top_idx = lax.top_k(masked, TOP_K)
    gate = jnp.take_along_axis(scores, top_idx, axis=-1)
    gate = gate / gate.sum(-1, keepdims=True) * ROUTED_SCALE

    A = N * TOP_K
    flat_e = top_idx.reshape(-1).astype(jnp.int32)
    flat_tok = jnp.repeat(jnp.arange(N, dtype=jnp.int32), TOP_K)
    flat_w = gate.reshape(-1)
    order = jnp.argsort(flat_e)
    se, stok, sw = flat_e[order], flat_tok[order], flat_w[order]
    counts = jnp.bincount(flat_e, length=E)
    starts = jnp.cumsum(counts) - counts
    pcounts = (counts + MOE_BLOCK - 1) // MOE_BLOCK * MOE_BLOCK
    pends = jnp.cumsum(pcounts)
    pstarts = pends - pcounts
    dest = pstarts[se] + jnp.arange(A, dtype=jnp.int32) - starts[se]
    n_blocks = -(-A // MOE_BLOCK) + E
    P = n_blocks * MOE_BLOCK
    buf_tok = jnp.full((P,), N, jnp.int32).at[dest].set(stok)
    buf_w = jnp.zeros((P,), jnp.float32).at[dest].set(sw)
    block_e = jnp.minimum(
        jnp.searchsorted(pends, jnp.arange(n_blocks, dtype=jnp.int32) * MOE_BLOCK, side='right'),
        E - 1)
    x_pad = jnp.concatenate([xt, jnp.zeros((1, D), xt.dtype)], axis=0)

    def block_step(acc, blk):
        tok, wgt, e = blk
        xb = x_pad[tok]
        hb = jax.nn.silu(xb @ w1[e]) * (xb @ w3[e])
        yb = (hb @ w2[e]).astype(jnp.float32) * wgt[:, None]
        return acc.at[tok].add(yb), None

    acc0 = jnp.zeros((N + 1, D), jnp.float32)
    acc, _ = lax.scan(block_step, acc0, (buf_tok.reshape(n_blocks, MOE_BLOCK),
                                         buf_w.reshape(n_blocks, MOE_BLOCK), block_e))
    shared = (jax.nn.silu(xt @ ws1) * (xt @ ws3)) @ ws2
    out = acc[:N] + shared.astype(jnp.float32)
    return out.astype(x.dtype).reshape(Bsz, L, D)


def setup_inputs(seed: int = 0) -> dict:
    key = jax.random.key(seed)
    ks = jax.random.split(key, 24)
    f32 = jnp.float32
    D, E, F = D_MODEL, N_EXPERTS, D_EXPERT
    nA, nB = N_GLA_LAYERS, N_CONV_LAYERS

    def nrm(k, shape, scale):
        return jax.random.normal(k, shape, f32) * scale

    return {
        'x': nrm(ks[0], (BATCH, SEQ, D), 1.0),
        'meta_tokens': nrm(ks[1], (N_META, D), 1.0),
        'gla_w_in': nrm(ks[2], (nA, D, GLA_IN_COLS), D ** -0.5),
        'gla_w_gate_up': nrm(ks[3], (nA, GLA_GATE_RANK, GLA_HEADS * GLA_DK), GLA_GATE_RANK ** -0.5),
        'gla_b_gate': nrm(ks[4], (nA, GLA_HEADS * GLA_DK), 0.1),
        'gla_norm_g': 1.0 + nrm(ks[5], (nA, GLA_HEADS * GLA_DV), 0.02),
        'gla_w_out': nrm(ks[6], (nA, GLA_HEADS * GLA_DV, D), (GLA_HEADS * GLA_DV) ** -0.5 * DN_BETA),
        'conv_w_in': nrm(ks[7], (nB, D, 3 * D), D ** -0.5),
        'conv_w': nrm(ks[8], (nB, CONV_WIDTH, D), CONV_WIDTH ** -0.5),
        'conv_w_out': nrm(ks[9], (nB, D, D), D ** -0.5 * DN_BETA),
        'ln1_g': 1.0 + nrm(ks[10], (DEPTH, D), 0.02),
        'ln1_b': nrm(ks[11], (DEPTH, D), 0.02),
        'router_w': nrm(ks[12], (DEPTH, D, E), D ** -0.5),
        'router_bias': nrm(ks[13], (DEPTH, E), 0.01),
        'exp_w1': nrm(ks[14], (DEPTH, E, D, F), D ** -0.5),
        'exp_w3': nrm(ks[15], (DEPTH, E, D, F), D ** -0.5),
        'exp_w2': nrm(ks[16], (DEPTH, E, F, D), F ** -0.5 * DN_BETA),
        'shared_w1': nrm(ks[17], (DEPTH, D, F), D ** -0.5),
        'shared_w3': nrm(ks[18], (DEPTH, D, F), D ** -0.5),
        'shared_w2': nrm(ks[19], (DEPTH, F, D), F ** -0.5 * DN_BETA),
        'ln2_g': 1.0 + nrm(ks[20], (DEPTH, D), 0.02),
        'ln2_b': nrm(ks[21], (DEPTH, D), 0.02),
    }


def reference(x, meta_tokens, gla_w_in, gla_w_gate_up, gla_b_gate, gla_norm_g, gla_w_out,
              conv_w_in, conv_w, conv_w_out, ln1_g, ln1_b, router_w, router_bias,
              exp_w1, exp_w3, exp_w2, shared_w1, shared_w3, shared_w2, ln2_g, ln2_b):
    Bsz = x.shape[0]
    meta = jnp.broadcast_to(meta_tokens.astype(x.dtype)[None], (Bsz, N_META, D_MODEL))
    h = jnp.concatenate([meta, x], axis=1)
    for i in range(DEPTH):
        j = i // N_MIXERS
        if i % N_MIXERS == 0:
            mix = gla_mixer(h, gla_w_in[j], gla_w_gate_up[j], gla_b_gate[j],
                            gla_norm_g[j], gla_w_out[j])
        else:
            mix = short_conv_mixer(h, conv_w_in[j], conv_w[j], conv_w_out[j])
        h = layer_norm(DN_ALPHA * h + mix, ln1_g[i], ln1_b[i])
        ffn = moe(h, router_w[i], router_bias[i], exp_w1[i], exp_w3[i], exp_w2[i],
                  shared_w1[i], shared_w3[i], shared_w2[i])
        h = layer_norm(DN_ALPHA * h + ffn, ln2_g[i], ln2_b[i])
    return h[:, N_META:]
```

```python
import functools

import jax
import jax.numpy as jnp
from jax import lax
from jax.experimental import pallas as pl
from jax.experimental.pallas import tpu as pltpu

N_META = 16
GLA_HEADS = 4
GLA_GATE_TAU = 16.0
CONV_WIDTH = 3
N_EXPERTS = 64
TOP_K = 8
N_GROUPS = 8
TOPK_GROUPS = 4
ROUTED_SCALE = 2.5
LN_EPS = 1e-5
RMS_EPS = 1e-6
DEPTH = 2
DN_ALPHA = (2 * DEPTH) ** 0.25

LANES = 128
SUBLANES = 8
ROW_ALIGN = 128
GLA_CHUNK = 64
GLA_SUB = 16
GATE_PAD = LANES
MOE_BLOCK = 256
VMEM_LIMIT = 56 * 1024 * 1024

F32 = jnp.float32
BF16 = jnp.bfloat16


def _pick_tile(n, target, mult):
    best = None
    for t in range(mult, min(n, target) + 1, mult):
        if n % t == 0:
            best = t
    assert best is not None, (n, target, mult)
    return best


def _cparams(sem):
    return pltpu.CompilerParams(dimension_semantics=sem, vmem_limit_bytes=VMEM_LIMIT)


def _mm_kernel(x_ref, w_ref, o_ref):
    o_ref[...] = jnp.dot(x_ref[...], w_ref[...], preferred_element_type=F32).astype(o_ref.dtype)


def _matmul(x, w, out_dtype, tm_target, tn_target, name):
    m, k = x.shape
    n = w.shape[1]
    tm = _pick_tile(m, tm_target, 16)
    tn = _pick_tile(n, tn_target, LANES)
    return pl.pallas_call(
        _mm_kernel,
        grid=(m // tm, n // tn),
        in_specs=[pl.BlockSpec((tm, k), lambda i, j: (i, 0)),
                  pl.BlockSpec((k, tn), lambda i, j: (0, j))],
        out_specs=pl.BlockSpec((tm, tn), lambda i, j: (i, j)),
        out_shape=jax.ShapeDtypeStruct((m, n), out_dtype),
        compiler_params=_cparams(("parallel", "arbitrary")),
        name=name,
    )(x, w)


def _gla_kernel(q_ref, k_ref, v_ref, r_ref, gl_ref, wg_ref, bg_ref, ng_ref, o_ref, st_ref,
                *, n_chunks, dk):
    C, S = GLA_CHUNK, GLA_SUB

    @pl.when(pl.program_id(2) == 0)
    def _():
        st_ref[...] = jnp.zeros_like(st_ref)

    ri = lax.broadcasted_iota(jnp.int32, (C, C), 0)
    ci = lax.broadcasted_iota(jnp.int32, (C, C), 1)
    tri = (ri >= ci).astype(BF16)
    rs = lax.broadcasted_iota(jnp.int32, (S, S), 0)
    cs = lax.broadcasted_iota(jnp.int32, (S, S), 1)
    nt = (((1,), (1,)), ((), ()))
    tn = (((0,), (0,)), ((), ()))

    def chunk(c, carry):
        r0 = pl.multiple_of(c * C, C)
        q = q_ref[pl.ds(r0, C), :].astype(F32) * (dk ** -0.5)
        kk = k_ref[pl.ds(r0, C), :].astype(F32)
        vv = v_ref[pl.ds(r0, C), :]
        gl = gl_ref[pl.ds(r0, C), :].astype(BF16)
        z = jnp.dot(gl, wg_ref[...], preferred_element_type=F32) + bg_ref[...]
        la = (jnp.minimum(z, 0.0) - jnp.log(1.0 + jnp.exp(-jnp.abs(z)))) * (1.0 / GLA_GATE_TAU)
        h1 = la.astype(BF16)
        e1 = la - h1.astype(F32)
        h2 = e1.astype(BF16)
        h3 = (e1 - h2.astype(F32)).astype(BF16)
        b = (jnp.dot(tri, h1, preferred_element_type=F32)
             + jnp.dot(tri, h2, preferred_element_type=F32)
             + jnp.dot(tri, h3, preferred_element_type=F32))

        st = st_ref[...]
        qe = (q * jnp.exp(b)).astype(BF16)
        o_inter = lax.dot_general(qe, st.astype(BF16), nt, preferred_element_type=F32)

        rows = []
        for i in range(C // S):
            lo = i * S
            qs = q[lo:lo + S]
            bs = b[lo:lo + S]
            dmat = jnp.zeros((S, S), F32)
            for j in range(S):
                kj = kk[lo + j:lo + j + 1]
                bj = b[lo + j:lo + j + 1]
                p = qs * kj * jnp.exp(jnp.minimum(bs - bj, 0.0))
                a = jnp.sum(p, axis=-1, keepdims=True)
                dmat = jnp.where(cs == j, a, dmat)
            dmat = jnp.where(rs >= cs, dmat, 0.0)
            o_i = jnp.dot(dmat.astype(BF16), vv[lo:lo + S], preferred_element_type=F32)
            if i > 0:
                bref = b[lo - 1:lo]
                qt = (qs * jnp.exp(bs - bref)).astype(BF16)
                kt = (kk[:lo] * jnp.exp(bref - b[:lo])).astype(BF16)
                off = lax.dot_general(qt, kt, nt, preferred_element_type=F32)
                o_i = o_i + jnp.dot(off.astype(BF16), vv[:lo], preferred_element_type=F32)
            rows.append(o_i)
        o = o_inter + jnp.concatenate(rows, axis=0)

        bl = b[C - 1:C]
        khat = (kk * jnp.exp(bl - b)).astype(BF16)
        upd = lax.dot_general(vv, khat, tn, preferred_element_type=F32)
        st_ref[...] = st * jnp.exp(bl) + upd

        ms = jnp.mean(o * o, axis=-1, keepdims=True)
        r = r_ref[pl.ds(r0, C), :].astype(F32)
        y = o * lax.rsqrt(ms + RMS_EPS) * ng_ref[...] * (r * jax.nn.sigmoid(r))
        o_ref[pl.ds(r0, C), :] = y.astype(o_ref.dtype)
        return carry

    lax.fori_loop(0, n_chunks, chunk, 0)


def _gla(qkvr, glow, wg, bg, ng, *, batch, seq_rows, d_model):
    H = GLA_HEADS
    dk = d_model // 2 // H
    dv = d_model // H
    n_rows = batch * seq_rows
    rblk = _pick_tile(seq_rows, 640, GLA_CHUNK)
    steps = seq_rows // rblk
    kq, kv = (H * dk) // dk, (2 * H * dk) // dv

    def rowmap(off):
        return lambda b, h, i: (b * steps + i, off + h)

    return pl.pallas_call(
        functools.partial(_gla_kernel, n_chunks=rblk // GLA_CHUNK, dk=dk),
        grid=(batch, H, steps),
        in_specs=[pl.BlockSpec((rblk, dk), rowmap(0)),
                  pl.BlockSpec((rblk, dk), rowmap(kq)),
                  pl.BlockSpec((rblk, dv), rowmap(kv)),
                  pl.BlockSpec((rblk, dv), rowmap(kv + H)),
                  pl.BlockSpec((rblk, GATE_PAD), lambda b, h, i: (b * steps + i, 0)),
                  pl.BlockSpec((GATE_PAD, dk), lambda b, h, i: (0, h)),
                  pl.BlockSpec((1, dk), lambda b, h, i: (0, h)),
                  pl.BlockSpec((1, dv), lambda b, h, i: (0, h))],
        out_specs=pl.BlockSpec((rblk, dv), lambda b, h, i: (b * steps + i, h)),
        out_shape=jax.ShapeDtypeStruct((n_rows, H * dv), BF16),
        scratch_shapes=[pltpu.VMEM((dv, dk), F32)],
        compiler_params=_cparams(("parallel", "parallel", "arbitrary")),
        name="gla_chunks",
    )(qkvr, qkvr, qkvr, qkvr, glow, wg, bg, ng)


def _conv_kernel(x_ref, wb_ref, wc_ref, wh_ref, cw_ref, o_ref, ubuf_ref, carry_ref,
                 *, seq_rows, pad_rows, tm):
    i = pl.program_id(0)
    j = pl.program_id(1)
    x = x_ref[...]
    bg = jnp.dot(x, wb_ref[...], preferred_element_type=F32)
    cg = jnp.dot(x, wc_ref[...], preferred_element_type=F32)
    hh = jnp.dot(x, wh_ref[...], preferred_element_type=F32)
    row = i * tm + lax.broadcasted_iota(jnp.int32, (tm, 1), 0)
    u = jnp.where(row % seq_rows >= pad_rows, cg * hh, 0.0)

    @pl.when(i == 0)
    def _():
        carry_ref[j] = jnp.zeros(carry_ref.shape[1:], F32)

    ubuf_ref[pl.ds(0, SUBLANES), :] = carry_ref[j]
    ubuf_ref[pl.ds(SUBLANES, tm), :] = u
    carry_ref[j] = u[tm - SUBLANES:]
    cw = cw_ref[...]
    conv = u * cw[CONV_WIDTH - 1:CONV_WIDTH]
    for s in range(1, CONV_WIDTH):
        conv = conv + ubuf_ref[pl.ds(SUBLANES - s, tm), :] * cw[CONV_WIDTH - 1 - s:CONV_WIDTH - s]
    o_ref[...] = (bg * conv).astype(o_ref.dtype)


def _conv_mix(xb, w_in, conv_w, *, seq_rows, pad_rows):
    n_rows, d = xb.shape
    tm = _pick_tile(n_rows, 1024, SUBLANES)
    tn = _pick_tile(d, 512, LANES)
    nj = d // tn
    return pl.pallas_call(
        functools.partial(_conv_kernel, seq_rows=seq_rows, pad_rows=pad_rows, tm=tm),
        grid=(n_rows // tm, nj),
        in_specs=[pl.BlockSpec((tm, d), lambda i, j: (i, 0)),
                  pl.BlockSpec((d, tn), lambda i, j: (0, j)),
                  pl.BlockSpec((d, tn), lambda i, j: (0, nj + j)),
                  pl.BlockSpec((d, tn), lambda i, j: (0, 2 * nj + j)),
                  pl.BlockSpec((CONV_WIDTH, tn), lambda i, j: (0, j))],
        out_specs=pl.BlockSpec((tm, tn), lambda i, j: (i, j)),
        out_shape=jax.ShapeDtypeStruct((n_rows, d), BF16),
        scratch_shapes=[pltpu.VMEM((tm + SUBLANES, tn), F32),
                        pltpu.VMEM((nj, SUBLANES, tn), F32)],
        compiler_params=_cparams(("arbitrary", "arbitrary")),
        name="conv_mix",
    )(xb, w_in, w_in, w_in, conv_w)


def _layer_norm(y, g, b):
    mu = jnp.mean(y, axis=-1, keepdims=True)
    yc = y - mu
    var = jnp.mean(yc * yc, axis=-1, keepdims=True)
    return yc * lax.rsqrt(var + LN_EPS) * g + b


def _proj_ln_kernel(a_ref, w_ref, h_ref, g_ref, b_ref, rw_ref, o_ref, lg_ref):
    y = jnp.dot(a_ref[...], w_ref[...], preferred_element_type=F32) + DN_ALPHA * h_ref[...]
    h = _layer_norm(y, g_ref[...], b_ref[...])
    o_ref[...] = h
    lg_ref[...] = lax.dot_general(rw_ref[...], h.astype(BF16), (((1,), (1,)), ((), ())),
                                  preferred_element_type=F32)


def _proj_ln_router(a, w, h, g, b, rw_t):
    n_rows, kin = a.shape
    d = w.shape[1]
    e = rw_t.shape[0]
    tm = _pick_tile(n_rows, 256, LANES)
    return pl.pallas_call(
        _proj_ln_kernel,
        grid=(n_rows // tm,),
        in_specs=[pl.BlockSpec((tm, kin), lambda i: (i, 0)),
                  pl.BlockSpec((kin, d), lambda i: (0, 0)),
                  pl.BlockSpec((tm, d), lambda i: (i, 0)),
                  pl.BlockSpec((1, d), lambda i: (0, 0)),
                  pl.BlockSpec((1, d), lambda i: (0, 0)),
                  pl.BlockSpec((e, d), lambda i: (0, 0))],
        out_specs=[pl.BlockSpec((tm, d), lambda i: (i, 0)),
                   pl.BlockSpec((e, tm), lambda i: (0, i))],
        out_shape=[jax.ShapeDtypeStruct((n_rows, d), F32),
                   jax.ShapeDtypeStruct((e, n_rows), F32)],
        compiler_params=_cparams(("parallel",)),
        name="proj_ln_router",
    )(a, w, h, g, b, rw_t)


def _beats(cand, cand_idx, ref, ref_idx):
    return (cand > ref) | ((cand == ref) & (cand_idx < ref_idx))


def _route_kernel(lg_ref, bias_ref, w_ref, pos_ref, cnt_ref, run_ref, *, t):
    E, G = N_EXPERTS, N_GROUPS
    gs = E // G

    @pl.when(pl.program_id(0) == 0)
    def _():
        run_ref[...] = jnp.zeros_like(run_ref)

    s = jax.nn.sigmoid(lg_ref[...])
    c = s + bias_ref[...][:, :1]
    sub = lax.broadcasted_iota(jnp.int32, (gs, t), 0)

    grp_rows = []
    for g in range(G):
        cg = c[g * gs:(g + 1) * gs]
        rank = jnp.zeros((gs, t), jnp.int32)
        for m in range(gs):
            rank = rank + _beats(cg[m:m + 1], m, cg, sub).astype(jnp.int32)
        grp_rows.append(jnp.sum(jnp.where(rank < 2, cg, 0.0), axis=0, keepdims=True))
    gidx = lax.broadcasted_iota(jnp.int32, (G, t), 0)
    gscore = jnp.zeros((G, t), F32)
    for g in range(G):
        gscore = jnp.where(gidx == g, grp_rows[g], gscore)
    grank = jnp.zeros((G, t), jnp.int32)
    for m in range(G):
        grank = grank + _beats(gscore[m:m + 1], m, gscore, gidx).astype(jnp.int32)
    gsel = grank < TOPK_GROUPS

    masked = jnp.concatenate(
        [jnp.where(gsel[g:g + 1], c[g * gs:(g + 1) * gs], -jnp.inf) for g in range(G)], axis=0)
    eidx = lax.broadcasted_iota(jnp.int32, (E, t), 0)
    rank = jnp.zeros((E, t), jnp.int32)
    for m in range(E):
        rank = rank + _beats(masked[m:m + 1], m, masked, eidx).astype(jnp.int32)
    sel = rank < TOP_K
    gate = jnp.where(sel, s, 0.0)
    w_ref[...] = gate / jnp.sum(gate, axis=0, keepdims=True) * ROUTED_SCALE

    li = lax.broadcasted_iota(jnp.int32, (t, t), 0)
    lj = lax.broadcasted_iota(jnp.int32, (t, t), 1)
    upper = (li <= lj).astype(BF16)
    self_ = sel.astype(F32)
    incl = jnp.dot(sel.astype(BF16), upper, preferred_element_type=F32)
    run = run_ref[...][:, :1]
    pos_ref[...] = jnp.where(sel, run + incl - self_, -1.0)
    run_new = run + jnp.sum(self_, axis=1, keepdims=True)
    run_ref[...] = jnp.broadcast_to(run_new, run_ref.shape)
    cnt_ref[...] = jnp.broadcast_to(run_new, cnt_ref.shape)


def _route(logits_t, bias_col):
    e, n_rows = logits_t.shape
    t = _pick_tile(n_rows, 512, LANES)
    return pl.pallas_call(
        functools.partial(_route_kernel, t=t),
        grid=(n_rows // t,),
        in_specs=[pl.BlockSpec((e, t), lambda i: (0, i)),
                  pl.BlockSpec((e, LANES), lambda i: (0, 0))],
        out_specs=[pl.BlockSpec((e, t), lambda i: (0, i)),
                   pl.BlockSpec((e, t), lambda i: (0, i)),
                   pl.BlockSpec((e, LANES), lambda i: (0, 0))],
        out_shape=[jax.ShapeDtypeStruct((e, n_rows), F32),
                   jax.ShapeDtypeStruct((e, n_rows), F32),
                   jax.ShapeDtypeStruct((e, LANES), F32)],
        scratch_shapes=[pltpu.VMEM((e, LANES), F32)],
        compiler_params=_cparams(("arbitrary",)),
        name="route",
    )(logits_t, bias_col)


def _compact_kernel(w_ref, pos_ref, pst_ref, slot_ref, w8_ref, *, t):
    E = N_EXPERTS
    pos = pos_ref[...]
    sel = pos >= 0.0
    ri = lax.broadcasted_iota(jnp.int32, (E, E), 0)
    ci = lax.broadcasted_iota(jnp.int32, (E, E), 1)
    below = (ci < ri).astype(BF16)
    order = jnp.dot(below, sel.astype(BF16), preferred_element_type=F32)
    slot = pst_ref[...][:, :1] + pos
    wd = w_ref[...]
    kidx = lax.broadcasted_iota(jnp.int32, (TOP_K, t), 0)
    slots = jnp.zeros((TOP_K, t), F32)
    w8 = jnp.zeros((TOP_K, t), F32)
    for k in range(TOP_K):
        m = sel & (order == float(k))
        slots = jnp.where(kidx == k, jnp.sum(jnp.where(m, slot, 0.0), axis=0, keepdims=True), slots)
        w8 = jnp.where(kidx == k, jnp.sum(jnp.where(m, wd, 0.0), axis=0, keepdims=True), w8)
    slot_ref[...] = slots.astype(jnp.int32)
    w8_ref[...] = w8


def _compact(w_dense, pos_dense, pstart_col):
    e, n_rows = w_dense.shape
    t = _pick_tile(n_rows, 512, LANES)
    return pl.pallas_call(
        functools.partial(_compact_kernel, t=t),
        grid=(n_rows // t,),
        in_specs=[pl.BlockSpec((e, t), lambda i: (0, i)),
                  pl.BlockSpec((e, t), lambda i: (0, i)),
                  pl.BlockSpec((e, LANES), lambda i: (0, 0))],
        out_specs=[pl.BlockSpec((TOP_K, t), lambda i: (0, i)),
                   pl.BlockSpec((TOP_K, t), lambda i: (0, i))],
        out_shape=[jax.ShapeDtypeStruct((TOP_K, n_rows), jnp.int32),
                   jax.ShapeDtypeStruct((TOP_K, n_rows), F32)],
        compiler_params=_cparams(("parallel",)),
        name="compact",
    )(w_dense, pos_dense, pstart_col)


def _dispatch_kernel(pad_lo_ref, pad_hi_ref, slot_ref, h_ref, xs_ref, zero_ref, sem, zsem, *, t):
    n_copy = TOP_K * t

    def row_copy(a):
        tok = a % t
        return pltpu.make_async_copy(h_ref.at[pl.ds(tok, 1), :],
                                     xs_ref.at[pl.ds(slot_ref[a], 1), :], sem)

    def issue(a, carry):
        row_copy(a).start()
        return carry

    lax.fori_loop(0, n_copy, issue, 0)

    @pl.when(pl.program_id(0) == 0)
    def _():
        zero_ref[...] = jnp.zeros_like(zero_ref)

        def zero_copy(p):
            return pltpu.make_async_copy(zero_ref, xs_ref.at[pl.ds(p, 1), :], zsem)

        def per_expert(e, carry):
            lo, hi = pad_lo_ref[e], pad_hi_ref[e]

            def zissue(p, c):
                zero_copy(p).start()
                return c

            def zwait(p, c):
                zero_copy(p).wait()
                return c

            lax.fori_loop(lo, hi, zissue, 0)
            lax.fori_loop(lo, hi, zwait, 0)
            return carry

        lax.fori_loop(0, N_EXPERTS, per_expert, 0)

    def drain(a, carry):
        row_copy(a).wait()
        return carry

    lax.fori_loop(0, n_copy, drain, 0)


def _dispatch(h, slots_tiled, pad_lo, pad_hi, n_slots, t):
    n_rows, d = h.shape
    n_tiles = n_rows // t
    return pl.pallas_call(
        functools.partial(_dispatch_kernel, t=t),
        grid_spec=pltpu.PrefetchScalarGridSpec(
            num_scalar_prefetch=2,
            grid=(n_tiles,),
            in_specs=[pl.BlockSpec((None, None, TOP_K * t), lambda i, lo, hi: (i, 0, 0),
                                   memory_space=pltpu.SMEM),
                      pl.BlockSpec((t, d), lambda i, lo, hi: (i, 0))],
            out_specs=pl.BlockSpec(memory_space=pl.ANY),
            scratch_shapes=[pltpu.VMEM((1, d), F32),
                            pltpu.SemaphoreType.DMA(()),
                            pltpu.SemaphoreType.DMA(())]),
        out_shape=jax.ShapeDtypeStruct((n_slots, d), F32),
        compiler_params=_cparams(("arbitrary",)),
        name="dispatch",
    )(pad_lo, pad_hi, slots_tiled, h)


def _swiglu(x, w1, w3, w2):
    a = jnp.dot(x, w1, preferred_element_type=F32)
    g = jnp.dot(x, w3, preferred_element_type=F32)
    hmid = (a * jax.nn.sigmoid(a) * g).astype(BF16)
    return jnp.dot(hmid, w2, preferred_element_type=F32)


def _expert_kernel(be_ref, nu_ref, x_ref, w1_ref, w3_ref, w2_ref, y_ref):
    @pl.when(pl.program_id(0) < nu_ref[0])
    def _():
        y_ref[...] = _swiglu(x_ref[...].astype(BF16), w1_ref[...], w3_ref[...], w2_ref[...])


def _experts(xs, w1, w3, w2, block_e, n_used):
    n_slots, d = xs.shape
    f = w1.shape[2]
    nb = n_slots // MOE_BLOCK

    def blk(b, be, nu):
        return jnp.minimum(b, nu[0] - 1)

    return pl.pallas_call(
        _expert_kernel,
        grid_spec=pltpu.PrefetchScalarGridSpec(
            num_scalar_prefetch=2,
            grid=(nb,),
            in_specs=[pl.BlockSpec((MOE_BLOCK, d), lambda b, be, nu: (blk(b, be, nu), 0)),
                      pl.BlockSpec((None, d, f), lambda b, be, nu: (be[blk(b, be, nu)], 0, 0)),
                      pl.BlockSpec((None, d, f), lambda b, be, nu: (be[blk(b, be, nu)], 0, 0)),
                      pl.BlockSpec((None, f, d), lambda b, be, nu: (be[blk(b, be, nu)], 0, 0))],
            out_specs=pl.BlockSpec((MOE_BLOCK, d), lambda b, be, nu: (blk(b, be, nu), 0))),
        out_shape=jax.ShapeDtypeStruct((n_slots, d), F32),
        compiler_params=_cparams(("arbitrary",)),
        name="experts",
    )(block_e, n_used, xs, w1, w3, w2)


def _combine_kernel(slot_ref, h_ref, w8_ref, ws1_ref, ws3_ref, ws2_ref, g_ref, b_ref, ys_ref,
                    o_ref, ob_ref, ybuf_ref, sem, *, t):
    n_copy = TOP_K * t

    def row_copy(a):
        k = a // t
        tok = a % t
        return pltpu.make_async_copy(ys_ref.at[pl.ds(slot_ref[a], 1), :],
                                     ybuf_ref.at[k, pl.ds(tok, 1), :], sem)

    def issue(a, carry):
        row_copy(a).start()
        return carry

    lax.fori_loop(0, n_copy, issue, 0)

    h = h_ref[...]
    shared = _swiglu(h.astype(BF16), ws1_ref[...], ws3_ref[...], ws2_ref[...])

    def drain(a, carry):
        row_copy(a).wait()
        return carry

    lax.fori_loop(0, n_copy, drain, 0)

    w8 = w8_ref[...]
    acc = ybuf_ref[0] * w8[:, 0:1]
    for k in range(1, TOP_K):
        acc = acc + ybuf_ref[k] * w8[:, k:k + 1]
    y = DN_ALPHA * h + (acc + shared)
    out = _layer_norm(y, g_ref[...], b_ref[...])
    o_ref[...] = out
    ob_ref[...] = out.astype(BF16)


def _combine(h, ys, slots_tiled, w8_t, ws1, ws3, ws2, g, b, t):
    n_rows, d = h.shape
    f = ws1.shape[1]
    n_tiles = n_rows // t
    return pl.pallas_call(
        functools.partial(_combine_kernel, t=t),
        grid=(n_tiles,),
        in_specs=[pl.BlockSpec((None, None, TOP_K * t), lambda i: (i, 0, 0),
                               memory_space=pltpu.SMEM),
                  pl.BlockSpec((t, d), lambda i: (i, 0)),
                  pl.BlockSpec((t, TOP_K), lambda i: (i, 0)),
                  pl.BlockSpec((d, f), lambda i: (0, 0)),
                  pl.BlockSpec((d, f), lambda i: (0, 0)),
                  pl.BlockSpec((f, d), lambda i: (0, 0)),
                  pl.BlockSpec((1, d), lambda i: (0, 0)),
                  pl.BlockSpec((1, d), lambda i: (0, 0)),
                  pl.BlockSpec(memory_space=pl.ANY)],
        out_specs=[pl.BlockSpec((t, d), lambda i: (i, 0)),
                   pl.BlockSpec((t, d), lambda i: (i, 0))],
        out_shape=[jax.ShapeDtypeStruct((n_rows, d), F32),
                   jax.ShapeDtypeStruct((n_rows, d), BF16)],
        scratch_shapes=[pltpu.VMEM((TOP_K, t, d), F32),
                        pltpu.SemaphoreType.DMA(())],
        compiler_params=_cparams(("arbitrary",)),
        name="combine_ln",
    )(slots_tiled, h, w8_t, ws1, ws3, ws2, g, b, ys)


def _moe_ln(h, logits_t, router_bias, w1, w3, w2, ws1, ws3, ws2, g, b):
    n_rows, d = h.shape
    E = N_EXPERTS
    bias_col = jnp.broadcast_to(router_bias.astype(F32)[:, None], (E, LANES))
    w_dense, pos_dense, cnt = _route(logits_t, bias_col)

    counts = cnt[:, 0].astype(jnp.int32)
    pcounts = (counts + MOE_BLOCK - 1) // MOE_BLOCK * MOE_BLOCK
    pends = jnp.cumsum(pcounts)
    pstarts = pends - pcounts
    n_blocks = -(-(n_rows * TOP_K) // MOE_BLOCK) + E
    n_slots = n_blocks * MOE_BLOCK
    block_e = jnp.minimum(
        jnp.searchsorted(pends, jnp.arange(n_blocks, dtype=jnp.int32) * MOE_BLOCK, side="right"),
        E - 1).astype(jnp.int32)
    n_used = (pends[-1:] // MOE_BLOCK).astype(jnp.int32)
    pstart_col = jnp.broadcast_to(pstarts.astype(F32)[:, None], (E, LANES))

    slots, w8 = _compact(w_dense, pos_dense, pstart_col)
    t = _pick_tile(n_rows, 128, LANES)
    n_tiles = n_rows // t
    slots_tiled = slots.reshape(TOP_K, n_tiles, t).transpose(1, 0, 2).reshape(n_tiles, 1, TOP_K * t)
    xs = _dispatch(h, slots_tiled, (pstarts + counts).astype(jnp.int32), pends.astype(jnp.int32),
                   n_slots, t)
    ys = _experts(xs, w1, w3, w2, block_e, n_used)
    return _combine(h, ys, slots_tiled, w8.T, ws1, ws3, ws2, g, b, t)


def kernel(x, meta_tokens, gla_w_in, gla_w_gate_up, gla_b_gate, gla_norm_g, gla_w_out,
           conv_w_in, conv_w, conv_w_out, ln1_g, ln1_b, router_w, router_bias,
           exp_w1, exp_w3, exp_w2, shared_w1, shared_w3, shared_w2, ln2_g, ln2_b):
    batch, seq, d = x.shape
    H = GLA_HEADS
    dk = d // 2 // H
    n_qkvr = 2 * H * dk + 2 * d
    rank = gla_w_in.shape[2] - n_qkvr
    seq_rows = -(-(N_META + seq) // ROW_ALIGN) * ROW_ALIGN
    pad_rows = seq_rows - N_META - seq
    n_rows = batch * seq_rows

    meta = jnp.broadcast_to(meta_tokens.astype(x.dtype)[None], (batch, N_META, d))
    h = jnp.concatenate([jnp.zeros((batch, pad_rows, d), x.dtype), meta, x], axis=1)
    h = h.reshape(n_rows, d)
    hb = h.astype(BF16)

    def row(v):
        return v.astype(F32)[None, :]

    for i in range(DEPTH):
        jm = i // 2
        if i % 2 == 0:
            w_in = gla_w_in[jm]
            w_qkvr = w_in[:, :n_qkvr].astype(BF16)
            w_gate = jnp.pad(w_in[:, n_qkvr:], ((0, 0), (0, GATE_PAD - rank))).astype(BF16)
            wg_up = jnp.pad(gla_w_gate_up[jm], ((0, GATE_PAD - rank), (0, 0))).astype(BF16)
            qkvr = _matmul(hb, w_qkvr, BF16, 1664, 768, "gla_in_proj")
            glow = _matmul(hb, w_gate, F32, 1664, LANES, "gla_gate_proj")
            mix = _gla(qkvr, glow, wg_up, row(gla_b_gate[jm]), row(gla_norm_g[jm]),
                       batch=batch, seq_rows=seq_rows, d_model=d)
            w_out = gla_w_out[jm].astype(BF16)
        else:
            mix = _conv_mix(hb, conv_w_in[jm].astype(BF16), conv_w[jm].astype(F32),
                            seq_rows=seq_rows, pad_rows=pad_rows)
            w_out = conv_w_out[jm].astype(BF16)
        h, logits_t = _proj_ln_router(mix, w_out, h, row(ln1_g[i]), row(ln1_b[i]),
                                      router_w[i].T.astype(BF16))
        h, hb = _moe_ln(h, logits_t, router_bias[i],
                        exp_w1[i].astype(BF16), exp_w3[i].astype(BF16), exp_w2[i].astype(BF16),
                        shared_w1[i].astype(BF16), shared_w3[i].astype(BF16),
                        shared_w2[i].astype(BF16), row(ln2_g[i]), row(ln2_b[i]))
    return h.reshape(batch, seq_rows, d)[:, pad_rows + N_META:]
```

```python
import functools

import jax
import jax.numpy as jnp
from jax import lax
from jax.experimental import pallas as pl
from jax.experimental.pallas import tpu as pltpu

N_META = 16
GLA_HEADS = 4
GLA_GATE_TAU = 16.0
CONV_WIDTH = 3
N_EXPERTS = 64
TOP_K = 8
N_GROUPS = 8
TOPK_GROUPS = 4
ROUTED_SCALE = 2.5
LN_EPS = 1e-5
RMS_EPS = 1e-6
DEPTH = 2
DN_ALPHA = (2 * DEPTH) ** 0.25

LANES = 128
SUBLANES = 8
ROW_ALIGN = 128
GLA_CHUNK = 64
GLA_SUB = 16
GATE_PAD = LANES
MOE_BLOCK = 256
VMEM_LIMIT = 56 * 1024 * 1024

F32 = jnp.float32
BF16 = jnp.bfloat16


def _pick_tile(n, target, mult):
    best = None
    for t in range(mult, min(n, target) + 1, mult):
        if n % t == 0:
            best = t
    assert best is not None, (n, target, mult)
    return best


def _cparams(sem):
    return pltpu.CompilerParams(dimension_semantics=sem, vmem_limit_bytes=VMEM_LIMIT)


def _mm_kernel(x_ref, w_ref, o_ref):
    o_ref[...] = jnp.dot(x_ref[...], w_ref[...], preferred_element_type=F32).astype(o_ref.dtype)


def _matmul(x, w, out_dtype, tm_target, tn_target, name):
    m, k = x.shape
    n = w.shape[1]
    tm = _pick_tile(m, tm_target, 16)
    tn = _pick_tile(n, tn_target, LANES)
    return pl.pallas_call(
        _mm_kernel,
        grid=(m // tm, n // tn),
        in_specs=[pl.BlockSpec((tm, k), lambda i, j: (i, 0)),
                  pl.BlockSpec((k, tn), lambda i, j: (0, j))],
        out_specs=pl.BlockSpec((tm, tn), lambda i, j: (i, j)),
        out_shape=jax.ShapeDtypeStruct((m, n), out_dtype),
        compiler_params=_cparams(("parallel", "arbitrary")),
        name=name,
    )(x, w)


def _gla_kernel(q_ref, k_ref, v_ref, r_ref, gl_ref, wg_ref, bg_ref, ng_ref, o_ref, st_ref,
                *, n_chunks, dk):
    C, S = GLA_CHUNK, GLA_SUB

    @pl.when(pl.program_id(2) == 0)
    def _():
        st_ref[...] = jnp.zeros_like(st_ref)

    ri = lax.broadcasted_iota(jnp.int32, (C, C), 0)
    ci = lax.broadcasted_iota(jnp.int32, (C, C), 1)
    tri = (ri >= ci).astype(BF16)
    rs = lax.broadcasted_iota(jnp.int32, (S, S), 0)
    cs = lax.broadcasted_iota(jnp.int32, (S, S), 1)
    nt = (((1,), (1,)), ((), ()))
    tn = (((0,), (0,)), ((), ()))

    def chunk(c, carry):
        r0 = pl.multiple_of(c * C, C)
        q = q_ref[pl.ds(r0, C), :].astype(F32) * (dk ** -0.5)
        kk = k_ref[pl.ds(r0, C), :].astype(F32)
        vv = v_ref[pl.ds(r0, C), :]
        gl = gl_ref[pl.ds(r0, C), :].astype(BF16)
        z = jnp.dot(gl, wg_ref[...], preferred_element_type=F32) + bg_ref[...]
        la = (jnp.minimum(z, 0.0) - jnp.log(1.0 + jnp.exp(-jnp.abs(z)))) * (1.0 / GLA_GATE_TAU)
        h1 = la.astype(BF16)
        e1 = la - h1.astype(F32)
        h2 = e1.astype(BF16)
        h3 = (e1 - h2.astype(F32)).astype(BF16)
        b = (jnp.dot(tri, h1, preferred_element_type=F32)
             + jnp.dot(tri, h2, preferred_element_type=F32)
             + jnp.dot(tri, h3, preferred_element_type=F32))

        st = st_ref[...]
        qe = (q * jnp.exp(b)).astype(BF16)
        o_inter = lax.dot_general(qe, st.astype(BF16), nt, preferred_element_type=F32)

        rows = []
        for i in range(C // S):
            lo = i * S
            qs = q[lo:lo + S]
            bs = b[lo:lo + S]
            dmat = jnp.zeros((S, S), F32)
            for j in range(S):
                kj = kk[lo + j:lo + j + 1]
                bj = b[lo + j:lo + j + 1]
                p = qs * kj * jnp.exp(jnp.minimum(bs - bj, 0.0))
                a = jnp.sum(p, axis=-1, keepdims=True)
                dmat = jnp.where(cs == j, a, dmat)
            dmat = jnp.where(rs >= cs, dmat, 0.0)
            o_i = jnp.dot(dmat.astype(BF16), vv[lo:lo + S], preferred_element_type=F32)
            if i > 0:
                bref = b[lo - 1:lo]
                qt = (qs * jnp.exp(bs - bref)).astype(BF16)
                kt = (kk[:lo] * jnp.exp(bref - b[:lo])).astype(BF16)
                off = lax.dot_general(qt, kt, nt, preferred_element_type=F32)
                o_i = o_i + jnp.dot(off.astype(BF16), vv[:lo], preferred_element_type=F32)
            rows.append(o_i)
        o = o_inter + jnp.concatenate(rows, axis=0)

        bl = b[C - 1:C]
        khat = (kk * jnp.exp(bl - b)).astype(BF16)
        upd = lax.dot_general(vv, khat, tn, preferred_element_type=F32)
        st_ref[...] = st * jnp.exp(bl) + upd

        ms = jnp.mean(o * o, axis=-1, keepdims=True)
        r = r_ref[pl.ds(r0, C), :].astype(F32)
        y = o * lax.rsqrt(ms + RMS_EPS) * ng_ref[...] * (r * jax.nn.sigmoid(r))
        o_ref[pl.ds(r0, C), :] = y.astype(o_ref.dtype)
        return carry

    lax.fori_loop(0, n_chunks, chunk, 0)


def _gla(qkvr, glow, wg, bg, ng, *, batch, seq_rows, d_model):
    H = GLA_HEADS
    dk = d_model // 2 // H
    dv = d_model // H
    n_rows = batch * seq_rows
    rblk = _pick_tile(seq_rows, 640, GLA_CHUNK)
    steps = seq_rows // rblk
    kq, kv = (H * dk) // dk, (2 * H * dk) // dv

    def rowmap(off):
        return lambda b, h, i: (b * steps + i, off + h)

    return pl.pallas_call(
        functools.partial(_gla_kernel, n_chunks=rblk // GLA_CHUNK, dk=dk),
        grid=(batch, H, steps),
        in_specs=[pl.BlockSpec((rblk, dk), rowmap(0)),
                  pl.BlockSpec((rblk, dk), rowmap(kq)),
                  pl.BlockSpec((rblk, dv), rowmap(kv)),
                  pl.BlockSpec((rblk, dv), rowmap(kv + H)),
                  pl.BlockSpec((rblk, GATE_PAD), lambda b, h, i: (b * steps + i, 0)),
                  pl.BlockSpec((GATE_PAD, dk), lambda b, h, i: (0, h)),
                  pl.BlockSpec((1, dk), lambda b, h, i: (0, h)),
                  pl.BlockSpec((1, dv), lambda b, h, i: (0, h))],
        out_specs=pl.BlockSpec((rblk, dv), lambda b, h, i: (b * steps + i, h)),
        out_shape=jax.ShapeDtypeStruct((n_rows, H * dv), BF16),
        scratch_shapes=[pltpu.VMEM((dv, dk), F32)],
        compiler_params=_cparams(("parallel", "parallel", "arbitrary")),
        name="gla_chunks",
    )(qkvr, qkvr, qkvr, qkvr, glow, wg, bg, ng)


def _conv_kernel(x_ref, wb_ref, wc_ref, wh_ref, cw_ref, o_ref, ubuf_ref, carry_ref,
                 *, seq_rows, pad_rows, tm):
    i = pl.program_id(0)
    j = pl.program_id(1)
    x = x_ref[...]
    bg = jnp.dot(x, wb_ref[...], preferred_element_type=F32)
    cg = jnp.dot(x, wc_ref[...], preferred_element_type=F32)
    hh = jnp.dot(x, wh_ref[...], preferred_element_type=F32)
    row = i * tm + lax.broadcasted_iota(jnp.int32, (tm, 1), 0)
    u = jnp.where(row % seq_rows >= pad_rows, cg * hh, 0.0)

    @pl.when(i == 0)
    def _():
        carry_ref[j] = jnp.zeros(carry_ref.shape[1:], F32)

    ubuf_ref[pl.ds(0, SUBLANES), :] = carry_ref[j]
    ubuf_ref[pl.ds(SUBLANES, tm), :] = u
    carry_ref[j] = u[tm - SUBLANES:]
    cw = cw_ref[...]
    conv = u * cw[CONV_WIDTH - 1:CONV_WIDTH]
    for s in range(1, CONV_WIDTH):
        conv = conv + ubuf_ref[pl.ds(SUBLANES - s, tm), :] * cw[CONV_WIDTH - 1 - s:CONV_WIDTH - s]
    o_ref[...] = (bg * conv).astype(o_ref.dtype)


def _conv_mix(xb, w_in, conv_w, *, seq_rows, pad_rows):
    n_rows, d = xb.shape
    tm = _pick_tile(n_rows, 1024, SUBLANES)
    tn = _pick_tile(d, 512, LANES)
    nj = d // tn
    return pl.pallas_call(
        functools.partial(_conv_kernel, seq_rows=seq_rows, pad_rows=pad_rows, tm=tm),
        grid=(n_rows // tm, nj),
        in_specs=[pl.BlockSpec((tm, d), lambda i, j: (i, 0)),
                  pl.BlockSpec((d, tn), lambda i, j: (0, j)),
                  pl.BlockSpec((d, tn), lambda i, j: (0, nj + j)),
                  pl.BlockSpec((d, tn), lambda i, j: (0, 2 * nj + j)),
                  pl.BlockSpec((CONV_WIDTH, tn), lambda i, j: (0, j))],
        out_specs=pl.BlockSpec((tm, tn), lambda i, j: (i, j)),
        out_shape=jax.ShapeDtypeStruct((n_rows, d), BF16),
        scratch_shapes=[pltpu.VMEM((tm + SUBLANES, tn), F32),
                        pltpu.VMEM((nj, SUBLANES, tn), F32)],
        compiler_params=_cparams(("arbitrary", "arbitrary")),
        name="conv_mix",
    )(xb, w_in, w_in, w_in, conv_w)


def _layer_norm(y, g, b):
    mu = jnp.mean(y, axis=-1, keepdims=True)
    yc = y - mu
    var = jnp.mean(yc * yc, axis=-1, keepdims=True)
    return yc * lax.rsqrt(var + LN_EPS) * g + b


def _proj_ln_kernel(a_ref, w_ref, h_ref, g_ref, b_ref, rw_ref, o_ref, lg_ref):
    y = jnp.dot(a_ref[...], w_ref[...], preferred_element_type=F32) + DN_ALPHA * h_ref[...]
    h = _layer_norm(y, g_ref[...], b_ref[...])
    o_ref[...] = h
    lg_ref[...] = lax.dot_general(rw_ref[...], h.astype(BF16), (((1,), (1,)), ((), ())),
                                  preferred_element_type=F32)


def _proj_ln_router(a, w, h, g, b, rw_t):
    n_rows, kin = a.shape
    d = w.shape[1]
    e = rw_t.shape[0]
    tm = _pick_tile(n_rows, 256, LANES)
    return pl.pallas_call(
        _proj_ln_kernel,
        grid=(n_rows // tm,),
        in_specs=[pl.BlockSpec((tm, kin), lambda i: (i, 0)),
                  pl.BlockSpec((kin, d), lambda i: (0, 0)),
                  pl.BlockSpec((tm, d), lambda i: (i, 0)),
                  pl.BlockSpec((1, d), lambda i: (0, 0)),
                  pl.BlockSpec((1, d), lambda i: (0, 0)),
                  pl.BlockSpec((e, d), lambda i: (0, 0))],
        out_specs=[pl.BlockSpec((tm, d), lambda i: (i, 0)),
                   pl.BlockSpec((e, tm), lambda i: (0, i))],
        out_shape=[jax.ShapeDtypeStruct((n_rows, d), F32),
                   jax.ShapeDtypeStruct((e, n_rows), F32)],
        compiler_params=_cparams(("parallel",)),
        name="proj_ln_router",
    )(a, w, h, g, b, rw_t)


def _beats(cand, cand_idx, ref, ref_idx):
    return (cand > ref) | ((cand == ref) & (cand_idx < ref_idx))


def _route_kernel(lg_ref, bias_ref, w_ref, pos_ref, cnt_ref, run_ref, *, t):
    E, G = N_EXPERTS, N_GROUPS
    gs = E // G

    @pl.when(pl.program_id(0) == 0)
    def _():
        run_ref[...] = jnp.zeros_like(run_ref)

    s = jax.nn.sigmoid(lg_ref[...])
    c = s + bias_ref[...][:, :1]
    sub = lax.broadcasted_iota(jnp.int32, (gs, t), 0)

    grp_rows = []
    for g in range(G):
        cg = c[g * gs:(g + 1) * gs]
        rank = jnp.zeros((gs, t), jnp.int32)
        for m in range(gs):
            rank = rank + _beats(cg[m:m + 1], m, cg, sub).astype(jnp.int32)
        grp_rows.append(jnp.sum(jnp.where(rank < 2, cg, 0.0), axis=0, keepdims=True))
    gidx = lax.broadcasted_iota(jnp.int32, (G, t), 0)
    gscore = jnp.zeros((G, t), F32)
    for g in range(G):
        gscore = jnp.where(gidx == g, grp_rows[g], gscore)
    grank = jnp.zeros((G, t), jnp.int32)
    for m in range(G):
        grank = grank + _beats(gscore[m:m + 1], m, gscore, gidx).astype(jnp.int32)
    gsel = grank < TOPK_GROUPS

    masked = jnp.concatenate(
        [jnp.where(gsel[g:g + 1], c[g * gs:(g + 1) * gs], -jnp.inf) for g in range(G)], axis=0)
    eidx = lax.broadcasted_iota(jnp.int32, (E, t), 0)
    rank = jnp.zeros((E, t), jnp.int32)
    for m in range(E):
        rank = rank + _beats(masked[m:m + 1], m, masked, eidx).astype(jnp.int32)
    sel = rank < TOP_K
    gate = jnp.where(sel, s, 0.0)
    w_ref[...] = gate / jnp.sum(gate, axis=0, keepdims=True) * ROUTED_SCALE

    li = lax.broadcasted_iota(jnp.int32, (t, t), 0)
    lj = lax.broadcasted_iota(jnp.int32, (t, t), 1)
    upper = (li <= lj).astype(BF16)
    self_ = sel.astype(F32)
    incl = jnp.dot(sel.astype(BF16), upper, preferred_element_type=F32)
    run = run_ref[...][:, :1]
    pos_ref[...] = jnp.where(sel, run + incl - self_, -1.0)
    run_new = run + jnp.sum(self_, axis=1, keepdims=True)
    run_ref[...] = jnp.broadcast_to(run_new, run_ref.shape)
    cnt_ref[...] = jnp.broadcast_to(run_new, cnt_ref.shape)


def _route(logits_t, bias_col):
    e, n_rows = logits_t.shape
    t = _pick_tile(n_rows, 512, LANES)
    return pl.pallas_call(
        functools.partial(_route_kernel, t=t),
        grid=(n_rows // t,),
        in_specs=[pl.BlockSpec((e, t), lambda i: (0, i)),
                  pl.BlockSpec((e, LANES), lambda i: (0, 0))],
        out_specs=[pl.BlockSpec((e, t), lambda i: (0, i)),
                   pl.BlockSpec((e, t), lambda i: (0, i)),
                   pl.BlockSpec((e, LANES), lambda i: (0, 0))],
        out_shape=[jax.ShapeDtypeStruct((e, n_rows), F32),
                   jax.ShapeDtypeStruct((e, n_rows), F32),
                   jax.ShapeDtypeStruct((e, LANES), F32)],
        scratch_shapes=[pltpu.VMEM((e, LANES), F32)],
        compiler_params=_cparams(("arbitrary",)),
        name="route",
    )(logits_t, bias_col)


def _compact_kernel(w_ref, pos_ref, pst_ref, slot_ref, w8_ref, *, t):
    E = N_EXPERTS
    pos = pos_ref[...]
    sel = pos >= 0.0
    ri = lax.broadcasted_iota(jnp.int32, (E, E), 0)
    ci = lax.broadcasted_iota(jnp.int32, (E, E), 1)
    below = (ci < ri).astype(BF16)
    order = jnp.dot(below, sel.astype(BF16), preferred_element_type=F32)
    slot = pst_ref[...][:, :1] + pos
    wd = w_ref[...]
    kidx = lax.broadcasted_iota(jnp.int32, (TOP_K, t), 0)
    slots = jnp.zeros((TOP_K, t), F32)
    w8 = jnp.zeros((TOP_K, t), F32)
    for k in range(TOP_K):
        m = sel & (order == float(k))
        slots = jnp.where(kidx == k, jnp.sum(jnp.where(m, slot, 0.0), axis=0, keepdims=True), slots)
        w8 = jnp.where(kidx == k, jnp.sum(jnp.where(m, wd, 0.0), axis=0, keepdims=True), w8)
    slot_ref[...] = slots.astype(jnp.int32)
    w8_ref[...] = w8


def _compact(w_dense, pos_dense, pstart_col):
    e, n_rows = w_dense.shape
    t = _pick_tile(n_rows, 512, LANES)
    return pl.pallas_call(
        functools.partial(_compact_kernel, t=t),
        grid=(n_rows // t,),
        in_specs=[pl.BlockSpec((e, t), lambda i: (0, i)),
                  pl.BlockSpec((e, t), lambda i: (0, i)),
                  pl.BlockSpec((e, LANES), lambda i: (0, 0))],
        out_specs=[pl.BlockSpec((TOP_K, t), lambda i: (0, i)),
                   pl.BlockSpec((TOP_K, t), lambda i: (0, i))],
        out_shape=[jax.ShapeDtypeStruct((TOP_K, n_rows), jnp.int32),
                   jax.ShapeDtypeStruct((TOP_K, n_rows), F32)],
        compiler_params=_cparams(("parallel",)),
        name="compact",
    )(w_dense, pos_dense, pstart_col)


def _dispatch_kernel(pad_lo_ref, pad_hi_ref, slot_ref, h_ref, xs_ref, zero_ref, sem, zsem, *, t):
    def issue(tok, carry):
        src = h_ref.at[pl.ds(tok, 1), :]
        for k in range(TOP_K):
            pltpu.make_async_copy(src, xs_ref.at[pl.ds(slot_ref[k * t + tok], 1), :], sem).start()
        return carry

    lax.fori_loop(0, t, issue, 0, unroll=2)

    @pl.when(pl.program_id(0) == 0)
    def _():
        zero_ref[...] = jnp.zeros_like(zero_ref)

        def zero_copy(p):
            return pltpu.make_async_copy(zero_ref, xs_ref.at[pl.ds(p, 1), :], zsem)

        def per_expert(e, carry):
            lo, hi = pad_lo_ref[e], pad_hi_ref[e]

            def zissue(p, c):
                zero_copy(p).start()
                return c

            def zwait(p, c):
                zero_copy(p).wait()
                return c

            lax.fori_loop(lo, hi, zissue, 0)
            lax.fori_loop(lo, hi, zwait, 0)
            return carry

        lax.fori_loop(0, N_EXPERTS, per_expert, 0)

    for k in range(TOP_K):
        pltpu.make_async_copy(h_ref, xs_ref.at[pl.ds(0, t), :], sem).wait()


def _dispatch(h, slots_tiled, pad_lo, pad_hi, n_slots, t):
    n_rows, d = h.shape
    n_tiles = n_rows // t
    return pl.pallas_call(
        functools.partial(_dispatch_kernel, t=t),
        grid_spec=pltpu.PrefetchScalarGridSpec(
            num_scalar_prefetch=2,
            grid=(n_tiles,),
            in_specs=[pl.BlockSpec((None, None, TOP_K * t), lambda i, lo, hi: (i, 0, 0),
                                   memory_space=pltpu.SMEM),
                      pl.BlockSpec((t, d), lambda i, lo, hi: (i, 0))],
            out_specs=pl.BlockSpec(memory_space=pl.ANY),
            scratch_shapes=[pltpu.VMEM((1, d), F32),
                            pltpu.SemaphoreType.DMA(()),
                            pltpu.SemaphoreType.DMA(())]),
        out_shape=jax.ShapeDtypeStruct((n_slots, d), F32),
        compiler_params=_cparams(("arbitrary",)),
        name="dispatch",
    )(pad_lo, pad_hi, slots_tiled, h)


def _swiglu(x, w1, w3, w2):
    a = jnp.dot(x, w1, preferred_element_type=F32)
    g = jnp.dot(x, w3, preferred_element_type=F32)
    hmid = (a * jax.nn.sigmoid(a) * g).astype(BF16)
    return jnp.dot(hmid, w2, preferred_element_type=F32)


def _expert_kernel(be_ref, nu_ref, x_ref, w1_ref, w3_ref, w2_ref, y_ref, w1b_ref, w3b_ref, w2b_ref):
    b = pl.program_id(0)

    @pl.when(b < nu_ref[0])
    def _():
        @pl.when((b == 0) | (be_ref[b] != be_ref[jnp.maximum(b - 1, 0)]))
        def _():
            w1b_ref[...] = w1_ref[...].astype(BF16)
            w3b_ref[...] = w3_ref[...].astype(BF16)
            w2b_ref[...] = w2_ref[...].astype(BF16)

        y_ref[...] = _swiglu(x_ref[...].astype(BF16), w1b_ref[...], w3b_ref[...], w2b_ref[...])


def _experts(xs, w1, w3, w2, block_e, n_used):
    n_slots, d = xs.shape
    f = w1.shape[2]
    nb = n_slots // MOE_BLOCK

    def blk(b, be, nu):
        return jnp.minimum(b, nu[0] - 1)

    return pl.pallas_call(
        _expert_kernel,
        grid_spec=pltpu.PrefetchScalarGridSpec(
            num_scalar_prefetch=2,
            grid=(nb,),
            in_specs=[pl.BlockSpec((MOE_BLOCK, d), lambda b, be, nu: (blk(b, be, nu), 0)),
                      pl.BlockSpec((None, d, f), lambda b, be, nu: (be[blk(b, be, nu)], 0, 0)),
                      pl.BlockSpec((None, d, f), lambda b, be, nu: (be[blk(b, be, nu)], 0, 0)),
                      pl.BlockSpec((None, f, d), lambda b, be, nu: (be[blk(b, be, nu)], 0, 0))],
            out_specs=pl.BlockSpec((MOE_BLOCK, d), lambda b, be, nu: (blk(b, be, nu), 0)),
            scratch_shapes=[pltpu.VMEM((d, f), BF16), pltpu.VMEM((d, f), BF16),
                            pltpu.VMEM((f, d), BF16)]),
        out_shape=jax.ShapeDtypeStruct((n_slots, d), F32),
        compiler_params=_cparams(("arbitrary",)),
        name="experts",
    )(block_e, n_used, xs, w1, w3, w2)


def _combine_kernel(slot_ref, h_ref, w8_ref, ws1_ref, ws3_ref, ws2_ref, g_ref, b_ref, ys_ref,
                    o_ref, ob_ref, ybuf_ref, sem, *, t):
    def issue(tok, carry):
        for k in range(TOP_K):
            pltpu.make_async_copy(ys_ref.at[pl.ds(slot_ref[k * t + tok], 1), :],
                                  ybuf_ref.at[k, pl.ds(tok, 1), :], sem).start()
        return carry

    lax.fori_loop(0, t, issue, 0, unroll=2)

    h = h_ref[...]
    shared = _swiglu(h.astype(BF16), ws1_ref[...], ws3_ref[...], ws2_ref[...])

    for k in range(TOP_K):
        pltpu.make_async_copy(ys_ref.at[pl.ds(0, t), :], ybuf_ref.at[k], sem).wait()

    w8 = w8_ref[...]
    acc = ybuf_ref[0] * w8[:, 0:1]
    for k in range(1, TOP_K):
        acc = acc + ybuf_ref[k] * w8[:, k:k + 1]
    y = DN_ALPHA * h + (acc + shared)
    out = _layer_norm(y, g_ref[...], b_ref[...])
    o_ref[...] = out
    ob_ref[...] = out.astype(BF16)


def _combine(h, ys, slots_tiled, w8_t, ws1, ws3, ws2, g, b, t):
    n_rows, d = h.shape
    f = ws1.shape[1]
    n_tiles = n_rows // t
    return pl.pallas_call(
        functools.partial(_combine_kernel, t=t),
        grid=(n_tiles,),
        in_specs=[pl.BlockSpec((None, None, TOP_K * t), lambda i: (i, 0, 0),
                               memory_space=pltpu.SMEM),
                  pl.BlockSpec((t, d), lambda i: (i, 0)),
                  pl.BlockSpec((t, TOP_K), lambda i: (i, 0)),
                  pl.BlockSpec((d, f), lambda i: (0, 0)),
                  pl.BlockSpec((d, f), lambda i: (0, 0)),
                  pl.BlockSpec((f, d), lambda i: (0, 0)),
                  pl.BlockSpec((1, d), lambda i: (0, 0)),
                  pl.BlockSpec((1, d), lambda i: (0, 0)),
                  pl.BlockSpec(memory_space=pl.ANY)],
        out_specs=[pl.BlockSpec((t, d), lambda i: (i, 0)),
                   pl.BlockSpec((t, d), lambda i: (i, 0))],
        out_shape=[jax.ShapeDtypeStruct((n_rows, d), F32),
                   jax.ShapeDtypeStruct((n_rows, d), BF16)],
        scratch_shapes=[pltpu.VMEM((TOP_K, t, d), F32),
                        pltpu.SemaphoreType.DMA(())],
        compiler_params=_cparams(("arbitrary",)),
        name="combine_ln",
    )(slots_tiled, h, w8_t, ws1, ws3, ws2, g, b, ys)


def _moe_ln(h, logits_t, router_bias, w1, w3, w2, ws1, ws3, ws2, g, b):
    n_rows, d = h.shape
    E = N_EXPERTS
    bias_col = jnp.broadcast_to(router_bias.astype(F32)[:, None], (E, LANES))
    w_dense, pos_dense, cnt = _route(logits_t, bias_col)

    counts = cnt[:, 0].astype(jnp.int32)
    pcounts = (counts + MOE_BLOCK - 1) // MOE_BLOCK * MOE_BLOCK
    pends = jnp.cumsum(pcounts)
    pstarts = pends - pcounts
    n_blocks = -(-(n_rows * TOP_K) // MOE_BLOCK) + E
    n_slots = n_blocks * MOE_BLOCK
    block_lo = jnp.arange(n_blocks, dtype=jnp.int32) * MOE_BLOCK
    block_e = jnp.minimum(jnp.sum(pends[None, :] <= block_lo[:, None], axis=1), E - 1).astype(jnp.int32)
    n_used = (pends[-1:] // MOE_BLOCK).astype(jnp.int32)
    pstart_col = jnp.broadcast_to(pstarts.astype(F32)[:, None], (E, LANES))

    slots, w8 = _compact(w_dense, pos_dense, pstart_col)
    t = _pick_tile(n_rows, 128, LANES)
    n_tiles = n_rows // t
    slots_tiled = slots.reshape(TOP_K, n_tiles, t).transpose(1, 0, 2).reshape(n_tiles, 1, TOP_K * t)
    xs = _dispatch(h, slots_tiled, (pstarts + counts).astype(jnp.int32), pends.astype(jnp.int32),
                   n_slots, t)
    ys = _experts(xs, w1, w3, w2, block_e, n_used)
    return _combine(h, ys, slots_tiled, w8.T, ws1, ws3, ws2, g, b, t)


def kernel(x, meta_tokens, gla_w_in, gla_w_gate_up, gla_b_gate, gla_norm_g, gla_w_out,
           conv_w_in, conv_w, conv_w_out, ln1_g, ln1_b, router_w, router_bias,
           exp_w1, exp_w3, exp_w2, shared_w1, shared_w3, shared_w2, ln2_g, ln2_b):
    batch, seq, d = x.shape
    H = GLA_HEADS
    dk = d // 2 // H
    n_qkvr = 2 * H * dk + 2 * d
    rank = gla_w_in.shape[2] - n_qkvr
    seq_rows = -(-(N_META + seq) // ROW_ALIGN) * ROW_ALIGN
    pad_rows = seq_rows - N_META - seq
    n_rows = batch * seq_rows

    meta = jnp.broadcast_to(meta_tokens.astype(x.dtype)[None], (batch, N_META, d))
    h = jnp.concatenate([jnp.zeros((batch, pad_rows, d), x.dtype), meta, x], axis=1)
    h = h.reshape(n_rows, d)
    hb = h.astype(BF16)

    def row(v):
        return v.astype(F32)[None, :]

    for i in range(DEPTH):
        jm = i // 2
        if i % 2 == 0:
            w_in = gla_w_in[jm]
            w_qkvr = w_in[:, :n_qkvr].astype(BF16)
            w_gate = jnp.pad(w_in[:, n_qkvr:], ((0, 0), (0, GATE_PAD - rank))).astype(BF16)
            wg_up = jnp.pad(gla_w_gate_up[jm], ((0, GATE_PAD - rank), (0, 0))).astype(BF16)
            qkvr = _matmul(hb, w_qkvr, BF16, 1664, 768, "gla_in_proj")
            glow = _matmul(hb, w_gate, F32, 1664, LANES, "gla_gate_proj")
            mix = _gla(qkvr, glow, wg_up, row(gla_b_gate[jm]), row(gla_norm_g[jm]),
                       batch=batch, seq_rows=seq_rows, d_model=d)
            w_out = gla_w_out[jm].astype(BF16)
        else:
            mix = _conv_mix(hb, conv_w_in[jm].astype(BF16), conv_w[jm].astype(F32),
                            seq_rows=seq_rows, pad_rows=pad_rows)
            w_out = conv_w_out[jm].astype(BF16)
        h, logits_t = _proj_ln_router(mix, w_out, h, row(ln1_g[i]), row(ln1_b[i]),
                                      router_w[i].T.astype(BF16))
        h, hb = _moe_ln(h, logits_t, router_bias[i],
                        exp_w1[i], exp_w3[i], exp_w2[i],
                        shared_w1[i].astype(BF16), shared_w3[i].astype(BF16),
                        shared_w2[i].astype(BF16), row(ln2_g[i]), row(ln2_b[i]))
    return h.reshape(batch, seq_rows, d)[:, pad_rows + N_META:]
```

```python
import functools

import jax
import jax.numpy as jnp
from jax import lax
from jax.experimental import pallas as pl
from jax.experimental.pallas import tpu as pltpu

N_META = 16
GLA_HEADS = 4
GLA_GATE_TAU = 16.0
CONV_WIDTH = 3
N_EXPERTS = 64
TOP_K = 8
N_GROUPS = 8
TOPK_GROUPS = 4
ROUTED_SCALE = 2.5
LN_EPS = 1e-5
RMS_EPS = 1e-6
DEPTH = 2
DN_ALPHA = (2 * DEPTH) ** 0.25

LANES = 128
SUBLANES = 8
ROW_ALIGN = 128
GLA_CHUNK = 64
GLA_SUB = 16
GATE_PAD = LANES
MOE_BLOCK = 256
VMEM_LIMIT = 56 * 1024 * 1024

F32 = jnp.float32
BF16 = jnp.bfloat16


def _pick_tile(n, target, mult):
    best = None
    for t in range(mult, min(n, target) + 1, mult):
        if n % t == 0:
            best = t
    assert best is not None, (n, target, mult)
    return best


def _cparams(sem):
    return pltpu.CompilerParams(dimension_semantics=sem, vmem_limit_bytes=VMEM_LIMIT)


def _mm_kernel(x_ref, w_ref, o_ref):
    o_ref[...] = jnp.dot(x_ref[...], w_ref[...], preferred_element_type=F32).astype(o_ref.dtype)


def _matmul(x, w, out_dtype, tm_target, tn_target, name):
    m, k = x.shape
    n = w.shape[1]
    tm = _pick_tile(m, tm_target, 16)
    tn = _pick_tile(n, tn_target, LANES)
    return pl.pallas_call(
        _mm_kernel,
        grid=(m // tm, n // tn),
        in_specs=[pl.BlockSpec((tm, k), lambda i, j: (i, 0)),
                  pl.BlockSpec((k, tn), lambda i, j: (0, j))],
        out_specs=pl.BlockSpec((tm, tn), lambda i, j: (i, j)),
        out_shape=jax.ShapeDtypeStruct((m, n), out_dtype),
        compiler_params=_cparams(("parallel", "arbitrary")),
        name=name,
    )(x, w)


def _gla_kernel(q_ref, k_ref, v_ref, r_ref, gl_ref, wg_ref, bg_ref, ng_ref, o_ref, st_ref,
                *, n_chunks, dk):
    C, S = GLA_CHUNK, GLA_SUB

    @pl.when(pl.program_id(2) == 0)
    def _():
        st_ref[...] = jnp.zeros_like(st_ref)

    ri = lax.broadcasted_iota(jnp.int32, (C, C), 0)
    ci = lax.broadcasted_iota(jnp.int32, (C, C), 1)
    tri = (ri >= ci).astype(BF16)
    rs = lax.broadcasted_iota(jnp.int32, (S, S), 0)
    cs = lax.broadcasted_iota(jnp.int32, (S, S), 1)
    nt = (((1,), (1,)), ((), ()))
    tn = (((0,), (0,)), ((), ()))

    def chunk(c, carry):
        r0 = pl.multiple_of(c * C, C)
        q = q_ref[pl.ds(r0, C), :].astype(F32) * (dk ** -0.5)
        kk = k_ref[pl.ds(r0, C), :].astype(F32)
        vv = v_ref[pl.ds(r0, C), :]
        gl = gl_ref[pl.ds(r0, C), :].astype(BF16)
        z = jnp.dot(gl, wg_ref[...], preferred_element_type=F32) + bg_ref[...]
        la = (jnp.minimum(z, 0.0) - jnp.log(1.0 + jnp.exp(-jnp.abs(z)))) * (1.0 / GLA_GATE_TAU)
        h1 = la.astype(BF16)
        e1 = la - h1.astype(F32)
        h2 = e1.astype(BF16)
        h3 = (e1 - h2.astype(F32)).astype(BF16)
        b = (jnp.dot(tri, h1, preferred_element_type=F32)
             + jnp.dot(tri, h2, preferred_element_type=F32)
             + jnp.dot(tri, h3, preferred_element_type=F32))

        st = st_ref[...]
        qe = (q * jnp.exp(b)).astype(BF16)
        o_inter = lax.dot_general(qe, st.astype(BF16), nt, preferred_element_type=F32)

        rows = []
        for i in range(C // S):
            lo = i * S
            qs = q[lo:lo + S]
            bs = b[lo:lo + S]
            dmat = jnp.zeros((S, S), F32)
            for j in range(S):
                kj = kk[lo + j:lo + j + 1]
                bj = b[lo + j:lo + j + 1]
                p = qs * kj * jnp.exp(jnp.minimum(bs - bj, 0.0))
                a = jnp.sum(p, axis=-1, keepdims=True)
                dmat = jnp.where(cs == j, a, dmat)
            dmat = jnp.where(rs >= cs, dmat, 0.0)
            o_i = jnp.dot(dmat.astype(BF16), vv[lo:lo + S], preferred_element_type=F32)
            if i > 0:
                bref = b[lo - 1:lo]
                qt = (qs * jnp.exp(bs - bref)).astype(BF16)
                kt = (kk[:lo] * jnp.exp(bref - b[:lo])).astype(BF16)
                off = lax.dot_general(qt, kt, nt, preferred_element_type=F32)
                o_i = o_i + jnp.dot(off.astype(BF16), vv[:lo], preferred_element_type=F32)
            rows.append(o_i)
        o = o_inter + jnp.concatenate(rows, axis=0)

        bl = b[C - 1:C]
        khat = (kk * jnp.exp(bl - b)).astype(BF16)
        upd = lax.dot_general(vv, khat, tn, preferred_element_type=F32)
        st_ref[...] = st * jnp.exp(bl) + upd

        ms = jnp.mean(o * o, axis=-1, keepdims=True)
        r = r_ref[pl.ds(r0, C), :].astype(F32)
        y = o * lax.rsqrt(ms + RMS_EPS) * ng_ref[...] * (r * jax.nn.sigmoid(r))
        o_ref[pl.ds(r0, C), :] = y.astype(o_ref.dtype)
        return carry

    lax.fori_loop(0, n_chunks, chunk, 0)


def _gla(qkvr, glow, wg, bg, ng, *, batch, seq_rows, d_model):
    H = GLA_HEADS
    dk = d_model // 2 // H
    dv = d_model // H
    n_rows = batch * seq_rows
    rblk = _pick_tile(seq_rows, 640, GLA_CHUNK)
    steps = seq_rows // rblk
    kq, kv = (H * dk) // dk, (2 * H * dk) // dv

    def rowmap(off):
        return lambda b, h, i: (b * steps + i, off + h)

    return pl.pallas_call(
        functools.partial(_gla_kernel, n_chunks=rblk // GLA_CHUNK, dk=dk),
        grid=(batch, H, steps),
        in_specs=[pl.BlockSpec((rblk, dk), rowmap(0)),
                  pl.BlockSpec((rblk, dk), rowmap(kq)),
                  pl.BlockSpec((rblk, dv), rowmap(kv)),
                  pl.BlockSpec((rblk, dv), rowmap(kv + H)),
                  pl.BlockSpec((rblk, GATE_PAD), lambda b, h, i: (b * steps + i, 0)),
                  pl.BlockSpec((GATE_PAD, dk), lambda b, h, i: (0, h)),
                  pl.BlockSpec((1, dk), lambda b, h, i: (0, h)),
                  pl.BlockSpec((1, dv), lambda b, h, i: (0, h))],
        out_specs=pl.BlockSpec((rblk, dv), lambda b, h, i: (b * steps + i, h)),
        out_shape=jax.ShapeDtypeStruct((n_rows, H * dv), BF16),
        scratch_shapes=[pltpu.VMEM((dv, dk), F32)],
        compiler_params=_cparams(("parallel", "parallel", "arbitrary")),
        name="gla_chunks",
    )(qkvr, qkvr, qkvr, qkvr, glow, wg, bg, ng)


def _conv_kernel(x_ref, wb_ref, wc_ref, wh_ref, cw_ref, o_ref, ubuf_ref, carry_ref,
                 *, seq_rows, pad_rows, tm):
    i = pl.program_id(0)
    j = pl.program_id(1)
    x = x_ref[...]
    bg = jnp.dot(x, wb_ref[...], preferred_element_type=F32)
    cg = jnp.dot(x, wc_ref[...], preferred_element_type=F32)
    hh = jnp.dot(x, wh_ref[...], preferred_element_type=F32)
    row = i * tm + lax.broadcasted_iota(jnp.int32, (tm, 1), 0)
    u = jnp.where(row % seq_rows >= pad_rows, cg * hh, 0.0)

    @pl.when(i == 0)
    def _():
        carry_ref[j] = jnp.zeros(carry_ref.shape[1:], F32)

    ubuf_ref[pl.ds(0, SUBLANES), :] = carry_ref[j]
    ubuf_ref[pl.ds(SUBLANES, tm), :] = u
    carry_ref[j] = u[tm - SUBLANES:]
    cw = cw_ref[...]
    conv = u * cw[CONV_WIDTH - 1:CONV_WIDTH]
    for s in range(1, CONV_WIDTH):
        conv = conv + ubuf_ref[pl.ds(SUBLANES - s, tm), :] * cw[CONV_WIDTH - 1 - s:CONV_WIDTH - s]
    o_ref[...] = (bg * conv).astype(o_ref.dtype)


def _conv_mix(xb, w_in, conv_w, *, seq_rows, pad_rows):
    n_rows, d = xb.shape
    tm = _pick_tile(n_rows, 1024, SUBLANES)
    tn = _pick_tile(d, 512, LANES)
    nj = d // tn
    return pl.pallas_call(
        functools.partial(_conv_kernel, seq_rows=seq_rows, pad_rows=pad_rows, tm=tm),
        grid=(n_rows // tm, nj),
        in_specs=[pl.BlockSpec((tm, d), lambda i, j: (i, 0)),
                  pl.BlockSpec((d, tn), lambda i, j: (0, j)),
                  pl.BlockSpec((d, tn), lambda i, j: (0, nj + j)),
                  pl.BlockSpec((d, tn), lambda i, j: (0, 2 * nj + j)),
                  pl.BlockSpec((CONV_WIDTH, tn), lambda i, j: (0, j))],
        out_specs=pl.BlockSpec((tm, tn), lambda i, j: (i, j)),
        out_shape=jax.ShapeDtypeStruct((n_rows, d), BF16),
        scratch_shapes=[pltpu.VMEM((tm + SUBLANES, tn), F32),
                        pltpu.VMEM((nj, SUBLANES, tn), F32)],
        compiler_params=_cparams(("arbitrary", "arbitrary")),
        name="conv_mix",
    )(xb, w_in, w_in, w_in, conv_w)


def _layer_norm(y, g, b):
    mu = jnp.mean(y, axis=-1, keepdims=True)
    yc = y - mu
    var = jnp.mean(yc * yc, axis=-1, keepdims=True)
    return yc * lax.rsqrt(var + LN_EPS) * g + b


def _proj_ln_kernel(a_ref, w_ref, h_ref, g_ref, b_ref, rw_ref, o_ref, lg_ref):
    y = jnp.dot(a_ref[...], w_ref[...], preferred_element_type=F32) + DN_ALPHA * h_ref[...]
    h = _layer_norm(y, g_ref[...], b_ref[...])
    o_ref[...] = h
    lg_ref[...] = lax.dot_general(rw_ref[...], h.astype(BF16), (((1,), (1,)), ((), ())),
                                  preferred_element_type=F32)


def _proj_ln_router(a, w, h, g, b, rw_t):
    n_rows, kin = a.shape
    d = w.shape[1]
    e = rw_t.shape[0]
    tm = _pick_tile(n_rows, 256, LANES)
    return pl.pallas_call(
        _proj_ln_kernel,
        grid=(n_rows // tm,),
        in_specs=[pl.BlockSpec((tm, kin), lambda i: (i, 0)),
                  pl.BlockSpec((kin, d), lambda i: (0, 0)),
                  pl.BlockSpec((tm, d), lambda i: (i, 0)),
                  pl.BlockSpec((1, d), lambda i: (0, 0)),
                  pl.BlockSpec((1, d), lambda i: (0, 0)),
                  pl.BlockSpec((e, d), lambda i: (0, 0))],
        out_specs=[pl.BlockSpec((tm, d), lambda i: (i, 0)),
                   pl.BlockSpec((e, tm), lambda i: (0, i))],
        out_shape=[jax.ShapeDtypeStruct((n_rows, d), F32),
                   jax.ShapeDtypeStruct((e, n_rows), F32)],
        compiler_params=_cparams(("parallel",)),
        name="proj_ln_router",
    )(a, w, h, g, b, rw_t)


def _beats(cand, cand_idx, ref, ref_idx):
    return (cand > ref) | ((cand == ref) & (cand_idx < ref_idx))


def _route_kernel(lg_ref, bias_ref, w_ref, pos_ref, cnt_ref, run_ref, *, t):
    E, G = N_EXPERTS, N_GROUPS
    gs = E // G

    @pl.when(pl.program_id(0) == 0)
    def _():
        run_ref[...] = jnp.zeros_like(run_ref)

    s = jax.nn.sigmoid(lg_ref[...])
    c = s + bias_ref[...][:, :1]
    sub = lax.broadcasted_iota(jnp.int32, (gs, t), 0)

    grp_rows = []
    for g in range(G):
        cg = c[g * gs:(g + 1) * gs]
        rank = jnp.zeros((gs, t), jnp.int32)
        for m in range(gs):
            rank = rank + _beats(cg[m:m + 1], m, cg, sub).astype(jnp.int32)
        grp_rows.append(jnp.sum(jnp.where(rank < 2, cg, 0.0), axis=0, keepdims=True))
    gidx = lax.broadcasted_iota(jnp.int32, (G, t), 0)
    gscore = jnp.zeros((G, t), F32)
    for g in range(G):
        gscore = jnp.where(gidx == g, grp_rows[g], gscore)
    grank = jnp.zeros((G, t), jnp.int32)
    for m in range(G):
        grank = grank + _beats(gscore[m:m + 1], m, gscore, gidx).astype(jnp.int32)
    gsel = grank < TOPK_GROUPS

    masked = jnp.concatenate(
        [jnp.where(gsel[g:g + 1], c[g * gs:(g + 1) * gs], -jnp.inf) for g in range(G)], axis=0)
    eidx = lax.broadcasted_iota(jnp.int32, (E, t), 0)
    rank = jnp.zeros((E, t), jnp.int32)
    for m in range(E):
        rank = rank + _beats(masked[m:m + 1], m, masked, eidx).astype(jnp.int32)
    sel = rank < TOP_K
    gate = jnp.where(sel, s, 0.0)
    w_ref[...] = gate / jnp.sum(gate, axis=0, keepdims=True) * ROUTED_SCALE

    li = lax.broadcasted_iota(jnp.int32, (t, t), 0)
    lj = lax.broadcasted_iota(jnp.int32, (t, t), 1)
    upper = (li <= lj).astype(BF16)
    self_ = sel.astype(F32)
    incl = jnp.dot(sel.astype(BF16), upper, preferred_element_type=F32)
    run = run_ref[...][:, :1]
    pos_ref[...] = jnp.where(sel, run + incl - self_, -1.0)
    run_new = run + jnp.sum(self_, axis=1, keepdims=True)
    run_ref[...] = jnp.broadcast_to(run_new, run_ref.shape)
    cnt_ref[...] = jnp.broadcast_to(run_new, cnt_ref.shape)


def _route(logits_t, bias_col):
    e, n_rows = logits_t.shape
    t = _pick_tile(n_rows, 512, LANES)
    return pl.pallas_call(
        functools.partial(_route_kernel, t=t),
        grid=(n_rows // t,),
        in_specs=[pl.BlockSpec((e, t), lambda i: (0, i)),
                  pl.BlockSpec((e, LANES), lambda i: (0, 0))],
        out_specs=[pl.BlockSpec((e, t), lambda i: (0, i)),
                   pl.BlockSpec((e, t), lambda i: (0, i)),
                   pl.BlockSpec((e, LANES), lambda i: (0, 0))],
        out_shape=[jax.ShapeDtypeStruct((e, n_rows), F32),
                   jax.ShapeDtypeStruct((e, n_rows), F32),
                   jax.ShapeDtypeStruct((e, LANES), F32)],
        scratch_shapes=[pltpu.VMEM((e, LANES), F32)],
        compiler_params=_cparams(("arbitrary",)),
        name="route",
    )(logits_t, bias_col)


def _compact_kernel(w_ref, pos_ref, pst_ref, slot_ref, w8_ref, *, t):
    E = N_EXPERTS
    pos = pos_ref[...]
    sel = pos >= 0.0
    ri = lax.broadcasted_iota(jnp.int32, (E, E), 0)
    ci = lax.broadcasted_iota(jnp.int32, (E, E), 1)
    below = (ci < ri).astype(BF16)
    order = jnp.dot(below, sel.astype(BF16), preferred_element_type=F32)
    slot = pst_ref[...][:, :1] + pos
    wd = w_ref[...]
    kidx = lax.broadcasted_iota(jnp.int32, (TOP_K, t), 0)
    slots = jnp.zeros((TOP_K, t), F32)
    w8 = jnp.zeros((TOP_K, t), F32)
    for k in range(TOP_K):
        m = sel & (order == float(k))
        slots = jnp.where(kidx == k, jnp.sum(jnp.where(m, slot, 0.0), axis=0, keepdims=True), slots)
        w8 = jnp.where(kidx == k, jnp.sum(jnp.where(m, wd, 0.0), axis=0, keepdims=True), w8)
    slot_ref[...] = slots.astype(jnp.int32)
    w8_ref[...] = w8


def _compact(w_dense, pos_dense, pstart_col):
    e, n_rows = w_dense.shape
    t = _pick_tile(n_rows, 512, LANES)
    return pl.pallas_call(
        functools.partial(_compact_kernel, t=t),
        grid=(n_rows // t,),
        in_specs=[pl.BlockSpec((e, t), lambda i: (0, i)),
                  pl.BlockSpec((e, t), lambda i: (0, i)),
                  pl.BlockSpec((e, LANES), lambda i: (0, 0))],
        out_specs=[pl.BlockSpec((TOP_K, t), lambda i: (0, i)),
                   pl.BlockSpec((TOP_K, t), lambda i: (0, i))],
        out_shape=[jax.ShapeDtypeStruct((TOP_K, n_rows), jnp.int32),
                   jax.ShapeDtypeStruct((TOP_K, n_rows), F32)],
        compiler_params=_cparams(("parallel",)),
        name="compact",
    )(w_dense, pos_dense, pstart_col)


def _pack_bf16_pairs(x):
    half = x.shape[1] // 2
    lo = lax.bitcast_convert_type(x[:, :half].astype(BF16).astype(F32), jnp.uint32)
    hi = lax.bitcast_convert_type(x[:, half:].astype(BF16).astype(F32), jnp.uint32)
    return (lo >> 16) | hi


def _unpack_bf16_pairs(w):
    lo = lax.bitcast_convert_type(w << 16, F32)
    hi = lax.bitcast_convert_type(w & jnp.uint32(0xFFFF0000), F32)
    return lo, hi


def _dispatch_kernel(pad_lo_ref, pad_hi_ref, slot_ref, h_ref, xs_ref, hp_ref, zero_ref, sem, zsem,
                     *, t):
    hp_ref[...] = _pack_bf16_pairs(h_ref[...])

    def issue(tok, carry):
        src = hp_ref.at[pl.ds(tok, 1), :]
        for k in range(TOP_K):
            pltpu.make_async_copy(src, xs_ref.at[pl.ds(slot_ref[k * t + tok], 1), :], sem).start()
        return carry

    lax.fori_loop(0, t, issue, 0, unroll=2)

    @pl.when(pl.program_id(0) == 0)
    def _():
        zero_ref[...] = jnp.zeros_like(zero_ref)

        def zero_copy(p):
            return pltpu.make_async_copy(zero_ref, xs_ref.at[pl.ds(p, 1), :], zsem)

        def per_expert(e, carry):
            lo, hi = pad_lo_ref[e], pad_hi_ref[e]

            def zissue(p, c):
                zero_copy(p).start()
                return c

            def zwait(p, c):
                zero_copy(p).wait()
                return c

            lax.fori_loop(lo, hi, zissue, 0)
            lax.fori_loop(lo, hi, zwait, 0)
            return carry

        lax.fori_loop(0, N_EXPERTS, per_expert, 0)

    for k in range(TOP_K):
        pltpu.make_async_copy(hp_ref, xs_ref.at[pl.ds(0, t), :], sem).wait()


def _dispatch(h, slots_tiled, pad_lo, pad_hi, n_slots, t):
    n_rows, d = h.shape
    n_tiles = n_rows // t
    return pl.pallas_call(
        functools.partial(_dispatch_kernel, t=t),
        grid_spec=pltpu.PrefetchScalarGridSpec(
            num_scalar_prefetch=2,
            grid=(n_tiles,),
            in_specs=[pl.BlockSpec((None, None, TOP_K * t), lambda i, lo, hi: (i, 0, 0),
                                   memory_space=pltpu.SMEM),
                      pl.BlockSpec((t, d), lambda i, lo, hi: (i, 0))],
            out_specs=pl.BlockSpec(memory_space=pl.ANY),
            scratch_shapes=[pltpu.VMEM((t, d // 2), jnp.uint32),
                            pltpu.VMEM((1, d // 2), jnp.uint32),
                            pltpu.SemaphoreType.DMA(()),
                            pltpu.SemaphoreType.DMA(())]),
        out_shape=jax.ShapeDtypeStruct((n_slots, d // 2), jnp.uint32),
        compiler_params=_cparams(("arbitrary",)),
        name="dispatch",
    )(pad_lo, pad_hi, slots_tiled, h)


def _swiglu(x, w1, w3, w2):
    a = jnp.dot(x, w1, preferred_element_type=F32)
    g = jnp.dot(x, w3, preferred_element_type=F32)
    hmid = (a * jax.nn.sigmoid(a) * g).astype(BF16)
    return jnp.dot(hmid, w2, preferred_element_type=F32)


def _expert_kernel(be_ref, nu_ref, x_ref, w1_ref, w3_ref, w2_ref, y_ref, w1b_ref, w3b_ref, w2b_ref):
    b = pl.program_id(0)

    @pl.when(b < nu_ref[0])
    def _():
        @pl.when((b == 0) | (be_ref[b] != be_ref[jnp.maximum(b - 1, 0)]))
        def _():
            w1b_ref[...] = w1_ref[...].astype(BF16)
            w3b_ref[...] = w3_ref[...].astype(BF16)
            w2b_ref[...] = w2_ref[...].astype(BF16)

        lo, hi = _unpack_bf16_pairs(x_ref[...])
        x = jnp.concatenate([lo.astype(BF16), hi.astype(BF16)], axis=1)
        y_ref[...] = _pack_bf16_pairs(_swiglu(x, w1b_ref[...], w3b_ref[...], w2b_ref[...]))


def _experts(xs, w1, w3, w2, layer, block_e, n_used):
    n_slots, dh = xs.shape
    d, f = w1.shape[2], w1.shape[3]
    nb = n_slots // MOE_BLOCK

    def blk(b, be, nu):
        return jnp.minimum(b, nu[0] - 1)

    def wmap(b, be, nu):
        return (layer, be[blk(b, be, nu)], 0, 0)

    return pl.pallas_call(
        _expert_kernel,
        grid_spec=pltpu.PrefetchScalarGridSpec(
            num_scalar_prefetch=2,
            grid=(nb,),
            in_specs=[pl.BlockSpec((MOE_BLOCK, dh), lambda b, be, nu: (blk(b, be, nu), 0)),
                      pl.BlockSpec((None, None, d, f), wmap),
                      pl.BlockSpec((None, None, d, f), wmap),
                      pl.BlockSpec((None, None, f, d), wmap)],
            out_specs=pl.BlockSpec((MOE_BLOCK, dh), lambda b, be, nu: (blk(b, be, nu), 0)),
            scratch_shapes=[pltpu.VMEM((d, f), BF16), pltpu.VMEM((d, f), BF16),
                            pltpu.VMEM((f, d), BF16)]),
        out_shape=jax.ShapeDtypeStruct((n_slots, dh), jnp.uint32),
        compiler_params=_cparams(("arbitrary",)),
        name="experts",
    )(block_e, n_used, xs, w1, w3, w2)


def _combine_kernel(slot_ref, nslot_ref, h_ref, w8_ref, ws1_ref, ws3_ref, ws2_ref, g_ref, b_ref,
                    ys_ref, o_ref, ob_ref, ybuf_ref, sem, *, t):
    i = pl.program_id(0)
    cur = i % 2

    def issue_tile(s_ref, buf):
        def issue(tok, carry):
            for k in range(TOP_K):
                pltpu.make_async_copy(ys_ref.at[pl.ds(s_ref[k * t + tok], 1), :],
                                      ybuf_ref.at[buf, k, pl.ds(tok, 1), :], sem.at[buf]).start()
            return carry

        lax.fori_loop(0, t, issue, 0, unroll=2)

    @pl.when(i == 0)
    def _():
        issue_tile(slot_ref, 0)

    @pl.when(i + 1 < pl.num_programs(0))
    def _():
        issue_tile(nslot_ref, 1 - cur)

    h = h_ref[...]
    shared = _swiglu(h.astype(BF16), ws1_ref[...], ws3_ref[...], ws2_ref[...])

    for k in range(TOP_K):
        pltpu.make_async_copy(ys_ref.at[pl.ds(0, t), :], ybuf_ref.at[cur, k], sem.at[cur]).wait()

    w8 = w8_ref[...]
    acc_lo, acc_hi = None, None
    for k in range(TOP_K):
        lo, hi = _unpack_bf16_pairs(ybuf_ref[cur, k])
        wk = w8[:, k:k + 1]
        acc_lo = lo * wk if acc_lo is None else acc_lo + lo * wk
        acc_hi = hi * wk if acc_hi is None else acc_hi + hi * wk
    acc = jnp.concatenate([acc_lo, acc_hi], axis=1)
    y = DN_ALPHA * h + (acc + shared)
    out = _layer_norm(y, g_ref[...], b_ref[...])
    o_ref[...] = out
    ob_ref[...] = out.astype(BF16)


def _combine(h, ys, slots_tiled, w8_t, ws1, ws3, ws2, g, b, t, out_rows, out_map):
    n_rows, d = h.shape
    f = ws1.shape[1]
    n_tiles = n_rows // t
    return pl.pallas_call(
        functools.partial(_combine_kernel, t=t),
        grid=(n_tiles,),
        in_specs=[pl.BlockSpec((None, None, TOP_K * t), lambda i: (i, 0, 0),
                               memory_space=pltpu.SMEM),
                  pl.BlockSpec((None, None, TOP_K * t),
                               lambda i: (jnp.minimum(i + 1, n_tiles - 1), 0, 0),
                               memory_space=pltpu.SMEM),
                  pl.BlockSpec((t, d), lambda i: (i, 0)),
                  pl.BlockSpec((t, TOP_K), lambda i: (i, 0)),
                  pl.BlockSpec((d, f), lambda i: (0, 0)),
                  pl.BlockSpec((d, f), lambda i: (0, 0)),
                  pl.BlockSpec((f, d), lambda i: (0, 0)),
                  pl.BlockSpec((1, d), lambda i: (0, 0)),
                  pl.BlockSpec((1, d), lambda i: (0, 0)),
                  pl.BlockSpec(memory_space=pl.ANY)],
        out_specs=[pl.BlockSpec((t, d), lambda i: (out_map(i), 0)),
                   pl.BlockSpec((t, d), lambda i: (i, 0))],
        out_shape=[jax.ShapeDtypeStruct((out_rows, d), F32),
                   jax.ShapeDtypeStruct((n_rows, d), BF16)],
        scratch_shapes=[pltpu.VMEM((2, TOP_K, t, d // 2), jnp.uint32),
                        pltpu.SemaphoreType.DMA((2,))],
        compiler_params=_cparams(("arbitrary",)),
        name="combine_ln",
    )(slots_tiled, slots_tiled, h, w8_t, ws1, ws3, ws2, g, b, ys)


def _moe_ln(h, logits_t, router_bias, w1, w3, w2, layer, ws1, ws3, ws2, g, b, drop_rows, seq_rows):
    n_rows, d = h.shape
    E = N_EXPERTS
    bias_col = jnp.broadcast_to(router_bias.astype(F32)[:, None], (E, LANES))
    w_dense, pos_dense, cnt = _route(logits_t, bias_col)

    counts = cnt[:, 0].astype(jnp.int32)
    pcounts = (counts + MOE_BLOCK - 1) // MOE_BLOCK * MOE_BLOCK
    pends = jnp.cumsum(pcounts)
    pstarts = pends - pcounts
    n_blocks = -(-(n_rows * TOP_K) // MOE_BLOCK) + E
    n_slots = n_blocks * MOE_BLOCK
    block_lo = jnp.arange(n_blocks, dtype=jnp.int32) * MOE_BLOCK
    block_e = jnp.minimum(jnp.sum(pends[None, :] <= block_lo[:, None], axis=1), E - 1).astype(jnp.int32)
    n_used = (pends[-1:] // MOE_BLOCK).astype(jnp.int32)
    pstart_col = jnp.broadcast_to(pstarts.astype(F32)[:, None], (E, LANES))

    slots, w8 = _compact(w_dense, pos_dense, pstart_col)
    t = _pick_tile(n_rows, 128, LANES)
    n_tiles = n_rows // t
    slots_tiled = slots.reshape(TOP_K, n_tiles, t).transpose(1, 0, 2).reshape(n_tiles, 1, TOP_K * t)
    xs = _dispatch(h, slots_tiled, (pstarts + counts).astype(jnp.int32), pends.astype(jnp.int32),
                   n_slots, t)
    ys = _experts(xs, w1, w3, w2, layer, block_e, n_used)
    if drop_rows == t:
        per_seq = seq_rows // t
        out_rows = n_rows - (n_rows // seq_rows) * t

        def out_map(i):
            return (i // per_seq) * (per_seq - 1) + jnp.maximum(i % per_seq - 1, 0)
    else:
        out_rows, out_map = n_rows, (lambda i: i)
    return _combine(h, ys, slots_tiled, w8.T, ws1, ws3, ws2, g, b, t, out_rows, out_map)


def kernel(x, meta_tokens, gla_w_in, gla_w_gate_up, gla_b_gate, gla_norm_g, gla_w_out,
           conv_w_in, conv_w, conv_w_out, ln1_g, ln1_b, router_w, router_bias,
           exp_w1, exp_w3, exp_w2, shared_w1, shared_w3, shared_w2, ln2_g, ln2_b):
    batch, seq, d = x.shape
    H = GLA_HEADS
    dk = d // 2 // H
    n_qkvr = 2 * H * dk + 2 * d
    rank = gla_w_in.shape[2] - n_qkvr
    seq_rows = -(-(N_META + seq) // ROW_ALIGN) * ROW_ALIGN
    pad_rows = seq_rows - N_META - seq
    n_rows = batch * seq_rows

    meta = jnp.broadcast_to(meta_tokens.astype(x.dtype)[None], (batch, N_META, d))
    h = jnp.concatenate([jnp.zeros((batch, pad_rows, d), x.dtype), meta, x], axis=1)
    h = h.reshape(n_rows, d)
    hb = h.astype(BF16)

    def row(v):
        return v.astype(F32)[None, :]

    for i in range(DEPTH):
        jm = i // 2
        if i % 2 == 0:
            w_in = gla_w_in[jm]
            w_qkvr = w_in[:, :n_qkvr].astype(BF16)
            w_gate = jnp.pad(w_in[:, n_qkvr:], ((0, 0), (0, GATE_PAD - rank))).astype(BF16)
            wg_up = jnp.pad(gla_w_gate_up[jm], ((0, GATE_PAD - rank), (0, 0))).astype(BF16)
            qkvr = _matmul(hb, w_qkvr, BF16, 1664, 768, "gla_in_proj")
            glow = _matmul(hb, w_gate, F32, 1664, LANES, "gla_gate_proj")
            mix = _gla(qkvr, glow, wg_up, row(gla_b_gate[jm]), row(gla_norm_g[jm]),
                       batch=batch, seq_rows=seq_rows, d_model=d)
            w_out = gla_w_out[jm].astype(BF16)
        else:
            mix = _conv_mix(hb, conv_w_in[jm].astype(BF16), conv_w[jm].astype(F32),
                            seq_rows=seq_rows, pad_rows=pad_rows)
            w_out = conv_w_out[jm].astype(BF16)
        h, logits_t = _proj_ln_router(mix, w_out, h, row(ln1_g[i]), row(ln1_b[i]),
                                      router_w[i].T.astype(BF16))
        drop = pad_rows + N_META if i == DEPTH - 1 else 0
        h, hb = _moe_ln(h, logits_t, router_bias[i], exp_w1, exp_w3, exp_w2, i,
                        shared_w1[i].astype(BF16), shared_w3[i].astype(BF16),
                        shared_w2[i].astype(BF16), row(ln2_g[i]), row(ln2_b[i]), drop, seq_rows)
    if h.shape[0] == batch * seq:
        return h.reshape(batch, seq, d)
    return h.reshape(batch, seq_rows, d)[:, pad_rows + N_META:]
```

```python
import functools

import jax
import jax.numpy as jnp
from jax import lax
from jax.experimental import pallas as pl
from jax.experimental.pallas import tpu as pltpu

N_META = 16
GLA_HEADS = 4
GLA_GATE_TAU = 16.0
CONV_WIDTH = 3
N_EXPERTS = 64
TOP_K = 8
N_GROUPS = 8
TOPK_GROUPS = 4
ROUTED_SCALE = 2.5
LN_EPS = 1e-5
RMS_EPS = 1e-6
DEPTH = 2
DN_ALPHA = (2 * DEPTH) ** 0.25

LANES = 128
SUBLANES = 8
ROW_ALIGN = 128
GLA_CHUNK = 64
GLA_SUB = 16
GLA_HEADS_PER_STEP = 4
GATE_PAD = LANES
MOE_BLOCK = 512
VMEM_LIMIT = 56 * 1024 * 1024

F32 = jnp.float32
BF16 = jnp.bfloat16


def _pick_tile(n, target, mult):
    best = None
    for t in range(mult, min(n, target) + 1, mult):
        if n % t == 0:
            best = t
    assert best is not None, (n, target, mult)
    return best


def _cparams(sem):
    return pltpu.CompilerParams(dimension_semantics=sem, vmem_limit_bytes=VMEM_LIMIT)


def _mm_kernel(x_ref, w_ref, o_ref):
    o_ref[...] = jnp.dot(x_ref[...], w_ref[...], preferred_element_type=F32).astype(o_ref.dtype)


def _matmul(x, w, out_dtype, tm_target, tn_target, name):
    m, k = x.shape
    n = w.shape[1]
    tm = _pick_tile(m, tm_target, 16)
    tn = _pick_tile(n, tn_target, LANES)
    return pl.pallas_call(
        _mm_kernel,
        grid=(m // tm, n // tn),
        in_specs=[pl.BlockSpec((tm, k), lambda i, j: (i, 0)),
                  pl.BlockSpec((k, tn), lambda i, j: (0, j))],
        out_specs=pl.BlockSpec((tm, tn), lambda i, j: (i, j)),
        out_shape=jax.ShapeDtypeStruct((m, n), out_dtype),
        compiler_params=_cparams(("parallel", "arbitrary")),
        name=name,
    )(x, w)


def _gla_kernel(q_ref, k_ref, v_ref, r_ref, gl_ref, wg_ref, bg_ref, ng_ref, o_ref, st_ref,
                *, n_chunks, dk, dv, heads):
    C, S = GLA_CHUNK, GLA_SUB

    @pl.when(pl.program_id(2) == 0)
    def _():
        st_ref[...] = jnp.zeros_like(st_ref)

    ri = lax.broadcasted_iota(jnp.int32, (C, C), 0)
    ci = lax.broadcasted_iota(jnp.int32, (C, C), 1)
    tri = (ri >= ci).astype(BF16)
    rs = lax.broadcasted_iota(jnp.int32, (S, S), 0)
    cs = lax.broadcasted_iota(jnp.int32, (S, S), 1)
    nt = (((1,), (1,)), ((), ()))
    tn = (((0,), (0,)), ((), ()))
    half = S // 2

    def chunk(c, carry):
        r0 = pl.multiple_of(c * C, C)
        rows_c = pl.ds(r0, C)
        hs = range(heads)
        kcol = [pl.ds(hd * dk, dk) for hd in hs]
        vcol = [pl.ds(hd * dv, dv) for hd in hs]
        gl = gl_ref[rows_c, :].astype(BF16)
        q = [q_ref[rows_c, kcol[hd]].astype(F32) * (dk ** -0.5) for hd in hs]
        kk = [k_ref[rows_c, kcol[hd]].astype(F32) for hd in hs]
        vv = [v_ref[rows_c, vcol[hd]] for hd in hs]
        st = [st_ref[hd] for hd in hs]
        z = [jnp.dot(gl, wg_ref[:, kcol[hd]], preferred_element_type=F32) + bg_ref[:, kcol[hd]]
             for hd in hs]
        b = []
        for hd in hs:
            la = ((jnp.minimum(z[hd], 0.0) - jnp.log(1.0 + jnp.exp(-jnp.abs(z[hd]))))
                  * (1.0 / GLA_GATE_TAU))
            h1 = la.astype(BF16)
            e1 = la - h1.astype(F32)
            h2 = e1.astype(BF16)
            h3 = (e1 - h2.astype(F32)).astype(BF16)
            b3 = jnp.dot(tri, jnp.concatenate([h1, h2, h3], axis=1), preferred_element_type=F32)
            b.append(b3[:, :dk] + b3[:, dk:2 * dk] + b3[:, 2 * dk:])

        o_inter, st_new, off = [], [], []
        for hd in hs:
            qe = (q[hd] * jnp.exp(b[hd])).astype(BF16)
            o_inter.append(lax.dot_general(qe, st[hd].astype(BF16), nt, preferred_element_type=F32))
            bl = b[hd][C - 1:C]
            khat = (kk[hd] * jnp.exp(bl - b[hd])).astype(BF16)
            upd = lax.dot_general(vv[hd], khat, tn, preferred_element_type=F32)
            st_new.append(st[hd] * jnp.exp(bl) + upd)
            offs = []
            for i in range(1, C // S):
                lo = i * S
                bref = b[hd][lo - 1:lo]
                qt = (q[hd][lo:lo + S] * jnp.exp(b[hd][lo:lo + S] - bref)).astype(BF16)
                kt = (kk[hd][:lo] * jnp.exp(bref - b[hd][:lo])).astype(BF16)
                offs.append(lax.dot_general(qt, kt, nt, preferred_element_type=F32))
            off.append(offs)

        dmats = []
        for hd in hs:
            blocks = []
            for i in range(C // S):
                lo = i * S
                qs = q[hd][lo:lo + S]
                bs = b[hd][lo:lo + S]
                dmat = jnp.zeros((S, S), F32)
                for j in range(S):
                    kj = kk[hd][lo + j:lo + j + 1]
                    bj = b[hd][lo + j:lo + j + 1]
                    if j < half:
                        p = qs * kj * jnp.exp(jnp.minimum(bs - bj, 0.0))
                        a = jnp.sum(p, axis=-1, keepdims=True)
                    else:
                        p = qs[half:] * kj * jnp.exp(jnp.minimum(bs[half:] - bj, 0.0))
                        a = jnp.concatenate([jnp.zeros((half, 1), F32),
                                             jnp.sum(p, axis=-1, keepdims=True)], axis=0)
                    dmat = jnp.where(cs == j, a, dmat)
                blocks.append(jnp.where(rs >= cs, dmat, 0.0).astype(BF16))
            dmats.append(blocks)

        for hd in hs:
            rows = []
            for i in range(C // S):
                lo = i * S
                o_i = jnp.dot(dmats[hd][i], vv[hd][lo:lo + S], preferred_element_type=F32)
                if i > 0:
                    o_i = o_i + jnp.dot(off[hd][i - 1].astype(BF16), vv[hd][:lo],
                                        preferred_element_type=F32)
                rows.append(o_i)
            o = o_inter[hd] + jnp.concatenate(rows, axis=0)
            ms = jnp.mean(o * o, axis=-1, keepdims=True)
            r = r_ref[rows_c, vcol[hd]].astype(F32)
            y = o * lax.rsqrt(ms + RMS_EPS) * ng_ref[:, vcol[hd]] * (r * jax.nn.sigmoid(r))
            st_ref[hd] = st_new[hd]
            o_ref[rows_c, vcol[hd]] = y.astype(o_ref.dtype)
        return carry

    lax.fori_loop(0, n_chunks, chunk, 0)


def _gla(qkvr, glow, wg, bg, ng, *, batch, seq_rows, d_model):
    H, hps = GLA_HEADS, GLA_HEADS_PER_STEP
    dk = d_model // 2 // H
    dv = d_model // H
    n_rows = batch * seq_rows
    rblk = _pick_tile(seq_rows, 640, GLA_CHUNK)
    steps = seq_rows // rblk
    groups = H // hps
    off_k, off_v, off_r = groups, (2 * H * dk) // (hps * dv), (2 * H * dk) // (hps * dv) + groups

    def rowmap(off):
        return lambda b, h, i: (b * steps + i, off + h)

    return pl.pallas_call(
        functools.partial(_gla_kernel, n_chunks=rblk // GLA_CHUNK, dk=dk, dv=dv, heads=hps),
        grid=(batch, groups, steps),
        in_specs=[pl.BlockSpec((rblk, hps * dk), rowmap(0)),
                  pl.BlockSpec((rblk, hps * dk), rowmap(off_k)),
                  pl.BlockSpec((rblk, hps * dv), rowmap(off_v)),
                  pl.BlockSpec((rblk, hps * dv), rowmap(off_r)),
                  pl.BlockSpec((rblk, GATE_PAD), lambda b, h, i: (b * steps + i, 0)),
                  pl.BlockSpec((GATE_PAD, hps * dk), lambda b, h, i: (0, h)),
                  pl.BlockSpec((1, hps * dk), lambda b, h, i: (0, h)),
                  pl.BlockSpec((1, hps * dv), lambda b, h, i: (0, h))],
        out_specs=pl.BlockSpec((rblk, hps * dv), lambda b, h, i: (b * steps + i, h)),
        out_shape=jax.ShapeDtypeStruct((n_rows, H * dv), BF16),
        scratch_shapes=[pltpu.VMEM((hps, dv, dk), F32)],
        compiler_params=_cparams(("parallel", "parallel", "arbitrary")),
        name="gla_chunks",
    )(qkvr, qkvr, qkvr, qkvr, glow, wg, bg, ng)


def _conv_kernel(x_ref, wb_ref, wc_ref, wh_ref, cw_ref, o_ref, ubuf_ref, carry_ref,
                 *, seq_rows, pad_rows, tm):
    i = pl.program_id(0)
    j = pl.program_id(1)
    x = x_ref[...]
    bg = jnp.dot(x, wb_ref[...], preferred_element_type=F32)
    cg = jnp.dot(x, wc_ref[...], preferred_element_type=F32)
    hh = jnp.dot(x, wh_ref[...], preferred_element_type=F32)
    row = i * tm + lax.broadcasted_iota(jnp.int32, (tm, 1), 0)
    u = jnp.where(row % seq_rows >= pad_rows, cg * hh, 0.0)

    @pl.when(i == 0)
    def _():
        carry_ref[j] = jnp.zeros(carry_ref.shape[1:], F32)

    ubuf_ref[pl.ds(0, SUBLANES), :] = carry_ref[j]
    ubuf_ref[pl.ds(SUBLANES, tm), :] = u
    carry_ref[j] = u[tm - SUBLANES:]
    cw = cw_ref[...]
    conv = u * cw[CONV_WIDTH - 1:CONV_WIDTH]
    for s in range(1, CONV_WIDTH):
        conv = conv + ubuf_ref[pl.ds(SUBLANES - s, tm), :] * cw[CONV_WIDTH - 1 - s:CONV_WIDTH - s]
    o_ref[...] = (bg * conv).astype(o_ref.dtype)


def _conv_mix(xb, w_in, conv_w, *, seq_rows, pad_rows):
    n_rows, d = xb.shape
    tm = _pick_tile(n_rows, 1024, SUBLANES)
    tn = _pick_tile(d, 512, LANES)
    nj = d // tn
    return pl.pallas_call(
        functools.partial(_conv_kernel, seq_rows=seq_rows, pad_rows=pad_rows, tm=tm),
        grid=(n_rows // tm, nj),
        in_specs=[pl.BlockSpec((tm, d), lambda i, j: (i, 0)),
                  pl.BlockSpec((d, tn), lambda i, j: (0, j)),
                  pl.BlockSpec((d, tn), lambda i, j: (0, nj + j)),
                  pl.BlockSpec((d, tn), lambda i, j: (0, 2 * nj + j)),
                  pl.BlockSpec((CONV_WIDTH, tn), lambda i, j: (0, j))],
        out_specs=pl.BlockSpec((tm, tn), lambda i, j: (i, j)),
        out_shape=jax.ShapeDtypeStruct((n_rows, d), BF16),
        scratch_shapes=[pltpu.VMEM((tm + SUBLANES, tn), F32),
                        pltpu.VMEM((nj, SUBLANES, tn), F32)],
        compiler_params=_cparams(("arbitrary", "arbitrary")),
        name="conv_mix",
    )(xb, w_in, w_in, w_in, conv_w)


def _layer_norm(y, g, b):
    mu = jnp.mean(y, axis=-1, keepdims=True)
    yc = y - mu
    var = jnp.mean(yc * yc, axis=-1, keepdims=True)
    return yc * lax.rsqrt(var + LN_EPS) * g + b


def _proj_ln_kernel(a_ref, w_ref, h_ref, g_ref, b_ref, rw_ref, o_ref, lg_ref):
    y = jnp.dot(a_ref[...], w_ref[...], preferred_element_type=F32) + DN_ALPHA * h_ref[...]
    h = _layer_norm(y, g_ref[...], b_ref[...])
    o_ref[...] = h
    lg_ref[...] = lax.dot_general(rw_ref[...], h.astype(BF16), (((1,), (1,)), ((), ())),
                                  preferred_element_type=F32)


def _proj_ln_router(a, w, h, g, b, rw_t):
    n_rows, kin = a.shape
    d = w.shape[1]
    e = rw_t.shape[0]
    tm = _pick_tile(n_rows, 256, LANES)
    return pl.pallas_call(
        _proj_ln_kernel,
        grid=(n_rows // tm,),
        in_specs=[pl.BlockSpec((tm, kin), lambda i: (i, 0)),
                  pl.BlockSpec((kin, d), lambda i: (0, 0)),
                  pl.BlockSpec((tm, d), lambda i: (i, 0)),
                  pl.BlockSpec((1, d), lambda i: (0, 0)),
                  pl.BlockSpec((1, d), lambda i: (0, 0)),
                  pl.BlockSpec((e, d), lambda i: (0, 0))],
        out_specs=[pl.BlockSpec((tm, d), lambda i: (i, 0)),
                   pl.BlockSpec((e, tm), lambda i: (0, i))],
        out_shape=[jax.ShapeDtypeStruct((n_rows, d), F32),
                   jax.ShapeDtypeStruct((e, n_rows), F32)],
        compiler_params=_cparams(("parallel",)),
        name="proj_ln_router",
    )(a, w, h, g, b, rw_t)


def _beats(cand, cand_idx, ref, ref_idx):
    return (cand > ref) | ((cand == ref) & (cand_idx < ref_idx))


def _route_kernel(lg_ref, bias_ref, w_ref, pos_ref, cnt_ref, run_ref, *, t):
    E, G = N_EXPERTS, N_GROUPS
    gs = E // G

    @pl.when(pl.program_id(0) == 0)
    def _():
        run_ref[...] = jnp.zeros_like(run_ref)

    s = jax.nn.sigmoid(lg_ref[...])
    c = s + bias_ref[...][:, :1]
    sub = lax.broadcasted_iota(jnp.int32, (gs, t), 0)

    grp_rows = []
    for g in range(G):
        cg = c[g * gs:(g + 1) * gs]
        rank = jnp.zeros((gs, t), jnp.int32)
        for m in range(gs):
            rank = rank + _beats(cg[m:m + 1], m, cg, sub).astype(jnp.int32)
        grp_rows.append(jnp.sum(jnp.where(rank < 2, cg, 0.0), axis=0, keepdims=True))
    gidx = lax.broadcasted_iota(jnp.int32, (G, t), 0)
    gscore = jnp.zeros((G, t), F32)
    for g in range(G):
        gscore = jnp.where(gidx == g, grp_rows[g], gscore)
    grank = jnp.zeros((G, t), jnp.int32)
    for m in range(G):
        grank = grank + _beats(gscore[m:m + 1], m, gscore, gidx).astype(jnp.int32)
    gsel = grank < TOPK_GROUPS

    masked = jnp.concatenate(
        [jnp.where(gsel[g:g + 1], c[g * gs:(g + 1) * gs], -jnp.inf) for g in range(G)], axis=0)
    eidx = lax.broadcasted_iota(jnp.int32, (E, t), 0)
    rank = jnp.zeros((E, t), jnp.int32)
    for m in range(E):
        rank = rank + _beats(masked[m:m + 1], m, masked, eidx).astype(jnp.int32)
    sel = rank < TOP_K
    gate = jnp.where(sel, s, 0.0)
    w_ref[...] = gate / jnp.sum(gate, axis=0, keepdims=True) * ROUTED_SCALE

    li = lax.broadcasted_iota(jnp.int32, (t, t), 0)
    lj = lax.broadcasted_iota(jnp.int32, (t, t), 1)
    upper = (li <= lj).astype(BF16)
    self_ = sel.astype(F32)
    incl = jnp.dot(sel.astype(BF16), upper, preferred_element_type=F32)
    run = run_ref[...][:, :1]
    pos_ref[...] = jnp.where(sel, run + incl - self_, -1.0)
    run_new = run + jnp.sum(self_, axis=1, keepdims=True)
    run_ref[...] = jnp.broadcast_to(run_new, run_ref.shape)
    cnt_ref[...] = jnp.broadcast_to(run_new, cnt_ref.shape)


def _route(logits_t, bias_col):
    e, n_rows = logits_t.shape
    t = _pick_tile(n_rows, 512, LANES)
    return pl.pallas_call(
        functools.partial(_route_kernel, t=t),
        grid=(n_rows // t,),
        in_specs=[pl.BlockSpec((e, t), lambda i: (0, i)),
                  pl.BlockSpec((e, LANES), lambda i: (0, 0))],
        out_specs=[pl.BlockSpec((e, t), lambda i: (0, i)),
                   pl.BlockSpec((e, t), lambda i: (0, i)),
                   pl.BlockSpec((e, LANES), lambda i: (0, 0))],
        out_shape=[jax.ShapeDtypeStruct((e, n_rows), F32),
                   jax.ShapeDtypeStruct((e, n_rows), F32),
                   jax.ShapeDtypeStruct((e, LANES), F32)],
        scratch_shapes=[pltpu.VMEM((e, LANES), F32)],
        compiler_params=_cparams(("arbitrary",)),
        name="route",
    )(logits_t, bias_col)


def _compact_kernel(w_ref, pos_ref, pst_ref, slot_ref, w8_ref, *, t):
    E = N_EXPERTS
    pos = pos_ref[...]
    sel = pos >= 0.0
    ri = lax.broadcasted_iota(jnp.int32, (E, E), 0)
    ci = lax.broadcasted_iota(jnp.int32, (E, E), 1)
    below = (ci < ri).astype(BF16)
    order = jnp.dot(below, sel.astype(BF16), preferred_element_type=F32)
    slot = pst_ref[...][:, :1] + pos
    wd = w_ref[...]
    kidx = lax.broadcasted_iota(jnp.int32, (TOP_K, t), 0)
    slots = jnp.zeros((TOP_K, t), F32)
    w8 = jnp.zeros((TOP_K, t), F32)
    for k in range(TOP_K):
        m = sel & (order == float(k))
        slots = jnp.where(kidx == k, jnp.sum(jnp.where(m, slot, 0.0), axis=0, keepdims=True), slots)
        w8 = jnp.where(kidx == k, jnp.sum(jnp.where(m, wd, 0.0), axis=0, keepdims=True), w8)
    slot_ref[...] = slots.astype(jnp.int32)
    w8_ref[...] = w8


def _compact(w_dense, pos_dense, pstart_col):
    e, n_rows = w_dense.shape
    t = _pick_tile(n_rows, 512, LANES)
    return pl.pallas_call(
        functools.partial(_compact_kernel, t=t),
        grid=(n_rows // t,),
        in_specs=[pl.BlockSpec((e, t), lambda i: (0, i)),
                  pl.BlockSpec((e, t), lambda i: (0, i)),
                  pl.BlockSpec((e, LANES), lambda i: (0, 0))],
        out_specs=[pl.BlockSpec((TOP_K, t), lambda i: (0, i)),
                   pl.BlockSpec((TOP_K, t), lambda i: (0, i))],
        out_shape=[jax.ShapeDtypeStruct((TOP_K, n_rows), jnp.int32),
                   jax.ShapeDtypeStruct((TOP_K, n_rows), F32)],
        compiler_params=_cparams(("parallel",)),
        name="compact",
    )(w_dense, pos_dense, pstart_col)


def _pack_bf16_pairs(x):
    half = x.shape[1] // 2
    lo = lax.bitcast_convert_type(x[:, :half].astype(BF16).astype(F32), jnp.uint32)
    hi = lax.bitcast_convert_type(x[:, half:].astype(BF16).astype(F32), jnp.uint32)
    return (lo >> 16) | hi


def _unpack_bf16_pairs(w):
    lo = lax.bitcast_convert_type(w << 16, F32)
    hi = lax.bitcast_convert_type(w & jnp.uint32(0xFFFF0000), F32)
    return lo, hi


def _dispatch_kernel(pad_lo_ref, pad_hi_ref, slot_ref, h_ref, xs_ref, hp_ref, zero_ref, sem, zsem,
                     *, t):
    hp_ref[...] = _pack_bf16_pairs(h_ref[...])

    def issue(tok, carry):
        src = hp_ref.at[pl.ds(tok, 1), :]
        for k in range(TOP_K):
            pltpu.make_async_copy(src, xs_ref.at[pl.ds(slot_ref[k * t + tok], 1), :], sem).start()
        return carry

    lax.fori_loop(0, t, issue, 0, unroll=2)

    @pl.when(pl.program_id(0) == 0)
    def _():
        zero_ref[...] = jnp.zeros_like(zero_ref)

        def zero_copy(p):
            return pltpu.make_async_copy(zero_ref, xs_ref.at[pl.ds(p, 1), :], zsem)

        def per_expert(e, carry):
            lo, hi = pad_lo_ref[e], pad_hi_ref[e]

            def zissue(p, c):
                zero_copy(p).start()
                return c

            def zwait(p, c):
                zero_copy(p).wait()
                return c

            lax.fori_loop(lo, hi, zissue, 0)
            lax.fori_loop(lo, hi, zwait, 0)
            return carry

        lax.fori_loop(0, N_EXPERTS, per_expert, 0)

    for k in range(TOP_K):
        pltpu.make_async_copy(hp_ref, xs_ref.at[pl.ds(0, t), :], sem).wait()


def _dispatch(h, slots_tiled, pad_lo, pad_hi, n_slots, t):
    n_rows, d = h.shape
    n_tiles = n_rows // t
    return pl.pallas_call(
        functools.partial(_dispatch_kernel, t=t),
        grid_spec=pltpu.PrefetchScalarGridSpec(
            num_scalar_prefetch=2,
            grid=(n_tiles,),
            in_specs=[pl.BlockSpec((None, None, TOP_K * t), lambda i, lo, hi: (i, 0, 0),
                                   memory_space=pltpu.SMEM),
                      pl.BlockSpec((t, d), lambda i, lo, hi: (i, 0))],
            out_specs=pl.BlockSpec(memory_space=pl.ANY),
            scratch_shapes=[pltpu.VMEM((t, d // 2), jnp.uint32),
                            pltpu.VMEM((1, d // 2), jnp.uint32),
                            pltpu.SemaphoreType.DMA(()),
                            pltpu.SemaphoreType.DMA(())]),
        out_shape=jax.ShapeDtypeStruct((n_slots, d // 2), jnp.uint32),
        compiler_params=_cparams(("arbitrary",)),
        name="dispatch",
    )(pad_lo, pad_hi, slots_tiled, h)


def _swiglu(x, w1, w3, w2):
    a = jnp.dot(x, w1, preferred_element_type=F32)
    g = jnp.dot(x, w3, preferred_element_type=F32)
    hmid = (a * jax.nn.sigmoid(a) * g).astype(BF16)
    return jnp.dot(hmid, w2, preferred_element_type=F32)


def _expert_kernel(be_ref, nu_ref, x_ref, w1_ref, w3_ref, w2_ref, y_ref, w1b_ref, w3b_ref, w2b_ref):
    b = pl.program_id(0)

    @pl.when(b < nu_ref[0])
    def _():
        @pl.when((b == 0) | (be_ref[b] != be_ref[jnp.maximum(b - 1, 0)]))
        def _():
            w1b_ref[...] = w1_ref[...].astype(BF16)
            w3b_ref[...] = w3_ref[...].astype(BF16)
            w2b_ref[...] = w2_ref[...].astype(BF16)

        lo, hi = _unpack_bf16_pairs(x_ref[...])
        x = jnp.concatenate([lo.astype(BF16), hi.astype(BF16)], axis=1)
        y_ref[...] = _pack_bf16_pairs(_swiglu(x, w1b_ref[...], w3b_ref[...], w2b_ref[...]))


def _experts(xs, w1, w3, w2, layer, block_e, n_used):
    n_slots, dh = xs.shape
    d, f = w1.shape[2], w1.shape[3]
    nb = n_slots // MOE_BLOCK

    def blk(b, be, nu):
        return jnp.minimum(b, nu[0] - 1)

    def wmap(b, be, nu):
        return (layer, be[blk(b, be, nu)], 0, 0)

    return pl.pallas_call(
        _expert_kernel,
        grid_spec=pltpu.PrefetchScalarGridSpec(
            num_scalar_prefetch=2,
            grid=(nb,),
            in_specs=[pl.BlockSpec((MOE_BLOCK, dh), lambda b, be, nu: (blk(b, be, nu), 0)),
                      pl.BlockSpec((None, None, d, f), wmap),
                      pl.BlockSpec((None, None, d, f), wmap),
                      pl.BlockSpec((None, None, f, d), wmap)],
            out_specs=pl.BlockSpec((MOE_BLOCK, dh), lambda b, be, nu: (blk(b, be, nu), 0)),
            scratch_shapes=[pltpu.VMEM((d, f), BF16), pltpu.VMEM((d, f), BF16),
                            pltpu.VMEM((f, d), BF16)]),
        out_shape=jax.ShapeDtypeStruct((n_slots, dh), jnp.uint32),
        compiler_params=_cparams(("arbitrary",)),
        name="experts",
    )(block_e, n_used, xs, w1, w3, w2)


def _combine_kernel(slot_ref, nslot_ref, h_ref, w8_ref, ws1_ref, ws3_ref, ws2_ref, g_ref, b_ref,
                    ys_ref, o_ref, ob_ref, ybuf_ref, sem, *, t):
    i = pl.program_id(0)
    cur = i % 2

    def issue_tile(s_ref, buf):
        def issue(tok, carry):
            for k in range(TOP_K):
                pltpu.make_async_copy(ys_ref.at[pl.ds(s_ref[k * t + tok], 1), :],
                                      ybuf_ref.at[buf, k, pl.ds(tok, 1), :], sem.at[buf]).start()
            return carry

        lax.fori_loop(0, t, issue, 0, unroll=2)

    @pl.when(i == 0)
    def _():
        issue_tile(slot_ref, 0)

    @pl.when(i + 1 < pl.num_programs(0))
    def _():
        issue_tile(nslot_ref, 1 - cur)

    h = h_ref[...]
    shared = _swiglu(h.astype(BF16), ws1_ref[...], ws3_ref[...], ws2_ref[...])

    for k in range(TOP_K):
        pltpu.make_async_copy(ys_ref.at[pl.ds(0, t), :], ybuf_ref.at[cur, k], sem.at[cur]).wait()

    w8 = w8_ref[...]
    acc_lo, acc_hi = None, None
    for k in range(TOP_K):
        lo, hi = _unpack_bf16_pairs(ybuf_ref[cur, k])
        wk = w8[:, k:k + 1]
        acc_lo = lo * wk if acc_lo is None else acc_lo + lo * wk
        acc_hi = hi * wk if acc_hi is None else acc_hi + hi * wk
    acc = jnp.concatenate([acc_lo, acc_hi], axis=1)
    y = DN_ALPHA * h + (acc + shared)
    out = _layer_norm(y, g_ref[...], b_ref[...])
    o_ref[...] = out
    ob_ref[...] = out.astype(BF16)


def _combine(h, ys, slots_tiled, w8_t, ws1, ws3, ws2, g, b, t, out_rows, out_map):
    n_rows, d = h.shape
    f = ws1.shape[1]
    n_tiles = n_rows // t
    return pl.pallas_call(
        functools.partial(_combine_kernel, t=t),
        grid=(n_tiles,),
        in_specs=[pl.BlockSpec((None, None, TOP_K * t), lambda i: (i, 0, 0),
                               memory_space=pltpu.SMEM),
                  pl.BlockSpec((None, None, TOP_K * t),
                               lambda i: (jnp.minimum(i + 1, n_tiles - 1), 0, 0),
                               memory_space=pltpu.SMEM),
                  pl.BlockSpec((t, d), lambda i: (i, 0)),
                  pl.BlockSpec((t, TOP_K), lambda i: (i, 0)),
                  pl.BlockSpec((d, f), lambda i: (0, 0)),
                  pl.BlockSpec((d, f), lambda i: (0, 0)),
                  pl.BlockSpec((f, d), lambda i: (0, 0)),
                  pl.BlockSpec((1, d), lambda i: (0, 0)),
                  pl.BlockSpec((1, d), lambda i: (0, 0)),
                  pl.BlockSpec(memory_space=pl.ANY)],
        out_specs=[pl.BlockSpec((t, d), lambda i: (out_map(i), 0)),
                   pl.BlockSpec((t, d), lambda i: (i, 0))],
        out_shape=[jax.ShapeDtypeStruct((out_rows, d), F32),
                   jax.ShapeDtypeStruct((n_rows, d), BF16)],
        scratch_shapes=[pltpu.VMEM((2, TOP_K, t, d // 2), jnp.uint32),
                        pltpu.SemaphoreType.DMA((2,))],
        compiler_params=_cparams(("arbitrary",)),
        name="combine_ln",
    )(slots_tiled, slots_tiled, h, w8_t, ws1, ws3, ws2, g, b, ys)


def _moe_ln(h, logits_t, router_bias, w1, w3, w2, layer, ws1, ws3, ws2, g, b, drop_rows, seq_rows):
    n_rows, d = h.shape
    E = N_EXPERTS
    bias_col = jnp.broadcast_to(router_bias.astype(F32)[:, None], (E, LANES))
    w_dense, pos_dense, cnt = _route(logits_t, bias_col)

    counts = cnt[:, 0].astype(jnp.int32)
    pcounts = (counts + MOE_BLOCK - 1) // MOE_BLOCK * MOE_BLOCK
    pends = jnp.cumsum(pcounts)
    pstarts = pends - pcounts
    n_blocks = -(-(n_rows * TOP_K) // MOE_BLOCK) + E
    n_slots = n_blocks * MOE_BLOCK
    block_lo = jnp.arange(n_blocks, dtype=jnp.int32) * MOE_BLOCK
    block_e = jnp.minimum(jnp.sum(pends[None, :] <= block_lo[:, None], axis=1), E - 1).astype(jnp.int32)
    n_used = (pends[-1:] // MOE_BLOCK).astype(jnp.int32)
    pstart_col = jnp.broadcast_to(pstarts.astype(F32)[:, None], (E, LANES))

    slots, w8 = _compact(w_dense, pos_dense, pstart_col)
    t = _pick_tile(n_rows, 128, LANES)
    n_tiles = n_rows // t
    slots_tiled = slots.reshape(TOP_K, n_tiles, t).transpose(1, 0, 2).reshape(n_tiles, 1, TOP_K * t)
    xs = _dispatch(h, slots_tiled, (pstarts + counts).astype(jnp.int32), pends.astype(jnp.int32),
                   n_slots, t)
    ys = _experts(xs, w1, w3, w2, layer, block_e, n_used)
    if drop_rows == t:
        per_seq = seq_rows // t
        out_rows = n_rows - (n_rows // seq_rows) * t

        def out_map(i):
            return (i // per_seq) * (per_seq - 1) + jnp.maximum(i % per_seq - 1, 0)
    else:
        out_rows, out_map = n_rows, (lambda i: i)
    return _combine(h, ys, slots_tiled, w8.T, ws1, ws3, ws2, g, b, t, out_rows, out_map)


def kernel(x, meta_tokens, gla_w_in, gla_w_gate_up, gla_b_gate, gla_norm_g, gla_w_out,
           conv_w_in, conv_w, conv_w_out, ln1_g, ln1_b, router_w, router_bias,
           exp_w1, exp_w3, exp_w2, shared_w1, shared_w3, shared_w2, ln2_g, ln2_b):
    batch, seq, d = x.shape
    H = GLA_HEADS
    dk = d // 2 // H
    n_qkvr = 2 * H * dk + 2 * d
    rank = gla_w_in.shape[2] - n_qkvr
    seq_rows = -(-(N_META + seq) // ROW_ALIGN) * ROW_ALIGN
    pad_rows = seq_rows - N_META - seq
    n_rows = batch * seq_rows

    meta = jnp.broadcast_to(meta_tokens.astype(x.dtype)[None], (batch, N_META, d))
    h = jnp.concatenate([jnp.zeros((batch, pad_rows, d), x.dtype), meta, x], axis=1)
    h = h.reshape(n_rows, d)
    hb = h.astype(BF16)

    def row(v):
        return v.astype(F32)[None, :]

    for i in range(DEPTH):
        jm = i // 2
        if i % 2 == 0:
            w_in = gla_w_in[jm]
            w_qkvr = w_in[:, :n_qkvr].astype(BF16)
            w_gate = jnp.pad(w_in[:, n_qkvr:], ((0, 0), (0, GATE_PAD - rank))).astype(BF16)
            wg_up = jnp.pad(gla_w_gate_up[jm], ((0, GATE_PAD - rank), (0, 0))).astype(BF16)
            qkvr = _matmul(hb, w_qkvr, BF16, 1664, 768, "gla_in_proj")
            glow = _matmul(hb, w_gate, F32, 1664, LANES, "gla_gate_proj")
            mix = _gla(qkvr, glow, wg_up, row(gla_b_gate[jm]), row(gla_norm_g[jm]),
                       batch=batch, seq_rows=seq_rows, d_model=d)
            w_out = gla_w_out[jm].astype(BF16)
        else:
            mix = _conv_mix(hb, conv_w_in[jm].astype(BF16), conv_w[jm].astype(F32),
                            seq_rows=seq_rows, pad_rows=pad_rows)
            w_out = conv_w_out[jm].astype(BF16)
        h, logits_t = _proj_ln_router(mix, w_out, h, row(ln1_g[i]), row(ln1_b[i]),
                                      router_w[i].T.astype(BF16))
        drop = pad_rows + N_META if i == DEPTH - 1 else 0
        h, hb = _moe_ln(h, logits_t, router_bias[i], exp_w1, exp_w3, exp_w2, i,
                        shared_w1[i].astype(BF16), shared_w3[i].astype(BF16),
                        shared_w2[i].astype(BF16), row(ln2_g[i]), row(ln2_b[i]), drop, seq_rows)
    if h.shape[0] == batch * seq:
        return h.reshape(batch, seq, d)
    return h.reshape(batch, seq_rows, d)[:, pad_rows + N_META:]
```

```python
import functools

import jax
import jax.numpy as jnp
from jax import lax
from jax.experimental import pallas as pl
from jax.experimental.pallas import tpu as pltpu

N_META = 16
GLA_HEADS = 4
GLA_GATE_TAU = 16.0
CONV_WIDTH = 3
N_EXPERTS = 64
TOP_K = 8
N_GROUPS = 8
TOPK_GROUPS = 4
ROUTED_SCALE = 2.5
LN_EPS = 1e-5
RMS_EPS = 1e-6
DEPTH = 2
DN_ALPHA = (2 * DEPTH) ** 0.25

LANES = 128
SUBLANES = 8
ROW_ALIGN = 128
GLA_CHUNK = 64
GLA_SUB = 16
GLA_HEADS_PER_STEP = 4
GATE_PAD = LANES
MOE_BLOCK = 512
VMEM_LIMIT = 56 * 1024 * 1024

F32 = jnp.float32
BF16 = jnp.bfloat16


def _pick_tile(n, target, mult):
    best = None
    for t in range(mult, min(n, target) + 1, mult):
        if n % t == 0:
            best = t
    assert best is not None, (n, target, mult)
    return best


def _cparams(sem):
    return pltpu.CompilerParams(dimension_semantics=sem, vmem_limit_bytes=VMEM_LIMIT)


def _mm_kernel(x_ref, w_ref, o_ref):
    o_ref[...] = jnp.dot(x_ref[...], w_ref[...], preferred_element_type=F32).astype(o_ref.dtype)


def _matmul(x, w, out_dtype, tm_target, tn_target, name):
    m, k = x.shape
    n = w.shape[1]
    tm = _pick_tile(m, tm_target, 16)
    tn = _pick_tile(n, tn_target, LANES)
    return pl.pallas_call(
        _mm_kernel,
        grid=(m // tm, n // tn),
        in_specs=[pl.BlockSpec((tm, k), lambda i, j: (i, 0)),
                  pl.BlockSpec((k, tn), lambda i, j: (0, j))],
        out_specs=pl.BlockSpec((tm, tn), lambda i, j: (i, j)),
        out_shape=jax.ShapeDtypeStruct((m, n), out_dtype),
        compiler_params=_cparams(("parallel", "arbitrary")),
        name=name,
    )(x, w)


def _gla_kernel(q_ref, k_ref, v_ref, r_ref, gl_ref, wg_ref, bg_ref, ng_ref, o_ref, st_ref,
                *, n_chunks, dk, dv, heads):
    C, S = GLA_CHUNK, GLA_SUB

    @pl.when(pl.program_id(2) == 0)
    def _():
        st_ref[...] = jnp.zeros_like(st_ref)

    ri = lax.broadcasted_iota(jnp.int32, (C, C), 0)
    ci = lax.broadcasted_iota(jnp.int32, (C, C), 1)
    tri = (ri >= ci).astype(BF16)
    rs = lax.broadcasted_iota(jnp.int32, (S, S), 0)
    cs = lax.broadcasted_iota(jnp.int32, (S, S), 1)
    nt = (((1,), (1,)), ((), ()))
    tn = (((0,), (0,)), ((), ()))
    half = S // 2

    def chunk(c, carry):
        r0 = pl.multiple_of(c * C, C)
        rows_c = pl.ds(r0, C)
        hs = range(heads)
        kcol = [pl.ds(hd * dk, dk) for hd in hs]
        vcol = [pl.ds(hd * dv, dv) for hd in hs]
        gl = gl_ref[rows_c, :].astype(BF16)
        q = [q_ref[rows_c, kcol[hd]].astype(F32) * (dk ** -0.5) for hd in hs]
        kk = [k_ref[rows_c, kcol[hd]].astype(F32) for hd in hs]
        vv = [v_ref[rows_c, vcol[hd]] for hd in hs]
        st = [st_ref[hd] for hd in hs]
        z = [jnp.dot(gl, wg_ref[:, kcol[hd]], preferred_element_type=F32) + bg_ref[:, kcol[hd]]
             for hd in hs]
        b = []
        for hd in hs:
            la = ((jnp.minimum(z[hd], 0.0) - jnp.log(1.0 + jnp.exp(-jnp.abs(z[hd]))))
                  * (1.0 / GLA_GATE_TAU))
            h1 = la.astype(BF16)
            e1 = la - h1.astype(F32)
            h2 = e1.astype(BF16)
            h3 = (e1 - h2.astype(F32)).astype(BF16)
            b3 = jnp.dot(tri, jnp.concatenate([h1, h2, h3], axis=1), preferred_element_type=F32)
            b.append(b3[:, :dk] + b3[:, dk:2 * dk] + b3[:, 2 * dk:])

        o_inter, st_new, off = [], [], []
        for hd in hs:
            qe = (q[hd] * jnp.exp(b[hd])).astype(BF16)
            o_inter.append(lax.dot_general(qe, st[hd].astype(BF16), nt, preferred_element_type=F32))
            bl = b[hd][C - 1:C]
            khat = (kk[hd] * jnp.exp(bl - b[hd])).astype(BF16)
            upd = lax.dot_general(vv[hd], khat, tn, preferred_element_type=F32)
            st_new.append(st[hd] * jnp.exp(bl) + upd)
            offs = []
            for i in range(1, C // S):
                lo = i * S
                bref = b[hd][lo - 1:lo]
                qt = (q[hd][lo:lo + S] * jnp.exp(b[hd][lo:lo + S] - bref)).astype(BF16)
                kt = (kk[hd][:lo] * jnp.exp(bref - b[hd][:lo])).astype(BF16)
                offs.append(lax.dot_general(qt, kt, nt, preferred_element_type=F32))
            off.append(offs)

        dmats = []
        for hd in hs:
            blocks = []
            for i in range(C // S):
                lo = i * S
                qs = q[hd][lo:lo + S]
                bs = b[hd][lo:lo + S]
                dmat = jnp.zeros((S, S), F32)
                for j in range(S):
                    kj = kk[hd][lo + j:lo + j + 1]
                    bj = b[hd][lo + j:lo + j + 1]
                    if j < half:
                        p = qs * kj * jnp.exp(jnp.minimum(bs - bj, 0.0))
                        a = jnp.sum(p, axis=-1, keepdims=True)
                    else:
                        p = qs[half:] * kj * jnp.exp(jnp.minimum(bs[half:] - bj, 0.0))
                        a = jnp.concatenate([jnp.zeros((half, 1), F32),
                                             jnp.sum(p, axis=-1, keepdims=True)], axis=0)
                    dmat = jnp.where(cs == j, a, dmat)
                blocks.append(jnp.where(rs >= cs, dmat, 0.0).astype(BF16))
            dmats.append(blocks)

        for hd in hs:
            rows = []
            for i in range(C // S):
                lo = i * S
                o_i = jnp.dot(dmats[hd][i], vv[hd][lo:lo + S], preferred_element_type=F32)
                if i > 0:
                    o_i = o_i + jnp.dot(off[hd][i - 1].astype(BF16), vv[hd][:lo],
                                        preferred_element_type=F32)
                rows.append(o_i)
            o = o_inter[hd] + jnp.concatenate(rows, axis=0)
            ms = jnp.mean(o * o, axis=-1, keepdims=True)
            r = r_ref[rows_c, vcol[hd]].astype(F32)
            y = o * lax.rsqrt(ms + RMS_EPS) * ng_ref[:, vcol[hd]] * (r * jax.nn.sigmoid(r))
            st_ref[hd] = st_new[hd]
            o_ref[rows_c, vcol[hd]] = y.astype(o_ref.dtype)
        return carry

    lax.fori_loop(0, n_chunks, chunk, 0)


def _gla(qkvr, glow, wg, bg, ng, *, batch, seq_rows, d_model):
    H, hps = GLA_HEADS, GLA_HEADS_PER_STEP
    dk = d_model // 2 // H
    dv = d_model // H
    n_rows = batch * seq_rows
    rblk = _pick_tile(seq_rows, 640, GLA_CHUNK)
    steps = seq_rows // rblk
    groups = H // hps
    off_k, off_v, off_r = groups, (2 * H * dk) // (hps * dv), (2 * H * dk) // (hps * dv) + groups

    def rowmap(off):
        return lambda b, h, i: (b * steps + i, off + h)

    return pl.pallas_call(
        functools.partial(_gla_kernel, n_chunks=rblk // GLA_CHUNK, dk=dk, dv=dv, heads=hps),
        grid=(batch, groups, steps),
        in_specs=[pl.BlockSpec((rblk, hps * dk), rowmap(0)),
                  pl.BlockSpec((rblk, hps * dk), rowmap(off_k)),
                  pl.BlockSpec((rblk, hps * dv), rowmap(off_v)),
                  pl.BlockSpec((rblk, hps * dv), rowmap(off_r)),
                  pl.BlockSpec((rblk, GATE_PAD), lambda b, h, i: (b * steps + i, 0)),
                  pl.BlockSpec((GATE_PAD, hps * dk), lambda b, h, i: (0, h)),
                  pl.BlockSpec((1, hps * dk), lambda b, h, i: (0, h)),
                  pl.BlockSpec((1, hps * dv), lambda b, h, i: (0, h))],
        out_specs=pl.BlockSpec((rblk, hps * dv), lambda b, h, i: (b * steps + i, h)),
        out_shape=jax.ShapeDtypeStruct((n_rows, H * dv), BF16),
        scratch_shapes=[pltpu.VMEM((hps, dv, dk), F32)],
        compiler_params=_cparams(("parallel", "parallel", "arbitrary")),
        name="gla_chunks",
    )(qkvr, qkvr, qkvr, qkvr, glow, wg, bg, ng)


def _conv_kernel(x_ref, wb_ref, wc_ref, wh_ref, cw_ref, o_ref, ubuf_ref, carry_ref,
                 *, seq_rows, pad_rows, tm):
    i = pl.program_id(0)
    j = pl.program_id(1)
    x = x_ref[...]
    bg = jnp.dot(x, wb_ref[...], preferred_element_type=F32)
    cg = jnp.dot(x, wc_ref[...], preferred_element_type=F32)
    hh = jnp.dot(x, wh_ref[...], preferred_element_type=F32)
    row = i * tm + lax.broadcasted_iota(jnp.int32, (tm, 1), 0)
    u = jnp.where(row % seq_rows >= pad_rows, cg * hh, 0.0)

    @pl.when(i == 0)
    def _():
        carry_ref[j] = jnp.zeros(carry_ref.shape[1:], F32)

    ubuf_ref[pl.ds(0, SUBLANES), :] = carry_ref[j]
    ubuf_ref[pl.ds(SUBLANES, tm), :] = u
    carry_ref[j] = u[tm - SUBLANES:]
    cw = cw_ref[...]
    conv = u * cw[CONV_WIDTH - 1:CONV_WIDTH]
    for s in range(1, CONV_WIDTH):
        conv = conv + ubuf_ref[pl.ds(SUBLANES - s, tm), :] * cw[CONV_WIDTH - 1 - s:CONV_WIDTH - s]
    o_ref[...] = (bg * conv).astype(o_ref.dtype)


def _conv_mix(xb, w_in, conv_w, *, seq_rows, pad_rows):
    n_rows, d = xb.shape
    tm = _pick_tile(n_rows, 1024, SUBLANES)
    tn = _pick_tile(d, 512, LANES)
    nj = d // tn
    return pl.pallas_call(
        functools.partial(_conv_kernel, seq_rows=seq_rows, pad_rows=pad_rows, tm=tm),
        grid=(n_rows // tm, nj),
        in_specs=[pl.BlockSpec((tm, d), lambda i, j: (i, 0)),
                  pl.BlockSpec((d, tn), lambda i, j: (0, j)),
                  pl.BlockSpec((d, tn), lambda i, j: (0, nj + j)),
                  pl.BlockSpec((d, tn), lambda i, j: (0, 2 * nj + j)),
                  pl.BlockSpec((CONV_WIDTH, tn), lambda i, j: (0, j))],
        out_specs=pl.BlockSpec((tm, tn), lambda i, j: (i, j)),
        out_shape=jax.ShapeDtypeStruct((n_rows, d), BF16),
        scratch_shapes=[pltpu.VMEM((tm + SUBLANES, tn), F32),
                        pltpu.VMEM((nj, SUBLANES, tn), F32)],
        compiler_params=_cparams(("arbitrary", "arbitrary")),
        name="conv_mix",
    )(xb, w_in, w_in, w_in, conv_w)


def _layer_norm(y, g, b):
    mu = jnp.mean(y, axis=-1, keepdims=True)
    yc = y - mu
    var = jnp.mean(yc * yc, axis=-1, keepdims=True)
    return yc * lax.rsqrt(var + LN_EPS) * g + b


def _proj_ln_kernel(a_ref, w_ref, h_ref, g_ref, b_ref, rw_ref, o_ref, lg_ref):
    y = jnp.dot(a_ref[...], w_ref[...], preferred_element_type=F32) + DN_ALPHA * h_ref[...]
    h = _layer_norm(y, g_ref[...], b_ref[...])
    o_ref[...] = h
    lg_ref[...] = lax.dot_general(rw_ref[...], h.astype(BF16), (((1,), (1,)), ((), ())),
                                  preferred_element_type=F32)


def _proj_ln_router(a, w, h, g, b, rw_t):
    n_rows, kin = a.shape
    d = w.shape[1]
    e = rw_t.shape[0]
    tm = _pick_tile(n_rows, 256, LANES)
    return pl.pallas_call(
        _proj_ln_kernel,
        grid=(n_rows // tm,),
        in_specs=[pl.BlockSpec((tm, kin), lambda i: (i, 0)),
                  pl.BlockSpec((kin, d), lambda i: (0, 0)),
                  pl.BlockSpec((tm, d), lambda i: (i, 0)),
                  pl.BlockSpec((1, d), lambda i: (0, 0)),
                  pl.BlockSpec((1, d), lambda i: (0, 0)),
                  pl.BlockSpec((e, d), lambda i: (0, 0))],
        out_specs=[pl.BlockSpec((tm, d), lambda i: (i, 0)),
                   pl.BlockSpec((e, tm), lambda i: (0, i))],
        out_shape=[jax.ShapeDtypeStruct((n_rows, d), F32),
                   jax.ShapeDtypeStruct((e, n_rows), F32)],
        compiler_params=_cparams(("parallel",)),
        name="proj_ln_router",
    )(a, w, h, g, b, rw_t)


def _beats(cand, cand_idx, ref, ref_idx):
    return (cand > ref) | ((cand == ref) & (cand_idx < ref_idx))


def _route_kernel(lg_ref, bias_ref, w_ref, pos_ref, cnt_ref, run_ref, *, t):
    E, G = N_EXPERTS, N_GROUPS
    gs = E // G

    @pl.when(pl.program_id(0) == 0)
    def _():
        run_ref[...] = jnp.zeros_like(run_ref)

    s = jax.nn.sigmoid(lg_ref[...])
    c = s + bias_ref[...][:, :1]
    sub = lax.broadcasted_iota(jnp.int32, (gs, t), 0)

    grp_rows = []
    for g in range(G):
        cg = c[g * gs:(g + 1) * gs]
        rank = jnp.zeros((gs, t), jnp.int32)
        for m in range(gs):
            rank = rank + _beats(cg[m:m + 1], m, cg, sub).astype(jnp.int32)
        grp_rows.append(jnp.sum(jnp.where(rank < 2, cg, 0.0), axis=0, keepdims=True))
    gidx = lax.broadcasted_iota(jnp.int32, (G, t), 0)
    gscore = jnp.zeros((G, t), F32)
    for g in range(G):
        gscore = jnp.where(gidx == g, grp_rows[g], gscore)
    grank = jnp.zeros((G, t), jnp.int32)
    for m in range(G):
        grank = grank + _beats(gscore[m:m + 1], m, gscore, gidx).astype(jnp.int32)
    gsel = grank < TOPK_GROUPS

    masked = jnp.concatenate(
        [jnp.where(gsel[g:g + 1], c[g * gs:(g + 1) * gs], -jnp.inf) for g in range(G)], axis=0)
    eidx = lax.broadcasted_iota(jnp.int32, (E, t), 0)
    rank = jnp.zeros((E, t), jnp.int32)
    for m in range(E):
        rank = rank + _beats(masked[m:m + 1], m, masked, eidx).astype(jnp.int32)
    sel = rank < TOP_K
    gate = jnp.where(sel, s, 0.0)
    w_ref[...] = gate / jnp.sum(gate, axis=0, keepdims=True) * ROUTED_SCALE

    li = lax.broadcasted_iota(jnp.int32, (t, t), 0)
    lj = lax.broadcasted_iota(jnp.int32, (t, t), 1)
    upper = (li <= lj).astype(BF16)
    self_ = sel.astype(F32)
    incl = jnp.dot(sel.astype(BF16), upper, preferred_element_type=F32)
    run = run_ref[...][:, :1]
    pos_ref[...] = jnp.where(sel, run + incl - self_, -1.0)
    run_new = run + jnp.sum(self_, axis=1, keepdims=True)
    run_ref[...] = jnp.broadcast_to(run_new, run_ref.shape)
    cnt_ref[...] = jnp.broadcast_to(run_new, cnt_ref.shape)


def _route(logits_t, bias_col):
    e, n_rows = logits_t.shape
    t = _pick_tile(n_rows, 512, LANES)
    return pl.pallas_call(
        functools.partial(_route_kernel, t=t),
        grid=(n_rows // t,),
        in_specs=[pl.BlockSpec((e, t), lambda i: (0, i)),
                  pl.BlockSpec((e, LANES), lambda i: (0, 0))],
        out_specs=[pl.BlockSpec((e, t), lambda i: (0, i)),
                   pl.BlockSpec((e, t), lambda i: (0, i)),
                   pl.BlockSpec((e, LANES), lambda i: (0, 0))],
        out_shape=[jax.ShapeDtypeStruct((e, n_rows), F32),
                   jax.ShapeDtypeStruct((e, n_rows), F32),
                   jax.ShapeDtypeStruct((e, LANES), F32)],
        scratch_shapes=[pltpu.VMEM((e, LANES), F32)],
        compiler_params=_cparams(("arbitrary",)),
        name="route",
    )(logits_t, bias_col)


def _compact_kernel(w_ref, pos_ref, pst_ref, slot_ref, w8_ref, *, t):
    E = N_EXPERTS
    pos = pos_ref[...]
    sel = pos >= 0.0
    ri = lax.broadcasted_iota(jnp.int32, (E, E), 0)
    ci = lax.broadcasted_iota(jnp.int32, (E, E), 1)
    below = (ci < ri).astype(BF16)
    order = jnp.dot(below, sel.astype(BF16), preferred_element_type=F32)
    slot = pst_ref[...][:, :1] + pos
    wd = w_ref[...]
    kidx = lax.broadcasted_iota(jnp.int32, (TOP_K, t), 0)
    slots = jnp.zeros((TOP_K, t), F32)
    w8 = jnp.zeros((TOP_K, t), F32)
    for k in range(TOP_K):
        m = sel & (order == float(k))
        slots = jnp.where(kidx == k, jnp.sum(jnp.where(m, slot, 0.0), axis=0, keepdims=True), slots)
        w8 = jnp.where(kidx == k, jnp.sum(jnp.where(m, wd, 0.0), axis=0, keepdims=True), w8)
    slot_ref[...] = slots.astype(jnp.int32)
    w8_ref[...] = w8


def _compact(w_dense, pos_dense, pstart_col):
    e, n_rows = w_dense.shape
    t = _pick_tile(n_rows, 512, LANES)
    return pl.pallas_call(
        functools.partial(_compact_kernel, t=t),
        grid=(n_rows // t,),
        in_specs=[pl.BlockSpec((e, t), lambda i: (0, i)),
                  pl.BlockSpec((e, t), lambda i: (0, i)),
                  pl.BlockSpec((e, LANES), lambda i: (0, 0))],
        out_specs=[pl.BlockSpec((TOP_K, t), lambda i: (0, i)),
                   pl.BlockSpec((TOP_K, t), lambda i: (0, i))],
        out_shape=[jax.ShapeDtypeStruct((TOP_K, n_rows), jnp.int32),
                   jax.ShapeDtypeStruct((TOP_K, n_rows), F32)],
        compiler_params=_cparams(("parallel",)),
        name="compact",
    )(w_dense, pos_dense, pstart_col)


def _pack_bf16_pairs(x):
    half = x.shape[1] // 2
    lo = lax.bitcast_convert_type(x[:, :half].astype(BF16).astype(F32), jnp.uint32)
    hi = lax.bitcast_convert_type(x[:, half:].astype(BF16).astype(F32), jnp.uint32)
    return (lo >> 16) | hi


def _unpack_bf16_pairs(w):
    lo = lax.bitcast_convert_type(w << 16, F32)
    hi = lax.bitcast_convert_type(w & jnp.uint32(0xFFFF0000), F32)
    return lo, hi


def _dispatch_kernel(pad_lo_ref, pad_hi_ref, slot_ref, h_ref, xs_ref, hp_ref, zero_ref, sem, zsem,
                     *, t):
    i = pl.program_id(0)
    cur = i % 2
    hp_ref[cur] = _pack_bf16_pairs(h_ref[...])

    def issue(tok, carry):
        src = hp_ref.at[cur, pl.ds(tok, 1), :]
        for k in range(TOP_K):
            pltpu.make_async_copy(src, xs_ref.at[pl.ds(slot_ref[k * t + tok], 1), :],
                                  sem.at[cur]).start()
        return carry

    lax.fori_loop(0, t, issue, 0, unroll=2)

    @pl.when(i == 0)
    def _():
        zero_ref[...] = jnp.zeros_like(zero_ref)

        def zero_copy(p):
            return pltpu.make_async_copy(zero_ref, xs_ref.at[pl.ds(p, 1), :], zsem)

        def per_expert(step):
            def body(e, carry):
                lax.fori_loop(pad_lo_ref[e], pad_hi_ref[e], step, 0)
                return carry
            return body

        def zissue(p, c):
            zero_copy(p).start()
            return c

        def zwait(p, c):
            zero_copy(p).wait()
            return c

        lax.fori_loop(0, N_EXPERTS, per_expert(zissue), 0)
        lax.fori_loop(0, N_EXPERTS, per_expert(zwait), 0)

    def wait_tile(buf):
        for k in range(TOP_K):
            pltpu.make_async_copy(hp_ref.at[buf], xs_ref.at[pl.ds(0, t), :], sem.at[buf]).wait()

    @pl.when(i > 0)
    def _():
        wait_tile(1 - cur)

    @pl.when(i == pl.num_programs(0) - 1)
    def _():
        wait_tile(cur)


def _dispatch(h, slots_tiled, pad_lo, pad_hi, n_slots, t):
    n_rows, d = h.shape
    n_tiles = n_rows // t
    return pl.pallas_call(
        functools.partial(_dispatch_kernel, t=t),
        grid_spec=pltpu.PrefetchScalarGridSpec(
            num_scalar_prefetch=2,
            grid=(n_tiles,),
            in_specs=[pl.BlockSpec((None, None, TOP_K * t), lambda i, lo, hi: (i, 0, 0),
                                   memory_space=pltpu.SMEM),
                      pl.BlockSpec((t, d), lambda i, lo, hi: (i, 0))],
            out_specs=pl.BlockSpec(memory_space=pl.ANY),
            scratch_shapes=[pltpu.VMEM((2, t, d // 2), jnp.uint32),
                            pltpu.VMEM((1, d // 2), jnp.uint32),
                            pltpu.SemaphoreType.DMA((2,)),
                            pltpu.SemaphoreType.DMA(())]),
        out_shape=jax.ShapeDtypeStruct((n_slots, d // 2), jnp.uint32),
        compiler_params=_cparams(("arbitrary",)),
        name="dispatch",
    )(pad_lo, pad_hi, slots_tiled, h)


def _swiglu(x, w1, w3, w2):
    a = jnp.dot(x, w1, preferred_element_type=F32)
    g = jnp.dot(x, w3, preferred_element_type=F32)
    hmid = (a * jax.nn.sigmoid(a) * g).astype(BF16)
    return jnp.dot(hmid, w2, preferred_element_type=F32)


def _expert_kernel(be_ref, nu_ref, x_ref, w1_ref, w3_ref, w2_ref, y_ref, w1b_ref, w3b_ref, w2b_ref):
    b = pl.program_id(0)

    @pl.when(b < nu_ref[0])
    def _():
        @pl.when((b == 0) | (be_ref[b] != be_ref[jnp.maximum(b - 1, 0)]))
        def _():
            w1b_ref[...] = w1_ref[...].astype(BF16)
            w3b_ref[...] = w3_ref[...].astype(BF16)
            w2b_ref[...] = w2_ref[...].astype(BF16)

        lo, hi = _unpack_bf16_pairs(x_ref[...])
        x = jnp.concatenate([lo.astype(BF16), hi.astype(BF16)], axis=1)
        y_ref[...] = _pack_bf16_pairs(_swiglu(x, w1b_ref[...], w3b_ref[...], w2b_ref[...]))


def _experts(xs, w1, w3, w2, layer, block_e, n_used):
    n_slots, dh = xs.shape
    d, f = w1.shape[2], w1.shape[3]
    nb = n_slots // MOE_BLOCK

    def blk(b, be, nu):
        return jnp.minimum(b, nu[0] - 1)

    def wmap(b, be, nu):
        return (layer, be[blk(b, be, nu)], 0, 0)

    return pl.pallas_call(
        _expert_kernel,
        grid_spec=pltpu.PrefetchScalarGridSpec(
            num_scalar_prefetch=2,
            grid=(nb,),
            in_specs=[pl.BlockSpec((MOE_BLOCK, dh), lambda b, be, nu: (blk(b, be, nu), 0)),
                      pl.BlockSpec((None, None, d, f), wmap),
                      pl.BlockSpec((None, None, d, f), wmap),
                      pl.BlockSpec((None, None, f, d), wmap)],
            out_specs=pl.BlockSpec((MOE_BLOCK, dh), lambda b, be, nu: (blk(b, be, nu), 0)),
            scratch_shapes=[pltpu.VMEM((d, f), BF16), pltpu.VMEM((d, f), BF16),
                            pltpu.VMEM((f, d), BF16)]),
        out_shape=jax.ShapeDtypeStruct((n_slots, dh), jnp.uint32),
        compiler_params=_cparams(("arbitrary",)),
        name="experts",
    )(block_e, n_used, xs, w1, w3, w2)


def _combine_kernel(slot_ref, nslot_ref, h_ref, w8_ref, ws1_ref, ws3_ref, ws2_ref, g_ref, b_ref,
                    ys_ref, o_ref, ob_ref, ybuf_ref, sem, *, t):
    i = pl.program_id(0)
    last = pl.num_programs(0) - 1

    def row_copy(s_ref, buf, k, tok):
        return pltpu.make_async_copy(ys_ref.at[pl.ds(s_ref[k * t + tok], 1), :],
                                     ybuf_ref.at[buf, k, pl.ds(tok, 1), :], sem.at[buf])

    def wait_tile(buf):
        for k in range(TOP_K):
            pltpu.make_async_copy(ys_ref.at[pl.ds(0, t), :], ybuf_ref.at[buf, k], sem.at[buf]).wait()

    @pl.when(i == 0)
    def _():
        def issue(tok, carry):
            for k in range(TOP_K):
                row_copy(slot_ref, 0, k, tok).start()
            return carry

        lax.fori_loop(0, t, issue, 0, unroll=2)

    group = t // TOP_K

    def reduce_tile(cur):
        nxt = 1 - cur

        def issue_group(g):
            for tok in range(g * group, (g + 1) * group):
                for k in range(TOP_K):
                    row_copy(nslot_ref, nxt, k, tok).start()

        wait_tile(cur)
        h = h_ref[...]
        issue_group(0)
        shared = _swiglu(h.astype(BF16), ws1_ref[...], ws3_ref[...], ws2_ref[...])
        w8 = w8_ref[...]
        acc_lo, acc_hi = None, None
        for k in range(TOP_K):
            if k + 1 < TOP_K:
                issue_group(k + 1)
            lo, hi = _unpack_bf16_pairs(ybuf_ref[cur, k])
            wk = w8[:, k:k + 1]
            acc_lo = lo * wk if acc_lo is None else acc_lo + lo * wk
            acc_hi = hi * wk if acc_hi is None else acc_hi + hi * wk
        acc = jnp.concatenate([acc_lo, acc_hi], axis=1)
        y = DN_ALPHA * h + (acc + shared)
        out = _layer_norm(y, g_ref[...], b_ref[...])
        o_ref[...] = out
        ob_ref[...] = out.astype(BF16)

        @pl.when(i == last)
        def _():
            wait_tile(nxt)

    for parity in range(2):
        @pl.when(i % 2 == parity)
        def _():
            reduce_tile(parity)


def _combine(h, ys, slots_tiled, w8_t, ws1, ws3, ws2, g, b, t, out_rows, out_map):
    n_rows, d = h.shape
    f = ws1.shape[1]
    n_tiles = n_rows // t
    return pl.pallas_call(
        functools.partial(_combine_kernel, t=t),
        grid=(n_tiles,),
        in_specs=[pl.BlockSpec((None, None, TOP_K * t), lambda i: (i, 0, 0),
                               memory_space=pltpu.SMEM),
                  pl.BlockSpec((None, None, TOP_K * t),
                               lambda i: (jnp.minimum(i + 1, n_tiles - 1), 0, 0),
                               memory_space=pltpu.SMEM),
                  pl.BlockSpec((t, d), lambda i: (i, 0)),
                  pl.BlockSpec((t, TOP_K), lambda i: (i, 0)),
                  pl.BlockSpec((d, f), lambda i: (0, 0)),
                  pl.BlockSpec((d, f), lambda i: (0, 0)),
                  pl.BlockSpec((f, d), lambda i: (0, 0)),
                  pl.BlockSpec((1, d), lambda i: (0, 0)),
                  pl.BlockSpec((1, d), lambda i: (0, 0)),
                  pl.BlockSpec(memory_space=pl.ANY)],
        out_specs=[pl.BlockSpec((t, d), lambda i: (out_map(i), 0)),
                   pl.BlockSpec((t, d), lambda i: (i, 0))],
        out_shape=[jax.ShapeDtypeStruct((out_rows, d), F32),
                   jax.ShapeDtypeStruct((n_rows, d), BF16)],
        scratch_shapes=[pltpu.VMEM((2, TOP_K, t, d // 2), jnp.uint32),
                        pltpu.SemaphoreType.DMA((2,))],
        compiler_params=_cparams(("arbitrary",)),
        name="combine_ln",
    )(slots_tiled, slots_tiled, h, w8_t, ws1, ws3, ws2, g, b, ys)


def _moe_ln(h, logits_t, router_bias, w1, w3, w2, layer, ws1, ws3, ws2, g, b, drop_rows, seq_rows):
    n_rows, d = h.shape
    E = N_EXPERTS
    bias_col = jnp.broadcast_to(router_bias.astype(F32)[:, None], (E, LANES))
    w_dense, pos_dense, cnt = _route(logits_t, bias_col)

    counts = cnt[:, 0].astype(jnp.int32)
    pcounts = (counts + MOE_BLOCK - 1) // MOE_BLOCK * MOE_BLOCK
    pends = jnp.cumsum(pcounts)
    pstarts = pends - pcounts
    n_blocks = -(-(n_rows * TOP_K) // MOE_BLOCK) + E
    n_slots = n_blocks * MOE_BLOCK
    block_lo = jnp.arange(n_blocks, dtype=jnp.int32) * MOE_BLOCK
    block_e = jnp.minimum(jnp.sum(pends[None, :] <= block_lo[:, None], axis=1), E - 1).astype(jnp.int32)
    n_used = (pends[-1:] // MOE_BLOCK).astype(jnp.int32)
    pstart_col = jnp.broadcast_to(pstarts.astype(F32)[:, None], (E, LANES))

    slots, w8 = _compact(w_dense, pos_dense, pstart_col)
    t = _pick_tile(n_rows, 128, LANES)
    n_tiles = n_rows // t
    slots_tiled = slots.reshape(TOP_K, n_tiles, t).transpose(1, 0, 2).reshape(n_tiles, 1, TOP_K * t)
    xs = _dispatch(h, slots_tiled, (pstarts + counts).astype(jnp.int32), pends.astype(jnp.int32),
                   n_slots, t)
    ys = _experts(xs, w1, w3, w2, layer, block_e, n_used)
    if drop_rows == t:
        per_seq = seq_rows // t
        out_rows = n_rows - (n_rows // seq_rows) * t

        def out_map(i):
            return (i // per_seq) * (per_seq - 1) + jnp.maximum(i % per_seq - 1, 0)
    else:
        out_rows, out_map = n_rows, (lambda i: i)
    return _combine(h, ys, slots_tiled, w8.T, ws1, ws3, ws2, g, b, t, out_rows, out_map)


def kernel(x, meta_tokens, gla_w_in, gla_w_gate_up, gla_b_gate, gla_norm_g, gla_w_out,
           conv_w_in, conv_w, conv_w_out, ln1_g, ln1_b, router_w, router_bias,
           exp_w1, exp_w3, exp_w2, shared_w1, shared_w3, shared_w2, ln2_g, ln2_b):
    batch, seq, d = x.shape
    H = GLA_HEADS
    dk = d // 2 // H
    n_qkvr = 2 * H * dk + 2 * d
    rank = gla_w_in.shape[2] - n_qkvr
    seq_rows = -(-(N_META + seq) // ROW_ALIGN) * ROW_ALIGN
    pad_rows = seq_rows - N_META - seq
    n_rows = batch * seq_rows

    meta = jnp.broadcast_to(meta_tokens.astype(x.dtype)[None], (batch, N_META, d))
    h = jnp.concatenate([jnp.zeros((batch, pad_rows, d), x.dtype), meta, x], axis=1)
    h = h.reshape(n_rows, d)
    hb = h.astype(BF16)

    def row(v):
        return v.astype(F32)[None, :]

    for i in range(DEPTH):
        jm = i // 2
        if i % 2 == 0:
            w_in = gla_w_in[jm]
            w_qkvr = w_in[:, :n_qkvr].astype(BF16)
            w_gate = jnp.pad(w_in[:, n_qkvr:], ((0, 0), (0, GATE_PAD - rank))).astype(BF16)
            wg_up = jnp.pad(gla_w_gate_up[jm], ((0, GATE_PAD - rank), (0, 0))).astype(BF16)
            qkvr = _matmul(hb, w_qkvr, BF16, 1664, 768, "gla_in_proj")
            glow = _matmul(hb, w_gate, F32, 1664, LANES, "gla_gate_proj")
            mix = _gla(qkvr, glow, wg_up, row(gla_b_gate[jm]), row(gla_norm_g[jm]),
                       batch=batch, seq_rows=seq_rows, d_model=d)
            w_out = gla_w_out[jm].astype(BF16)
        else:
            mix = _conv_mix(hb, conv_w_in[jm].astype(BF16), conv_w[jm].astype(F32),
                            seq_rows=seq_rows, pad_rows=pad_rows)
            w_out = conv_w_out[jm].astype(BF16)
        h, logits_t = _proj_ln_router(mix, w_out, h, row(ln1_g[i]), row(ln1_b[i]),
                                      router_w[i].T.astype(BF16))
        drop = pad_rows + N_META if i == DEPTH - 1 else 0
        h, hb = _moe_ln(h, logits_t, router_bias[i], exp_w1, exp_w3, exp_w2, i,
                        shared_w1[i].astype(BF16), shared_w3[i].astype(BF16),
                        shared_w2[i].astype(BF16), row(ln2_g[i]), row(ln2_b[i]), drop, seq_rows)
    if h.shape[0] == batch * seq:
        return h.reshape(batch, seq, d)
    return h.reshape(batch, seq_rows, d)[:, pad_rows + N_META:]
```

```python
import functools

import jax
import jax.numpy as jnp
from jax import lax
from jax.experimental import pallas as pl
from jax.experimental.pallas import tpu as pltpu

N_META = 16
GLA_HEADS = 4
GLA_GATE_TAU = 16.0
CONV_WIDTH = 3
N_EXPERTS = 64
TOP_K = 8
N_GROUPS = 8
TOPK_GROUPS = 4
ROUTED_SCALE = 2.5
LN_EPS = 1e-5
RMS_EPS = 1e-6
DEPTH = 2
DN_ALPHA = (2 * DEPTH) ** 0.25

LANES = 128
SUBLANES = 8
ROW_ALIGN = 128
GLA_CHUNK = 64
GLA_SUB = 16
GLA_HEADS_PER_STEP = 4
GATE_PAD = LANES
MOE_BLOCK = 512
VMEM_LIMIT = 56 * 1024 * 1024

F32 = jnp.float32
BF16 = jnp.bfloat16


def _pick_tile(n, target, mult):
    best = None
    for t in range(mult, min(n, target) + 1, mult):
        if n % t == 0:
            best = t
    assert best is not None, (n, target, mult)
    return best


def _cparams(sem):
    return pltpu.CompilerParams(dimension_semantics=sem, vmem_limit_bytes=VMEM_LIMIT)


def _mm_kernel(x_ref, w_ref, o_ref):
    o_ref[...] = jnp.dot(x_ref[...], w_ref[...], preferred_element_type=F32).astype(o_ref.dtype)


def _matmul(x, w, out_dtype, tm_target, tn_target, name):
    m, k = x.shape
    n = w.shape[1]
    tm = _pick_tile(m, tm_target, 16)
    tn = _pick_tile(n, tn_target, LANES)
    return pl.pallas_call(
        _mm_kernel,
        grid=(m // tm, n // tn),
        in_specs=[pl.BlockSpec((tm, k), lambda i, j: (i, 0)),
                  pl.BlockSpec((k, tn), lambda i, j: (0, j))],
        out_specs=pl.BlockSpec((tm, tn), lambda i, j: (i, j)),
        out_shape=jax.ShapeDtypeStruct((m, n), out_dtype),
        compiler_params=_cparams(("parallel", "arbitrary")),
        name=name,
    )(x, w)


def _gla_kernel(q_ref, k_ref, v_ref, r_ref, gl_ref, wg_ref, bg_ref, ng_ref, o_ref, st_ref,
                *, n_chunks, dk, dv, heads):
    C, S = GLA_CHUNK, GLA_SUB

    @pl.when(pl.program_id(2) == 0)
    def _():
        st_ref[...] = jnp.zeros_like(st_ref)

    ri = lax.broadcasted_iota(jnp.int32, (C, C), 0)
    ci = lax.broadcasted_iota(jnp.int32, (C, C), 1)
    tri = (ri >= ci).astype(BF16)
    rs = lax.broadcasted_iota(jnp.int32, (S, S), 0)
    cs = lax.broadcasted_iota(jnp.int32, (S, S), 1)
    nt = (((1,), (1,)), ((), ()))
    tn = (((0,), (0,)), ((), ()))
    half = S // 2

    def chunk(c, carry):
        r0 = pl.multiple_of(c * C, C)
        rows_c = pl.ds(r0, C)
        hs = range(heads)
        kcol = [pl.ds(hd * dk, dk) for hd in hs]
        vcol = [pl.ds(hd * dv, dv) for hd in hs]
        gl = gl_ref[rows_c, :].astype(BF16)
        q = [q_ref[rows_c, kcol[hd]].astype(F32) * (dk ** -0.5) for hd in hs]
        kk = [k_ref[rows_c, kcol[hd]].astype(F32) for hd in hs]
        vv = [v_ref[rows_c, vcol[hd]] for hd in hs]
        st = [st_ref[hd] for hd in hs]
        z = [jnp.dot(gl, wg_ref[:, kcol[hd]], preferred_element_type=F32) + bg_ref[:, kcol[hd]]
             for hd in hs]
        b = []
        for hd in hs:
            la = ((jnp.minimum(z[hd], 0.0) - jnp.log(1.0 + jnp.exp(-jnp.abs(z[hd]))))
                  * (1.0 / GLA_GATE_TAU))
            h1 = la.astype(BF16)
            e1 = la - h1.astype(F32)
            h2 = e1.astype(BF16)
            h3 = (e1 - h2.astype(F32)).astype(BF16)
            b3 = jnp.dot(tri, jnp.concatenate([h1, h2, h3], axis=1), preferred_element_type=F32)
            b.append(b3[:, :dk] + b3[:, dk:2 * dk] + b3[:, 2 * dk:])

        o_inter, st_new, off = [], [], []
        for hd in hs:
            qe = (q[hd] * jnp.exp(b[hd])).astype(BF16)
            o_inter.append(lax.dot_general(qe, st[hd].astype(BF16), nt, preferred_element_type=F32))
            bl = b[hd][C - 1:C]
            khat = (kk[hd] * jnp.exp(bl - b[hd])).astype(BF16)
            upd = lax.dot_general(vv[hd], khat, tn, preferred_element_type=F32)
            st_new.append(st[hd] * jnp.exp(bl) + upd)
            offs = []
            for i in range(1, C // S):
                lo = i * S
                bref = b[hd][lo - 1:lo]
                qt = (q[hd][lo:lo + S] * jnp.exp(b[hd][lo:lo + S] - bref)).astype(BF16)
                kt = (kk[hd][:lo] * jnp.exp(bref - b[hd][:lo])).astype(BF16)
                offs.append(lax.dot_general(qt, kt, nt, preferred_element_type=F32))
            off.append(offs)

        dmats = []
        for hd in hs:
            blocks = []
            for i in range(C // S):
                lo = i * S
                qs = q[hd][lo:lo + S]
                bs = b[hd][lo:lo + S]
                dmat = jnp.zeros((S, S), F32)
                for j in range(S):
                    kj = kk[hd][lo + j:lo + j + 1]
                    bj = b[hd][lo + j:lo + j + 1]
                    if j < half:
                        p = qs * kj * jnp.exp(jnp.minimum(bs - bj, 0.0))
                        a = jnp.sum(p, axis=-1, keepdims=True)
                    else:
                        p = qs[half:] * kj * jnp.exp(jnp.minimum(bs[half:] - bj, 0.0))
                        a = jnp.concatenate([jnp.zeros((half, 1), F32),
                                             jnp.sum(p, axis=-1, keepdims=True)], axis=0)
                    dmat = jnp.where(cs == j, a, dmat)
                blocks.append(jnp.where(rs >= cs, dmat, 0.0).astype(BF16))
            dmats.append(blocks)

        for hd in hs:
            rows = []
            for i in range(C // S):
                lo = i * S
                o_i = jnp.dot(dmats[hd][i], vv[hd][lo:lo + S], preferred_element_type=F32)
                if i > 0:
                    o_i = o_i + jnp.dot(off[hd][i - 1].astype(BF16), vv[hd][:lo],
                                        preferred_element_type=F32)
                rows.append(o_i)
            o = o_inter[hd] + jnp.concatenate(rows, axis=0)
            ms = jnp.mean(o * o, axis=-1, keepdims=True)
            r = r_ref[rows_c, vcol[hd]].astype(F32)
            y = o * lax.rsqrt(ms + RMS_EPS) * ng_ref[:, vcol[hd]] * (r * jax.nn.sigmoid(r))
            st_ref[hd] = st_new[hd]
            o_ref[rows_c, vcol[hd]] = y.astype(o_ref.dtype)
        return carry

    lax.fori_loop(0, n_chunks, chunk, 0)


def _gla(qkvr, glow, wg, bg, ng, *, batch, seq_rows, d_model):
    H, hps = GLA_HEADS, GLA_HEADS_PER_STEP
    dk = d_model // 2 // H
    dv = d_model // H
    n_rows = batch * seq_rows
    rblk = _pick_tile(seq_rows, 640, GLA_CHUNK)
    steps = seq_rows // rblk
    groups = H // hps
    off_k, off_v, off_r = groups, (2 * H * dk) // (hps * dv), (2 * H * dk) // (hps * dv) + groups

    def rowmap(off):
        return lambda b, h, i: (b * steps + i, off + h)

    return pl.pallas_call(
        functools.partial(_gla_kernel, n_chunks=rblk // GLA_CHUNK, dk=dk, dv=dv, heads=hps),
        grid=(batch, groups, steps),
        in_specs=[pl.BlockSpec((rblk, hps * dk), rowmap(0)),
                  pl.BlockSpec((rblk, hps * dk), rowmap(off_k)),
                  pl.BlockSpec((rblk, hps * dv), rowmap(off_v)),
                  pl.BlockSpec((rblk, hps * dv), rowmap(off_r)),
                  pl.BlockSpec((rblk, GATE_PAD), lambda b, h, i: (b * steps + i, 0)),
                  pl.BlockSpec((GATE_PAD, hps * dk), lambda b, h, i: (0, h)),
                  pl.BlockSpec((1, hps * dk), lambda b, h, i: (0, h)),
                  pl.BlockSpec((1, hps * dv), lambda b, h, i: (0, h))],
        out_specs=pl.BlockSpec((rblk, hps * dv), lambda b, h, i: (b * steps + i, h)),
        out_shape=jax.ShapeDtypeStruct((n_rows, H * dv), BF16),
        scratch_shapes=[pltpu.VMEM((hps, dv, dk), F32)],
        compiler_params=_cparams(("parallel", "parallel", "arbitrary")),
        name="gla_chunks",
    )(qkvr, qkvr, qkvr, qkvr, glow, wg, bg, ng)


def _conv_kernel(x_ref, wb_ref, wc_ref, wh_ref, cw_ref, o_ref, ubuf_ref, carry_ref,
                 *, seq_rows, pad_rows, tm):
    i = pl.program_id(0)
    j = pl.program_id(1)
    x = x_ref[...]
    bg = jnp.dot(x, wb_ref[...], preferred_element_type=F32)
    cg = jnp.dot(x, wc_ref[...], preferred_element_type=F32)
    hh = jnp.dot(x, wh_ref[...], preferred_element_type=F32)
    row = i * tm + lax.broadcasted_iota(jnp.int32, (tm, 1), 0)
    u = jnp.where(row % seq_rows >= pad_rows, cg * hh, 0.0)

    @pl.when(i == 0)
    def _():
        carry_ref[j] = jnp.zeros(carry_ref.shape[1:], F32)

    ubuf_ref[pl.ds(0, SUBLANES), :] = carry_ref[j]
    ubuf_ref[pl.ds(SUBLANES, tm), :] = u
    carry_ref[j] = u[tm - SUBLANES:]
    cw = cw_ref[...]
    conv = u * cw[CONV_WIDTH - 1:CONV_WIDTH]
    for s in range(1, CONV_WIDTH):
        conv = conv + ubuf_ref[pl.ds(SUBLANES - s, tm), :] * cw[CONV_WIDTH - 1 - s:CONV_WIDTH - s]
    o_ref[...] = (bg * conv).astype(o_ref.dtype)


def _conv_mix(xb, w_in, conv_w, *, seq_rows, pad_rows):
    n_rows, d = xb.shape
    tm = _pick_tile(n_rows, 1024, SUBLANES)
    tn = _pick_tile(d, 512, LANES)
    nj = d // tn
    return pl.pallas_call(
        functools.partial(_conv_kernel, seq_rows=seq_rows, pad_rows=pad_rows, tm=tm),
        grid=(n_rows // tm, nj),
        in_specs=[pl.BlockSpec((tm, d), lambda i, j: (i, 0)),
                  pl.BlockSpec((d, tn), lambda i, j: (0, j)),
                  pl.BlockSpec((d, tn), lambda i, j: (0, nj + j)),
                  pl.BlockSpec((d, tn), lambda i, j: (0, 2 * nj + j)),
                  pl.BlockSpec((CONV_WIDTH, tn), lambda i, j: (0, j))],
        out_specs=pl.BlockSpec((tm, tn), lambda i, j: (i, j)),
        out_shape=jax.ShapeDtypeStruct((n_rows, d), BF16),
        scratch_shapes=[pltpu.VMEM((tm + SUBLANES, tn), F32),
                        pltpu.VMEM((nj, SUBLANES, tn), F32)],
        compiler_params=_cparams(("arbitrary", "arbitrary")),
        name="conv_mix",
    )(xb, w_in, w_in, w_in, conv_w)


def _layer_norm(y, g, b):
    mu = jnp.mean(y, axis=-1, keepdims=True)
    yc = y - mu
    var = jnp.mean(yc * yc, axis=-1, keepdims=True)
    return yc * lax.rsqrt(var + LN_EPS) * g + b


def _proj_ln_kernel(a_ref, w_ref, h_ref, g_ref, b_ref, rw_ref, o_ref, lg_ref):
    y = jnp.dot(a_ref[...], w_ref[...], preferred_element_type=F32) + DN_ALPHA * h_ref[...]
    h = _layer_norm(y, g_ref[...], b_ref[...])
    o_ref[...] = h
    lg_ref[...] = lax.dot_general(rw_ref[...], h.astype(BF16), (((1,), (1,)), ((), ())),
                                  preferred_element_type=F32)


def _proj_ln_router(a, w, h, g, b, rw_t):
    n_rows, kin = a.shape
    d = w.shape[1]
    e = rw_t.shape[0]
    tm = _pick_tile(n_rows, 512, LANES)
    return pl.pallas_call(
        _proj_ln_kernel,
        grid=(n_rows // tm,),
        in_specs=[pl.BlockSpec((tm, kin), lambda i: (i, 0)),
                  pl.BlockSpec((kin, d), lambda i: (0, 0)),
                  pl.BlockSpec((tm, d), lambda i: (i, 0)),
                  pl.BlockSpec((1, d), lambda i: (0, 0)),
                  pl.BlockSpec((1, d), lambda i: (0, 0)),
                  pl.BlockSpec((e, d), lambda i: (0, 0))],
        out_specs=[pl.BlockSpec((tm, d), lambda i: (i, 0)),
                   pl.BlockSpec((e, tm), lambda i: (0, i))],
        out_shape=[jax.ShapeDtypeStruct((n_rows, d), F32),
                   jax.ShapeDtypeStruct((e, n_rows), F32)],
        compiler_params=_cparams(("parallel",)),
        name="proj_ln_router",
    )(a, w, h, g, b, rw_t)


def _beats(cand, cand_idx, ref, ref_idx):
    return (cand > ref) | ((cand == ref) & (cand_idx < ref_idx))


def _route_kernel(lg_ref, bias_ref, w_ref, pos_ref, cnt_ref, run_ref, *, t):
    E, G = N_EXPERTS, N_GROUPS
    gs = E // G

    @pl.when(pl.program_id(0) == 0)
    def _():
        run_ref[...] = jnp.zeros_like(run_ref)

    s = jax.nn.sigmoid(lg_ref[...])
    c = s + bias_ref[...][:, :1]
    sub = lax.broadcasted_iota(jnp.int32, (gs, t), 0)

    grp_rows = []
    for g in range(G):
        cg = c[g * gs:(g + 1) * gs]
        rank = jnp.zeros((gs, t), jnp.int32)
        for m in range(gs):
            rank = rank + _beats(cg[m:m + 1], m, cg, sub).astype(jnp.int32)
        grp_rows.append(jnp.sum(jnp.where(rank < 2, cg, 0.0), axis=0, keepdims=True))
    gidx = lax.broadcasted_iota(jnp.int32, (G, t), 0)
    gscore = jnp.zeros((G, t), F32)
    for g in range(G):
        gscore = jnp.where(gidx == g, grp_rows[g], gscore)
    grank = jnp.zeros((G, t), jnp.int32)
    for m in range(G):
        grank = grank + _beats(gscore[m:m + 1], m, gscore, gidx).astype(jnp.int32)
    gsel = grank < TOPK_GROUPS

    masked = jnp.concatenate(
        [jnp.where(gsel[g:g + 1], c[g * gs:(g + 1) * gs], -jnp.inf) for g in range(G)], axis=0)
    eidx = lax.broadcasted_iota(jnp.int32, (E, t), 0)
    rank = jnp.zeros((E, t), jnp.int32)
    for m in range(E):
        rank = rank + _beats(masked[m:m + 1], m, masked, eidx).astype(jnp.int32)
    sel = rank < TOP_K
    gate = jnp.where(sel, s, 0.0)
    w_ref[...] = gate / jnp.sum(gate, axis=0, keepdims=True) * ROUTED_SCALE

    li = lax.broadcasted_iota(jnp.int32, (t, t), 0)
    lj = lax.broadcasted_iota(jnp.int32, (t, t), 1)
    upper = (li <= lj).astype(BF16)
    self_ = sel.astype(F32)
    incl = jnp.dot(sel.astype(BF16), upper, preferred_element_type=F32)
    run = run_ref[...][:, :1]
    pos_ref[...] = jnp.where(sel, run + incl - self_, -1.0)
    run_new = run + jnp.sum(self_, axis=1, keepdims=True)
    run_ref[...] = jnp.broadcast_to(run_new, run_ref.shape)
    cnt_ref[...] = jnp.broadcast_to(run_new, cnt_ref.shape)


def _route(logits_t, bias_col):
    e, n_rows = logits_t.shape
    t = _pick_tile(n_rows, 512, LANES)
    return pl.pallas_call(
        functools.partial(_route_kernel, t=t),
        grid=(n_rows // t,),
        in_specs=[pl.BlockSpec((e, t), lambda i: (0, i)),
                  pl.BlockSpec((e, LANES), lambda i: (0, 0))],
        out_specs=[pl.BlockSpec((e, t), lambda i: (0, i)),
                   pl.BlockSpec((e, t), lambda i: (0, i)),
                   pl.BlockSpec((e, LANES), lambda i: (0, 0))],
        out_shape=[jax.ShapeDtypeStruct((e, n_rows), F32),
                   jax.ShapeDtypeStruct((e, n_rows), F32),
                   jax.ShapeDtypeStruct((e, LANES), F32)],
        scratch_shapes=[pltpu.VMEM((e, LANES), F32)],
        compiler_params=_cparams(("arbitrary",)),
        name="route",
    )(logits_t, bias_col)


def _compact_kernel(w_ref, pos_ref, pst_ref, slot_ref, w8_ref, *, t):
    E = N_EXPERTS
    pos = pos_ref[...]
    sel = pos >= 0.0
    ri = lax.broadcasted_iota(jnp.int32, (E, E), 0)
    ci = lax.broadcasted_iota(jnp.int32, (E, E), 1)
    below = (ci < ri).astype(BF16)
    order = jnp.dot(below, sel.astype(BF16), preferred_element_type=F32)
    slot = pst_ref[...][:, :1] + pos
    wd = w_ref[...]
    kidx = lax.broadcasted_iota(jnp.int32, (TOP_K, t), 0)
    slots = jnp.zeros((TOP_K, t), F32)
    w8 = jnp.zeros((TOP_K, t), F32)
    for k in range(TOP_K):
        m = sel & (order == float(k))
        slots = jnp.where(kidx == k, jnp.sum(jnp.where(m, slot, 0.0), axis=0, keepdims=True), slots)
        w8 = jnp.where(kidx == k, jnp.sum(jnp.where(m, wd, 0.0), axis=0, keepdims=True), w8)
    slot_ref[...] = slots.astype(jnp.int32)
    w8_ref[...] = w8


def _compact(w_dense, pos_dense, pstart_col):
    e, n_rows = w_dense.shape
    t = _pick_tile(n_rows, 512, LANES)
    return pl.pallas_call(
        functools.partial(_compact_kernel, t=t),
        grid=(n_rows // t,),
        in_specs=[pl.BlockSpec((e, t), lambda i: (0, i)),
                  pl.BlockSpec((e, t), lambda i: (0, i)),
                  pl.BlockSpec((e, LANES), lambda i: (0, 0))],
        out_specs=[pl.BlockSpec((TOP_K, t), lambda i: (0, i)),
                   pl.BlockSpec((TOP_K, t), lambda i: (0, i))],
        out_shape=[jax.ShapeDtypeStruct((TOP_K, n_rows), jnp.int32),
                   jax.ShapeDtypeStruct((TOP_K, n_rows), F32)],
        compiler_params=_cparams(("parallel",)),
        name="compact",
    )(w_dense, pos_dense, pstart_col)


def _pack_bf16_pairs(x):
    half = x.shape[1] // 2
    lo = lax.bitcast_convert_type(x[:, :half].astype(BF16).astype(F32), jnp.uint32)
    hi = lax.bitcast_convert_type(x[:, half:].astype(BF16).astype(F32), jnp.uint32)
    return (lo >> 16) | hi


def _unpack_bf16_pairs(w):
    lo = lax.bitcast_convert_type(w << 16, F32)
    hi = lax.bitcast_convert_type(w & jnp.uint32(0xFFFF0000), F32)
    return lo, hi


def _dispatch_kernel(pad_lo_ref, pad_hi_ref, slot_ref, h_ref, xs_ref, hp_ref, zero_ref, sem, zsem,
                     *, t):
    i = pl.program_id(0)
    cur = i % 2
    hp_ref[cur] = _pack_bf16_pairs(h_ref[...])

    def issue(tok, carry):
        src = hp_ref.at[cur, pl.ds(tok, 1), :]
        for k in range(TOP_K):
            pltpu.make_async_copy(src, xs_ref.at[pl.ds(slot_ref[k * t + tok], 1), :],
                                  sem.at[cur]).start(priority=k % 2)
        return carry

    lax.fori_loop(0, t, issue, 0, unroll=2)

    @pl.when(i == 0)
    def _():
        zero_ref[...] = jnp.zeros_like(zero_ref)

        def zero_copy(p):
            return pltpu.make_async_copy(zero_ref, xs_ref.at[pl.ds(p, 1), :], zsem)

        def per_expert(step):
            def body(e, carry):
                lax.fori_loop(pad_lo_ref[e], pad_hi_ref[e], step, 0)
                return carry
            return body

        def zissue(p, c):
            zero_copy(p).start()
            return c

        def zwait(p, c):
            zero_copy(p).wait()
            return c

        lax.fori_loop(0, N_EXPERTS, per_expert(zissue), 0)
        lax.fori_loop(0, N_EXPERTS, per_expert(zwait), 0)

    def wait_tile(buf):
        for k in range(TOP_K):
            pltpu.make_async_copy(hp_ref.at[buf], xs_ref.at[pl.ds(0, t), :], sem.at[buf]).wait()

    @pl.when(i > 0)
    def _():
        wait_tile(1 - cur)

    @pl.when(i == pl.num_programs(0) - 1)
    def _():
        wait_tile(cur)


def _dispatch(h, slots_tiled, pad_lo, pad_hi, n_slots, t):
    n_rows, d = h.shape
    n_tiles = n_rows // t
    return pl.pallas_call(
        functools.partial(_dispatch_kernel, t=t),
        grid_spec=pltpu.PrefetchScalarGridSpec(
            num_scalar_prefetch=2,
            grid=(n_tiles,),
            in_specs=[pl.BlockSpec((None, None, TOP_K * t), lambda i, lo, hi: (i, 0, 0),
                                   memory_space=pltpu.SMEM),
                      pl.BlockSpec((t, d), lambda i, lo, hi: (i, 0))],
            out_specs=pl.BlockSpec(memory_space=pl.ANY),
            scratch_shapes=[pltpu.VMEM((2, t, d // 2), jnp.uint32),
                            pltpu.VMEM((1, d // 2), jnp.uint32),
                            pltpu.SemaphoreType.DMA((2,)),
                            pltpu.SemaphoreType.DMA(())]),
        out_shape=jax.ShapeDtypeStruct((n_slots, d // 2), jnp.uint32),
        compiler_params=_cparams(("arbitrary",)),
        name="dispatch",
    )(pad_lo, pad_hi, slots_tiled, h)


def _swiglu(x, w1, w3, w2):
    a = jnp.dot(x, w1, preferred_element_type=F32)
    g = jnp.dot(x, w3, preferred_element_type=F32)
    hmid = (a * jax.nn.sigmoid(a) * g).astype(BF16)
    return jnp.dot(hmid, w2, preferred_element_type=F32)


def _expert_kernel(be_ref, nu_ref, x_ref, w1_ref, w3_ref, w2_ref, y_ref, w1b_ref, w3b_ref, w2b_ref):
    b = pl.program_id(0)

    @pl.when(b < nu_ref[0])
    def _():
        @pl.when((b == 0) | (be_ref[b] != be_ref[jnp.maximum(b - 1, 0)]))
        def _():
            w1b_ref[...] = w1_ref[...].astype(BF16)
            w3b_ref[...] = w3_ref[...].astype(BF16)
            w2b_ref[...] = w2_ref[...].astype(BF16)

        lo, hi = _unpack_bf16_pairs(x_ref[...])
        x = jnp.concatenate([lo.astype(BF16), hi.astype(BF16)], axis=1)
        y_ref[...] = _pack_bf16_pairs(_swiglu(x, w1b_ref[...], w3b_ref[...], w2b_ref[...]))


def _experts(xs, w1, w3, w2, layer, block_e, n_used):
    n_slots, dh = xs.shape
    d, f = w1.shape[2], w1.shape[3]
    nb = n_slots // MOE_BLOCK

    def blk(b, be, nu):
        return jnp.minimum(b, nu[0] - 1)

    def wmap(b, be, nu):
        return (layer, be[blk(b, be, nu)], 0, 0)

    return pl.pallas_call(
        _expert_kernel,
        grid_spec=pltpu.PrefetchScalarGridSpec(
            num_scalar_prefetch=2,
            grid=(nb,),
            in_specs=[pl.BlockSpec((MOE_BLOCK, dh), lambda b, be, nu: (blk(b, be, nu), 0)),
                      pl.BlockSpec((None, None, d, f), wmap),
                      pl.BlockSpec((None, None, d, f), wmap),
                      pl.BlockSpec((None, None, f, d), wmap)],
            out_specs=pl.BlockSpec((MOE_BLOCK, dh), lambda b, be, nu: (blk(b, be, nu), 0)),
            scratch_shapes=[pltpu.VMEM((d, f), BF16), pltpu.VMEM((d, f), BF16),
                            pltpu.VMEM((f, d), BF16)]),
        out_shape=jax.ShapeDtypeStruct((n_slots, dh), jnp.uint32),
        compiler_params=_cparams(("arbitrary",)),
        name="experts",
    )(block_e, n_used, xs, w1, w3, w2)


def _combine_kernel(slot_ref, nslot_ref, h_ref, w8_ref, ws1_ref, ws3_ref, ws2_ref, g_ref, b_ref,
                    ys_ref, o_ref, ob_ref, ybuf_ref, sem, *, t):
    i = pl.program_id(0)
    last = pl.num_programs(0) - 1

    def row_copy(s_ref, buf, k, tok):
        return pltpu.make_async_copy(ys_ref.at[pl.ds(s_ref[k * t + tok], 1), :],
                                     ybuf_ref.at[buf, k, pl.ds(tok, 1), :], sem.at[buf])

    def wait_tile(buf):
        for k in range(TOP_K):
            pltpu.make_async_copy(ys_ref.at[pl.ds(0, t), :], ybuf_ref.at[buf, k], sem.at[buf]).wait()

    @pl.when(i == 0)
    def _():
        def issue(tok, carry):
            for k in range(TOP_K):
                row_copy(slot_ref, 0, k, tok).start(priority=k % 2)
            return carry

        lax.fori_loop(0, t, issue, 0, unroll=2)

    group = t // TOP_K

    def reduce_tile(cur):
        nxt = 1 - cur

        def issue_group(g):
            for tok in range(g * group, (g + 1) * group):
                for k in range(TOP_K):
                    row_copy(nslot_ref, nxt, k, tok).start(priority=k % 2)

        wait_tile(cur)
        h = h_ref[...]
        issue_group(0)
        shared = _swiglu(h.astype(BF16), ws1_ref[...], ws3_ref[...], ws2_ref[...])
        w8 = w8_ref[...]
        acc_lo, acc_hi = None, None
        for k in range(TOP_K):
            if k + 1 < TOP_K:
                issue_group(k + 1)
            lo, hi = _unpack_bf16_pairs(ybuf_ref[cur, k])
            wk = w8[:, k:k + 1]
            acc_lo = lo * wk if acc_lo is None else acc_lo + lo * wk
            acc_hi = hi * wk if acc_hi is None else acc_hi + hi * wk
        acc = jnp.concatenate([acc_lo, acc_hi], axis=1)
        y = DN_ALPHA * h + (acc + shared)
        out = _layer_norm(y, g_ref[...], b_ref[...])
        o_ref[...] = out
        ob_ref[...] = out.astype(BF16)

        @pl.when(i == last)
        def _():
            wait_tile(nxt)

    for parity in range(2):
        @pl.when(i % 2 == parity)
        def _():
            reduce_tile(parity)


def _combine(h, ys, slots_tiled, w8_t, ws1, ws3, ws2, g, b, t, out_rows, out_map):
    n_rows, d = h.shape
    f = ws1.shape[1]
    n_tiles = n_rows // t
    return pl.pallas_call(
        functools.partial(_combine_kernel, t=t),
        grid=(n_tiles,),
        in_specs=[pl.BlockSpec((None, None, TOP_K * t), lambda i: (i, 0, 0),
                               memory_space=pltpu.SMEM),
                  pl.BlockSpec((None, None, TOP_K * t),
                               lambda i: (jnp.minimum(i + 1, n_tiles - 1), 0, 0),
                               memory_space=pltpu.SMEM),
                  pl.BlockSpec((t, d), lambda i: (i, 0)),
                  pl.BlockSpec((t, TOP_K), lambda i: (i, 0)),
                  pl.BlockSpec((d, f), lambda i: (0, 0)),
                  pl.BlockSpec((d, f), lambda i: (0, 0)),
                  pl.BlockSpec((f, d), lambda i: (0, 0)),
                  pl.BlockSpec((1, d), lambda i: (0, 0)),
                  pl.BlockSpec((1, d), lambda i: (0, 0)),
                  pl.BlockSpec(memory_space=pl.ANY)],
        out_specs=[pl.BlockSpec((t, d), lambda i: (out_map(i), 0)),
                   pl.BlockSpec((t, d), lambda i: (i, 0))],
        out_shape=[jax.ShapeDtypeStruct((out_rows, d), F32),
                   jax.ShapeDtypeStruct((n_rows, d), BF16)],
        scratch_shapes=[pltpu.VMEM((2, TOP_K, t, d // 2), jnp.uint32),
                        pltpu.SemaphoreType.DMA((2,))],
        compiler_params=_cparams(("arbitrary",)),
        name="combine_ln",
    )(slots_tiled, slots_tiled, h, w8_t, ws1, ws3, ws2, g, b, ys)


def _moe_ln(h, logits_t, router_bias, w1, w3, w2, layer, ws1, ws3, ws2, g, b, drop_rows, seq_rows):
    n_rows, d = h.shape
    E = N_EXPERTS
    bias_col = jnp.broadcast_to(router_bias.astype(F32)[:, None], (E, LANES))
    w_dense, pos_dense, cnt = _route(logits_t, bias_col)

    counts = cnt[:, 0].astype(jnp.int32)
    pcounts = (counts + MOE_BLOCK - 1) // MOE_BLOCK * MOE_BLOCK
    pends = jnp.cumsum(pcounts)
    pstarts = pends - pcounts
    n_blocks = -(-(n_rows * TOP_K) // MOE_BLOCK) + E
    n_slots = n_blocks * MOE_BLOCK
    block_lo = jnp.arange(n_blocks, dtype=jnp.int32) * MOE_BLOCK
    block_e = jnp.minimum(jnp.sum(pends[None, :] <= block_lo[:, None], axis=1), E - 1).astype(jnp.int32)
    n_used = (pends[-1:] // MOE_BLOCK).astype(jnp.int32)
    pstart_col = jnp.broadcast_to(pstarts.astype(F32)[:, None], (E, LANES))

    slots, w8 = _compact(w_dense, pos_dense, pstart_col)
    t = _pick_tile(n_rows, 128, LANES)
    n_tiles = n_rows // t
    slots_tiled = slots.reshape(TOP_K, n_tiles, t).transpose(1, 0, 2).reshape(n_tiles, 1, TOP_K * t)
    xs = _dispatch(h, slots_tiled, (pstarts + counts).astype(jnp.int32), pends.astype(jnp.int32),
                   n_slots, t)
    ys = _experts(xs, w1, w3, w2, layer, block_e, n_used)
    if drop_rows == t:
        per_seq = seq_rows // t
        out_rows = n_rows - (n_rows // seq_rows) * t

        def out_map(i):
            return (i // per_seq) * (per_seq - 1) + jnp.maximum(i % per_seq - 1, 0)
    else:
        out_rows, out_map = n_rows, (lambda i: i)
    return _combine(h, ys, slots_tiled, w8.T, ws1, ws3, ws2, g, b, t, out_rows, out_map)


def kernel(x, meta_tokens, gla_w_in, gla_w_gate_up, gla_b_gate, gla_norm_g, gla_w_out,
           conv_w_in, conv_w, conv_w_out, ln1_g, ln1_b, router_w, router_bias,
           exp_w1, exp_w3, exp_w2, shared_w1, shared_w3, shared_w2, ln2_g, ln2_b):
    batch, seq, d = x.shape
    H = GLA_HEADS
    dk = d // 2 // H
    n_qkvr = 2 * H * dk + 2 * d
    rank = gla_w_in.shape[2] - n_qkvr
    seq_rows = -(-(N_META + seq) // ROW_ALIGN) * ROW_ALIGN
    pad_rows = seq_rows - N_META - seq
    n_rows = batch * seq_rows

    meta = jnp.broadcast_to(meta_tokens.astype(x.dtype)[None], (batch, N_META, d))
    h = jnp.concatenate([jnp.zeros((batch, pad_rows, d), x.dtype), meta, x], axis=1)
    h = h.reshape(n_rows, d)
    hb = h.astype(BF16)

    def row(v):
        return v.astype(F32)[None, :]

    for i in range(DEPTH):
        jm = i // 2
        if i % 2 == 0:
            w_in = gla_w_in[jm]
            w_qkvr = w_in[:, :n_qkvr].astype(BF16)
            w_gate = jnp.pad(w_in[:, n_qkvr:], ((0, 0), (0, GATE_PAD - rank))).astype(BF16)
            wg_up = jnp.pad(gla_w_gate_up[jm], ((0, GATE_PAD - rank), (0, 0))).astype(BF16)
            qkvr = _matmul(hb, w_qkvr, BF16, 1664, 768, "gla_in_proj")
            glow = _matmul(hb, w_gate, F32, 1664, LANES, "gla_gate_proj")
            mix = _gla(qkvr, glow, wg_up, row(gla_b_gate[jm]), row(gla_norm_g[jm]),
                       batch=batch, seq_rows=seq_rows, d_model=d)
            w_out = gla_w_out[jm].astype(BF16)
        else:
            mix = _conv_mix(hb, conv_w_in[jm].astype(BF16), conv_w[jm].astype(F32),
                            seq_rows=seq_rows, pad_rows=pad_rows)
            w_out = conv_w_out[jm].astype(BF16)
        h, logits_t = _proj_ln_router(mix, w_out, h, row(ln1_g[i]), row(ln1_b[i]),
                                      router_w[i].T.astype(BF16))
        drop = pad_rows + N_META if i == DEPTH - 1 else 0
        h, hb = _moe_ln(h, logits_t, router_bias[i], exp_w1, exp_w3, exp_w2, i,
                        shared_w1[i].astype(BF16), shared_w3[i].astype(BF16),
                        shared_w2[i].astype(BF16), row(ln2_g[i]), row(ln2_b[i]), drop, seq_rows)
    if h.shape[0] == batch * seq:
        return h.reshape(batch, seq, d)
    return h.reshape(batch, seq_rows, d)[:, pad_rows + N_META:]
```

```python
import functools

import jax
import jax.numpy as jnp
from jax import lax
from jax.experimental import pallas as pl
from jax.experimental.pallas import tpu as pltpu

N_META = 16
GLA_HEADS = 4
GLA_GATE_TAU = 16.0
CONV_WIDTH = 3
N_EXPERTS = 64
TOP_K = 8
N_GROUPS = 8
TOPK_GROUPS = 4
ROUTED_SCALE = 2.5
LN_EPS = 1e-5
RMS_EPS = 1e-6
DEPTH = 2
DN_ALPHA = (2 * DEPTH) ** 0.25

LANES = 128
SUBLANES = 8
ROW_ALIGN = 128
GLA_CHUNK = 64
GLA_SUB = 16
GLA_HEADS_PER_STEP = 4
GATE_PAD = LANES
MOE_BLOCK = 512
VMEM_LIMIT = 56 * 1024 * 1024

F32 = jnp.float32
BF16 = jnp.bfloat16


def _pick_tile(n, target, mult):
    best = None
    for t in range(mult, min(n, target) + 1, mult):
        if n % t == 0:
            best = t
    assert best is not None, (n, target, mult)
    return best


def _cparams(sem):
    return pltpu.CompilerParams(dimension_semantics=sem, vmem_limit_bytes=VMEM_LIMIT)


def _mm_kernel(x_ref, w_ref, o_ref):
    o_ref[...] = jnp.dot(x_ref[...], w_ref[...], preferred_element_type=F32).astype(o_ref.dtype)


def _matmul(x, w, out_dtype, tm_target, tn_target, name):
    m, k = x.shape
    n = w.shape[1]
    tm = _pick_tile(m, tm_target, 16)
    tn = _pick_tile(n, tn_target, LANES)
    return pl.pallas_call(
        _mm_kernel,
        grid=(m // tm, n // tn),
        in_specs=[pl.BlockSpec((tm, k), lambda i, j: (i, 0)),
                  pl.BlockSpec((k, tn), lambda i, j: (0, j))],
        out_specs=pl.BlockSpec((tm, tn), lambda i, j: (i, j)),
        out_shape=jax.ShapeDtypeStruct((m, n), out_dtype),
        compiler_params=_cparams(("parallel", "arbitrary")),
        name=name,
    )(x, w)


def _gla_kernel(q_ref, k_ref, v_ref, r_ref, gl_ref, wg_ref, bg_ref, ng_ref, o_ref, st_ref,
                *, n_chunks, dk, dv, heads):
    C, S = GLA_CHUNK, GLA_SUB

    @pl.when(pl.program_id(2) == 0)
    def _():
        st_ref[...] = jnp.zeros_like(st_ref)

    ri = lax.broadcasted_iota(jnp.int32, (C, C), 0)
    ci = lax.broadcasted_iota(jnp.int32, (C, C), 1)
    tri = (ri >= ci).astype(BF16)
    rs = lax.broadcasted_iota(jnp.int32, (S, S), 0)
    cs = lax.broadcasted_iota(jnp.int32, (S, S), 1)
    nt = (((1,), (1,)), ((), ()))
    tn = (((0,), (0,)), ((), ()))
    half = S // 2

    def chunk(c, carry):
        r0 = pl.multiple_of(c * C, C)
        rows_c = pl.ds(r0, C)
        hs = range(heads)
        kcol = [pl.ds(hd * dk, dk) for hd in hs]
        vcol = [pl.ds(hd * dv, dv) for hd in hs]
        gl = gl_ref[rows_c, :].astype(BF16)
        q = [q_ref[rows_c, kcol[hd]].astype(F32) * (dk ** -0.5) for hd in hs]
        kk = [k_ref[rows_c, kcol[hd]].astype(F32) for hd in hs]
        vv = [v_ref[rows_c, vcol[hd]] for hd in hs]
        st = [st_ref[hd] for hd in hs]
        z = [jnp.dot(gl, wg_ref[:, kcol[hd]], preferred_element_type=F32) + bg_ref[:, kcol[hd]]
             for hd in hs]
        b = []
        for hd in hs:
            la = ((jnp.minimum(z[hd], 0.0) - jnp.log(1.0 + jnp.exp(-jnp.abs(z[hd]))))
                  * (1.0 / GLA_GATE_TAU))
            h1 = la.astype(BF16)
            e1 = la - h1.astype(F32)
            h2 = e1.astype(BF16)
            h3 = (e1 - h2.astype(F32)).astype(BF16)
            b3 = jnp.dot(tri, jnp.concatenate([h1, h2, h3], axis=1), preferred_element_type=F32)
            b.append(b3[:, :dk] + b3[:, dk:2 * dk] + b3[:, 2 * dk:])

        o_inter, st_new, off = [], [], []
        for hd in hs:
            qe = (q[hd] * jnp.exp(b[hd])).astype(BF16)
            o_inter.append(lax.dot_general(qe, st[hd].astype(BF16), nt, preferred_element_type=F32))
            bl = b[hd][C - 1:C]
            khat = (kk[hd] * jnp.exp(bl - b[hd])).astype(BF16)
            upd = lax.dot_general(vv[hd], khat, tn, preferred_element_type=F32)
            st_new.append(st[hd] * jnp.exp(bl) + upd)
            offs = []
            for i in range(1, C // S):
                lo = i * S
                bref = b[hd][lo - 1:lo]
                qt = (q[hd][lo:lo + S] * jnp.exp(b[hd][lo:lo + S] - bref)).astype(BF16)
                kt = (kk[hd][:lo] * jnp.exp(bref - b[hd][:lo])).astype(BF16)
                offs.append(lax.dot_general(qt, kt, nt, preferred_element_type=F32))
            off.append(offs)

        dmats = []
        for hd in hs:
            blocks = []
            for i in range(C // S):
                lo = i * S
                qs = q[hd][lo:lo + S]
                bs = b[hd][lo:lo + S]
                dmat = jnp.zeros((S, S), F32)
                for j in range(S):
                    kj = kk[hd][lo + j:lo + j + 1]
                    bj = b[hd][lo + j:lo + j + 1]
                    if j < half:
                        p = qs * kj * jnp.exp(jnp.minimum(bs - bj, 0.0))
                        a = jnp.sum(p, axis=-1, keepdims=True)
                    else:
                        p = qs[half:] * kj * jnp.exp(jnp.minimum(bs[half:] - bj, 0.0))
                        a = jnp.concatenate([jnp.zeros((half, 1), F32),
                                             jnp.sum(p, axis=-1, keepdims=True)], axis=0)
                    dmat = jnp.where(cs == j, a, dmat)
                blocks.append(jnp.where(rs >= cs, dmat, 0.0).astype(BF16))
            dmats.append(blocks)

        for hd in hs:
            rows = []
            for i in range(C // S):
                lo = i * S
                o_i = jnp.dot(dmats[hd][i], vv[hd][lo:lo + S], preferred_element_type=F32)
                if i > 0:
                    o_i = o_i + jnp.dot(off[hd][i - 1].astype(BF16), vv[hd][:lo],
                                        preferred_element_type=F32)
                rows.append(o_i)
            o = o_inter[hd] + jnp.concatenate(rows, axis=0)
            ms = jnp.mean(o * o, axis=-1, keepdims=True)
            r = r_ref[rows_c, vcol[hd]].astype(F32)
            y = o * lax.rsqrt(ms + RMS_EPS) * ng_ref[:, vcol[hd]] * (r * jax.nn.sigmoid(r))
            st_ref[hd] = st_new[hd]
            o_ref[rows_c, vcol[hd]] = y.astype(o_ref.dtype)
        return carry

    lax.fori_loop(0, n_chunks, chunk, 0)


def _gla(qkvr, glow, wg, bg, ng, *, batch, seq_rows, d_model):
    H, hps = GLA_HEADS, GLA_HEADS_PER_STEP
    dk = d_model // 2 // H
    dv = d_model // H
    n_rows = batch * seq_rows
    rblk = _pick_tile(seq_rows, 640, GLA_CHUNK)
    steps = seq_rows // rblk
    groups = H // hps
    off_k, off_v, off_r = groups, (2 * H * dk) // (hps * dv), (2 * H * dk) // (hps * dv) + groups

    def rowmap(off):
        return lambda b, h, i: (b * steps + i, off + h)

    return pl.pallas_call(
        functools.partial(_gla_kernel, n_chunks=rblk // GLA_CHUNK, dk=dk, dv=dv, heads=hps),
        grid=(batch, groups, steps),
        in_specs=[pl.BlockSpec((rblk, hps * dk), rowmap(0)),
                  pl.BlockSpec((rblk, hps * dk), rowmap(off_k)),
                  pl.BlockSpec((rblk, hps * dv), rowmap(off_v)),
                  pl.BlockSpec((rblk, hps * dv), rowmap(off_r)),
                  pl.BlockSpec((rblk, GATE_PAD), lambda b, h, i: (b * steps + i, 0)),
                  pl.BlockSpec((GATE_PAD, hps * dk), lambda b, h, i: (0, h)),
                  pl.BlockSpec((1, hps * dk), lambda b, h, i: (0, h)),
                  pl.BlockSpec((1, hps * dv), lambda b, h, i: (0, h))],
        out_specs=pl.BlockSpec((rblk, hps * dv), lambda b, h, i: (b * steps + i, h)),
        out_shape=jax.ShapeDtypeStruct((n_rows, H * dv), BF16),
        scratch_shapes=[pltpu.VMEM((hps, dv, dk), F32)],
        compiler_params=_cparams(("parallel", "parallel", "arbitrary")),
        name="gla_chunks",
    )(qkvr, qkvr, qkvr, qkvr, glow, wg, bg, ng)


def _conv_kernel(x_ref, wb_ref, wc_ref, wh_ref, cw_ref, o_ref, ubuf_ref, carry_ref,
                 *, seq_rows, pad_rows, tm):
    i = pl.program_id(0)
    j = pl.program_id(1)
    x = x_ref[...]
    bg = jnp.dot(x, wb_ref[...], preferred_element_type=F32)
    cg = jnp.dot(x, wc_ref[...], preferred_element_type=F32)
    hh = jnp.dot(x, wh_ref[...], preferred_element_type=F32)
    row = i * tm + lax.broadcasted_iota(jnp.int32, (tm, 1), 0)
    u = jnp.where(row % seq_rows >= pad_rows, cg * hh, 0.0)

    @pl.when(i == 0)
    def _():
        carry_ref[j] = jnp.zeros(carry_ref.shape[1:], F32)

    ubuf_ref[pl.ds(0, SUBLANES), :] = carry_ref[j]
    ubuf_ref[pl.ds(SUBLANES, tm), :] = u
    carry_ref[j] = u[tm - SUBLANES:]
    cw = cw_ref[...]
    conv = u * cw[CONV_WIDTH - 1:CONV_WIDTH]
    for s in range(1, CONV_WIDTH):
        conv = conv + ubuf_ref[pl.ds(SUBLANES - s, tm), :] * cw[CONV_WIDTH - 1 - s:CONV_WIDTH - s]
    o_ref[...] = (bg * conv).astype(o_ref.dtype)


def _conv_mix(xb, w_in, conv_w, *, seq_rows, pad_rows):
    n_rows, d = xb.shape
    tm = _pick_tile(n_rows, 1024, SUBLANES)
    tn = _pick_tile(d, 512, LANES)
    nj = d // tn
    return pl.pallas_call(
        functools.partial(_conv_kernel, seq_rows=seq_rows, pad_rows=pad_rows, tm=tm),
        grid=(n_rows // tm, nj),
        in_specs=[pl.BlockSpec((tm, d), lambda i, j: (i, 0)),
                  pl.BlockSpec((d, tn), lambda i, j: (0, j)),
                  pl.BlockSpec((d, tn), lambda i, j: (0, nj + j)),
                  pl.BlockSpec((d, tn), lambda i, j: (0, 2 * nj + j)),
                  pl.BlockSpec((CONV_WIDTH, tn), lambda i, j: (0, j))],
        out_specs=pl.BlockSpec((tm, tn), lambda i, j: (i, j)),
        out_shape=jax.ShapeDtypeStruct((n_rows, d), BF16),
        scratch_shapes=[pltpu.VMEM((tm + SUBLANES, tn), F32),
                        pltpu.VMEM((nj, SUBLANES, tn), F32)],
        compiler_params=_cparams(("arbitrary", "arbitrary")),
        name="conv_mix",
    )(xb, w_in, w_in, w_in, conv_w)


def _layer_norm(y, g, b):
    mu = jnp.mean(y, axis=-1, keepdims=True)
    yc = y - mu
    var = jnp.mean(yc * yc, axis=-1, keepdims=True)
    return yc * lax.rsqrt(var + LN_EPS) * g + b


def _proj_ln_kernel(a_ref, w_ref, h_ref, g_ref, b_ref, rw_ref, o_ref, lg_ref):
    y = jnp.dot(a_ref[...], w_ref[...], preferred_element_type=F32) + DN_ALPHA * h_ref[...]
    h = _layer_norm(y, g_ref[...], b_ref[...])
    o_ref[...] = h
    lg_ref[...] = lax.dot_general(rw_ref[...], h.astype(BF16), (((1,), (1,)), ((), ())),
                                  preferred_element_type=F32)


def _proj_ln_router(a, w, h, g, b, rw_t):
    n_rows, kin = a.shape
    d = w.shape[1]
    e = rw_t.shape[0]
    tm = _pick_tile(n_rows, 512, LANES)
    return pl.pallas_call(
        _proj_ln_kernel,
        grid=(n_rows // tm,),
        in_specs=[pl.BlockSpec((tm, kin), lambda i: (i, 0)),
                  pl.BlockSpec((kin, d), lambda i: (0, 0)),
                  pl.BlockSpec((tm, d), lambda i: (i, 0)),
                  pl.BlockSpec((1, d), lambda i: (0, 0)),
                  pl.BlockSpec((1, d), lambda i: (0, 0)),
                  pl.BlockSpec((e, d), lambda i: (0, 0))],
        out_specs=[pl.BlockSpec((tm, d), lambda i: (i, 0)),
                   pl.BlockSpec((e, tm), lambda i: (0, i))],
        out_shape=[jax.ShapeDtypeStruct((n_rows, d), F32),
                   jax.ShapeDtypeStruct((e, n_rows), F32)],
        compiler_params=_cparams(("parallel",)),
        name="proj_ln_router",
    )(a, w, h, g, b, rw_t)


def _beats(cand, cand_idx, ref, ref_idx):
    return (cand > ref) | ((cand == ref) & (cand_idx < ref_idx))


def _route_kernel(lg_ref, bias_ref, w_ref, pos_ref, cnt_ref, run_ref, *, t):
    E, G = N_EXPERTS, N_GROUPS
    gs = E // G

    @pl.when(pl.program_id(0) == 0)
    def _():
        run_ref[...] = jnp.zeros_like(run_ref)

    s = jax.nn.sigmoid(lg_ref[...])
    c = s + bias_ref[...][:, :1]
    sub = lax.broadcasted_iota(jnp.int32, (gs, t), 0)

    grp_rows = []
    for g in range(G):
        cg = c[g * gs:(g + 1) * gs]
        rank = jnp.zeros((gs, t), jnp.int32)
        for m in range(gs):
            rank = rank + _beats(cg[m:m + 1], m, cg, sub).astype(jnp.int32)
        grp_rows.append(jnp.sum(jnp.where(rank < 2, cg, 0.0), axis=0, keepdims=True))
    gidx = lax.broadcasted_iota(jnp.int32, (G, t), 0)
    gscore = jnp.zeros((G, t), F32)
    for g in range(G):
        gscore = jnp.where(gidx == g, grp_rows[g], gscore)
    grank = jnp.zeros((G, t), jnp.int32)
    for m in range(G):
        grank = grank + _beats(gscore[m:m + 1], m, gscore, gidx).astype(jnp.int32)
    gsel = grank < TOPK_GROUPS

    masked = jnp.concatenate(
        [jnp.where(gsel[g:g + 1], c[g * gs:(g + 1) * gs], -jnp.inf) for g in range(G)], axis=0)
    eidx = lax.broadcasted_iota(jnp.int32, (E, t), 0)
    rank = jnp.zeros((E, t), jnp.int32)
    for m in range(E):
        rank = rank + _beats(masked[m:m + 1], m, masked, eidx).astype(jnp.int32)
    sel = rank < TOP_K
    gate = jnp.where(sel, s, 0.0)
    w_ref[...] = gate / jnp.sum(gate, axis=0, keepdims=True) * ROUTED_SCALE

    li = lax.broadcasted_iota(jnp.int32, (t, t), 0)
    lj = lax.broadcasted_iota(jnp.int32, (t, t), 1)
    upper = (li <= lj).astype(BF16)
    self_ = sel.astype(F32)
    incl = jnp.dot(sel.astype(BF16), upper, preferred_element_type=F32)
    run = run_ref[...][:, :1]
    pos_ref[...] = jnp.where(sel, run + incl - self_, -1.0)
    run_new = run + jnp.sum(self_, axis=1, keepdims=True)
    run_ref[...] = jnp.broadcast_to(run_new, run_ref.shape)
    cnt_ref[...] = jnp.broadcast_to(run_new, cnt_ref.shape)


def _route(logits_t, bias_col):
    e, n_rows = logits_t.shape
    t = _pick_tile(n_rows, 512, LANES)
    return pl.pallas_call(
        functools.partial(_route_kernel, t=t),
        grid=(n_rows // t,),
        in_specs=[pl.BlockSpec((e, t), lambda i: (0, i)),
                  pl.BlockSpec((e, LANES), lambda i: (0, 0))],
        out_specs=[pl.BlockSpec((e, t), lambda i: (0, i)),
                   pl.BlockSpec((e, t), lambda i: (0, i)),
                   pl.BlockSpec((e, LANES), lambda i: (0, 0))],
        out_shape=[jax.ShapeDtypeStruct((e, n_rows), F32),
                   jax.ShapeDtypeStruct((e, n_rows), F32),
                   jax.ShapeDtypeStruct((e, LANES), F32)],
        scratch_shapes=[pltpu.VMEM((e, LANES), F32)],
        compiler_params=_cparams(("arbitrary",)),
        name="route",
    )(logits_t, bias_col)


def _compact_kernel(w_ref, pos_ref, pst_ref, slot_ref, w8_ref, *, t):
    E = N_EXPERTS
    pos = pos_ref[...]
    sel = pos >= 0.0
    ri = lax.broadcasted_iota(jnp.int32, (E, E), 0)
    ci = lax.broadcasted_iota(jnp.int32, (E, E), 1)
    below = (ci < ri).astype(BF16)
    order = jnp.dot(below, sel.astype(BF16), preferred_element_type=F32)
    slot = pst_ref[...][:, :1] + pos
    wd = w_ref[...]
    kidx = lax.broadcasted_iota(jnp.int32, (TOP_K, t), 0)
    slots = jnp.zeros((TOP_K, t), F32)
    w8 = jnp.zeros((TOP_K, t), F32)
    for k in range(TOP_K):
        m = sel & (order == float(k))
        slots = jnp.where(kidx == k, jnp.sum(jnp.where(m, slot, 0.0), axis=0, keepdims=True), slots)
        w8 = jnp.where(kidx == k, jnp.sum(jnp.where(m, wd, 0.0), axis=0, keepdims=True), w8)
    slot_ref[...] = (slots * float(SUBLANES)).astype(jnp.int32)
    w8_ref[...] = w8


def _compact(w_dense, pos_dense, pstart_col):
    e, n_rows = w_dense.shape
    t = _pick_tile(n_rows, 512, LANES)
    return pl.pallas_call(
        functools.partial(_compact_kernel, t=t),
        grid=(n_rows // t,),
        in_specs=[pl.BlockSpec((e, t), lambda i: (0, i)),
                  pl.BlockSpec((e, t), lambda i: (0, i)),
                  pl.BlockSpec((e, LANES), lambda i: (0, 0))],
        out_specs=[pl.BlockSpec((TOP_K, t), lambda i: (0, i)),
                   pl.BlockSpec((TOP_K, t), lambda i: (0, i))],
        out_shape=[jax.ShapeDtypeStruct((TOP_K, n_rows), jnp.int32),
                   jax.ShapeDtypeStruct((TOP_K, n_rows), F32)],
        compiler_params=_cparams(("parallel",)),
        name="compact",
    )(w_dense, pos_dense, pstart_col)


def _pack_bf16_pairs(x):
    half = x.shape[1] // 2
    lo = lax.bitcast_convert_type(x[:, :half].astype(BF16).astype(F32), jnp.uint32)
    hi = lax.bitcast_convert_type(x[:, half:].astype(BF16).astype(F32), jnp.uint32)
    return (lo >> 16) | hi


def _unpack_bf16_pairs(w):
    lo = lax.bitcast_convert_type(w << 16, F32)
    hi = lax.bitcast_convert_type(w & jnp.uint32(0xFFFF0000), F32)
    return lo, hi


def _store_rows_as_tiles(ref_at, x, rows):
    for c in range(SUBLANES):
        ref_at[pl.ds(c, rows, stride=SUBLANES), :] = x[:, c * LANES:(c + 1) * LANES]


def _load_tile_chunk(ref_at, c, rows):
    return ref_at[pl.ds(c, rows, stride=SUBLANES), :]


def _dispatch_kernel(pad_lo_ref, pad_hi_ref, slot_ref, h_ref, xs_ref, hp_ref, zero_ref, sem, zsem,
                     *, t):
    i = pl.program_id(0)
    cur = i % 2
    _store_rows_as_tiles(hp_ref.at[cur], _pack_bf16_pairs(h_ref[...]), t)

    def issue(tok, carry):
        src = hp_ref.at[cur, pl.ds(pl.multiple_of(tok * SUBLANES, SUBLANES), SUBLANES), :]
        for k in range(TOP_K):
            dst = xs_ref.at[pl.ds(pl.multiple_of(slot_ref[k * t + tok], SUBLANES), SUBLANES), :]
            pltpu.make_async_copy(src, dst, sem.at[cur]).start(priority=k % 2)
        return carry

    lax.fori_loop(0, t, issue, 0, unroll=2)

    @pl.when(i == 0)
    def _():
        zero_ref[...] = jnp.zeros_like(zero_ref)

        def zero_copy(p):
            return pltpu.make_async_copy(
                zero_ref, xs_ref.at[pl.ds(pl.multiple_of(p * SUBLANES, SUBLANES), SUBLANES), :], zsem)

        def per_expert(step):
            def body(e, carry):
                lax.fori_loop(pad_lo_ref[e], pad_hi_ref[e], step, 0)
                return carry
            return body

        def zissue(p, c):
            zero_copy(p).start()
            return c

        def zwait(p, c):
            zero_copy(p).wait()
            return c

        lax.fori_loop(0, N_EXPERTS, per_expert(zissue), 0)
        lax.fori_loop(0, N_EXPERTS, per_expert(zwait), 0)

    def wait_tile(buf):
        for k in range(TOP_K):
            pltpu.make_async_copy(hp_ref.at[buf], xs_ref.at[pl.ds(0, t * SUBLANES), :],
                                  sem.at[buf]).wait()

    @pl.when(i > 0)
    def _():
        wait_tile(1 - cur)

    @pl.when(i == pl.num_programs(0) - 1)
    def _():
        wait_tile(cur)


def _dispatch(h, slots_tiled, pad_lo, pad_hi, n_slots, t):
    n_rows, d = h.shape
    n_tiles = n_rows // t
    return pl.pallas_call(
        functools.partial(_dispatch_kernel, t=t),
        grid_spec=pltpu.PrefetchScalarGridSpec(
            num_scalar_prefetch=2,
            grid=(n_tiles,),
            in_specs=[pl.BlockSpec((None, None, TOP_K * t), lambda i, lo, hi: (i, 0, 0),
                                   memory_space=pltpu.SMEM),
                      pl.BlockSpec((t, d), lambda i, lo, hi: (i, 0))],
            out_specs=pl.BlockSpec(memory_space=pl.ANY),
            scratch_shapes=[pltpu.VMEM((2, t * SUBLANES, LANES), jnp.uint32),
                            pltpu.VMEM((SUBLANES, LANES), jnp.uint32),
                            pltpu.SemaphoreType.DMA((2,)),
                            pltpu.SemaphoreType.DMA(())]),
        out_shape=jax.ShapeDtypeStruct((n_slots * SUBLANES, LANES), jnp.uint32),
        compiler_params=_cparams(("arbitrary",)),
        name="dispatch",
    )(pad_lo, pad_hi, slots_tiled, h)


def _swiglu(x, w1, w3, w2):
    a = jnp.dot(x, w1, preferred_element_type=F32)
    g = jnp.dot(x, w3, preferred_element_type=F32)
    hmid = (a * jax.nn.sigmoid(a) * g).astype(BF16)
    return jnp.dot(hmid, w2, preferred_element_type=F32)


def _expert_kernel(be_ref, nu_ref, x_ref, w1_ref, w3_ref, w2_ref, y_ref, w1b_ref, w3b_ref, w2b_ref):
    b = pl.program_id(0)

    @pl.when(b < nu_ref[0])
    def _():
        @pl.when((b == 0) | (be_ref[b] != be_ref[jnp.maximum(b - 1, 0)]))
        def _():
            w1b_ref[...] = w1_ref[...].astype(BF16)
            w3b_ref[...] = w3_ref[...].astype(BF16)
            w2b_ref[...] = w2_ref[...].astype(BF16)

        halves = [_unpack_bf16_pairs(_load_tile_chunk(x_ref, c, MOE_BLOCK)) for c in range(SUBLANES)]
        x = jnp.concatenate([lo.astype(BF16) for lo, _ in halves]
                            + [hi.astype(BF16) for _, hi in halves], axis=1)
        y = _pack_bf16_pairs(_swiglu(x, w1b_ref[...], w3b_ref[...], w2b_ref[...]))
        _store_rows_as_tiles(y_ref, y, MOE_BLOCK)


def _experts(xs, w1, w3, w2, layer, block_e, n_used):
    d, f = w1.shape[2], w1.shape[3]
    blk_rows = MOE_BLOCK * SUBLANES
    nb = xs.shape[0] // blk_rows

    def blk(b, be, nu):
        return jnp.minimum(b, nu[0] - 1)

    def wmap(b, be, nu):
        return (layer, be[blk(b, be, nu)], 0, 0)

    return pl.pallas_call(
        _expert_kernel,
        grid_spec=pltpu.PrefetchScalarGridSpec(
            num_scalar_prefetch=2,
            grid=(nb,),
            in_specs=[pl.BlockSpec((blk_rows, LANES), lambda b, be, nu: (blk(b, be, nu), 0)),
                      pl.BlockSpec((None, None, d, f), wmap),
                      pl.BlockSpec((None, None, d, f), wmap),
                      pl.BlockSpec((None, None, f, d), wmap)],
            out_specs=pl.BlockSpec((blk_rows, LANES), lambda b, be, nu: (blk(b, be, nu), 0)),
            scratch_shapes=[pltpu.VMEM((d, f), BF16), pltpu.VMEM((d, f), BF16),
                            pltpu.VMEM((f, d), BF16)]),
        out_shape=jax.ShapeDtypeStruct(xs.shape, jnp.uint32),
        compiler_params=_cparams(("arbitrary",)),
        name="experts",
    )(block_e, n_used, xs, w1, w3, w2)


def _combine_kernel(slot_ref, nslot_ref, h_ref, w8_ref, ws1_ref, ws3_ref, ws2_ref, g_ref, b_ref,
                    ys_ref, o_ref, ob_ref, ybuf_ref, sem, *, t):
    i = pl.program_id(0)
    last = pl.num_programs(0) - 1

    def row_copy(s_ref, buf, k, tok):
        src = ys_ref.at[pl.ds(pl.multiple_of(s_ref[k * t + tok], SUBLANES), SUBLANES), :]
        return pltpu.make_async_copy(
            src, ybuf_ref.at[buf, k, pl.ds(pl.multiple_of(tok * SUBLANES, SUBLANES), SUBLANES), :],
            sem.at[buf])

    def wait_tile(buf):
        for k in range(TOP_K):
            pltpu.make_async_copy(ys_ref.at[pl.ds(0, t * SUBLANES), :], ybuf_ref.at[buf, k],
                                  sem.at[buf]).wait()

    @pl.when(i == 0)
    def _():
        def issue(tok, carry):
            for k in range(TOP_K):
                row_copy(slot_ref, 0, k, tok).start(priority=k % 2)
            return carry

        lax.fori_loop(0, t, issue, 0, unroll=2)

    group = t // TOP_K

    def reduce_tile(cur):
        nxt = 1 - cur

        def issue_group(g):
            for tok in range(g * group, (g + 1) * group):
                for k in range(TOP_K):
                    row_copy(nslot_ref, nxt, k, tok).start(priority=k % 2)

        wait_tile(cur)
        h = h_ref[...]
        issue_group(0)
        shared = _swiglu(h.astype(BF16), ws1_ref[...], ws3_ref[...], ws2_ref[...])
        w8 = w8_ref[...]
        acc_lo, acc_hi = [None] * SUBLANES, [None] * SUBLANES
        for k in range(TOP_K):
            if k + 1 < TOP_K:
                issue_group(k + 1)
            wk = w8[:, k:k + 1]
            for c in range(SUBLANES):
                lo, hi = _unpack_bf16_pairs(_load_tile_chunk(ybuf_ref.at[cur, k], c, t))
                acc_lo[c] = lo * wk if k == 0 else acc_lo[c] + lo * wk
                acc_hi[c] = hi * wk if k == 0 else acc_hi[c] + hi * wk
        acc = jnp.concatenate(acc_lo + acc_hi, axis=1)
        y = DN_ALPHA * h + (acc + shared)
        out = _layer_norm(y, g_ref[...], b_ref[...])
        o_ref[...] = out
        ob_ref[...] = out.astype(BF16)

        @pl.when(i == last)
        def _():
            wait_tile(nxt)

    for parity in range(2):
        @pl.when(i % 2 == parity)
        def _():
            reduce_tile(parity)


def _combine(h, ys, slots_tiled, w8_t, ws1, ws3, ws2, g, b, t, out_rows, out_map):
    n_rows, d = h.shape
    f = ws1.shape[1]
    n_tiles = n_rows // t
    return pl.pallas_call(
        functools.partial(_combine_kernel, t=t),
        grid=(n_tiles,),
        in_specs=[pl.BlockSpec((None, None, TOP_K * t), lambda i: (i, 0, 0),
                               memory_space=pltpu.SMEM),
                  pl.BlockSpec((None, None, TOP_K * t),
                               lambda i: (jnp.minimum(i + 1, n_tiles - 1), 0, 0),
                               memory_space=pltpu.SMEM),
                  pl.BlockSpec((t, d), lambda i: (i, 0)),
                  pl.BlockSpec((t, TOP_K), lambda i: (i, 0)),
                  pl.BlockSpec((d, f), lambda i: (0, 0)),
                  pl.BlockSpec((d, f), lambda i: (0, 0)),
                  pl.BlockSpec((f, d), lambda i: (0, 0)),
                  pl.BlockSpec((1, d), lambda i: (0, 0)),
                  pl.BlockSpec((1, d), lambda i: (0, 0)),
                  pl.BlockSpec(memory_space=pl.ANY)],
        out_specs=[pl.BlockSpec((t, d), lambda i: (out_map(i), 0)),
                   pl.BlockSpec((t, d), lambda i: (i, 0))],
        out_shape=[jax.ShapeDtypeStruct((out_rows, d), F32),
                   jax.ShapeDtypeStruct((n_rows, d), BF16)],
        scratch_shapes=[pltpu.VMEM((2, TOP_K, t * SUBLANES, LANES), jnp.uint32),
                        pltpu.SemaphoreType.DMA((2,))],
        compiler_params=_cparams(("arbitrary",)),
        name="combine_ln",
    )(slots_tiled, slots_tiled, h, w8_t, ws1, ws3, ws2, g, b, ys)


def _moe_ln(h, logits_t, router_bias, w1, w3, w2, layer, ws1, ws3, ws2, g, b, drop_rows, seq_rows):
    n_rows, d = h.shape
    E = N_EXPERTS
    assert d // 2 == SUBLANES * LANES, d
    bias_col = jnp.broadcast_to(router_bias.astype(F32)[:, None], (E, LANES))
    w_dense, pos_dense, cnt = _route(logits_t, bias_col)

    counts = cnt[:, 0].astype(jnp.int32)
    pcounts = (counts + MOE_BLOCK - 1) // MOE_BLOCK * MOE_BLOCK
    pends = jnp.cumsum(pcounts)
    pstarts = pends - pcounts
    n_blocks = -(-(n_rows * TOP_K) // MOE_BLOCK) + E
    n_slots = n_blocks * MOE_BLOCK
    block_lo = jnp.arange(n_blocks, dtype=jnp.int32) * MOE_BLOCK
    block_e = jnp.minimum(jnp.sum(pends[None, :] <= block_lo[:, None], axis=1), E - 1).astype(jnp.int32)
    n_used = (pends[-1:] // MOE_BLOCK).astype(jnp.int32)
    pstart_col = jnp.broadcast_to(pstarts.astype(F32)[:, None], (E, LANES))

    slots, w8 = _compact(w_dense, pos_dense, pstart_col)
    t = _pick_tile(n_rows, 128, LANES)
    n_tiles = n_rows // t
    slots_tiled = slots.reshape(TOP_K, n_tiles, t).transpose(1, 0, 2).reshape(n_tiles, 1, TOP_K * t)
    xs = _dispatch(h, slots_tiled, (pstarts + counts).astype(jnp.int32), pends.astype(jnp.int32),
                   n_slots, t)
    ys = _experts(xs, w1, w3, w2, layer, block_e, n_used)
    if drop_rows == t:
        per_seq = seq_rows // t
        out_rows = n_rows - (n_rows // seq_rows) * t

        def out_map(i):
            return (i // per_seq) * (per_seq - 1) + jnp.maximum(i % per_seq - 1, 0)
    else:
        out_rows, out_map = n_rows, (lambda i: i)
    return _combine(h, ys, slots_tiled, w8.T, ws1, ws3, ws2, g, b, t, out_rows, out_map)


def kernel(x, meta_tokens, gla_w_in, gla_w_gate_up, gla_b_gate, gla_norm_g, gla_w_out,
           conv_w_in, conv_w, conv_w_out, ln1_g, ln1_b, router_w, router_bias,
           exp_w1, exp_w3, exp_w2, shared_w1, shared_w3, shared_w2, ln2_g, ln2_b):
    batch, seq, d = x.shape
    H = GLA_HEADS
    dk = d // 2 // H
    n_qkvr = 2 * H * dk + 2 * d
    rank = gla_w_in.shape[2] - n_qkvr
    seq_rows = -(-(N_META + seq) // ROW_ALIGN) * ROW_ALIGN
    pad_rows = seq_rows - N_META - seq
    n_rows = batch * seq_rows

    meta = jnp.broadcast_to(meta_tokens.astype(x.dtype)[None], (batch, N_META, d))
    h = jnp.concatenate([jnp.zeros((batch, pad_rows, d), x.dtype), meta, x], axis=1)
    h = h.reshape(n_rows, d)
    hb = h.astype(BF16)

    def row(v):
        return v.astype(F32)[None, :]

    for i in range(DEPTH):
        jm = i // 2
        if i % 2 == 0:
            w_in = gla_w_in[jm]
            w_qkvr = w_in[:, :n_qkvr].astype(BF16)
            w_gate = jnp.pad(w_in[:, n_qkvr:], ((0, 0), (0, GATE_PAD - rank))).astype(BF16)
            wg_up = jnp.pad(gla_w_gate_up[jm], ((0, GATE_PAD - rank), (0, 0))).astype(BF16)
            qkvr = _matmul(hb, w_qkvr, BF16, 1664, 768, "gla_in_proj")
            glow = _matmul(hb, w_gate, F32, 1664, LANES, "gla_gate_proj")
            mix = _gla(qkvr, glow, wg_up, row(gla_b_gate[jm]), row(gla_norm_g[jm]),
                       batch=batch, seq_rows=seq_rows, d_model=d)
            w_out = gla_w_out[jm].astype(BF16)
        else:
            mix = _conv_mix(hb, conv_w_in[jm].astype(BF16), conv_w[jm].astype(F32),
                            seq_rows=seq_rows, pad_rows=pad_rows)
            w_out = conv_w_out[jm].astype(BF16)
        h, logits_t = _proj_ln_router(mix, w_out, h, row(ln1_g[i]), row(ln1_b[i]),
                                      router_w[i].T.astype(BF16))
        drop = pad_rows + N_META if i == DEPTH - 1 else 0
        h, hb = _moe_ln(h, logits_t, router_bias[i], exp_w1, exp_w3, exp_w2, i,
                        shared_w1[i].astype(BF16), shared_w3[i].astype(BF16),
                        shared_w2[i].astype(BF16), row(ln2_g[i]), row(ln2_b[i]), drop, seq_rows)
    if h.shape[0] == batch * seq:
        return h.reshape(batch, seq, d)
    return h.reshape(batch, seq_rows, d)[:, pad_rows + N_META:]
```

```python
import functools

import jax
import jax.numpy as jnp
from jax import lax
from jax.experimental import pallas as pl
from jax.experimental.pallas import tpu as pltpu

N_META = 16
GLA_HEADS = 4
GLA_GATE_TAU = 16.0
CONV_WIDTH = 3
N_EXPERTS = 64
TOP_K = 8
N_GROUPS = 8
TOPK_GROUPS = 4
ROUTED_SCALE = 2.5
LN_EPS = 1e-5
RMS_EPS = 1e-6
DEPTH = 2
DN_ALPHA = (2 * DEPTH) ** 0.25

LANES = 128
SUBLANES = 8
ROW_ALIGN = 128
GLA_CHUNK = 64
GLA_SUB = 16
GLA_HEADS_PER_STEP = 4
GATE_PAD = LANES
MOE_BLOCK = 512
VMEM_LIMIT = 56 * 1024 * 1024

F32 = jnp.float32
BF16 = jnp.bfloat16


def _pick_tile(n, target, mult):
    best = None
    for t in range(mult, min(n, target) + 1, mult):
        if n % t == 0:
            best = t
    assert best is not None, (n, target, mult)
    return best


def _cparams(sem):
    return pltpu.CompilerParams(dimension_semantics=sem, vmem_limit_bytes=VMEM_LIMIT)


def _mm_kernel(x_ref, w_ref, o_ref):
    o_ref[...] = jnp.dot(x_ref[...], w_ref[...], preferred_element_type=F32).astype(o_ref.dtype)


def _matmul(x, w, out_dtype, tm_target, tn_target, name):
    m, k = x.shape
    n = w.shape[1]
    tm = _pick_tile(m, tm_target, 16)
    tn = _pick_tile(n, tn_target, LANES)
    return pl.pallas_call(
        _mm_kernel,
        grid=(m // tm, n // tn),
        in_specs=[pl.BlockSpec((tm, k), lambda i, j: (i, 0)),
                  pl.BlockSpec((k, tn), lambda i, j: (0, j))],
        out_specs=pl.BlockSpec((tm, tn), lambda i, j: (i, j)),
        out_shape=jax.ShapeDtypeStruct((m, n), out_dtype),
        compiler_params=_cparams(("parallel", "arbitrary")),
        name=name,
    )(x, w)


def _gla_kernel(q_ref, k_ref, v_ref, r_ref, gl_ref, wg_ref, bg_ref, ng_ref, o_ref, st_ref,
                *, n_chunks, dk, dv, heads):
    C, S = GLA_CHUNK, GLA_SUB

    @pl.when(pl.program_id(2) == 0)
    def _():
        st_ref[...] = jnp.zeros_like(st_ref)

    ri = lax.broadcasted_iota(jnp.int32, (C, C), 0)
    ci = lax.broadcasted_iota(jnp.int32, (C, C), 1)
    tri = (ri >= ci).astype(BF16)
    rs = lax.broadcasted_iota(jnp.int32, (S, S), 0)
    cs = lax.broadcasted_iota(jnp.int32, (S, S), 1)
    nt = (((1,), (1,)), ((), ()))
    tn = (((0,), (0,)), ((), ()))
    half = S // 2

    def chunk(c, carry):
        r0 = pl.multiple_of(c * C, C)
        rows_c = pl.ds(r0, C)
        hs = range(heads)
        kcol = [pl.ds(hd * dk, dk) for hd in hs]
        vcol = [pl.ds(hd * dv, dv) for hd in hs]
        gl = gl_ref[rows_c, :].astype(BF16)
        q = [q_ref[rows_c, kcol[hd]].astype(F32) * (dk ** -0.5) for hd in hs]
        kk = [k_ref[rows_c, kcol[hd]].astype(F32) for hd in hs]
        vv = [v_ref[rows_c, vcol[hd]] for hd in hs]
        st = [st_ref[hd] for hd in hs]
        z = [jnp.dot(gl, wg_ref[:, kcol[hd]], preferred_element_type=F32) + bg_ref[:, kcol[hd]]
             for hd in hs]
        b = []
        for hd in hs:
            la = ((jnp.minimum(z[hd], 0.0) - jnp.log(1.0 + jnp.exp(-jnp.abs(z[hd]))))
                  * (1.0 / GLA_GATE_TAU))
            h1 = la.astype(BF16)
            e1 = la - h1.astype(F32)
            h2 = e1.astype(BF16)
            h3 = (e1 - h2.astype(F32)).astype(BF16)
            b3 = jnp.dot(tri, jnp.concatenate([h1, h2, h3], axis=1), preferred_element_type=F32)
            b.append(b3[:, :dk] + b3[:, dk:2 * dk] + b3[:, 2 * dk:])

        o_inter, st_new, off = [], [], []
        for hd in hs:
            qe = (q[hd] * jnp.exp(b[hd])).astype(BF16)
            o_inter.append(lax.dot_general(qe, st[hd].astype(BF16), nt, preferred_element_type=F32))
            bl = b[hd][C - 1:C]
            khat = (kk[hd] * jnp.exp(bl - b[hd])).astype(BF16)
            upd = lax.dot_general(vv[hd], khat, tn, preferred_element_type=F32)
            st_new.append(st[hd] * jnp.exp(bl) + upd)
            offs = []
            for i in range(1, C // S):
                lo = i * S
                bref = b[hd][lo - 1:lo]
                qt = (q[hd][lo:lo + S] * jnp.exp(b[hd][lo:lo + S] - bref)).astype(BF16)
                kt = (kk[hd][:lo] * jnp.exp(bref - b[hd][:lo])).astype(BF16)
                offs.append(lax.dot_general(qt, kt, nt, preferred_element_type=F32))
            off.append(offs)

        dmats = []
        for hd in hs:
            blocks = []
            for i in range(C // S):
                lo = i * S
                qs = q[hd][lo:lo + S]
                bs = b[hd][lo:lo + S]
                dmat = jnp.zeros((S, S), F32)
                for j in range(S):
                    kj = kk[hd][lo + j:lo + j + 1]
                    bj = b[hd][lo + j:lo + j + 1]
                    if j < half:
                        p = qs * kj * jnp.exp(jnp.minimum(bs - bj, 0.0))
                        a = jnp.sum(p, axis=-1, keepdims=True)
                    else:
                        p = qs[half:] * kj * jnp.exp(jnp.minimum(bs[half:] - bj, 0.0))
                        a = jnp.concatenate([jnp.zeros((half, 1), F32),
                                             jnp.sum(p, axis=-1, keepdims=True)], axis=0)
                    dmat = jnp.where(cs == j, a, dmat)
                blocks.append(jnp.where(rs >= cs, dmat, 0.0).astype(BF16))
            dmats.append(blocks)

        for hd in hs:
            rows = []
            for i in range(C // S):
                lo = i * S
                o_i = jnp.dot(dmats[hd][i], vv[hd][lo:lo + S], preferred_element_type=F32)
                if i > 0:
                    o_i = o_i + jnp.dot(off[hd][i - 1].astype(BF16), vv[hd][:lo],
                                        preferred_element_type=F32)
                rows.append(o_i)
            o = o_inter[hd] + jnp.concatenate(rows, axis=0)
            ms = jnp.mean(o * o, axis=-1, keepdims=True)
            r = r_ref[rows_c, vcol[hd]].astype(F32)
            y = o * lax.rsqrt(ms + RMS_EPS) * ng_ref[:, vcol[hd]] * (r * jax.nn.sigmoid(r))
            st_ref[hd] = st_new[hd]
            o_ref[rows_c, vcol[hd]] = y.astype(o_ref.dtype)
        return carry

    lax.fori_loop(0, n_chunks, chunk, 0)


def _gla(qkvr, glow, wg, bg, ng, *, batch, seq_rows, d_model):
    H, hps = GLA_HEADS, GLA_HEADS_PER_STEP
    dk = d_model // 2 // H
    dv = d_model // H
    n_rows = batch * seq_rows
    rblk = _pick_tile(seq_rows, 640, GLA_CHUNK)
    steps = seq_rows // rblk
    groups = H // hps
    off_k, off_v, off_r = groups, (2 * H * dk) // (hps * dv), (2 * H * dk) // (hps * dv) + groups

    def rowmap(off):
        return lambda b, h, i: (b * steps + i, off + h)

    return pl.pallas_call(
        functools.partial(_gla_kernel, n_chunks=rblk // GLA_CHUNK, dk=dk, dv=dv, heads=hps),
        grid=(batch, groups, steps),
        in_specs=[pl.BlockSpec((rblk, hps * dk), rowmap(0)),
                  pl.BlockSpec((rblk, hps * dk), rowmap(off_k)),
                  pl.BlockSpec((rblk, hps * dv), rowmap(off_v)),
                  pl.BlockSpec((rblk, hps * dv), rowmap(off_r)),
                  pl.BlockSpec((rblk, GATE_PAD), lambda b, h, i: (b * steps + i, 0)),
                  pl.BlockSpec((GATE_PAD, hps * dk), lambda b, h, i: (0, h)),
                  pl.BlockSpec((1, hps * dk), lambda b, h, i: (0, h)),
                  pl.BlockSpec((1, hps * dv), lambda b, h, i: (0, h))],
        out_specs=pl.BlockSpec((rblk, hps * dv), lambda b, h, i: (b * steps + i, h)),
        out_shape=jax.ShapeDtypeStruct((n_rows, H * dv), BF16),
        scratch_shapes=[pltpu.VMEM((hps, dv, dk), F32)],
        compiler_params=_cparams(("parallel", "parallel", "arbitrary")),
        name="gla_chunks",
    )(qkvr, qkvr, qkvr, qkvr, glow, wg, bg, ng)


def _conv_kernel(x_ref, wb_ref, wc_ref, wh_ref, cw_ref, o_ref, ubuf_ref, carry_ref,
                 *, seq_rows, pad_rows, tm):
    i = pl.program_id(0)
    j = pl.program_id(1)
    x = x_ref[...]
    bg = jnp.dot(x, wb_ref[...], preferred_element_type=F32)
    cg = jnp.dot(x, wc_ref[...], preferred_element_type=F32)
    hh = jnp.dot(x, wh_ref[...], preferred_element_type=F32)
    row = i * tm + lax.broadcasted_iota(jnp.int32, (tm, 1), 0)
    u = jnp.where(row % seq_rows >= pad_rows, cg * hh, 0.0)

    @pl.when(i == 0)
    def _():
        carry_ref[j] = jnp.zeros(carry_ref.shape[1:], F32)

    ubuf_ref[pl.ds(0, SUBLANES), :] = carry_ref[j]
    ubuf_ref[pl.ds(SUBLANES, tm), :] = u
    carry_ref[j] = u[tm - SUBLANES:]
    cw = cw_ref[...]
    conv = u * cw[CONV_WIDTH - 1:CONV_WIDTH]
    for s in range(1, CONV_WIDTH):
        conv = conv + ubuf_ref[pl.ds(SUBLANES - s, tm), :] * cw[CONV_WIDTH - 1 - s:CONV_WIDTH - s]
    o_ref[...] = (bg * conv).astype(o_ref.dtype)


def _conv_mix(xb, w_in, conv_w, *, seq_rows, pad_rows):
    n_rows, d = xb.shape
    tm = _pick_tile(n_rows, 1024, SUBLANES)
    tn = _pick_tile(d, 512, LANES)
    nj = d // tn
    return pl.pallas_call(
        functools.partial(_conv_kernel, seq_rows=seq_rows, pad_rows=pad_rows, tm=tm),
        grid=(n_rows // tm, nj),
        in_specs=[pl.BlockSpec((tm, d), lambda i, j: (i, 0)),
                  pl.BlockSpec((d, tn), lambda i, j: (0, j)),
                  pl.BlockSpec((d, tn), lambda i, j: (0, nj + j)),
                  pl.BlockSpec((d, tn), lambda i, j: (0, 2 * nj + j)),
                  pl.BlockSpec((CONV_WIDTH, tn), lambda i, j: (0, j))],
        out_specs=pl.BlockSpec((tm, tn), lambda i, j: (i, j)),
        out_shape=jax.ShapeDtypeStruct((n_rows, d), BF16),
        scratch_shapes=[pltpu.VMEM((tm + SUBLANES, tn), F32),
                        pltpu.VMEM((nj, SUBLANES, tn), F32)],
        compiler_params=_cparams(("arbitrary", "arbitrary")),
        name="conv_mix",
    )(xb, w_in, w_in, w_in, conv_w)


def _layer_norm(y, g, b):
    mu = jnp.mean(y, axis=-1, keepdims=True)
    yc = y - mu
    var = jnp.mean(yc * yc, axis=-1, keepdims=True)
    return yc * lax.rsqrt(var + LN_EPS) * g + b


def _proj_ln_kernel(a_ref, w_ref, h_ref, g_ref, b_ref, rw_ref, o_ref, lg_ref):
    y = jnp.dot(a_ref[...], w_ref[...], preferred_element_type=F32) + DN_ALPHA * h_ref[...]
    h = _layer_norm(y, g_ref[...], b_ref[...])
    o_ref[...] = h
    lg_ref[...] = lax.dot_general(rw_ref[...], h.astype(BF16), (((1,), (1,)), ((), ())),
                                  preferred_element_type=F32)


def _proj_ln_router(a, w, h, g, b, rw_t):
    n_rows, kin = a.shape
    d = w.shape[1]
    e = rw_t.shape[0]
    tm = _pick_tile(n_rows, 512, LANES)
    return pl.pallas_call(
        _proj_ln_kernel,
        grid=(n_rows // tm,),
        in_specs=[pl.BlockSpec((tm, kin), lambda i: (i, 0)),
                  pl.BlockSpec((kin, d), lambda i: (0, 0)),
                  pl.BlockSpec((tm, d), lambda i: (i, 0)),
                  pl.BlockSpec((1, d), lambda i: (0, 0)),
                  pl.BlockSpec((1, d), lambda i: (0, 0)),
                  pl.BlockSpec((e, d), lambda i: (0, 0))],
        out_specs=[pl.BlockSpec((tm, d), lambda i: (i, 0)),
                   pl.BlockSpec((e, tm), lambda i: (0, i))],
        out_shape=[jax.ShapeDtypeStruct((n_rows, d), F32),
                   jax.ShapeDtypeStruct((e, n_rows), F32)],
        compiler_params=_cparams(("parallel",)),
        name="proj_ln_router",
    )(a, w, h, g, b, rw_t)


def _beats(cand, cand_idx, ref, ref_idx):
    return (cand > ref) | ((cand == ref) & (cand_idx < ref_idx))


def _route_kernel(lg_ref, bias_ref, w_ref, pos_ref, cnt_ref, run_ref, *, t):
    E, G = N_EXPERTS, N_GROUPS
    gs = E // G

    @pl.when(pl.program_id(0) == 0)
    def _():
        run_ref[...] = jnp.zeros_like(run_ref)

    s = jax.nn.sigmoid(lg_ref[...])
    c = s + bias_ref[...][:, :1]
    sub = lax.broadcasted_iota(jnp.int32, (gs, t), 0)

    grp_rows = []
    for g in range(G):
        cg = c[g * gs:(g + 1) * gs]
        rank = jnp.zeros((gs, t), jnp.int32)
        for m in range(gs):
            rank = rank + _beats(cg[m:m + 1], m, cg, sub).astype(jnp.int32)
        grp_rows.append(jnp.sum(jnp.where(rank < 2, cg, 0.0), axis=0, keepdims=True))
    gidx = lax.broadcasted_iota(jnp.int32, (G, t), 0)
    gscore = jnp.zeros((G, t), F32)
    for g in range(G):
        gscore = jnp.where(gidx == g, grp_rows[g], gscore)
    grank = jnp.zeros((G, t), jnp.int32)
    for m in range(G):
        grank = grank + _beats(gscore[m:m + 1], m, gscore, gidx).astype(jnp.int32)
    gsel = grank < TOPK_GROUPS

    masked = jnp.concatenate(
        [jnp.where(gsel[g:g + 1], c[g * gs:(g + 1) * gs], -jnp.inf) for g in range(G)], axis=0)
    eidx = lax.broadcasted_iota(jnp.int32, (E, t), 0)
    rank = jnp.zeros((E, t), jnp.int32)
    for m in range(E):
        rank = rank + _beats(masked[m:m + 1], m, masked, eidx).astype(jnp.int32)
    sel = rank < TOP_K
    gate = jnp.where(sel, s, 0.0)
    w_ref[...] = gate / jnp.sum(gate, axis=0, keepdims=True) * ROUTED_SCALE

    li = lax.broadcasted_iota(jnp.int32, (t, t), 0)
    lj = lax.broadcasted_iota(jnp.int32, (t, t), 1)
    upper = (li <= lj).astype(BF16)
    self_ = sel.astype(F32)
    incl = jnp.dot(sel.astype(BF16), upper, preferred_element_type=F32)
    run = run_ref[...][:, :1]
    pos_ref[...] = jnp.where(sel, run + incl - self_, -1.0)
    run_new = run + jnp.sum(self_, axis=1, keepdims=True)
    run_ref[...] = jnp.broadcast_to(run_new, run_ref.shape)
    cnt_ref[...] = jnp.broadcast_to(run_new, cnt_ref.shape)


def _route(logits_t, bias_col):
    e, n_rows = logits_t.shape
    t = _pick_tile(n_rows, 512, LANES)
    return pl.pallas_call(
        functools.partial(_route_kernel, t=t),
        grid=(n_rows // t,),
        in_specs=[pl.BlockSpec((e, t), lambda i: (0, i)),
                  pl.BlockSpec((e, LANES), lambda i: (0, 0))],
        out_specs=[pl.BlockSpec((e, t), lambda i: (0, i)),
                   pl.BlockSpec((e, t), lambda i: (0, i)),
                   pl.BlockSpec((e, LANES), lambda i: (0, 0))],
        out_shape=[jax.ShapeDtypeStruct((e, n_rows), F32),
                   jax.ShapeDtypeStruct((e, n_rows), F32),
                   jax.ShapeDtypeStruct((e, LANES), F32)],
        scratch_shapes=[pltpu.VMEM((e, LANES), F32)],
        compiler_params=_cparams(("arbitrary",)),
        name="route",
    )(logits_t, bias_col)


def _compact_kernel(w_ref, pos_ref, pst_ref, slot_ref, w8_ref, *, t):
    E = N_EXPERTS
    pos = pos_ref[...]
    sel = pos >= 0.0
    ri = lax.broadcasted_iota(jnp.int32, (E, E), 0)
    ci = lax.broadcasted_iota(jnp.int32, (E, E), 1)
    below = (ci < ri).astype(BF16)
    order = jnp.dot(below, sel.astype(BF16), preferred_element_type=F32)
    slot = pst_ref[...][:, :1] + pos
    wd = w_ref[...]
    kidx = lax.broadcasted_iota(jnp.int32, (TOP_K, t), 0)
    slots = jnp.zeros((TOP_K, t), F32)
    w8 = jnp.zeros((TOP_K, t), F32)
    for k in range(TOP_K):
        m = sel & (order == float(k))
        slots = jnp.where(kidx == k, jnp.sum(jnp.where(m, slot, 0.0), axis=0, keepdims=True), slots)
        w8 = jnp.where(kidx == k, jnp.sum(jnp.where(m, wd, 0.0), axis=0, keepdims=True), w8)
    slot_ref[...] = (slots * float(SUBLANES)).astype(jnp.int32)
    w8_ref[...] = w8


def _compact(w_dense, pos_dense, pstart_col):
    e, n_rows = w_dense.shape
    t = _pick_tile(n_rows, 512, LANES)
    return pl.pallas_call(
        functools.partial(_compact_kernel, t=t),
        grid=(n_rows // t,),
        in_specs=[pl.BlockSpec((e, t), lambda i: (0, i)),
                  pl.BlockSpec((e, t), lambda i: (0, i)),
                  pl.BlockSpec((e, LANES), lambda i: (0, 0))],
        out_specs=[pl.BlockSpec((TOP_K, t), lambda i: (0, i)),
                   pl.BlockSpec((TOP_K, t), lambda i: (0, i))],
        out_shape=[jax.ShapeDtypeStruct((TOP_K, n_rows), jnp.int32),
                   jax.ShapeDtypeStruct((TOP_K, n_rows), F32)],
        compiler_params=_cparams(("parallel",)),
        name="compact",
    )(w_dense, pos_dense, pstart_col)


def _pack_bf16_pairs(x):
    half = x.shape[1] // 2
    lo = lax.bitcast_convert_type(x[:, :half].astype(BF16).astype(F32), jnp.uint32)
    hi = lax.bitcast_convert_type(x[:, half:].astype(BF16).astype(F32), jnp.uint32)
    return (lo >> 16) | hi


def _unpack_bf16_pairs(w):
    lo = lax.bitcast_convert_type(w << 16, F32)
    hi = lax.bitcast_convert_type(w & jnp.uint32(0xFFFF0000), F32)
    return lo, hi


def _store_rows_as_tiles(ref_at, x, rows):
    for c in range(SUBLANES):
        ref_at[pl.ds(c, rows, stride=SUBLANES), :] = x[:, c * LANES:(c + 1) * LANES]


def _load_tile_chunk(ref_at, c, rows):
    return ref_at[pl.ds(c, rows, stride=SUBLANES), :]


def _dispatch_kernel(pad_lo_ref, pad_hi_ref, slot_ref, h_ref, xs_ref, hp_ref, zero_ref, sem, zsem,
                     *, t):
    i = pl.program_id(0)
    cur = i % 2
    _store_rows_as_tiles(hp_ref.at[cur], _pack_bf16_pairs(h_ref[...]), t)

    def issue(tok, carry):
        src = hp_ref.at[cur, pl.ds(pl.multiple_of(tok * SUBLANES, SUBLANES), SUBLANES), :]
        for k in range(TOP_K):
            dst = xs_ref.at[pl.ds(pl.multiple_of(slot_ref[k * t + tok], SUBLANES), SUBLANES), :]
            pltpu.make_async_copy(src, dst, sem.at[cur]).start(priority=k % 2)
        return carry

    lax.fori_loop(0, t, issue, 0, unroll=2)

    @pl.when(i == 0)
    def _():
        zero_ref[...] = jnp.zeros_like(zero_ref)

        def zero_copy(p):
            return pltpu.make_async_copy(
                zero_ref, xs_ref.at[pl.ds(pl.multiple_of(p * SUBLANES, SUBLANES), SUBLANES), :], zsem)

        def per_expert(step):
            def body(e, carry):
                lax.fori_loop(pad_lo_ref[e], pad_hi_ref[e], step, 0)
                return carry
            return body

        def zissue(p, c):
            zero_copy(p).start()
            return c

        def zwait(p, c):
            zero_copy(p).wait()
            return c

        lax.fori_loop(0, N_EXPERTS, per_expert(zissue), 0)
        lax.fori_loop(0, N_EXPERTS, per_expert(zwait), 0)

    def wait_tile(buf):
        for k in range(TOP_K):
            pltpu.make_async_copy(hp_ref.at[buf], xs_ref.at[pl.ds(0, t * SUBLANES), :],
                                  sem.at[buf]).wait()

    @pl.when(i > 0)
    def _():
        wait_tile(1 - cur)

    @pl.when(i == pl.num_programs(0) - 1)
    def _():
        wait_tile(cur)


def _dispatch(h, slots_tiled, pad_lo, pad_hi, n_slots, t):
    n_rows, d = h.shape
    n_tiles = n_rows // t
    return pl.pallas_call(
        functools.partial(_dispatch_kernel, t=t),
        grid_spec=pltpu.PrefetchScalarGridSpec(
            num_scalar_prefetch=2,
            grid=(n_tiles,),
            in_specs=[pl.BlockSpec((None, None, TOP_K * t), lambda i, lo, hi: (i, 0, 0),
                                   memory_space=pltpu.SMEM),
                      pl.BlockSpec((t, d), lambda i, lo, hi: (i, 0))],
            out_specs=pl.BlockSpec(memory_space=pl.ANY),
            scratch_shapes=[pltpu.VMEM((2, t * SUBLANES, LANES), jnp.uint32),
                            pltpu.VMEM((SUBLANES, LANES), jnp.uint32),
                            pltpu.SemaphoreType.DMA((2,)),
                            pltpu.SemaphoreType.DMA(())]),
        out_shape=jax.ShapeDtypeStruct((n_slots * SUBLANES, LANES), jnp.uint32),
        compiler_params=_cparams(("arbitrary",)),
        name="dispatch",
    )(pad_lo, pad_hi, slots_tiled, h)


def _swiglu(x, w1, w3, w2):
    a = jnp.dot(x, w1, preferred_element_type=F32)
    g = jnp.dot(x, w3, preferred_element_type=F32)
    hmid = (a * jax.nn.sigmoid(a) * g).astype(BF16)
    return jnp.dot(hmid, w2, preferred_element_type=F32)


def _expert_kernel(be_ref, nu_ref, x_ref, w1_ref, w3_ref, w2_ref, y_ref, w1b_ref, w3b_ref, w2b_ref):
    b = pl.program_id(0)

    @pl.when(b < nu_ref[0])
    def _():
        @pl.when((b == 0) | (be_ref[b] != be_ref[jnp.maximum(b - 1, 0)]))
        def _():
            w1b_ref[...] = w1_ref[...].astype(BF16)
            w3b_ref[...] = w3_ref[...].astype(BF16)
            w2b_ref[...] = w2_ref[...].astype(BF16)

        halves = [_unpack_bf16_pairs(_load_tile_chunk(x_ref, c, MOE_BLOCK)) for c in range(SUBLANES)]
        x = jnp.concatenate([lo.astype(BF16) for lo, _ in halves]
                            + [hi.astype(BF16) for _, hi in halves], axis=1)
        y = _pack_bf16_pairs(_swiglu(x, w1b_ref[...], w3b_ref[...], w2b_ref[...]))
        _store_rows_as_tiles(y_ref, y, MOE_BLOCK)


def _experts(xs, w1, w3, w2, layer, block_e, n_used):
    d, f = w1.shape[2], w1.shape[3]
    blk_rows = MOE_BLOCK * SUBLANES
    nb = xs.shape[0] // blk_rows

    def blk(b, be, nu):
        return jnp.minimum(b, nu[0] - 1)

    def wmap(b, be, nu):
        return (layer, be[blk(b, be, nu)], 0, 0)

    return pl.pallas_call(
        _expert_kernel,
        grid_spec=pltpu.PrefetchScalarGridSpec(
            num_scalar_prefetch=2,
            grid=(nb,),
            in_specs=[pl.BlockSpec((blk_rows, LANES), lambda b, be, nu: (blk(b, be, nu), 0)),
                      pl.BlockSpec((None, None, d, f), wmap),
                      pl.BlockSpec((None, None, d, f), wmap),
                      pl.BlockSpec((None, None, f, d), wmap)],
            out_specs=pl.BlockSpec((blk_rows, LANES), lambda b, be, nu: (blk(b, be, nu), 0)),
            scratch_shapes=[pltpu.VMEM((d, f), BF16), pltpu.VMEM((d, f), BF16),
                            pltpu.VMEM((f, d), BF16)]),
        out_shape=jax.ShapeDtypeStruct(xs.shape, jnp.uint32),
        compiler_params=_cparams(("arbitrary",)),
        name="experts",
    )(block_e, n_used, xs, w1, w3, w2)


def _combine_kernel(slot_ref, nslot_ref, h_ref, w8_ref, ws1_ref, ws3_ref, ws2_ref, g_ref, b_ref,
                    ys_ref, o_ref, ob_ref, ybuf0_ref, ybuf1_ref, sem, *, t):
    i = pl.program_id(0)
    last = pl.num_programs(0) - 1
    ybuf = (ybuf0_ref, ybuf1_ref)

    def row_copy(s_ref, buf, k, tok):
        src = ys_ref.at[pl.ds(pl.multiple_of(s_ref[k * t + tok], SUBLANES), SUBLANES), :]
        return pltpu.make_async_copy(
            src, ybuf[buf].at[k, pl.ds(pl.multiple_of(tok * SUBLANES, SUBLANES), SUBLANES), :],
            sem.at[buf])

    def wait_tile(buf):
        for k in range(TOP_K):
            pltpu.make_async_copy(ys_ref.at[pl.ds(0, t * SUBLANES), :], ybuf[buf].at[k],
                                  sem.at[buf]).wait()

    @pl.when(i == 0)
    def _():
        def issue(tok, carry):
            for k in range(TOP_K):
                row_copy(slot_ref, 0, k, tok).start(priority=k % 2)
            return carry

        lax.fori_loop(0, t, issue, 0, unroll=2)

    group = t // TOP_K

    def reduce_tile(cur):
        nxt = 1 - cur

        def issue_group(g):
            for tok in range(g * group, (g + 1) * group):
                for k in range(TOP_K):
                    row_copy(nslot_ref, nxt, k, tok).start(priority=k % 2)

        wait_tile(cur)
        h = h_ref[...]
        issue_group(0)
        shared = _swiglu(h.astype(BF16), ws1_ref[...], ws3_ref[...], ws2_ref[...])
        w8 = w8_ref[...]
        wk = [w8[:, k:k + 1] for k in range(TOP_K)]
        acc_lo, acc_hi = [], []
        for c in range(SUBLANES):
            if c + 1 < TOP_K:
                issue_group(c + 1)
            lo_c, hi_c = None, None
            for k in range(TOP_K):
                lo, hi = _unpack_bf16_pairs(_load_tile_chunk(ybuf[cur].at[k], c, t))
                lo_c = lo * wk[k] if k == 0 else lo_c + lo * wk[k]
                hi_c = hi * wk[k] if k == 0 else hi_c + hi * wk[k]
            acc_lo.append(lo_c)
            acc_hi.append(hi_c)
        acc = jnp.concatenate(acc_lo + acc_hi, axis=1)
        y = DN_ALPHA * h + (acc + shared)
        out = _layer_norm(y, g_ref[...], b_ref[...])
        o_ref[...] = out
        ob_ref[...] = out.astype(BF16)

        @pl.when(i == last)
        def _():
            wait_tile(nxt)

    for parity in range(2):
        @pl.when(i % 2 == parity)
        def _():
            reduce_tile(parity)


def _combine(h, ys, slots_tiled, w8_t, ws1, ws3, ws2, g, b, t, out_rows, out_map):
    n_rows, d = h.shape
    f = ws1.shape[1]
    n_tiles = n_rows // t
    return pl.pallas_call(
        functools.partial(_combine_kernel, t=t),
        grid=(n_tiles,),
        in_specs=[pl.BlockSpec((None, None, TOP_K * t), lambda i: (i, 0, 0),
                               memory_space=pltpu.SMEM),
                  pl.BlockSpec((None, None, TOP_K * t),
                               lambda i: (jnp.minimum(i + 1, n_tiles - 1), 0, 0),
                               memory_space=pltpu.SMEM),
                  pl.BlockSpec((t, d), lambda i: (i, 0)),
                  pl.BlockSpec((t, TOP_K), lambda i: (i, 0)),
                  pl.BlockSpec((d, f), lambda i: (0, 0)),
                  pl.BlockSpec((d, f), lambda i: (0, 0)),
                  pl.BlockSpec((f, d), lambda i: (0, 0)),
                  pl.BlockSpec((1, d), lambda i: (0, 0)),
                  pl.BlockSpec((1, d), lambda i: (0, 0)),
                  pl.BlockSpec(memory_space=pl.ANY)],
        out_specs=[pl.BlockSpec((t, d), lambda i: (out_map(i), 0)),
                   pl.BlockSpec((t, d), lambda i: (i, 0))],
        out_shape=[jax.ShapeDtypeStruct((out_rows, d), F32),
                   jax.ShapeDtypeStruct((n_rows, d), BF16)],
        scratch_shapes=[pltpu.VMEM((TOP_K, t * SUBLANES, LANES), jnp.uint32),
                        pltpu.VMEM((TOP_K, t * SUBLANES, LANES), jnp.uint32),
                        pltpu.SemaphoreType.DMA((2,))],
        compiler_params=_cparams(("arbitrary",)),
        name="combine_ln",
    )(slots_tiled, slots_tiled, h, w8_t, ws1, ws3, ws2, g, b, ys)


def _moe_ln(h, logits_t, router_bias, w1, w3, w2, layer, ws1, ws3, ws2, g, b, drop_rows, seq_rows):
    n_rows, d = h.shape
    E = N_EXPERTS
    assert d // 2 == SUBLANES * LANES, d
    bias_col = jnp.broadcast_to(router_bias.astype(F32)[:, None], (E, LANES))
    w_dense, pos_dense, cnt = _route(logits_t, bias_col)

    counts = cnt[:, 0].astype(jnp.int32)
    pcounts = (counts + MOE_BLOCK - 1) // MOE_BLOCK * MOE_BLOCK
    pends = jnp.cumsum(pcounts)
    pstarts = pends - pcounts
    n_blocks = -(-(n_rows * TOP_K) // MOE_BLOCK) + E
    n_slots = n_blocks * MOE_BLOCK
    block_lo = jnp.arange(n_blocks, dtype=jnp.int32) * MOE_BLOCK
    block_e = jnp.minimum(jnp.sum(pends[None, :] <= block_lo[:, None], axis=1), E - 1).astype(jnp.int32)
    n_used = (pends[-1:] // MOE_BLOCK).astype(jnp.int32)
    pstart_col = jnp.broadcast_to(pstarts.astype(F32)[:, None], (E, LANES))

    slots, w8 = _compact(w_dense, pos_dense, pstart_col)
    t = _pick_tile(n_rows, 128, LANES)
    n_tiles = n_rows // t
    slots_tiled = slots.reshape(TOP_K, n_tiles, t).transpose(1, 0, 2).reshape(n_tiles, 1, TOP_K * t)
    xs = _dispatch(h, slots_tiled, (pstarts + counts).astype(jnp.int32), pends.astype(jnp.int32),
                   n_slots, t)
    ys = _experts(xs, w1, w3, w2, layer, block_e, n_used)
    if drop_rows == t:
        per_seq = seq_rows // t
        out_rows = n_rows - (n_rows // seq_rows) * t

        def out_map(i):
            return (i // per_seq) * (per_seq - 1) + jnp.maximum(i % per_seq - 1, 0)
    else:
        out_rows, out_map = n_rows, (lambda i: i)
    return _combine(h, ys, slots_tiled, w8.T, ws1, ws3, ws2, g, b, t, out_rows, out_map)


def kernel(x, meta_tokens, gla_w_in, gla_w_gate_up, gla_b_gate, gla_norm_g, gla_w_out,
           conv_w_in, conv_w, conv_w_out, ln1_g, ln1_b, router_w, router_bias,
           exp_w1, exp_w3, exp_w2, shared_w1, shared_w3, shared_w2, ln2_g, ln2_b):
    batch, seq, d = x.shape
    H = GLA_HEADS
    dk = d // 2 // H
    n_qkvr = 2 * H * dk + 2 * d
    rank = gla_w_in.shape[2] - n_qkvr
    seq_rows = -(-(N_META + seq) // ROW_ALIGN) * ROW_ALIGN
    pad_rows = seq_rows - N_META - seq
    n_rows = batch * seq_rows

    meta = jnp.broadcast_to(meta_tokens.astype(x.dtype)[None], (batch, N_META, d))
    h = jnp.concatenate([jnp.zeros((batch, pad_rows, d), x.dtype), meta, x], axis=1)
    h = h.reshape(n_rows, d)
    hb = h.astype(BF16)

    def row(v):
        return v.astype(F32)[None, :]

    for i in range(DEPTH):
        jm = i // 2
        if i % 2 == 0:
            w_in = gla_w_in[jm]
            w_qkvr = w_in[:, :n_qkvr].astype(BF16)
            w_gate = jnp.pad(w_in[:, n_qkvr:], ((0, 0), (0, GATE_PAD - rank))).astype(BF16)
            wg_up = jnp.pad(gla_w_gate_up[jm], ((0, GATE_PAD - rank), (0, 0))).astype(BF16)
            qkvr = _matmul(hb, w_qkvr, BF16, 1664, 768, "gla_in_proj")
            glow = _matmul(hb, w_gate, F32, 1664, LANES, "gla_gate_proj")
            mix = _gla(qkvr, glow, wg_up, row(gla_b_gate[jm]), row(gla_norm_g[jm]),
                       batch=batch, seq_rows=seq_rows, d_model=d)
            w_out = gla_w_out[jm].astype(BF16)
        else:
            mix = _conv_mix(hb, conv_w_in[jm].astype(BF16), conv_w[jm].astype(F32),
                            seq_rows=seq_rows, pad_rows=pad_rows)
            w_out = conv_w_out[jm].astype(BF16)
        h, logits_t = _proj_ln_router(mix, w_out, h, row(ln1_g[i]), row(ln1_b[i]),
                                      router_w[i].T.astype(BF16))
        drop = pad_rows + N_META if i == DEPTH - 1 else 0
        h, hb = _moe_ln(h, logits_t, router_bias[i], exp_w1, exp_w3, exp_w2, i,
                        shared_w1[i].astype(BF16), shared_w3[i].astype(BF16),
                        shared_w2[i].astype(BF16), row(ln2_g[i]), row(ln2_b[i]), drop, seq_rows)
    if h.shape[0] == batch * seq:
        return h.reshape(batch, seq, d)
    return h.reshape(batch, seq_rows, d)[:, pad_rows + N_META:]
```

```python
import functools

import jax
import jax.numpy as jnp
from jax import lax
from jax.experimental import pallas as pl
from jax.experimental.pallas import tpu as pltpu

N_META = 16
GLA_HEADS = 4
GLA_GATE_TAU = 16.0
CONV_WIDTH = 3
N_EXPERTS = 64
TOP_K = 8
N_GROUPS = 8
TOPK_GROUPS = 4
ROUTED_SCALE = 2.5
LN_EPS = 1e-5
RMS_EPS = 1e-6
DEPTH = 2
DN_ALPHA = (2 * DEPTH) ** 0.25
LOG2_E = 1.4426950408889634

LANES = 128
SUBLANES = 8
ROW_ALIGN = 128
GLA_CHUNK = 64
GLA_SUB = 16
GLA_HEADS_PER_STEP = 4
GATE_PAD = LANES
MOE_BLOCK = 512
VMEM_LIMIT = 56 * 1024 * 1024

F32 = jnp.float32
BF16 = jnp.bfloat16


def _pick_tile(n, target, mult):
    best = None
    for t in range(mult, min(n, target) + 1, mult):
        if n % t == 0:
            best = t
    assert best is not None, (n, target, mult)
    return best


def _cparams(sem):
    return pltpu.CompilerParams(dimension_semantics=sem, vmem_limit_bytes=VMEM_LIMIT)


def _mm_kernel(x_ref, w_ref, o_ref):
    x = x_ref[...].astype(BF16)
    o_ref[...] = jnp.dot(x, w_ref[...], preferred_element_type=F32).astype(o_ref.dtype)


def _matmul(x, w, out_dtype, tm_target, tn_target, name):
    m, k = x.shape
    n = w.shape[1]
    tm = _pick_tile(m, tm_target, 16)
    tn = _pick_tile(n, tn_target, LANES)
    return pl.pallas_call(
        _mm_kernel,
        grid=(m // tm, n // tn),
        in_specs=[pl.BlockSpec((tm, k), lambda i, j: (i, 0)),
                  pl.BlockSpec((k, tn), lambda i, j: (0, j))],
        out_specs=pl.BlockSpec((tm, tn), lambda i, j: (i, j)),
        out_shape=jax.ShapeDtypeStruct((m, n), out_dtype),
        compiler_params=_cparams(("parallel", "arbitrary")),
        name=name,
    )(x, w)


def _gla_kernel(q_ref, k_ref, v_ref, r_ref, gl_ref, wg_ref, bg_ref, ng_ref, o_ref, st_ref,
                *, n_chunks, dk, dv, heads):
    C, S = GLA_CHUNK, GLA_SUB

    @pl.when(pl.program_id(2) == 0)
    def _():
        st_ref[...] = jnp.zeros_like(st_ref)

    ri = lax.broadcasted_iota(jnp.int32, (C, C), 0)
    ci = lax.broadcasted_iota(jnp.int32, (C, C), 1)
    tri = (ri >= ci).astype(BF16)
    rs = lax.broadcasted_iota(jnp.int32, (S, S), 0)
    cs = lax.broadcasted_iota(jnp.int32, (S, S), 1)
    nt = (((1,), (1,)), ((), ()))
    tn = (((0,), (0,)), ((), ()))
    half = S // 2

    def chunk(c, carry):
        r0 = pl.multiple_of(c * C, C)
        rows_c = pl.ds(r0, C)
        hs = range(heads)
        kcol = [pl.ds(hd * dk, dk) for hd in hs]
        vcol = [pl.ds(hd * dv, dv) for hd in hs]
        gl = gl_ref[rows_c, :].astype(BF16)
        q = [q_ref[rows_c, kcol[hd]].astype(F32) * (dk ** -0.5) for hd in hs]
        kk = [k_ref[rows_c, kcol[hd]].astype(F32) for hd in hs]
        vv = [v_ref[rows_c, vcol[hd]] for hd in hs]
        st = [st_ref[hd] for hd in hs]
        z = [jnp.dot(gl, wg_ref[:, kcol[hd]], preferred_element_type=F32) + bg_ref[:, kcol[hd]]
             for hd in hs]
        b = []
        for hd in hs:
            la = ((jnp.minimum(z[hd], 0.0) - jnp.log(1.0 + jnp.exp(-jnp.abs(z[hd]))))
                  * (1.0 / GLA_GATE_TAU))
            h1 = la.astype(BF16)
            e1 = la - h1.astype(F32)
            h2 = e1.astype(BF16)
            h3 = (e1 - h2.astype(F32)).astype(BF16)
            b3 = jnp.dot(tri, jnp.concatenate([h1, h2, h3], axis=1), preferred_element_type=F32)
            b.append((b3[:, :dk] + b3[:, dk:2 * dk] + b3[:, 2 * dk:]) * LOG2_E)

        o_inter, st_new, off = [], [], []
        for hd in hs:
            qe = (q[hd] * jnp.exp2(b[hd])).astype(BF16)
            o_inter.append(lax.dot_general(qe, st[hd].astype(BF16), nt, preferred_element_type=F32))
            bl = b[hd][C - 1:C]
            khat = (kk[hd] * jnp.exp2(bl - b[hd])).astype(BF16)
            upd = lax.dot_general(vv[hd], khat, tn, preferred_element_type=F32)
            st_new.append(st[hd] * jnp.exp2(bl) + upd)
            offs = []
            for i in range(1, C // S):
                lo = i * S
                bref = b[hd][lo - 1:lo]
                qt = (q[hd][lo:lo + S] * jnp.exp2(b[hd][lo:lo + S] - bref)).astype(BF16)
                kt = (kk[hd][:lo] * jnp.exp2(bref - b[hd][:lo])).astype(BF16)
                offs.append(lax.dot_general(qt, kt, nt, preferred_element_type=F32))
            off.append(offs)

        dmats = []
        for hd in hs:
            blocks = []
            for i in range(C // S):
                lo = i * S
                qs = q[hd][lo:lo + S]
                bs = b[hd][lo:lo + S]
                dmat = jnp.zeros((S, S), F32)
                for j in range(S):
                    kj = kk[hd][lo + j:lo + j + 1]
                    bj = b[hd][lo + j:lo + j + 1]
                    if j < half:
                        p = qs * kj * jnp.exp2(bs - bj)
                        a = jnp.sum(p, axis=-1, keepdims=True)
                    else:
                        p = qs[half:] * kj * jnp.exp2(bs[half:] - bj)
                        a = jnp.concatenate([jnp.zeros((half, 1), F32),
                                             jnp.sum(p, axis=-1, keepdims=True)], axis=0)
                    dmat = jnp.where(cs == j, a, dmat)
                blocks.append(jnp.where(rs >= cs, dmat, 0.0).astype(BF16))
            dmats.append(blocks)

        for hd in hs:
            rows = []
            for i in range(C // S):
                lo = i * S
                o_i = jnp.dot(dmats[hd][i], vv[hd][lo:lo + S], preferred_element_type=F32)
                if i > 0:
                    o_i = o_i + jnp.dot(off[hd][i - 1].astype(BF16), vv[hd][:lo],
                                        preferred_element_type=F32)
                rows.append(o_i)
            o = o_inter[hd] + jnp.concatenate(rows, axis=0)
            ms = jnp.mean(o * o, axis=-1, keepdims=True)
            r = r_ref[rows_c, vcol[hd]].astype(F32)
            y = o * lax.rsqrt(ms + RMS_EPS) * ng_ref[:, vcol[hd]] * (r * jax.nn.sigmoid(r))
            st_ref[hd] = st_new[hd]
            o_ref[rows_c, vcol[hd]] = y.astype(o_ref.dtype)
        return carry

    lax.fori_loop(0, n_chunks, chunk, 0)


def _gla(qkvr, glow, wg, bg, ng, *, batch, seq_rows, d_model):
    H, hps = GLA_HEADS, GLA_HEADS_PER_STEP
    dk = d_model // 2 // H
    dv = d_model // H
    n_rows = batch * seq_rows
    rblk = _pick_tile(seq_rows, 640, GLA_CHUNK)
    steps = seq_rows // rblk
    groups = H // hps
    off_k, off_v, off_r = groups, (2 * H * dk) // (hps * dv), (2 * H * dk) // (hps * dv) + groups

    def rowmap(off):
        return lambda b, h, i: (b * steps + i, off + h)

    return pl.pallas_call(
        functools.partial(_gla_kernel, n_chunks=rblk // GLA_CHUNK, dk=dk, dv=dv, heads=hps),
        grid=(batch, groups, steps),
        in_specs=[pl.BlockSpec((rblk, hps * dk), rowmap(0)),
                  pl.BlockSpec((rblk, hps * dk), rowmap(off_k)),
                  pl.BlockSpec((rblk, hps * dv), rowmap(off_v)),
                  pl.BlockSpec((rblk, hps * dv), rowmap(off_r)),
                  pl.BlockSpec((rblk, GATE_PAD), lambda b, h, i: (b * steps + i, 0)),
                  pl.BlockSpec((GATE_PAD, hps * dk), lambda b, h, i: (0, h)),
                  pl.BlockSpec((1, hps * dk), lambda b, h, i: (0, h)),
                  pl.BlockSpec((1, hps * dv), lambda b, h, i: (0, h))],
        out_specs=pl.BlockSpec((rblk, hps * dv), lambda b, h, i: (b * steps + i, h)),
        out_shape=jax.ShapeDtypeStruct((n_rows, H * dv), BF16),
        scratch_shapes=[pltpu.VMEM((hps, dv, dk), F32)],
        compiler_params=_cparams(("parallel", "parallel", "arbitrary")),
        name="gla_chunks",
    )(qkvr, qkvr, qkvr, qkvr, glow, wg, bg, ng)


def _conv_kernel(x_ref, wb_ref, wc_ref, wh_ref, cw_ref, o_ref, ubuf_ref, carry_ref,
                 *, seq_rows, pad_rows, tm):
    i = pl.program_id(0)
    j = pl.program_id(1)
    x = x_ref[...]
    bg = jnp.dot(x, wb_ref[...], preferred_element_type=F32)
    cg = jnp.dot(x, wc_ref[...], preferred_element_type=F32)
    hh = jnp.dot(x, wh_ref[...], preferred_element_type=F32)
    row = i * tm + lax.broadcasted_iota(jnp.int32, (tm, 1), 0)
    u = jnp.where(row % seq_rows >= pad_rows, cg * hh, 0.0)

    @pl.when(i == 0)
    def _():
        carry_ref[j] = jnp.zeros(carry_ref.shape[1:], F32)

    ubuf_ref[pl.ds(0, SUBLANES), :] = carry_ref[j]
    ubuf_ref[pl.ds(SUBLANES, tm), :] = u
    carry_ref[j] = u[tm - SUBLANES:]
    cw = cw_ref[...]
    conv = u * cw[CONV_WIDTH - 1:CONV_WIDTH]
    for s in range(1, CONV_WIDTH):
        conv = conv + ubuf_ref[pl.ds(SUBLANES - s, tm), :] * cw[CONV_WIDTH - 1 - s:CONV_WIDTH - s]
    o_ref[...] = (bg * conv).astype(o_ref.dtype)


def _conv_mix(xb, w_in, conv_w, *, seq_rows, pad_rows):
    n_rows, d = xb.shape
    tm = _pick_tile(n_rows, 1024, SUBLANES)
    tn = _pick_tile(d, 512, LANES)
    nj = d // tn
    return pl.pallas_call(
        functools.partial(_conv_kernel, seq_rows=seq_rows, pad_rows=pad_rows, tm=tm),
        grid=(n_rows // tm, nj),
        in_specs=[pl.BlockSpec((tm, d), lambda i, j: (i, 0)),
                  pl.BlockSpec((d, tn), lambda i, j: (0, j)),
                  pl.BlockSpec((d, tn), lambda i, j: (0, nj + j)),
                  pl.BlockSpec((d, tn), lambda i, j: (0, 2 * nj + j)),
                  pl.BlockSpec((CONV_WIDTH, tn), lambda i, j: (0, j))],
        out_specs=pl.BlockSpec((tm, tn), lambda i, j: (i, j)),
        out_shape=jax.ShapeDtypeStruct((n_rows, d), BF16),
        scratch_shapes=[pltpu.VMEM((tm + SUBLANES, tn), F32),
                        pltpu.VMEM((nj, SUBLANES, tn), F32)],
        compiler_params=_cparams(("arbitrary", "arbitrary")),
        name="conv_mix",
    )(xb, w_in, w_in, w_in, conv_w)


def _layer_norm(y, g, b):
    mu = jnp.mean(y, axis=-1, keepdims=True)
    yc = y - mu
    var = jnp.mean(yc * yc, axis=-1, keepdims=True)
    return yc * lax.rsqrt(var + LN_EPS) * g + b


def _proj_ln_kernel(a_ref, w_ref, h_ref, g_ref, b_ref, rw_ref, o_ref, lg_ref):
    y = jnp.dot(a_ref[...], w_ref[...], preferred_element_type=F32) + DN_ALPHA * h_ref[...]
    h = _layer_norm(y, g_ref[...], b_ref[...])
    o_ref[...] = h
    lg_ref[...] = lax.dot_general(rw_ref[...], h.astype(BF16), (((1,), (1,)), ((), ())),
                                  preferred_element_type=F32)


def _proj_ln_router(a, w, h, g, b, rw_t):
    n_rows, kin = a.shape
    d = w.shape[1]
    e = rw_t.shape[0]
    tm = _pick_tile(n_rows, 512, LANES)
    return pl.pallas_call(
        _proj_ln_kernel,
        grid=(n_rows // tm,),
        in_specs=[pl.BlockSpec((tm, kin), lambda i: (i, 0)),
                  pl.BlockSpec((kin, d), lambda i: (0, 0)),
                  pl.BlockSpec((tm, d), lambda i: (i, 0)),
                  pl.BlockSpec((1, d), lambda i: (0, 0)),
                  pl.BlockSpec((1, d), lambda i: (0, 0)),
                  pl.BlockSpec((e, d), lambda i: (0, 0))],
        out_specs=[pl.BlockSpec((tm, d), lambda i: (i, 0)),
                   pl.BlockSpec((e, tm), lambda i: (0, i))],
        out_shape=[jax.ShapeDtypeStruct((n_rows, d), F32),
                   jax.ShapeDtypeStruct((e, n_rows), F32)],
        compiler_params=_cparams(("parallel",)),
        name="proj_ln_router",
    )(a, w, h, g, b, rw_t)


def _beats(cand, cand_idx, ref, ref_idx):
    return (cand > ref) | ((cand == ref) & (cand_idx < ref_idx))


def _route_kernel(lg_ref, bias_ref, w_ref, pos_ref, cnt_ref, run_ref, *, t):
    E, G = N_EXPERTS, N_GROUPS
    gs = E // G

    @pl.when(pl.program_id(0) == 0)
    def _():
        run_ref[...] = jnp.zeros_like(run_ref)

    s = jax.nn.sigmoid(lg_ref[...])
    c = s + bias_ref[...][:, :1]
    sub = lax.broadcasted_iota(jnp.int32, (gs, t), 0)

    grp_rows = []
    for g in range(G):
        cg = c[g * gs:(g + 1) * gs]
        rank = jnp.zeros((gs, t), jnp.int32)
        for m in range(gs):
            rank = rank + _beats(cg[m:m + 1], m, cg, sub).astype(jnp.int32)
        grp_rows.append(jnp.sum(jnp.where(rank < 2, cg, 0.0), axis=0, keepdims=True))
    gidx = lax.broadcasted_iota(jnp.int32, (G, t), 0)
    gscore = jnp.zeros((G, t), F32)
    for g in range(G):
        gscore = jnp.where(gidx == g, grp_rows[g], gscore)
    grank = jnp.zeros((G, t), jnp.int32)
    for m in range(G):
        grank = grank + _beats(gscore[m:m + 1], m, gscore, gidx).astype(jnp.int32)
    gsel = grank < TOPK_GROUPS

    masked = jnp.concatenate(
        [jnp.where(gsel[g:g + 1], c[g * gs:(g + 1) * gs], -jnp.inf) for g in range(G)], axis=0)
    eidx = lax.broadcasted_iota(jnp.int32, (E, t), 0)
    rank = jnp.zeros((E, t), jnp.int32)
    for m in range(E):
        rank = rank + _beats(masked[m:m + 1], m, masked, eidx).astype(jnp.int32)
    sel = rank < TOP_K
    gate = jnp.where(sel, s, 0.0)
    w_ref[...] = gate / jnp.sum(gate, axis=0, keepdims=True) * ROUTED_SCALE

    li = lax.broadcasted_iota(jnp.int32, (t, t), 0)
    lj = lax.broadcasted_iota(jnp.int32, (t, t), 1)
    upper = (li <= lj).astype(BF16)
    self_ = sel.astype(F32)
    incl = jnp.dot(sel.astype(BF16), upper, preferred_element_type=F32)
    run = run_ref[...][:, :1]
    pos_ref[...] = jnp.where(sel, run + incl - self_, -1.0)
    run_new = run + jnp.sum(self_, axis=1, keepdims=True)
    run_ref[...] = jnp.broadcast_to(run_new, run_ref.shape)
    cnt_ref[...] = jnp.broadcast_to(run_new, cnt_ref.shape)


def _route(logits_t, bias_col):
    e, n_rows = logits_t.shape
    t = _pick_tile(n_rows, 512, LANES)
    return pl.pallas_call(
        functools.partial(_route_kernel, t=t),
        grid=(n_rows // t,),
        in_specs=[pl.BlockSpec((e, t), lambda i: (0, i)),
                  pl.BlockSpec((e, LANES), lambda i: (0, 0))],
        out_specs=[pl.BlockSpec((e, t), lambda i: (0, i)),
                   pl.BlockSpec((e, t), lambda i: (0, i)),
                   pl.BlockSpec((e, LANES), lambda i: (0, 0))],
        out_shape=[jax.ShapeDtypeStruct((e, n_rows), F32),
                   jax.ShapeDtypeStruct((e, n_rows), F32),
                   jax.ShapeDtypeStruct((e, LANES), F32)],
        scratch_shapes=[pltpu.VMEM((e, LANES), F32)],
        compiler_params=_cparams(("arbitrary",)),
        name="route",
    )(logits_t, bias_col)


def _compact_kernel(w_ref, pos_ref, pst_ref, slot_ref, w8_ref, *, t):
    E = N_EXPERTS
    pos = pos_ref[...]
    sel = pos >= 0.0
    ri = lax.broadcasted_iota(jnp.int32, (E, E), 0)
    ci = lax.broadcasted_iota(jnp.int32, (E, E), 1)
    below = (ci < ri).astype(BF16)
    order = jnp.dot(below, sel.astype(BF16), preferred_element_type=F32)
    slot = pst_ref[...][:, :1] + pos
    wd = w_ref[...]
    kidx = lax.broadcasted_iota(jnp.int32, (TOP_K, t), 0)
    slots = jnp.zeros((TOP_K, t), F32)
    w8 = jnp.zeros((TOP_K, t), F32)
    for k in range(TOP_K):
        m = sel & (order == float(k))
        slots = jnp.where(kidx == k, jnp.sum(jnp.where(m, slot, 0.0), axis=0, keepdims=True), slots)
        w8 = jnp.where(kidx == k, jnp.sum(jnp.where(m, wd, 0.0), axis=0, keepdims=True), w8)
    slot_ref[...] = (slots * float(SUBLANES)).astype(jnp.int32)
    w8_ref[...] = w8


def _compact(w_dense, pos_dense, pstart_col):
    e, n_rows = w_dense.shape
    t = _pick_tile(n_rows, 512, LANES)
    return pl.pallas_call(
        functools.partial(_compact_kernel, t=t),
        grid=(n_rows // t,),
        in_specs=[pl.BlockSpec((e, t), lambda i: (0, i)),
                  pl.BlockSpec((e, t), lambda i: (0, i)),
                  pl.BlockSpec((e, LANES), lambda i: (0, 0))],
        out_specs=[pl.BlockSpec((TOP_K, t), lambda i: (0, i)),
                   pl.BlockSpec((TOP_K, t), lambda i: (0, i))],
        out_shape=[jax.ShapeDtypeStruct((TOP_K, n_rows), jnp.int32),
                   jax.ShapeDtypeStruct((TOP_K, n_rows), F32)],
        compiler_params=_cparams(("parallel",)),
        name="compact",
    )(w_dense, pos_dense, pstart_col)


def _pack_bf16_pairs(x):
    half = x.shape[1] // 2
    lo = lax.bitcast_convert_type(x[:, :half].astype(BF16).astype(F32), jnp.uint32)
    hi = lax.bitcast_convert_type(x[:, half:].astype(BF16).astype(F32), jnp.uint32)
    return (lo >> 16) | hi


def _unpack_bf16_pairs(w):
    lo = lax.bitcast_convert_type(w << 16, F32)
    hi = lax.bitcast_convert_type(w & jnp.uint32(0xFFFF0000), F32)
    return lo, hi


def _store_rows_as_tiles(ref_at, x, rows):
    for c in range(SUBLANES):
        ref_at[pl.ds(c, rows, stride=SUBLANES), :] = x[:, c * LANES:(c + 1) * LANES]


def _load_tile_chunk(ref_at, c, rows):
    return ref_at[pl.ds(c, rows, stride=SUBLANES), :]


def _dispatch_kernel(pad_lo_ref, pad_hi_ref, slot_ref, h_ref, xs_ref, hp_ref, zero_ref, sem, zsem,
                     *, t):
    i = pl.program_id(0)
    cur = i % 2
    _store_rows_as_tiles(hp_ref.at[cur], _pack_bf16_pairs(h_ref[...]), t)

    def issue(tok, carry):
        src = hp_ref.at[cur, pl.ds(pl.multiple_of(tok * SUBLANES, SUBLANES), SUBLANES), :]
        for k in range(TOP_K):
            dst = xs_ref.at[pl.ds(pl.multiple_of(slot_ref[k * t + tok], SUBLANES), SUBLANES), :]
            pltpu.make_async_copy(src, dst, sem.at[cur]).start(priority=k % 2)
        return carry

    lax.fori_loop(0, t, issue, 0, unroll=2)

    @pl.when(i == 0)
    def _():
        zero_ref[...] = jnp.zeros_like(zero_ref)

        def zero_copy(p):
            return pltpu.make_async_copy(
                zero_ref, xs_ref.at[pl.ds(pl.multiple_of(p * SUBLANES, SUBLANES), SUBLANES), :], zsem)

        def per_expert(step):
            def body(e, carry):
                lax.fori_loop(pad_lo_ref[e], pad_hi_ref[e], step, 0)
                return carry
            return body

        def zissue(p, c):
            zero_copy(p).start()
            return c

        def zwait(p, c):
            zero_copy(p).wait()
            return c

        lax.fori_loop(0, N_EXPERTS, per_expert(zissue), 0)
        lax.fori_loop(0, N_EXPERTS, per_expert(zwait), 0)

    def wait_tile(buf):
        for k in range(TOP_K):
            pltpu.make_async_copy(hp_ref.at[buf], xs_ref.at[pl.ds(0, t * SUBLANES), :],
                                  sem.at[buf]).wait()

    @pl.when(i > 0)
    def _():
        wait_tile(1 - cur)

    @pl.when(i == pl.num_programs(0) - 1)
    def _():
        wait_tile(cur)


def _dispatch(h, slots_tiled, pad_lo, pad_hi, n_slots, t):
    n_rows, d = h.shape
    n_tiles = n_rows // t
    return pl.pallas_call(
        functools.partial(_dispatch_kernel, t=t),
        grid_spec=pltpu.PrefetchScalarGridSpec(
            num_scalar_prefetch=2,
            grid=(n_tiles,),
            in_specs=[pl.BlockSpec((None, None, TOP_K * t), lambda i, lo, hi: (i, 0, 0),
                                   memory_space=pltpu.SMEM),
                      pl.BlockSpec((t, d), lambda i, lo, hi: (i, 0))],
            out_specs=pl.BlockSpec(memory_space=pl.ANY),
            scratch_shapes=[pltpu.VMEM((2, t * SUBLANES, LANES), jnp.uint32),
                            pltpu.VMEM((SUBLANES, LANES), jnp.uint32),
                            pltpu.SemaphoreType.DMA((2,)),
                            pltpu.SemaphoreType.DMA(())]),
        out_shape=jax.ShapeDtypeStruct((n_slots * SUBLANES, LANES), jnp.uint32),
        compiler_params=_cparams(("arbitrary",)),
        name="dispatch",
    )(pad_lo, pad_hi, slots_tiled, h)


def _swiglu(x, w1, w3, w2):
    a = jnp.dot(x, w1, preferred_element_type=F32)
    g = jnp.dot(x, w3, preferred_element_type=F32)
    hmid = (a * jax.nn.sigmoid(a) * g).astype(BF16)
    return jnp.dot(hmid, w2, preferred_element_type=F32)


def _expert_kernel(be_ref, nu_ref, x_ref, w1_ref, w3_ref, w2_ref, y_ref, w1b_ref, w3b_ref, w2b_ref):
    b = pl.program_id(0)

    @pl.when(b < nu_ref[0])
    def _():
        @pl.when((b == 0) | (be_ref[b] != be_ref[jnp.maximum(b - 1, 0)]))
        def _():
            w1b_ref[...] = w1_ref[...].astype(BF16)
            w3b_ref[...] = w3_ref[...].astype(BF16)
            w2b_ref[...] = w2_ref[...].astype(BF16)

        halves = [_unpack_bf16_pairs(_load_tile_chunk(x_ref, c, MOE_BLOCK)) for c in range(SUBLANES)]
        x = jnp.concatenate([lo.astype(BF16) for lo, _ in halves]
                            + [hi.astype(BF16) for _, hi in halves], axis=1)
        y = _pack_bf16_pairs(_swiglu(x, w1b_ref[...], w3b_ref[...], w2b_ref[...]))
        _store_rows_as_tiles(y_ref, y, MOE_BLOCK)


def _experts(xs, w1, w3, w2, layer, block_e, n_used):
    d, f = w1.shape[2], w1.shape[3]
    blk_rows = MOE_BLOCK * SUBLANES
    nb = xs.shape[0] // blk_rows

    def blk(b, be, nu):
        return jnp.minimum(b, nu[0] - 1)

    def wmap(b, be, nu):
        return (layer, be[blk(b, be, nu)], 0, 0)

    return pl.pallas_call(
        _expert_kernel,
        grid_spec=pltpu.PrefetchScalarGridSpec(
            num_scalar_prefetch=2,
            grid=(nb,),
            in_specs=[pl.BlockSpec((blk_rows, LANES), lambda b, be, nu: (blk(b, be, nu), 0)),
                      pl.BlockSpec((None, None, d, f), wmap),
                      pl.BlockSpec((None, None, d, f), wmap),
                      pl.BlockSpec((None, None, f, d), wmap)],
            out_specs=pl.BlockSpec((blk_rows, LANES), lambda b, be, nu: (blk(b, be, nu), 0)),
            scratch_shapes=[pltpu.VMEM((d, f), BF16), pltpu.VMEM((d, f), BF16),
                            pltpu.VMEM((f, d), BF16)]),
        out_shape=jax.ShapeDtypeStruct(xs.shape, jnp.uint32),
        compiler_params=_cparams(("arbitrary",)),
        name="experts",
    )(block_e, n_used, xs, w1, w3, w2)


def _combine_kernel(slot_ref, nslot_ref, h_ref, w8_ref, ws1_ref, ws3_ref, ws2_ref, g_ref, b_ref,
                    ys_ref, o_ref, ob_ref, ybuf0_ref, ybuf1_ref, sem, *, t):
    i = pl.program_id(0)
    last = pl.num_programs(0) - 1
    ybuf = (ybuf0_ref, ybuf1_ref)

    def row_copy(s_ref, buf, k, tok):
        src = ys_ref.at[pl.ds(pl.multiple_of(s_ref[k * t + tok], SUBLANES), SUBLANES), :]
        return pltpu.make_async_copy(
            src, ybuf[buf].at[k, pl.ds(pl.multiple_of(tok * SUBLANES, SUBLANES), SUBLANES), :],
            sem.at[buf])

    def wait_tile(buf):
        for k in range(TOP_K):
            pltpu.make_async_copy(ys_ref.at[pl.ds(0, t * SUBLANES), :], ybuf[buf].at[k],
                                  sem.at[buf]).wait()

    @pl.when(i == 0)
    def _():
        def issue(tok, carry):
            for k in range(TOP_K):
                row_copy(slot_ref, 0, k, tok).start(priority=k % 2)
            return carry

        lax.fori_loop(0, t, issue, 0, unroll=2)

    group = t // TOP_K

    def reduce_tile(cur):
        nxt = 1 - cur

        def issue_group(g):
            for tok in range(g * group, (g + 1) * group):
                for k in range(TOP_K):
                    row_copy(nslot_ref, nxt, k, tok).start(priority=k % 2)

        wait_tile(cur)
        h = h_ref[...]
        issue_group(0)
        shared = _swiglu(h.astype(BF16), ws1_ref[...], ws3_ref[...], ws2_ref[...])
        w8 = w8_ref[...]
        wk = [w8[:, k:k + 1] for k in range(TOP_K)]
        acc_lo, acc_hi = [], []
        for c in range(SUBLANES):
            if c + 1 < TOP_K:
                issue_group(c + 1)
            lo_c, hi_c = None, None
            for k in range(TOP_K):
                lo, hi = _unpack_bf16_pairs(_load_tile_chunk(ybuf[cur].at[k], c, t))
                lo_c = lo * wk[k] if k == 0 else lo_c + lo * wk[k]
                hi_c = hi * wk[k] if k == 0 else hi_c + hi * wk[k]
            acc_lo.append(lo_c)
            acc_hi.append(hi_c)
        acc = jnp.concatenate(acc_lo + acc_hi, axis=1)
        y = DN_ALPHA * h + (acc + shared)
        out = _layer_norm(y, g_ref[...], b_ref[...])
        o_ref[...] = out
        ob_ref[...] = out.astype(BF16)

        @pl.when(i == last)
        def _():
            wait_tile(nxt)

    for parity in range(2):
        @pl.when(i % 2 == parity)
        def _():
            reduce_tile(parity)


def _combine(h, ys, slots_tiled, w8_t, ws1, ws3, ws2, g, b, t, out_rows, out_map):
    n_rows, d = h.shape
    f = ws1.shape[1]
    n_tiles = n_rows // t
    return pl.pallas_call(
        functools.partial(_combine_kernel, t=t),
        grid=(n_tiles,),
        in_specs=[pl.BlockSpec((None, None, TOP_K * t), lambda i: (i, 0, 0),
                               memory_space=pltpu.SMEM),
                  pl.BlockSpec((None, None, TOP_K * t),
                               lambda i: (jnp.minimum(i + 1, n_tiles - 1), 0, 0),
                               memory_space=pltpu.SMEM),
                  pl.BlockSpec((t, d), lambda i: (i, 0)),
                  pl.BlockSpec((t, TOP_K), lambda i: (i, 0)),
                  pl.BlockSpec((d, f), lambda i: (0, 0)),
                  pl.BlockSpec((d, f), lambda i: (0, 0)),
                  pl.BlockSpec((f, d), lambda i: (0, 0)),
                  pl.BlockSpec((1, d), lambda i: (0, 0)),
                  pl.BlockSpec((1, d), lambda i: (0, 0)),
                  pl.BlockSpec(memory_space=pl.ANY)],
        out_specs=[pl.BlockSpec((t, d), lambda i: (out_map(i), 0)),
                   pl.BlockSpec((t, d), lambda i: (i, 0))],
        out_shape=[jax.ShapeDtypeStruct((out_rows, d), F32),
                   jax.ShapeDtypeStruct((n_rows, d), BF16)],
        scratch_shapes=[pltpu.VMEM((TOP_K, t * SUBLANES, LANES), jnp.uint32),
                        pltpu.VMEM((TOP_K, t * SUBLANES, LANES), jnp.uint32),
                        pltpu.SemaphoreType.DMA((2,))],
        compiler_params=_cparams(("arbitrary",)),
        name="combine_ln",
    )(slots_tiled, slots_tiled, h, w8_t, ws1, ws3, ws2, g, b, ys)


def _moe_ln(h, logits_t, router_bias, w1, w3, w2, layer, ws1, ws3, ws2, g, b, drop_rows, seq_rows):
    n_rows, d = h.shape
    E = N_EXPERTS
    assert d // 2 == SUBLANES * LANES, d
    bias_col = jnp.broadcast_to(router_bias.astype(F32)[:, None], (E, LANES))
    w_dense, pos_dense, cnt = _route(logits_t, bias_col)

    counts = cnt[:, 0].astype(jnp.int32)
    pcounts = (counts + MOE_BLOCK - 1) // MOE_BLOCK * MOE_BLOCK
    pends = jnp.cumsum(pcounts)
    pstarts = pends - pcounts
    n_blocks = -(-(n_rows * TOP_K) // MOE_BLOCK) + E
    n_slots = n_blocks * MOE_BLOCK
    block_lo = jnp.arange(n_blocks, dtype=jnp.int32) * MOE_BLOCK
    block_e = jnp.minimum(jnp.sum(pends[None, :] <= block_lo[:, None], axis=1), E - 1).astype(jnp.int32)
    n_used = (pends[-1:] // MOE_BLOCK).astype(jnp.int32)
    pstart_col = jnp.broadcast_to(pstarts.astype(F32)[:, None], (E, LANES))

    slots, w8 = _compact(w_dense, pos_dense, pstart_col)
    t = _pick_tile(n_rows, 128, LANES)
    n_tiles = n_rows // t
    slots_tiled = slots.reshape(TOP_K, n_tiles, t).transpose(1, 0, 2).reshape(n_tiles, 1, TOP_K * t)
    xs = _dispatch(h, slots_tiled, (pstarts + counts).astype(jnp.int32), pends.astype(jnp.int32),
                   n_slots, t)
    ys = _experts(xs, w1, w3, w2, layer, block_e, n_used)
    if drop_rows == t:
        per_seq = seq_rows // t
        out_rows = n_rows - (n_rows // seq_rows) * t

        def out_map(i):
            return (i // per_seq) * (per_seq - 1) + jnp.maximum(i % per_seq - 1, 0)
    else:
        out_rows, out_map = n_rows, (lambda i: i)
    return _combine(h, ys, slots_tiled, w8.T, ws1, ws3, ws2, g, b, t, out_rows, out_map)


def kernel(x, meta_tokens, gla_w_in, gla_w_gate_up, gla_b_gate, gla_norm_g, gla_w_out,
           conv_w_in, conv_w, conv_w_out, ln1_g, ln1_b, router_w, router_bias,
           exp_w1, exp_w3, exp_w2, shared_w1, shared_w3, shared_w2, ln2_g, ln2_b):
    batch, seq, d = x.shape
    H = GLA_HEADS
    dk = d // 2 // H
    n_qkvr = 2 * H * dk + 2 * d
    rank = gla_w_in.shape[2] - n_qkvr
    seq_rows = -(-(N_META + seq) // ROW_ALIGN) * ROW_ALIGN
    pad_rows = seq_rows - N_META - seq
    n_rows = batch * seq_rows

    meta = jnp.broadcast_to(meta_tokens.astype(x.dtype)[None], (batch, N_META, d))
    h = jnp.concatenate([jnp.zeros((batch, pad_rows, d), x.dtype), meta, x], axis=1)
    h = h.reshape(n_rows, d)
    hb = h

    def row(v):
        return v.astype(F32)[None, :]

    for i in range(DEPTH):
        jm = i // 2
        if i % 2 == 0:
            w_in = gla_w_in[jm]
            w_qkvr = w_in[:, :n_qkvr].astype(BF16)
            w_gate = jnp.pad(w_in[:, n_qkvr:], ((0, 0), (0, GATE_PAD - rank))).astype(BF16)
            wg_up = jnp.pad(gla_w_gate_up[jm], ((0, GATE_PAD - rank), (0, 0))).astype(BF16)
            qkvr = _matmul(hb, w_qkvr, BF16, 1664, 768, "gla_in_proj")
            glow = _matmul(hb, w_gate, F32, 1664, LANES, "gla_gate_proj")
            mix = _gla(qkvr, glow, wg_up, row(gla_b_gate[jm]), row(gla_norm_g[jm]),
                       batch=batch, seq_rows=seq_rows, d_model=d)
            w_out = gla_w_out[jm].astype(BF16)
        else:
            mix = _conv_mix(hb, conv_w_in[jm].astype(BF16), conv_w[jm].astype(F32),
                            seq_rows=seq_rows, pad_rows=pad_rows)
            w_out = conv_w_out[jm].astype(BF16)
        h, logits_t = _proj_ln_router(mix, w_out, h, row(ln1_g[i]), row(ln1_b[i]),
                                      router_w[i].T.astype(BF16))
        drop = pad_rows + N_META if i == DEPTH - 1 else 0
        h, hb = _moe_ln(h, logits_t, router_bias[i], exp_w1, exp_w3, exp_w2, i,
                        shared_w1[i].astype(BF16), shared_w3[i].astype(BF16),
                        shared_w2[i].astype(BF16), row(ln2_g[i]), row(ln2_b[i]), drop, seq_rows)
    if h.shape[0] == batch * seq:
        return h.reshape(batch, seq, d)
    return h.reshape(batch, seq_rows, d)[:, pad_rows + N_META:]
```

```python
import functools

import jax
import jax.numpy as jnp
from jax import lax
from jax.experimental import pallas as pl
from jax.experimental.pallas import tpu as pltpu

N_META = 16
GLA_HEADS = 4
GLA_GATE_TAU = 16.0
CONV_WIDTH = 3
N_EXPERTS = 64
TOP_K = 8
N_GROUPS = 8
TOPK_GROUPS = 4
ROUTED_SCALE = 2.5
LN_EPS = 1e-5
RMS_EPS = 1e-6
DEPTH = 2
DN_ALPHA = (2 * DEPTH) ** 0.25
LOG2_E = 1.4426950408889634

LANES = 128
SUBLANES = 8
ROW_ALIGN = 128
GLA_CHUNK = 64
GLA_SUB = 16
GLA_HEADS_PER_STEP = 4
GATE_PAD = LANES
MOE_BLOCK = 512
VMEM_LIMIT = 56 * 1024 * 1024

F32 = jnp.float32
BF16 = jnp.bfloat16


def _pick_tile(n, target, mult):
    best = None
    for t in range(mult, min(n, target) + 1, mult):
        if n % t == 0:
            best = t
    assert best is not None, (n, target, mult)
    return best


def _cparams(sem):
    return pltpu.CompilerParams(dimension_semantics=sem, vmem_limit_bytes=VMEM_LIMIT)


def _mm_kernel(x_ref, w_ref, o_ref):
    x = x_ref[...].astype(BF16)
    o_ref[...] = jnp.dot(x, w_ref[...], preferred_element_type=F32).astype(o_ref.dtype)


def _matmul(x, w, out_dtype, tm_target, tn_target, name):
    m, k = x.shape
    n = w.shape[1]
    tm = _pick_tile(m, tm_target, 16)
    tn = _pick_tile(n, tn_target, LANES)
    return pl.pallas_call(
        _mm_kernel,
        grid=(m // tm, n // tn),
        in_specs=[pl.BlockSpec((tm, k), lambda i, j: (i, 0)),
                  pl.BlockSpec((k, tn), lambda i, j: (0, j))],
        out_specs=pl.BlockSpec((tm, tn), lambda i, j: (i, j)),
        out_shape=jax.ShapeDtypeStruct((m, n), out_dtype),
        compiler_params=_cparams(("parallel", "arbitrary")),
        name=name,
    )(x, w)


def _gla_kernel(q_ref, k_ref, v_ref, r_ref, gl_ref, wg_ref, bg_ref, ng_ref, o_ref, st_ref,
                *, n_chunks, dk, dv, heads):
    C, S = GLA_CHUNK, GLA_SUB

    @pl.when(pl.program_id(2) == 0)
    def _():
        st_ref[...] = jnp.zeros_like(st_ref)

    ri = lax.broadcasted_iota(jnp.int32, (C, C), 0)
    ci = lax.broadcasted_iota(jnp.int32, (C, C), 1)
    tri = (ri >= ci).astype(BF16)
    rs = lax.broadcasted_iota(jnp.int32, (S, S), 0)
    cs = lax.broadcasted_iota(jnp.int32, (S, S), 1)
    nt = (((1,), (1,)), ((), ()))
    tn = (((0,), (0,)), ((), ()))
    half = S // 2

    def chunk(c, carry):
        r0 = pl.multiple_of(c * C, C)
        rows_c = pl.ds(r0, C)
        hs = range(heads)
        kcol = [pl.ds(hd * dk, dk) for hd in hs]
        vcol = [pl.ds(hd * dv, dv) for hd in hs]
        gl = gl_ref[rows_c, :]
        q = [q_ref[rows_c, kcol[hd]].astype(F32) * (dk ** -0.5) for hd in hs]
        kk = [k_ref[rows_c, kcol[hd]].astype(F32) for hd in hs]
        vv = [v_ref[rows_c, vcol[hd]] for hd in hs]
        st = [st_ref[hd] for hd in hs]
        z = [jnp.dot(gl, wg_ref[:, kcol[hd]], preferred_element_type=F32) + bg_ref[:, kcol[hd]]
             for hd in hs]
        b = []
        for hd in hs:
            la = ((jnp.minimum(z[hd], 0.0) - jnp.log(1.0 + jnp.exp(-jnp.abs(z[hd]))))
                  * (1.0 / GLA_GATE_TAU))
            h1 = la.astype(BF16)
            e1 = la - h1.astype(F32)
            h2 = e1.astype(BF16)
            h3 = (e1 - h2.astype(F32)).astype(BF16)
            b3 = jnp.dot(tri, jnp.concatenate([h1, h2, h3], axis=1), preferred_element_type=F32)
            b.append((b3[:, :dk] + b3[:, dk:2 * dk] + b3[:, 2 * dk:]) * LOG2_E)

        o_inter, st_new, off = [], [], []
        for hd in hs:
            qe = (q[hd] * jnp.exp2(b[hd])).astype(BF16)
            o_inter.append(lax.dot_general(qe, st[hd].astype(BF16), nt, preferred_element_type=F32))
            bl = b[hd][C - 1:C]
            khat = (kk[hd] * jnp.exp2(bl - b[hd])).astype(BF16)
            upd = lax.dot_general(vv[hd], khat, tn, preferred_element_type=F32)
            st_new.append(st[hd] * jnp.exp2(bl) + upd)
            offs = []
            for i in range(1, C // S):
                lo = i * S
                bref = b[hd][lo - 1:lo]
                qt = (q[hd][lo:lo + S] * jnp.exp2(b[hd][lo:lo + S] - bref)).astype(BF16)
                kt = (kk[hd][:lo] * jnp.exp2(bref - b[hd][:lo])).astype(BF16)
                offs.append(lax.dot_general(qt, kt, nt, preferred_element_type=F32))
            off.append(offs)

        dmats = []
        for hd in hs:
            blocks = []
            for i in range(C // S):
                lo = i * S
                qs = q[hd][lo:lo + S]
                bs = b[hd][lo:lo + S]
                dmat = jnp.zeros((S, S), F32)
                for j in range(S):
                    kj = kk[hd][lo + j:lo + j + 1]
                    bj = b[hd][lo + j:lo + j + 1]
                    if j < half:
                        p = qs * kj * jnp.exp2(bs - bj)
                        a = jnp.sum(p, axis=-1, keepdims=True)
                    else:
                        p = qs[half:] * kj * jnp.exp2(bs[half:] - bj)
                        a = jnp.concatenate([jnp.zeros((half, 1), F32),
                                             jnp.sum(p, axis=-1, keepdims=True)], axis=0)
                    dmat = jnp.where(cs == j, a, dmat)
                blocks.append(jnp.where(rs >= cs, dmat, 0.0).astype(BF16))
            dmats.append(blocks)

        for hd in hs:
            rows = []
            for i in range(C // S):
                lo = i * S
                o_i = jnp.dot(dmats[hd][i], vv[hd][lo:lo + S], preferred_element_type=F32)
                if i > 0:
                    o_i = o_i + jnp.dot(off[hd][i - 1].astype(BF16), vv[hd][:lo],
                                        preferred_element_type=F32)
                rows.append(o_i)
            o = o_inter[hd] + jnp.concatenate(rows, axis=0)
            ms = jnp.mean(o * o, axis=-1, keepdims=True)
            r = r_ref[rows_c, vcol[hd]].astype(F32)
            y = o * lax.rsqrt(ms + RMS_EPS) * ng_ref[:, vcol[hd]] * (r * jax.nn.sigmoid(r))
            st_ref[hd] = st_new[hd]
            o_ref[rows_c, vcol[hd]] = y.astype(o_ref.dtype)
        return carry

    lax.fori_loop(0, n_chunks, chunk, 0)


def _gla(qkvrg, wg, bg, ng, *, batch, seq_rows, d_model):
    H, hps = GLA_HEADS, GLA_HEADS_PER_STEP
    dk = d_model // 2 // H
    dv = d_model // H
    n_rows = batch * seq_rows
    rblk = _pick_tile(seq_rows, 640, GLA_CHUNK)
    steps = seq_rows // rblk
    groups = H // hps
    off_k, off_v, off_r = groups, (2 * H * dk) // (hps * dv), (2 * H * dk) // (hps * dv) + groups

    off_g = (2 * H * dk + 2 * H * dv) // GATE_PAD

    def rowmap(off):
        return lambda b, h, i: (b * steps + i, off + h)

    return pl.pallas_call(
        functools.partial(_gla_kernel, n_chunks=rblk // GLA_CHUNK, dk=dk, dv=dv, heads=hps),
        grid=(batch, groups, steps),
        in_specs=[pl.BlockSpec((rblk, hps * dk), rowmap(0)),
                  pl.BlockSpec((rblk, hps * dk), rowmap(off_k)),
                  pl.BlockSpec((rblk, hps * dv), rowmap(off_v)),
                  pl.BlockSpec((rblk, hps * dv), rowmap(off_r)),
                  pl.BlockSpec((rblk, GATE_PAD), lambda b, h, i: (b * steps + i, off_g)),
                  pl.BlockSpec((GATE_PAD, hps * dk), lambda b, h, i: (0, h)),
                  pl.BlockSpec((1, hps * dk), lambda b, h, i: (0, h)),
                  pl.BlockSpec((1, hps * dv), lambda b, h, i: (0, h))],
        out_specs=pl.BlockSpec((rblk, hps * dv), lambda b, h, i: (b * steps + i, h)),
        out_shape=jax.ShapeDtypeStruct((n_rows, H * dv), BF16),
        scratch_shapes=[pltpu.VMEM((hps, dv, dk), F32)],
        compiler_params=_cparams(("parallel", "parallel", "arbitrary")),
        name="gla_chunks",
    )(qkvrg, qkvrg, qkvrg, qkvrg, qkvrg, wg, bg, ng)


def _conv_kernel(x_ref, wb_ref, wc_ref, wh_ref, cw_ref, o_ref, ubuf_ref, carry_ref,
                 *, seq_rows, pad_rows, tm):
    i = pl.program_id(0)
    j = pl.program_id(1)
    x = x_ref[...]
    bg = jnp.dot(x, wb_ref[...], preferred_element_type=F32)
    cg = jnp.dot(x, wc_ref[...], preferred_element_type=F32)
    hh = jnp.dot(x, wh_ref[...], preferred_element_type=F32)
    row = i * tm + lax.broadcasted_iota(jnp.int32, (tm, 1), 0)
    u = jnp.where(row % seq_rows >= pad_rows, cg * hh, 0.0)

    @pl.when(i == 0)
    def _():
        carry_ref[j] = jnp.zeros(carry_ref.shape[1:], F32)

    ubuf_ref[pl.ds(0, SUBLANES), :] = carry_ref[j]
    ubuf_ref[pl.ds(SUBLANES, tm), :] = u
    carry_ref[j] = u[tm - SUBLANES:]
    cw = cw_ref[...]
    conv = u * cw[CONV_WIDTH - 1:CONV_WIDTH]
    for s in range(1, CONV_WIDTH):
        conv = conv + ubuf_ref[pl.ds(SUBLANES - s, tm), :] * cw[CONV_WIDTH - 1 - s:CONV_WIDTH - s]
    o_ref[...] = (bg * conv).astype(o_ref.dtype)


def _conv_mix(xb, w_in, conv_w, *, seq_rows, pad_rows):
    n_rows, d = xb.shape
    tm = _pick_tile(n_rows, 1024, SUBLANES)
    tn = _pick_tile(d, 512, LANES)
    nj = d // tn
    return pl.pallas_call(
        functools.partial(_conv_kernel, seq_rows=seq_rows, pad_rows=pad_rows, tm=tm),
        grid=(n_rows // tm, nj),
        in_specs=[pl.BlockSpec((tm, d), lambda i, j: (i, 0)),
                  pl.BlockSpec((d, tn), lambda i, j: (0, j)),
                  pl.BlockSpec((d, tn), lambda i, j: (0, nj + j)),
                  pl.BlockSpec((d, tn), lambda i, j: (0, 2 * nj + j)),
                  pl.BlockSpec((CONV_WIDTH, tn), lambda i, j: (0, j))],
        out_specs=pl.BlockSpec((tm, tn), lambda i, j: (i, j)),
        out_shape=jax.ShapeDtypeStruct((n_rows, d), BF16),
        scratch_shapes=[pltpu.VMEM((tm + SUBLANES, tn), F32),
                        pltpu.VMEM((nj, SUBLANES, tn), F32)],
        compiler_params=_cparams(("arbitrary", "arbitrary")),
        name="conv_mix",
    )(xb, w_in, w_in, w_in, conv_w)


def _layer_norm(y, g, b):
    mu = jnp.mean(y, axis=-1, keepdims=True)
    yc = y - mu
    var = jnp.mean(yc * yc, axis=-1, keepdims=True)
    return yc * lax.rsqrt(var + LN_EPS) * g + b


def _proj_ln_kernel(a_ref, w_ref, h_ref, g_ref, b_ref, rw_ref, o_ref, lg_ref):
    y = jnp.dot(a_ref[...], w_ref[...], preferred_element_type=F32) + DN_ALPHA * h_ref[...]
    h = _layer_norm(y, g_ref[...], b_ref[...])
    o_ref[...] = h
    lg_ref[...] = lax.dot_general(rw_ref[...], h.astype(BF16), (((1,), (1,)), ((), ())),
                                  preferred_element_type=F32)


def _proj_ln_router(a, w, h, g, b, rw_t):
    n_rows, kin = a.shape
    d = w.shape[1]
    e = rw_t.shape[0]
    tm = _pick_tile(n_rows, 512, LANES)
    return pl.pallas_call(
        _proj_ln_kernel,
        grid=(n_rows // tm,),
        in_specs=[pl.BlockSpec((tm, kin), lambda i: (i, 0)),
                  pl.BlockSpec((kin, d), lambda i: (0, 0)),
                  pl.BlockSpec((tm, d), lambda i: (i, 0)),
                  pl.BlockSpec((1, d), lambda i: (0, 0)),
                  pl.BlockSpec((1, d), lambda i: (0, 0)),
                  pl.BlockSpec((e, d), lambda i: (0, 0))],
        out_specs=[pl.BlockSpec((tm, d), lambda i: (i, 0)),
                   pl.BlockSpec((e, tm), lambda i: (0, i))],
        out_shape=[jax.ShapeDtypeStruct((n_rows, d), F32),
                   jax.ShapeDtypeStruct((e, n_rows), F32)],
        compiler_params=_cparams(("parallel",)),
        name="proj_ln_router",
    )(a, w, h, g, b, rw_t)


def _beats(cand, cand_idx, ref, ref_idx):
    return (cand > ref) | ((cand == ref) & (cand_idx < ref_idx))


def _route_kernel(lg_ref, bias_ref, w_ref, pos_ref, cnt_ref, run_ref, *, t):
    E, G = N_EXPERTS, N_GROUPS
    gs = E // G

    @pl.when(pl.program_id(0) == 0)
    def _():
        run_ref[...] = jnp.zeros_like(run_ref)

    s = jax.nn.sigmoid(lg_ref[...])
    c = s + bias_ref[...][:, :1]
    sub = lax.broadcasted_iota(jnp.int32, (gs, t), 0)

    grp_rows = []
    for g in range(G):
        cg = c[g * gs:(g + 1) * gs]
        rank = jnp.zeros((gs, t), jnp.int32)
        for m in range(gs):
            rank = rank + _beats(cg[m:m + 1], m, cg, sub).astype(jnp.int32)
        grp_rows.append(jnp.sum(jnp.where(rank < 2, cg, 0.0), axis=0, keepdims=True))
    gidx = lax.broadcasted_iota(jnp.int32, (G, t), 0)
    gscore = jnp.zeros((G, t), F32)
    for g in range(G):
        gscore = jnp.where(gidx == g, grp_rows[g], gscore)
    grank = jnp.zeros((G, t), jnp.int32)
    for m in range(G):
        grank = grank + _beats(gscore[m:m + 1], m, gscore, gidx).astype(jnp.int32)
    gsel = grank < TOPK_GROUPS

    masked = jnp.concatenate(
        [jnp.where(gsel[g:g + 1], c[g * gs:(g + 1) * gs], -jnp.inf) for g in range(G)], axis=0)
    eidx = lax.broadcasted_iota(jnp.int32, (E, t), 0).astype(F32)
    sel = jnp.zeros((E, t), jnp.bool_)
    for _ in range(TOP_K):
        best = jnp.max(masked, axis=0, keepdims=True)
        pick = jnp.min(jnp.where((masked == best) & jnp.logical_not(sel), eidx, float(E)),
                       axis=0, keepdims=True)
        hit = eidx == pick
        sel = sel | hit
        masked = jnp.where(hit, -jnp.inf, masked)
    gate = jnp.where(sel, s, 0.0)
    w_ref[...] = gate / jnp.sum(gate, axis=0, keepdims=True) * ROUTED_SCALE

    li = lax.broadcasted_iota(jnp.int32, (t, t), 0)
    lj = lax.broadcasted_iota(jnp.int32, (t, t), 1)
    upper = (li <= lj).astype(BF16)
    self_ = sel.astype(F32)
    incl = jnp.dot(sel.astype(BF16), upper, preferred_element_type=F32)
    run = run_ref[...][:, :1]
    pos_ref[...] = jnp.where(sel, run + incl - self_, -1.0)
    run_new = run + jnp.sum(self_, axis=1, keepdims=True)
    run_ref[...] = jnp.broadcast_to(run_new, run_ref.shape)
    cnt_ref[...] = jnp.broadcast_to(run_new, cnt_ref.shape)


def _route(logits_t, bias_col):
    e, n_rows = logits_t.shape
    t = _pick_tile(n_rows, 512, LANES)
    return pl.pallas_call(
        functools.partial(_route_kernel, t=t),
        grid=(n_rows // t,),
        in_specs=[pl.BlockSpec((e, t), lambda i: (0, i)),
                  pl.BlockSpec((e, LANES), lambda i: (0, 0))],
        out_specs=[pl.BlockSpec((e, t), lambda i: (0, i)),
                   pl.BlockSpec((e, t), lambda i: (0, i)),
                   pl.BlockSpec((e, LANES), lambda i: (0, 0))],
        out_shape=[jax.ShapeDtypeStruct((e, n_rows), F32),
                   jax.ShapeDtypeStruct((e, n_rows), F32),
                   jax.ShapeDtypeStruct((e, LANES), F32)],
        scratch_shapes=[pltpu.VMEM((e, LANES), F32)],
        compiler_params=_cparams(("arbitrary",)),
        name="route",
    )(logits_t, bias_col)


def _compact_kernel(w_ref, pos_ref, pst_ref, slot_ref, w8_ref, *, t):
    E = N_EXPERTS
    pos = pos_ref[...]
    sel = pos >= 0.0
    ri = lax.broadcasted_iota(jnp.int32, (E, E), 0)
    ci = lax.broadcasted_iota(jnp.int32, (E, E), 1)
    below = (ci < ri).astype(BF16)
    order = jnp.dot(below, sel.astype(BF16), preferred_element_type=F32)
    slot = pst_ref[...][:, :1] + pos
    wd = w_ref[...]
    kidx = lax.broadcasted_iota(jnp.int32, (TOP_K, t), 0)
    slots = jnp.zeros((TOP_K, t), F32)
    w8 = jnp.zeros((TOP_K, t), F32)
    for k in range(TOP_K):
        m = sel & (order == float(k))
        slots = jnp.where(kidx == k, jnp.sum(jnp.where(m, slot, 0.0), axis=0, keepdims=True), slots)
        w8 = jnp.where(kidx == k, jnp.sum(jnp.where(m, wd, 0.0), axis=0, keepdims=True), w8)
    slot_ref[...] = (slots * float(SUBLANES)).astype(jnp.int32)
    w8_ref[...] = w8


def _compact(w_dense, pos_dense, pstart_col):
    e, n_rows = w_dense.shape
    t = _pick_tile(n_rows, 512, LANES)
    return pl.pallas_call(
        functools.partial(_compact_kernel, t=t),
        grid=(n_rows // t,),
        in_specs=[pl.BlockSpec((e, t), lambda i: (0, i)),
                  pl.BlockSpec((e, t), lambda i: (0, i)),
                  pl.BlockSpec((e, LANES), lambda i: (0, 0))],
        out_specs=[pl.BlockSpec((TOP_K, t), lambda i: (0, i)),
                   pl.BlockSpec((TOP_K, t), lambda i: (0, i))],
        out_shape=[jax.ShapeDtypeStruct((TOP_K, n_rows), jnp.int32),
                   jax.ShapeDtypeStruct((TOP_K, n_rows), F32)],
        compiler_params=_cparams(("parallel",)),
        name="compact",
    )(w_dense, pos_dense, pstart_col)


def _pack_bf16_pairs(x):
    half = x.shape[1] // 2
    lo = lax.bitcast_convert_type(x[:, :half].astype(BF16).astype(F32), jnp.uint32)
    hi = lax.bitcast_convert_type(x[:, half:].astype(BF16).astype(F32), jnp.uint32)
    return (lo >> 16) | hi


def _unpack_bf16_pairs(w):
    lo = lax.bitcast_convert_type(w << 16, F32)
    hi = lax.bitcast_convert_type(w & jnp.uint32(0xFFFF0000), F32)
    return lo, hi


def _store_rows_as_tiles(ref_at, x, rows):
    for c in range(SUBLANES):
        ref_at[pl.ds(c, rows, stride=SUBLANES), :] = x[:, c * LANES:(c + 1) * LANES]


def _load_tile_chunk(ref_at, c, rows):
    return ref_at[pl.ds(c, rows, stride=SUBLANES), :]


def _dispatch_kernel(pad_lo_ref, pad_hi_ref, slot_ref, h_ref, xs_ref, hp_ref, zero_ref, sem, zsem,
                     *, t):
    i = pl.program_id(0)
    cur = i % 2
    _store_rows_as_tiles(hp_ref.at[cur], _pack_bf16_pairs(h_ref[...]), t)

    def issue(tok, carry):
        src = hp_ref.at[cur, pl.ds(pl.multiple_of(tok * SUBLANES, SUBLANES), SUBLANES), :]
        for k in range(TOP_K):
            dst = xs_ref.at[pl.ds(pl.multiple_of(slot_ref[k * t + tok], SUBLANES), SUBLANES), :]
            pltpu.make_async_copy(src, dst, sem.at[cur]).start(priority=k % 2)
        return carry

    lax.fori_loop(0, t, issue, 0, unroll=2)

    @pl.when(i == 0)
    def _():
        zero_ref[...] = jnp.zeros_like(zero_ref)

        def zero_copy(p):
            return pltpu.make_async_copy(
                zero_ref, xs_ref.at[pl.ds(pl.multiple_of(p * SUBLANES, SUBLANES), SUBLANES), :], zsem)

        def per_expert(step):
            def body(e, carry):
                lax.fori_loop(pad_lo_ref[e], pad_hi_ref[e], step, 0)
                return carry
            return body

        def zissue(p, c):
            zero_copy(p).start()
            return c

        def zwait(p, c):
            zero_copy(p).wait()
            return c

        lax.fori_loop(0, N_EXPERTS, per_expert(zissue), 0)
        lax.fori_loop(0, N_EXPERTS, per_expert(zwait), 0)

    def wait_tile(buf):
        for k in range(TOP_K):
            pltpu.make_async_copy(hp_ref.at[buf], xs_ref.at[pl.ds(0, t * SUBLANES), :],
                                  sem.at[buf]).wait()

    @pl.when(i > 0)
    def _():
        wait_tile(1 - cur)

    @pl.when(i == pl.num_programs(0) - 1)
    def _():
        wait_tile(cur)


def _dispatch(h, slots_tiled, pad_lo, pad_hi, n_slots, t):
    n_rows, d = h.shape
    n_tiles = n_rows // t
    return pl.pallas_call(
        functools.partial(_dispatch_kernel, t=t),
        grid_spec=pltpu.PrefetchScalarGridSpec(
            num_scalar_prefetch=2,
            grid=(n_tiles,),
            in_specs=[pl.BlockSpec((None, None, TOP_K * t), lambda i, lo, hi: (i, 0, 0),
                                   memory_space=pltpu.SMEM),
                      pl.BlockSpec((t, d), lambda i, lo, hi: (i, 0))],
            out_specs=pl.BlockSpec(memory_space=pl.ANY),
            scratch_shapes=[pltpu.VMEM((2, t * SUBLANES, LANES), jnp.uint32),
                            pltpu.VMEM((SUBLANES, LANES), jnp.uint32),
                            pltpu.SemaphoreType.DMA((2,)),
                            pltpu.SemaphoreType.DMA(())]),
        out_shape=jax.ShapeDtypeStruct((n_slots * SUBLANES, LANES), jnp.uint32),
        compiler_params=_cparams(("arbitrary",)),
        name="dispatch",
    )(pad_lo, pad_hi, slots_tiled, h)


def _swiglu(x, w1, w3, w2):
    a = jnp.dot(x, w1, preferred_element_type=F32)
    g = jnp.dot(x, w3, preferred_element_type=F32)
    hmid = (a * jax.nn.sigmoid(a) * g).astype(BF16)
    return jnp.dot(hmid, w2, preferred_element_type=F32)


def _expert_kernel(be_ref, nu_ref, x_ref, w1_ref, w3_ref, w2_ref, y_ref, w1b_ref, w3b_ref, w2b_ref):
    b = pl.program_id(0)

    @pl.when(b < nu_ref[0])
    def _():
        @pl.when((b == 0) | (be_ref[b] != be_ref[jnp.maximum(b - 1, 0)]))
        def _():
            w1b_ref[...] = w1_ref[...].astype(BF16)
            w3b_ref[...] = w3_ref[...].astype(BF16)
            w2b_ref[...] = w2_ref[...].astype(BF16)

        halves = [_unpack_bf16_pairs(_load_tile_chunk(x_ref, c, MOE_BLOCK)) for c in range(SUBLANES)]
        x = jnp.concatenate([lo.astype(BF16) for lo, _ in halves]
                            + [hi.astype(BF16) for _, hi in halves], axis=1)
        y = _pack_bf16_pairs(_swiglu(x, w1b_ref[...], w3b_ref[...], w2b_ref[...]))
        _store_rows_as_tiles(y_ref, y, MOE_BLOCK)


def _experts(xs, w1, w3, w2, layer, block_e, n_used):
    d, f = w1.shape[2], w1.shape[3]
    blk_rows = MOE_BLOCK * SUBLANES
    nb = xs.shape[0] // blk_rows

    def blk(b, be, nu):
        return jnp.minimum(b, nu[0] - 1)

    def wmap(b, be, nu):
        return (layer, be[blk(b, be, nu)], 0, 0)

    return pl.pallas_call(
        _expert_kernel,
        grid_spec=pltpu.PrefetchScalarGridSpec(
            num_scalar_prefetch=2,
            grid=(nb,),
            in_specs=[pl.BlockSpec((blk_rows, LANES), lambda b, be, nu: (blk(b, be, nu), 0)),
                      pl.BlockSpec((None, None, d, f), wmap),
                      pl.BlockSpec((None, None, d, f), wmap),
                      pl.BlockSpec((None, None, f, d), wmap)],
            out_specs=pl.BlockSpec((blk_rows, LANES), lambda b, be, nu: (blk(b, be, nu), 0)),
            scratch_shapes=[pltpu.VMEM((d, f), BF16), pltpu.VMEM((d, f), BF16),
                            pltpu.VMEM((f, d), BF16)]),
        out_shape=jax.ShapeDtypeStruct(xs.shape, jnp.uint32),
        compiler_params=_cparams(("arbitrary",)),
        name="experts",
    )(block_e, n_used, xs, w1, w3, w2)


def _combine_kernel(slot_ref, nslot_ref, h_ref, w8_ref, ws1_ref, ws3_ref, ws2_ref, g_ref, b_ref,
                    ys_ref, o_ref, ob_ref, ybuf0_ref, ybuf1_ref, sem, *, t):
    i = pl.program_id(0)
    last = pl.num_programs(0) - 1
    ybuf = (ybuf0_ref, ybuf1_ref)

    def row_copy(s_ref, buf, k, tok):
        src = ys_ref.at[pl.ds(pl.multiple_of(s_ref[k * t + tok], SUBLANES), SUBLANES), :]
        return pltpu.make_async_copy(
            src, ybuf[buf].at[k, pl.ds(pl.multiple_of(tok * SUBLANES, SUBLANES), SUBLANES), :],
            sem.at[buf])

    def wait_tile(buf):
        for k in range(TOP_K):
            pltpu.make_async_copy(ys_ref.at[pl.ds(0, t * SUBLANES), :], ybuf[buf].at[k],
                                  sem.at[buf]).wait()

    @pl.when(i == 0)
    def _():
        def issue(tok, carry):
            for k in range(TOP_K):
                row_copy(slot_ref, 0, k, tok).start(priority=k % 2)
            return carry

        lax.fori_loop(0, t, issue, 0, unroll=2)

    group = t // TOP_K

    def reduce_tile(cur):
        nxt = 1 - cur

        def issue_group(g):
            for tok in range(g * group, (g + 1) * group):
                for k in range(TOP_K):
                    row_copy(nslot_ref, nxt, k, tok).start(priority=k % 2)

        wait_tile(cur)
        h = h_ref[...]
        issue_group(0)
        shared = _swiglu(h.astype(BF16), ws1_ref[...], ws3_ref[...], ws2_ref[...])
        w8 = w8_ref[...]
        wk = [w8[:, k:k + 1] for k in range(TOP_K)]
        acc_lo, acc_hi = [], []
        for c in range(SUBLANES):
            if c + 1 < TOP_K:
                issue_group(c + 1)
            lo_c, hi_c = None, None
            for k in range(TOP_K):
                lo, hi = _unpack_bf16_pairs(_load_tile_chunk(ybuf[cur].at[k], c, t))
                lo_c = lo * wk[k] if k == 0 else lo_c + lo * wk[k]
                hi_c = hi * wk[k] if k == 0 else hi_c + hi * wk[k]
            acc_lo.append(lo_c)
            acc_hi.append(hi_c)
        acc = jnp.concatenate(acc_lo + acc_hi, axis=1)
        y = DN_ALPHA * h + (acc + shared)
        out = _layer_norm(y, g_ref[...], b_ref[...])
        o_ref[...] = out
        ob_ref[...] = out.astype(BF16)

        @pl.when(i == last)
        def _():
            wait_tile(nxt)

    for parity in range(2):
        @pl.when(i % 2 == parity)
        def _():
            reduce_tile(parity)


def _combine(h, ys, slots_tiled, w8_t, ws1, ws3, ws2, g, b, t, out_rows, out_map):
    n_rows, d = h.shape
    f = ws1.shape[1]
    n_tiles = n_rows // t
    return pl.pallas_call(
        functools.partial(_combine_kernel, t=t),
        grid=(n_tiles,),
        in_specs=[pl.BlockSpec((None, None, TOP_K * t), lambda i: (i, 0, 0),
                               memory_space=pltpu.SMEM),
                  pl.BlockSpec((None, None, TOP_K * t),
                               lambda i: (jnp.minimum(i + 1, n_tiles - 1), 0, 0),
                               memory_space=pltpu.SMEM),
                  pl.BlockSpec((t, d), lambda i: (i, 0)),
                  pl.BlockSpec((t, TOP_K), lambda i: (i, 0)),
                  pl.BlockSpec((d, f), lambda i: (0, 0)),
                  pl.BlockSpec((d, f), lambda i: (0, 0)),
                  pl.BlockSpec((f, d), lambda i: (0, 0)),
                  pl.BlockSpec((1, d), lambda i: (0, 0)),
                  pl.BlockSpec((1, d), lambda i: (0, 0)),
                  pl.BlockSpec(memory_space=pl.ANY)],
        out_specs=[pl.BlockSpec((t, d), lambda i: (out_map(i), 0)),
                   pl.BlockSpec((t, d), lambda i: (i, 0))],
        out_shape=[jax.ShapeDtypeStruct((out_rows, d), F32),
                   jax.ShapeDtypeStruct((n_rows, d), BF16)],
        scratch_shapes=[pltpu.VMEM((TOP_K, t * SUBLANES, LANES), jnp.uint32),
                        pltpu.VMEM((TOP_K, t * SUBLANES, LANES), jnp.uint32),
                        pltpu.SemaphoreType.DMA((2,))],
        compiler_params=_cparams(("arbitrary",)),
        name="combine_ln",
    )(slots_tiled, slots_tiled, h, w8_t, ws1, ws3, ws2, g, b, ys)


def _moe_ln(h, logits_t, router_bias, w1, w3, w2, layer, ws1, ws3, ws2, g, b, drop_rows, seq_rows):
    n_rows, d = h.shape
    E = N_EXPERTS
    assert d // 2 == SUBLANES * LANES, d
    bias_col = jnp.broadcast_to(router_bias.astype(F32)[:, None], (E, LANES))
    w_dense, pos_dense, cnt = _route(logits_t, bias_col)

    counts = cnt[:, 0].astype(jnp.int32)
    pcounts = (counts + MOE_BLOCK - 1) // MOE_BLOCK * MOE_BLOCK
    pends = jnp.cumsum(pcounts)
    pstarts = pends - pcounts
    n_blocks = -(-(n_rows * TOP_K) // MOE_BLOCK) + E
    n_slots = n_blocks * MOE_BLOCK
    block_lo = jnp.arange(n_blocks, dtype=jnp.int32) * MOE_BLOCK
    block_e = jnp.minimum(jnp.sum(pends[None, :] <= block_lo[:, None], axis=1), E - 1).astype(jnp.int32)
    n_used = (pends[-1:] // MOE_BLOCK).astype(jnp.int32)
    pstart_col = jnp.broadcast_to(pstarts.astype(F32)[:, None], (E, LANES))

    slots, w8 = _compact(w_dense, pos_dense, pstart_col)
    t = _pick_tile(n_rows, 128, LANES)
    n_tiles = n_rows // t
    slots_tiled = slots.reshape(TOP_K, n_tiles, t).transpose(1, 0, 2).reshape(n_tiles, 1, TOP_K * t)
    xs = _dispatch(h, slots_tiled, (pstarts + counts).astype(jnp.int32), pends.astype(jnp.int32),
                   n_slots, t)
    ys = _experts(xs, w1, w3, w2, layer, block_e, n_used)
    if drop_rows == t:
        per_seq = seq_rows // t
        out_rows = n_rows - (n_rows // seq_rows) * t

        def out_map(i):
            return (i // per_seq) * (per_seq - 1) + jnp.maximum(i % per_seq - 1, 0)
    else:
        out_rows, out_map = n_rows, (lambda i: i)
    return _combine(h, ys, slots_tiled, w8.T, ws1, ws3, ws2, g, b, t, out_rows, out_map)


def kernel(x, meta_tokens, gla_w_in, gla_w_gate_up, gla_b_gate, gla_norm_g, gla_w_out,
           conv_w_in, conv_w, conv_w_out, ln1_g, ln1_b, router_w, router_bias,
           exp_w1, exp_w3, exp_w2, shared_w1, shared_w3, shared_w2, ln2_g, ln2_b):
    batch, seq, d = x.shape
    H = GLA_HEADS
    dk = d // 2 // H
    n_qkvr = 2 * H * dk + 2 * d
    rank = gla_w_in.shape[2] - n_qkvr
    seq_rows = -(-(N_META + seq) // ROW_ALIGN) * ROW_ALIGN
    pad_rows = seq_rows - N_META - seq
    n_rows = batch * seq_rows

    meta = jnp.broadcast_to(meta_tokens.astype(x.dtype)[None], (batch, N_META, d))
    h = jnp.concatenate([jnp.zeros((batch, pad_rows, d), x.dtype), meta, x], axis=1)
    h = h.reshape(n_rows, d)
    hb = h

    def row(v):
        return v.astype(F32)[None, :]

    for i in range(DEPTH):
        jm = i // 2
        if i % 2 == 0:
            w_in = gla_w_in[jm]
            w_all = jnp.pad(w_in, ((0, 0), (0, GATE_PAD - rank))).astype(BF16)
            wg_up = jnp.pad(gla_w_gate_up[jm], ((0, GATE_PAD - rank), (0, 0))).astype(BF16)
            qkvrg = _matmul(hb, w_all, BF16, 1664, 896, "gla_in_proj")
            mix = _gla(qkvrg, wg_up, row(gla_b_gate[jm]), row(gla_norm_g[jm]),
                       batch=batch, seq_rows=seq_rows, d_model=d)
            w_out = gla_w_out[jm].astype(BF16)
        else:
            mix = _conv_mix(hb, conv_w_in[jm].astype(BF16), conv_w[jm].astype(F32),
                            seq_rows=seq_rows, pad_rows=pad_rows)
            w_out = conv_w_out[jm].astype(BF16)
        h, logits_t = _proj_ln_router(mix, w_out, h, row(ln1_g[i]), row(ln1_b[i]),
                                      router_w[i].T.astype(BF16))
        drop = pad_rows + N_META if i == DEPTH - 1 else 0
        h, hb = _moe_ln(h, logits_t, router_bias[i], exp_w1, exp_w3, exp_w2, i,
                        shared_w1[i].astype(BF16), shared_w3[i].astype(BF16),
                        shared_w2[i].astype(BF16), row(ln2_g[i]), row(ln2_b[i]), drop, seq_rows)
    if h.shape[0] == batch * seq:
        return h.reshape(batch, seq, d)
    return h.reshape(batch, seq_rows, d)[:, pad_rows + N_META:]
```

```python
import functools

import jax
import jax.numpy as jnp
from jax import lax
from jax.experimental import pallas as pl
from jax.experimental.pallas import tpu as pltpu

N_META = 16
GLA_HEADS = 4
GLA_GATE_TAU = 16.0
CONV_WIDTH = 3
N_EXPERTS = 64
TOP_K = 8
N_GROUPS = 8
TOPK_GROUPS = 4
ROUTED_SCALE = 2.5
LN_EPS = 1e-5
RMS_EPS = 1e-6
DEPTH = 2
DN_ALPHA = (2 * DEPTH) ** 0.25
LOG2_E = 1.4426950408889634

LANES = 128
SUBLANES = 8
ROW_ALIGN = 128
GLA_CHUNK = 64
GLA_SUB = 16
GLA_HEADS_PER_STEP = 4
MXU_WIDTH = 256
GATE_PAD = MXU_WIDTH
MOE_BLOCK = 512
TOKEN_TILE = 128
IN_PROJ_ROWS, IN_PROJ_COLS = 832, 1280
CONV_ROWS, CONV_COLS = 1024, 512
OUT_PROJ_ROWS = 512
GLA_ROWS = 640
ROUTE_LANES = 512
VMEM_LIMIT = 56 * 1024 * 1024

F32 = jnp.float32
BF16 = jnp.bfloat16


def _pick_tile(n, target, mult):
    best = None
    for t in range(mult, min(n, target) + 1, mult):
        if n % t == 0:
            best = t
    assert best is not None, (n, target, mult)
    return best


def _cparams(sem):
    return pltpu.CompilerParams(dimension_semantics=sem, vmem_limit_bytes=VMEM_LIMIT)


def _mm_kernel(x_ref, w_ref, o_ref):
    x = x_ref[...].astype(BF16)
    o_ref[...] = jnp.dot(x, w_ref[...], preferred_element_type=F32).astype(o_ref.dtype)


def _matmul(x, w, out_dtype, tm_target, tn_target, name):
    m, k = x.shape
    n = w.shape[1]
    tm = _pick_tile(m, tm_target, 16)
    tn = _pick_tile(n, tn_target, LANES)
    return pl.pallas_call(
        _mm_kernel,
        grid=(m // tm, n // tn),
        in_specs=[pl.BlockSpec((tm, k), lambda i, j: (i, 0)),
                  pl.BlockSpec((k, tn), lambda i, j: (0, j))],
        out_specs=pl.BlockSpec((tm, tn), lambda i, j: (i, j)),
        out_shape=jax.ShapeDtypeStruct((m, n), out_dtype),
        compiler_params=_cparams(("parallel", "arbitrary")),
        name=name,
    )(x, w)


def _gla_kernel(q_ref, k_ref, v_ref, r_ref, gl_ref, wg_ref, bg_ref, ng_ref, o_ref, st_ref,
                *, n_chunks, dk, dv, heads):
    C, S = GLA_CHUNK, GLA_SUB

    @pl.when(pl.program_id(2) == 0)
    def _():
        st_ref[...] = jnp.zeros_like(st_ref)

    ri = lax.broadcasted_iota(jnp.int32, (C, C), 0)
    ci = lax.broadcasted_iota(jnp.int32, (C, C), 1)
    tri = (ri >= ci).astype(BF16)
    rs = lax.broadcasted_iota(jnp.int32, (S, S), 0)
    cs = lax.broadcasted_iota(jnp.int32, (S, S), 1)
    nt = (((1,), (1,)), ((), ()))
    tn = (((0,), (0,)), ((), ()))
    half = S // 2

    def chunk(c, carry):
        r0 = pl.multiple_of(c * C, C)
        rows_c = pl.ds(r0, C)
        hs = range(heads)
        kcol = [pl.ds(hd * dk, dk) for hd in hs]
        vcol = [pl.ds(hd * dv, dv) for hd in hs]
        gl = gl_ref[rows_c, :]
        q = [q_ref[rows_c, kcol[hd]].astype(F32) * (dk ** -0.5) for hd in hs]
        kk = [k_ref[rows_c, kcol[hd]].astype(F32) for hd in hs]
        vv = [v_ref[rows_c, vcol[hd]] for hd in hs]
        st = [st_ref[hd] for hd in hs]
        z = [jnp.dot(gl, wg_ref[:, kcol[hd]], preferred_element_type=F32) + bg_ref[:, kcol[hd]]
             for hd in hs]
        b = []
        for hd in hs:
            la = ((jnp.minimum(z[hd], 0.0) - jnp.log(1.0 + jnp.exp(-jnp.abs(z[hd]))))
                  * (1.0 / GLA_GATE_TAU))
            h1 = la.astype(BF16)
            e1 = la - h1.astype(F32)
            h2 = e1.astype(BF16)
            h3 = (e1 - h2.astype(F32)).astype(BF16)
            b3 = jnp.dot(tri, jnp.concatenate([h1, h2, h3], axis=1), preferred_element_type=F32)
            b.append((b3[:, :dk] + b3[:, dk:2 * dk] + b3[:, 2 * dk:]) * LOG2_E)

        o_inter, st_new, off = [], [], []
        for hd in hs:
            qe = (q[hd] * jnp.exp2(b[hd])).astype(BF16)
            o_inter.append(lax.dot_general(qe, st[hd].astype(BF16), nt, preferred_element_type=F32))
            bl = b[hd][C - 1:C]
            khat = (kk[hd] * jnp.exp2(bl - b[hd])).astype(BF16)
            upd = lax.dot_general(vv[hd], khat, tn, preferred_element_type=F32)
            st_new.append(st[hd] * jnp.exp2(bl) + upd)
            offs = []
            for i in range(1, C // S):
                lo = i * S
                bref = b[hd][lo - 1:lo]
                qt = (q[hd][lo:lo + S] * jnp.exp2(b[hd][lo:lo + S] - bref)).astype(BF16)
                kt = (kk[hd][:lo] * jnp.exp2(bref - b[hd][:lo])).astype(BF16)
                offs.append(lax.dot_general(qt, kt, nt, preferred_element_type=F32))
            off.append(offs)

        dmats = []
        for hd in hs:
            blocks = []
            for i in range(C // S):
                lo = i * S
                qs = q[hd][lo:lo + S]
                bs = b[hd][lo:lo + S]
                dmat = jnp.zeros((S, S), F32)
                for j in range(S):
                    kj = kk[hd][lo + j:lo + j + 1]
                    bj = b[hd][lo + j:lo + j + 1]
                    if j < half:
                        p = qs * kj * jnp.exp2(bs - bj)
                        a = jnp.sum(p, axis=-1, keepdims=True)
                    else:
                        p = qs[half:] * kj * jnp.exp2(bs[half:] - bj)
                        a = jnp.concatenate([jnp.zeros((half, 1), F32),
                                             jnp.sum(p, axis=-1, keepdims=True)], axis=0)
                    dmat = jnp.where(cs == j, a, dmat)
                blocks.append(jnp.where(rs >= cs, dmat, 0.0).astype(BF16))
            dmats.append(blocks)

        for hd in hs:
            rows = []
            for i in range(C // S):
                lo = i * S
                o_i = jnp.dot(dmats[hd][i], vv[hd][lo:lo + S], preferred_element_type=F32)
                if i > 0:
                    o_i = o_i + jnp.dot(off[hd][i - 1].astype(BF16), vv[hd][:lo],
                                        preferred_element_type=F32)
                rows.append(o_i)
            o = o_inter[hd] + jnp.concatenate(rows, axis=0)
            ms = jnp.mean(o * o, axis=-1, keepdims=True)
            r = r_ref[rows_c, vcol[hd]].astype(F32)
            y = o * lax.rsqrt(ms + RMS_EPS) * ng_ref[:, vcol[hd]] * (r * jax.nn.sigmoid(r))
            st_ref[hd] = st_new[hd]
            o_ref[rows_c, vcol[hd]] = y.astype(o_ref.dtype)
        return carry

    lax.fori_loop(0, n_chunks, chunk, 0)


def _gla(qkvrg, wg, bg, ng, *, batch, seq_rows, d_model):
    H, hps = GLA_HEADS, GLA_HEADS_PER_STEP
    dk = d_model // 2 // H
    dv = d_model // H
    n_rows = batch * seq_rows
    rblk = _pick_tile(seq_rows, GLA_ROWS, GLA_CHUNK)
    steps = seq_rows // rblk
    groups = H // hps
    off_k, off_v, off_r = groups, (2 * H * dk) // (hps * dv), (2 * H * dk) // (hps * dv) + groups

    off_g = (2 * H * dk + 2 * H * dv) // GATE_PAD

    def rowmap(off):
        return lambda b, h, i: (b * steps + i, off + h)

    return pl.pallas_call(
        functools.partial(_gla_kernel, n_chunks=rblk // GLA_CHUNK, dk=dk, dv=dv, heads=hps),
        grid=(batch, groups, steps),
        in_specs=[pl.BlockSpec((rblk, hps * dk), rowmap(0)),
                  pl.BlockSpec((rblk, hps * dk), rowmap(off_k)),
                  pl.BlockSpec((rblk, hps * dv), rowmap(off_v)),
                  pl.BlockSpec((rblk, hps * dv), rowmap(off_r)),
                  pl.BlockSpec((rblk, GATE_PAD), lambda b, h, i: (b * steps + i, off_g)),
                  pl.BlockSpec((GATE_PAD, hps * dk), lambda b, h, i: (0, h)),
                  pl.BlockSpec((1, hps * dk), lambda b, h, i: (0, h)),
                  pl.BlockSpec((1, hps * dv), lambda b, h, i: (0, h))],
        out_specs=pl.BlockSpec((rblk, hps * dv), lambda b, h, i: (b * steps + i, h)),
        out_shape=jax.ShapeDtypeStruct((n_rows, H * dv), BF16),
        scratch_shapes=[pltpu.VMEM((hps, dv, dk), F32)],
        compiler_params=_cparams(("parallel", "parallel", "arbitrary")),
        name="gla_chunks",
    )(qkvrg, qkvrg, qkvrg, qkvrg, qkvrg, wg, bg, ng)


def _conv_kernel(x_ref, wb_ref, wc_ref, wh_ref, cw_ref, o_ref, ubuf_ref, carry_ref,
                 *, seq_rows, pad_rows, tm):
    i = pl.program_id(0)
    j = pl.program_id(1)
    x = x_ref[...]
    bg = jnp.dot(x, wb_ref[...], preferred_element_type=F32)
    cg = jnp.dot(x, wc_ref[...], preferred_element_type=F32)
    hh = jnp.dot(x, wh_ref[...], preferred_element_type=F32)
    row = i * tm + lax.broadcasted_iota(jnp.int32, (tm, 1), 0)
    u = jnp.where(row % seq_rows >= pad_rows, cg * hh, 0.0)

    @pl.when(i == 0)
    def _():
        carry_ref[j] = jnp.zeros(carry_ref.shape[1:], F32)

    ubuf_ref[pl.ds(0, SUBLANES), :] = carry_ref[j]
    ubuf_ref[pl.ds(SUBLANES, tm), :] = u
    carry_ref[j] = u[tm - SUBLANES:]
    cw = cw_ref[...]
    conv = u * cw[CONV_WIDTH - 1:CONV_WIDTH]
    for s in range(1, CONV_WIDTH):
        conv = conv + ubuf_ref[pl.ds(SUBLANES - s, tm), :] * cw[CONV_WIDTH - 1 - s:CONV_WIDTH - s]
    o_ref[...] = (bg * conv).astype(o_ref.dtype)


def _conv_mix(xb, w_in, conv_w, *, seq_rows, pad_rows):
    n_rows, d = xb.shape
    tm = _pick_tile(n_rows, CONV_ROWS, SUBLANES)
    tn = _pick_tile(d, CONV_COLS, LANES)
    nj = d // tn
    return pl.pallas_call(
        functools.partial(_conv_kernel, seq_rows=seq_rows, pad_rows=pad_rows, tm=tm),
        grid=(n_rows // tm, nj),
        in_specs=[pl.BlockSpec((tm, d), lambda i, j: (i, 0)),
                  pl.BlockSpec((d, tn), lambda i, j: (0, j)),
                  pl.BlockSpec((d, tn), lambda i, j: (0, nj + j)),
                  pl.BlockSpec((d, tn), lambda i, j: (0, 2 * nj + j)),
                  pl.BlockSpec((CONV_WIDTH, tn), lambda i, j: (0, j))],
        out_specs=pl.BlockSpec((tm, tn), lambda i, j: (i, j)),
        out_shape=jax.ShapeDtypeStruct((n_rows, d), BF16),
        scratch_shapes=[pltpu.VMEM((tm + SUBLANES, tn), F32),
                        pltpu.VMEM((nj, SUBLANES, tn), F32)],
        compiler_params=_cparams(("arbitrary", "arbitrary")),
        name="conv_mix",
    )(xb, w_in, w_in, w_in, conv_w)


def _layer_norm(y, g, b):
    mu = jnp.mean(y, axis=-1, keepdims=True)
    yc = y - mu
    var = jnp.mean(yc * yc, axis=-1, keepdims=True)
    return yc * lax.rsqrt(var + LN_EPS) * g + b


def _proj_ln_kernel(a_ref, w_ref, h_ref, g_ref, b_ref, rw_ref, o_ref, lg_ref):
    y = jnp.dot(a_ref[...], w_ref[...], preferred_element_type=F32) + DN_ALPHA * h_ref[...]
    h = _layer_norm(y, g_ref[...], b_ref[...])
    o_ref[...] = h
    lg_ref[...] = lax.dot_general(rw_ref[...], h.astype(BF16), (((1,), (1,)), ((), ())),
                                  preferred_element_type=F32)


def _proj_ln_router(a, w, h, g, b, rw_t):
    n_rows, kin = a.shape
    d = w.shape[1]
    e = rw_t.shape[0]
    tm = _pick_tile(n_rows, OUT_PROJ_ROWS, LANES)
    return pl.pallas_call(
        _proj_ln_kernel,
        grid=(n_rows // tm,),
        in_specs=[pl.BlockSpec((tm, kin), lambda i: (i, 0)),
                  pl.BlockSpec((kin, d), lambda i: (0, 0)),
                  pl.BlockSpec((tm, d), lambda i: (i, 0)),
                  pl.BlockSpec((1, d), lambda i: (0, 0)),
                  pl.BlockSpec((1, d), lambda i: (0, 0)),
                  pl.BlockSpec((e, d), lambda i: (0, 0))],
        out_specs=[pl.BlockSpec((tm, d), lambda i: (i, 0)),
                   pl.BlockSpec((e, tm), lambda i: (0, i))],
        out_shape=[jax.ShapeDtypeStruct((n_rows, d), F32),
                   jax.ShapeDtypeStruct((e, n_rows), F32)],
        compiler_params=_cparams(("parallel",)),
        name="proj_ln_router",
    )(a, w, h, g, b, rw_t)


def _beats(cand, cand_idx, ref, ref_idx):
    return (cand > ref) | ((cand == ref) & (cand_idx < ref_idx))


def _route_kernel(lg_ref, bias_ref, w_ref, pos_ref, cnt_ref, run_ref, *, t):
    E, G = N_EXPERTS, N_GROUPS
    gs = E // G

    @pl.when(pl.program_id(0) == 0)
    def _():
        run_ref[...] = jnp.zeros_like(run_ref)

    s = jax.nn.sigmoid(lg_ref[...])
    c = s + bias_ref[...][:, :1]
    sub = lax.broadcasted_iota(jnp.int32, (gs, t), 0)

    grp_rows = []
    for g in range(G):
        cg = c[g * gs:(g + 1) * gs]
        rank = jnp.zeros((gs, t), jnp.int32)
        for m in range(gs):
            rank = rank + _beats(cg[m:m + 1], m, cg, sub).astype(jnp.int32)
        grp_rows.append(jnp.sum(jnp.where(rank < 2, cg, 0.0), axis=0, keepdims=True))
    gidx = lax.broadcasted_iota(jnp.int32, (G, t), 0)
    gscore = jnp.zeros((G, t), F32)
    for g in range(G):
        gscore = jnp.where(gidx == g, grp_rows[g], gscore)
    grank = jnp.zeros((G, t), jnp.int32)
    for m in range(G):
        grank = grank + _beats(gscore[m:m + 1], m, gscore, gidx).astype(jnp.int32)
    gsel = grank < TOPK_GROUPS

    masked = jnp.concatenate(
        [jnp.where(gsel[g:g + 1], c[g * gs:(g + 1) * gs], -jnp.inf) for g in range(G)], axis=0)
    eidx = lax.broadcasted_iota(jnp.int32, (E, t), 0).astype(F32)
    sel = jnp.zeros((E, t), jnp.bool_)
    for _ in range(TOP_K):
        best = jnp.max(masked, axis=0, keepdims=True)
        pick = jnp.min(jnp.where((masked == best) & jnp.logical_not(sel), eidx, float(E)),
                       axis=0, keepdims=True)
        hit = eidx == pick
        sel = sel | hit
        masked = jnp.where(hit, -jnp.inf, masked)
    gate = jnp.where(sel, s, 0.0)
    w_ref[...] = gate / jnp.sum(gate, axis=0, keepdims=True) * ROUTED_SCALE

    li = lax.broadcasted_iota(jnp.int32, (t, t), 0)
    lj = lax.broadcasted_iota(jnp.int32, (t, t), 1)
    upper = (li <= lj).astype(BF16)
    self_ = sel.astype(F32)
    incl = jnp.dot(sel.astype(BF16), upper, preferred_element_type=F32)
    run = run_ref[...][:, :1]
    pos_ref[...] = jnp.where(sel, run + incl - self_, -1.0)
    run_new = run + jnp.sum(self_, axis=1, keepdims=True)
    run_ref[...] = jnp.broadcast_to(run_new, run_ref.shape)
    cnt_ref[...] = jnp.broadcast_to(run_new, cnt_ref.shape)


def _route(logits_t, bias_col):
    e, n_rows = logits_t.shape
    t = _pick_tile(n_rows, ROUTE_LANES, LANES)
    return pl.pallas_call(
        functools.partial(_route_kernel, t=t),
        grid=(n_rows // t,),
        in_specs=[pl.BlockSpec((e, t), lambda i: (0, i)),
                  pl.BlockSpec((e, LANES), lambda i: (0, 0))],
        out_specs=[pl.BlockSpec((e, t), lambda i: (0, i)),
                   pl.BlockSpec((e, t), lambda i: (0, i)),
                   pl.BlockSpec((e, LANES), lambda i: (0, 0))],
        out_shape=[jax.ShapeDtypeStruct((e, n_rows), F32),
                   jax.ShapeDtypeStruct((e, n_rows), F32),
                   jax.ShapeDtypeStruct((e, LANES), F32)],
        scratch_shapes=[pltpu.VMEM((e, LANES), F32)],
        compiler_params=_cparams(("arbitrary",)),
        name="route",
    )(logits_t, bias_col)


def _compact_kernel(w_ref, pos_ref, pst_ref, slot_ref, w8_ref, *, t):
    E = N_EXPERTS
    pos = pos_ref[...]
    sel = pos >= 0.0
    ri = lax.broadcasted_iota(jnp.int32, (E, E), 0)
    ci = lax.broadcasted_iota(jnp.int32, (E, E), 1)
    below = (ci < ri).astype(BF16)
    order = jnp.dot(below, sel.astype(BF16), preferred_element_type=F32)
    slot = pst_ref[...][:, :1] + pos
    wd = w_ref[...]
    kidx = lax.broadcasted_iota(jnp.int32, (TOP_K, t), 0)
    slots = jnp.zeros((TOP_K, t), F32)
    w8 = jnp.zeros((TOP_K, t), F32)
    for k in range(TOP_K):
        m = sel & (order == float(k))
        slots = jnp.where(kidx == k, jnp.sum(jnp.where(m, slot, 0.0), axis=0, keepdims=True), slots)
        w8 = jnp.where(kidx == k, jnp.sum(jnp.where(m, wd, 0.0), axis=0, keepdims=True), w8)
    slot_ref[...] = (slots * float(SUBLANES)).astype(jnp.int32)
    w8_ref[...] = w8


def _compact(w_dense, pos_dense, pstart_col):
    e, n_rows = w_dense.shape
    t = _pick_tile(n_rows, ROUTE_LANES, LANES)
    return pl.pallas_call(
        functools.partial(_compact_kernel, t=t),
        grid=(n_rows // t,),
        in_specs=[pl.BlockSpec((e, t), lambda i: (0, i)),
                  pl.BlockSpec((e, t), lambda i: (0, i)),
                  pl.BlockSpec((e, LANES), lambda i: (0, 0))],
        out_specs=[pl.BlockSpec((TOP_K, t), lambda i: (0, i)),
                   pl.BlockSpec((TOP_K, t), lambda i: (0, i))],
        out_shape=[jax.ShapeDtypeStruct((TOP_K, n_rows), jnp.int32),
                   jax.ShapeDtypeStruct((TOP_K, n_rows), F32)],
        compiler_params=_cparams(("parallel",)),
        name="compact",
    )(w_dense, pos_dense, pstart_col)


def _pack_bf16_pairs(x):
    half = x.shape[1] // 2
    lo = lax.bitcast_convert_type(x[:, :half].astype(BF16).astype(F32), jnp.uint32)
    hi = lax.bitcast_convert_type(x[:, half:].astype(BF16).astype(F32), jnp.uint32)
    return (lo >> 16) | hi


def _unpack_bf16_pairs(w):
    lo = lax.bitcast_convert_type(w << 16, F32)
    hi = lax.bitcast_convert_type(w & jnp.uint32(0xFFFF0000), F32)
    return lo, hi


def _store_rows_as_tiles(ref_at, x, rows):
    for c in range(SUBLANES):
        ref_at[pl.ds(c, rows, stride=SUBLANES), :] = x[:, c * LANES:(c + 1) * LANES]


def _load_tile_chunk(ref_at, c, rows):
    return ref_at[pl.ds(c, rows, stride=SUBLANES), :]


def _dispatch_kernel(pad_lo_ref, pad_hi_ref, slot_ref, h_ref, xs_ref, hp_ref, zero_ref, sem, zsem,
                     *, t):
    i = pl.program_id(0)
    cur = i % 2
    _store_rows_as_tiles(hp_ref.at[cur], _pack_bf16_pairs(h_ref[...]), t)

    def issue(tok, carry):
        src = hp_ref.at[cur, pl.ds(pl.multiple_of(tok * SUBLANES, SUBLANES), SUBLANES), :]
        for k in range(TOP_K):
            dst = xs_ref.at[pl.ds(pl.multiple_of(slot_ref[k * t + tok], SUBLANES), SUBLANES), :]
            pltpu.make_async_copy(src, dst, sem.at[cur]).start(priority=k % 2)
        return carry

    lax.fori_loop(0, t, issue, 0, unroll=2)

    @pl.when(i == 0)
    def _():
        zero_ref[...] = jnp.zeros_like(zero_ref)

        def zero_copy(p):
            return pltpu.make_async_copy(
                zero_ref, xs_ref.at[pl.ds(pl.multiple_of(p * SUBLANES, SUBLANES), SUBLANES), :], zsem)

        def per_expert(step):
            def body(e, carry):
                lax.fori_loop(pad_lo_ref[e], pad_hi_ref[e], step, 0)
                return carry
            return body

        def zissue(p, c):
            zero_copy(p).start()
            return c

        def zwait(p, c):
            zero_copy(p).wait()
            return c

        lax.fori_loop(0, N_EXPERTS, per_expert(zissue), 0)
        lax.fori_loop(0, N_EXPERTS, per_expert(zwait), 0)

    def wait_tile(buf):
        for k in range(TOP_K):
            pltpu.make_async_copy(hp_ref.at[buf], xs_ref.at[pl.ds(0, t * SUBLANES), :],
                                  sem.at[buf]).wait()

    @pl.when(i > 0)
    def _():
        wait_tile(1 - cur)

    @pl.when(i == pl.num_programs(0) - 1)
    def _():
        wait_tile(cur)


def _dispatch(h, slots_tiled, pad_lo, pad_hi, n_slots, t):
    n_rows, d = h.shape
    n_tiles = n_rows // t
    return pl.pallas_call(
        functools.partial(_dispatch_kernel, t=t),
        grid_spec=pltpu.PrefetchScalarGridSpec(
            num_scalar_prefetch=2,
            grid=(n_tiles,),
            in_specs=[pl.BlockSpec((None, None, TOP_K * t), lambda i, lo, hi: (i, 0, 0),
                                   memory_space=pltpu.SMEM),
                      pl.BlockSpec((t, d), lambda i, lo, hi: (i, 0))],
            out_specs=pl.BlockSpec(memory_space=pl.ANY),
            scratch_shapes=[pltpu.VMEM((2, t * SUBLANES, LANES), jnp.uint32),
                            pltpu.VMEM((SUBLANES, LANES), jnp.uint32),
                            pltpu.SemaphoreType.DMA((2,)),
                            pltpu.SemaphoreType.DMA(())]),
        out_shape=jax.ShapeDtypeStruct((n_slots * SUBLANES, LANES), jnp.uint32),
        compiler_params=_cparams(("arbitrary",)),
        name="dispatch",
    )(pad_lo, pad_hi, slots_tiled, h)


def _swiglu(x, w1, w3, w2):
    a = jnp.dot(x, w1, preferred_element_type=F32)
    g = jnp.dot(x, w3, preferred_element_type=F32)
    hmid = (a * jax.nn.sigmoid(a) * g).astype(BF16)
    return jnp.dot(hmid, w2, preferred_element_type=F32)


def _expert_kernel(be_ref, nu_ref, x_ref, w1_ref, w3_ref, w2_ref, y_ref, w1b_ref, w3b_ref, w2b_ref):
    b = pl.program_id(0)

    @pl.when(b < nu_ref[0])
    def _():
        @pl.when((b == 0) | (be_ref[b] != be_ref[jnp.maximum(b - 1, 0)]))
        def _():
            w1b_ref[...] = w1_ref[...].astype(BF16)
            w3b_ref[...] = w3_ref[...].astype(BF16)
            w2b_ref[...] = w2_ref[...].astype(BF16)

        halves = [_unpack_bf16_pairs(_load_tile_chunk(x_ref, c, MOE_BLOCK)) for c in range(SUBLANES)]
        x = jnp.concatenate([lo.astype(BF16) for lo, _ in halves]
                            + [hi.astype(BF16) for _, hi in halves], axis=1)
        y = _pack_bf16_pairs(_swiglu(x, w1b_ref[...], w3b_ref[...], w2b_ref[...]))
        _store_rows_as_tiles(y_ref, y, MOE_BLOCK)


def _experts(xs, w1, w3, w2, layer, block_e, n_used):
    d, f = w1.shape[2], w1.shape[3]
    blk_rows = MOE_BLOCK * SUBLANES
    nb = xs.shape[0] // blk_rows

    def blk(b, be, nu):
        return jnp.minimum(b, nu[0] - 1)

    def wmap(b, be, nu):
        return (layer, be[blk(b, be, nu)], 0, 0)

    return pl.pallas_call(
        _expert_kernel,
        grid_spec=pltpu.PrefetchScalarGridSpec(
            num_scalar_prefetch=2,
            grid=(nb,),
            in_specs=[pl.BlockSpec((blk_rows, LANES), lambda b, be, nu: (blk(b, be, nu), 0)),
                      pl.BlockSpec((None, None, d, f), wmap),
                      pl.BlockSpec((None, None, d, f), wmap),
                      pl.BlockSpec((None, None, f, d), wmap)],
            out_specs=pl.BlockSpec((blk_rows, LANES), lambda b, be, nu: (blk(b, be, nu), 0)),
            scratch_shapes=[pltpu.VMEM((d, f), BF16), pltpu.VMEM((d, f), BF16),
                            pltpu.VMEM((f, d), BF16)]),
        out_shape=jax.ShapeDtypeStruct(xs.shape, jnp.uint32),
        compiler_params=_cparams(("arbitrary",)),
        name="experts",
    )(block_e, n_used, xs, w1, w3, w2)


def _combine_kernel(slot_ref, nslot_ref, h_ref, w8_ref, ws1_ref, ws3_ref, ws2_ref, g_ref, b_ref,
                    ys_ref, o_ref, ob_ref, ybuf0_ref, ybuf1_ref, sem, *, t):
    i = pl.program_id(0)
    last = pl.num_programs(0) - 1
    ybuf = (ybuf0_ref, ybuf1_ref)

    def row_copy(s_ref, buf, k, tok):
        src = ys_ref.at[pl.ds(pl.multiple_of(s_ref[k * t + tok], SUBLANES), SUBLANES), :]
        return pltpu.make_async_copy(
            src, ybuf[buf].at[k, pl.ds(pl.multiple_of(tok * SUBLANES, SUBLANES), SUBLANES), :],
            sem.at[buf])

    def wait_tile(buf):
        for k in range(TOP_K):
            pltpu.make_async_copy(ys_ref.at[pl.ds(0, t * SUBLANES), :], ybuf[buf].at[k],
                                  sem.at[buf]).wait()

    @pl.when(i == 0)
    def _():
        def issue(tok, carry):
            for k in range(TOP_K):
                row_copy(slot_ref, 0, k, tok).start(priority=k % 2)
            return carry

        lax.fori_loop(0, t, issue, 0, unroll=2)

    group = t // TOP_K

    def reduce_tile(cur):
        nxt = 1 - cur

        def issue_group(g):
            for tok in range(g * group, (g + 1) * group):
                for k in range(TOP_K):
                    row_copy(nslot_ref, nxt, k, tok).start(priority=k % 2)

        wait_tile(cur)
        h = h_ref[...]
        issue_group(0)
        shared = _swiglu(h.astype(BF16), ws1_ref[...], ws3_ref[...], ws2_ref[...])
        w8 = w8_ref[...]
        wk = [w8[:, k:k + 1] for k in range(TOP_K)]
        acc_lo, acc_hi = [], []
        for c in range(SUBLANES):
            if c + 1 < TOP_K:
                issue_group(c + 1)
            lo_c, hi_c = None, None
            for k in range(TOP_K):
                lo, hi = _unpack_bf16_pairs(_load_tile_chunk(ybuf[cur].at[k], c, t))
                lo_c = lo * wk[k] if k == 0 else lo_c + lo * wk[k]
                hi_c = hi * wk[k] if k == 0 else hi_c + hi * wk[k]
            acc_lo.append(lo_c)
            acc_hi.append(hi_c)
        acc = jnp.concatenate(acc_lo + acc_hi, axis=1)
        y = DN_ALPHA * h + (acc + shared)
        out = _layer_norm(y, g_ref[...], b_ref[...])
        o_ref[...] = out
        ob_ref[...] = out.astype(BF16)

        @pl.when(i == last)
        def _():
            wait_tile(nxt)

    for parity in range(2):
        @pl.when(i % 2 == parity)
        def _():
            reduce_tile(parity)


def _combine(h, ys, slots_tiled, w8_t, ws1, ws3, ws2, g, b, t, out_rows, out_map):
    n_rows, d = h.shape
    f = ws1.shape[1]
    n_tiles = n_rows // t
    return pl.pallas_call(
        functools.partial(_combine_kernel, t=t),
        grid=(n_tiles,),
        in_specs=[pl.BlockSpec((None, None, TOP_K * t), lambda i: (i, 0, 0),
                               memory_space=pltpu.SMEM),
                  pl.BlockSpec((None, None, TOP_K * t),
                               lambda i: (jnp.minimum(i + 1, n_tiles - 1), 0, 0),
                               memory_space=pltpu.SMEM),
                  pl.BlockSpec((t, d), lambda i: (i, 0)),
                  pl.BlockSpec((t, TOP_K), lambda i: (i, 0)),
                  pl.BlockSpec((d, f), lambda i: (0, 0)),
                  pl.BlockSpec((d, f), lambda i: (0, 0)),
                  pl.BlockSpec((f, d), lambda i: (0, 0)),
                  pl.BlockSpec((1, d), lambda i: (0, 0)),
                  pl.BlockSpec((1, d), lambda i: (0, 0)),
                  pl.BlockSpec(memory_space=pl.ANY)],
        out_specs=[pl.BlockSpec((t, d), lambda i: (out_map(i), 0)),
                   pl.BlockSpec((t, d), lambda i: (i, 0))],
        out_shape=[jax.ShapeDtypeStruct((out_rows, d), F32),
                   jax.ShapeDtypeStruct((n_rows, d), BF16)],
        scratch_shapes=[pltpu.VMEM((TOP_K, t * SUBLANES, LANES), jnp.uint32),
                        pltpu.VMEM((TOP_K, t * SUBLANES, LANES), jnp.uint32),
                        pltpu.SemaphoreType.DMA((2,))],
        compiler_params=_cparams(("arbitrary",)),
        name="combine_ln",
    )(slots_tiled, slots_tiled, h, w8_t, ws1, ws3, ws2, g, b, ys)


def _moe_ln(h, logits_t, router_bias, w1, w3, w2, layer, ws1, ws3, ws2, g, b, drop_rows, seq_rows):
    n_rows, d = h.shape
    E = N_EXPERTS
    assert d // 2 == SUBLANES * LANES, d
    bias_col = jnp.broadcast_to(router_bias.astype(F32)[:, None], (E, LANES))
    w_dense, pos_dense, cnt = _route(logits_t, bias_col)

    counts = cnt[:, 0].astype(jnp.int32)
    pcounts = (counts + MOE_BLOCK - 1) // MOE_BLOCK * MOE_BLOCK
    pends = jnp.cumsum(pcounts)
    pstarts = pends - pcounts
    n_blocks = -(-(n_rows * TOP_K) // MOE_BLOCK) + E
    n_slots = n_blocks * MOE_BLOCK
    block_lo = jnp.arange(n_blocks, dtype=jnp.int32) * MOE_BLOCK
    block_e = jnp.minimum(jnp.sum(pends[None, :] <= block_lo[:, None], axis=1), E - 1).astype(jnp.int32)
    n_used = (pends[-1:] // MOE_BLOCK).astype(jnp.int32)
    pstart_col = jnp.broadcast_to(pstarts.astype(F32)[:, None], (E, LANES))

    slots, w8 = _compact(w_dense, pos_dense, pstart_col)
    t = _pick_tile(n_rows, TOKEN_TILE, LANES)
    n_tiles = n_rows // t
    slots_tiled = slots.reshape(TOP_K, n_tiles, t).transpose(1, 0, 2).reshape(n_tiles, 1, TOP_K * t)
    xs = _dispatch(h, slots_tiled, (pstarts + counts).astype(jnp.int32), pends.astype(jnp.int32),
                   n_slots, t)
    ys = _experts(xs, w1, w3, w2, layer, block_e, n_used)
    if drop_rows == t:
        per_seq = seq_rows // t
        out_rows = n_rows - (n_rows // seq_rows) * t

        def out_map(i):
            return (i // per_seq) * (per_seq - 1) + jnp.maximum(i % per_seq - 1, 0)
    else:
        out_rows, out_map = n_rows, (lambda i: i)
    return _combine(h, ys, slots_tiled, w8.T, ws1, ws3, ws2, g, b, t, out_rows, out_map)


def kernel(x, meta_tokens, gla_w_in, gla_w_gate_up, gla_b_gate, gla_norm_g, gla_w_out,
           conv_w_in, conv_w, conv_w_out, ln1_g, ln1_b, router_w, router_bias,
           exp_w1, exp_w3, exp_w2, shared_w1, shared_w3, shared_w2, ln2_g, ln2_b):
    batch, seq, d = x.shape
    H = GLA_HEADS
    dk = d // 2 // H
    n_qkvr = 2 * H * dk + 2 * d
    rank = gla_w_in.shape[2] - n_qkvr
    seq_rows = -(-(N_META + seq) // ROW_ALIGN) * ROW_ALIGN
    pad_rows = seq_rows - N_META - seq
    n_rows = batch * seq_rows

    meta = jnp.broadcast_to(meta_tokens.astype(x.dtype)[None], (batch, N_META, d))
    h = jnp.concatenate([jnp.zeros((batch, pad_rows, d), x.dtype), meta, x], axis=1)
    h = h.reshape(n_rows, d)
    hb = h

    def row(v):
        return v.astype(F32)[None, :]

    for i in range(DEPTH):
        jm = i // 2
        if i % 2 == 0:
            w_in = gla_w_in[jm]
            w_all = jnp.pad(w_in, ((0, 0), (0, GATE_PAD - rank))).astype(BF16)
            wg_up = jnp.pad(gla_w_gate_up[jm], ((0, GATE_PAD - rank), (0, 0))).astype(BF16)
            qkvrg = _matmul(hb, w_all, BF16, IN_PROJ_ROWS, IN_PROJ_COLS, "gla_in_proj")
            mix = _gla(qkvrg, wg_up, row(gla_b_gate[jm]), row(gla_norm_g[jm]),
                       batch=batch, seq_rows=seq_rows, d_model=d)
            w_out = gla_w_out[jm].astype(BF16)
        else:
            mix = _conv_mix(hb, conv_w_in[jm].astype(BF16), conv_w[jm].astype(F32),
                            seq_rows=seq_rows, pad_rows=pad_rows)
            w_out = conv_w_out[jm].astype(BF16)
        h, logits_t = _proj_ln_router(mix, w_out, h, row(ln1_g[i]), row(ln1_b[i]),
                                      router_w[i].T.astype(BF16))
        drop = pad_rows + N_META if i == DEPTH - 1 else 0
        h, hb = _moe_ln(h, logits_t, router_bias[i], exp_w1, exp_w3, exp_w2, i,
                        shared_w1[i].astype(BF16), shared_w3[i].astype(BF16),
                        shared_w2[i].astype(BF16), row(ln2_g[i]), row(ln2_b[i]), drop, seq_rows)
    if h.shape[0] == batch * seq:
        return h.reshape(batch, seq, d)
    return h.reshape(batch, seq_rows, d)[:, pad_rows + N_META:]
```

```python
import functools

import jax
import jax.numpy as jnp
from jax import lax
from jax.experimental import pallas as pl
from jax.experimental.pallas import tpu as pltpu
from jax.experimental.pallas import tpu_sc as plsc

N_META = 16
GLA_HEADS = 4
GLA_GATE_TAU = 16.0
CONV_WIDTH = 3
N_EXPERTS = 64
TOP_K = 8
N_GROUPS = 8
TOPK_GROUPS = 4
ROUTED_SCALE = 2.5
LN_EPS = 1e-5
RMS_EPS = 1e-6
DEPTH = 2
DN_ALPHA = (2 * DEPTH) ** 0.25
LOG2_E = 1.4426950408889634

LANES = 128
SUBLANES = 8
ROW_ALIGN = 128
GLA_CHUNK = 64
GLA_SUB = 16
GLA_HEADS_PER_STEP = 4
MXU_WIDTH = 256
GATE_PAD = MXU_WIDTH
MOE_BLOCK = 512
TOKEN_TILE = 128
IN_PROJ_ROWS, IN_PROJ_COLS = 832, 1280
CONV_ROWS, CONV_COLS = 1024, 512
OUT_PROJ_ROWS = 512
GLA_ROWS = 640
ROUTE_LANES = 512
VMEM_LIMIT = 56 * 1024 * 1024

F32 = jnp.float32
BF16 = jnp.bfloat16


def _pick_tile(n, target, mult):
    best = None
    for t in range(mult, min(n, target) + 1, mult):
        if n % t == 0:
            best = t
    assert best is not None, (n, target, mult)
    return best


def _cparams(sem):
    return pltpu.CompilerParams(dimension_semantics=sem, vmem_limit_bytes=VMEM_LIMIT)


def _mm_kernel(x_ref, w_ref, o_ref):
    x = x_ref[...].astype(BF16)
    o_ref[...] = jnp.dot(x, w_ref[...], preferred_element_type=F32).astype(o_ref.dtype)


def _matmul(x, w, out_dtype, tm_target, tn_target, name):
    m, k = x.shape
    n = w.shape[1]
    tm = _pick_tile(m, tm_target, 16)
    tn = _pick_tile(n, tn_target, LANES)
    return pl.pallas_call(
        _mm_kernel,
        grid=(m // tm, n // tn),
        in_specs=[pl.BlockSpec((tm, k), lambda i, j: (i, 0)),
                  pl.BlockSpec((k, tn), lambda i, j: (0, j))],
        out_specs=pl.BlockSpec((tm, tn), lambda i, j: (i, j)),
        out_shape=jax.ShapeDtypeStruct((m, n), out_dtype),
        compiler_params=_cparams(("parallel", "arbitrary")),
        name=name,
    )(x, w)


def _gla_kernel(q_ref, k_ref, v_ref, r_ref, gl_ref, wg_ref, bg_ref, ng_ref, o_ref, st_ref,
                *, n_chunks, dk, dv, heads):
    C, S = GLA_CHUNK, GLA_SUB

    @pl.when(pl.program_id(2) == 0)
    def _():
        st_ref[...] = jnp.zeros_like(st_ref)

    ri = lax.broadcasted_iota(jnp.int32, (C, C), 0)
    ci = lax.broadcasted_iota(jnp.int32, (C, C), 1)
    tri = (ri >= ci).astype(BF16)
    rs = lax.broadcasted_iota(jnp.int32, (S, S), 0)
    cs = lax.broadcasted_iota(jnp.int32, (S, S), 1)
    nt = (((1,), (1,)), ((), ()))
    tn = (((0,), (0,)), ((), ()))
    half = S // 2

    def chunk(c, carry):
        r0 = pl.multiple_of(c * C, C)
        rows_c = pl.ds(r0, C)
        hs = range(heads)
        kcol = [pl.ds(hd * dk, dk) for hd in hs]
        vcol = [pl.ds(hd * dv, dv) for hd in hs]
        gl = gl_ref[rows_c, :]
        q = [q_ref[rows_c, kcol[hd]].astype(F32) * (dk ** -0.5) for hd in hs]
        kk = [k_ref[rows_c, kcol[hd]].astype(F32) for hd in hs]
        vv = [v_ref[rows_c, vcol[hd]] for hd in hs]
        st = [st_ref[hd] for hd in hs]
        z = [jnp.dot(gl, wg_ref[:, kcol[hd]], preferred_element_type=F32) + bg_ref[:, kcol[hd]]
             for hd in hs]
        b = []
        for hd in hs:
            la = ((jnp.minimum(z[hd], 0.0) - jnp.log(1.0 + jnp.exp(-jnp.abs(z[hd]))))
                  * (1.0 / GLA_GATE_TAU))
            h1 = la.astype(BF16)
            e1 = la - h1.astype(F32)
            h2 = e1.astype(BF16)
            h3 = (e1 - h2.astype(F32)).astype(BF16)
            b3 = jnp.dot(tri, jnp.concatenate([h1, h2, h3], axis=1), preferred_element_type=F32)
            b.append((b3[:, :dk] + b3[:, dk:2 * dk] + b3[:, 2 * dk:]) * LOG2_E)

        o_inter, st_new, off = [], [], []
        for hd in hs:
            qe = (q[hd] * jnp.exp2(b[hd])).astype(BF16)
            o_inter.append(lax.dot_general(qe, st[hd].astype(BF16), nt, preferred_element_type=F32))
            bl = b[hd][C - 1:C]
            khat = (kk[hd] * jnp.exp2(bl - b[hd])).astype(BF16)
            upd = lax.dot_general(vv[hd], khat, tn, preferred_element_type=F32)
            st_new.append(st[hd] * jnp.exp2(bl) + upd)
            offs = []
            for i in range(1, C // S):
                lo = i * S
                bref = b[hd][lo - 1:lo]
                qt = (q[hd][lo:lo + S] * jnp.exp2(b[hd][lo:lo + S] - bref)).astype(BF16)
                kt = (kk[hd][:lo] * jnp.exp2(bref - b[hd][:lo])).astype(BF16)
                offs.append(lax.dot_general(qt, kt, nt, preferred_element_type=F32))
            off.append(offs)

        dmats = []
        for hd in hs:
            blocks = []
            for i in range(C // S):
                lo = i * S
                qs = q[hd][lo:lo + S]
                bs = b[hd][lo:lo + S]
                dmat = jnp.zeros((S, S), F32)
                for j in range(S):
                    kj = kk[hd][lo + j:lo + j + 1]
                    bj = b[hd][lo + j:lo + j + 1]
                    if j < half:
                        p = qs * kj * jnp.exp2(bs - bj)
                        a = jnp.sum(p, axis=-1, keepdims=True)
                    else:
                        p = qs[half:] * kj * jnp.exp2(bs[half:] - bj)
                        a = jnp.concatenate([jnp.zeros((half, 1), F32),
                                             jnp.sum(p, axis=-1, keepdims=True)], axis=0)
                    dmat = jnp.where(cs == j, a, dmat)
                blocks.append(jnp.where(rs >= cs, dmat, 0.0).astype(BF16))
            dmats.append(blocks)

        for hd in hs:
            rows = []
            for i in range(C // S):
                lo = i * S
                o_i = jnp.dot(dmats[hd][i], vv[hd][lo:lo + S], preferred_element_type=F32)
                if i > 0:
                    o_i = o_i + jnp.dot(off[hd][i - 1].astype(BF16), vv[hd][:lo],
                                        preferred_element_type=F32)
                rows.append(o_i)
            o = o_inter[hd] + jnp.concatenate(rows, axis=0)
            ms = jnp.mean(o * o, axis=-1, keepdims=True)
            r = r_ref[rows_c, vcol[hd]].astype(F32)
            y = o * lax.rsqrt(ms + RMS_EPS) * ng_ref[:, vcol[hd]] * (r * jax.nn.sigmoid(r))
            st_ref[hd] = st_new[hd]
            o_ref[rows_c, vcol[hd]] = y.astype(o_ref.dtype)
        return carry

    lax.fori_loop(0, n_chunks, chunk, 0)


def _gla(qkvrg, wg, bg, ng, *, batch, seq_rows, d_model):
    H, hps = GLA_HEADS, GLA_HEADS_PER_STEP
    dk = d_model // 2 // H
    dv = d_model // H
    n_rows = batch * seq_rows
    rblk = _pick_tile(seq_rows, GLA_ROWS, GLA_CHUNK)
    steps = seq_rows // rblk
    groups = H // hps
    off_k, off_v, off_r = groups, (2 * H * dk) // (hps * dv), (2 * H * dk) // (hps * dv) + groups

    off_g = (2 * H * dk + 2 * H * dv) // GATE_PAD

    def rowmap(off):
        return lambda b, h, i: (b * steps + i, off + h)

    return pl.pallas_call(
        functools.partial(_gla_kernel, n_chunks=rblk // GLA_CHUNK, dk=dk, dv=dv, heads=hps),
        grid=(batch, groups, steps),
        in_specs=[pl.BlockSpec((rblk, hps * dk), rowmap(0)),
                  pl.BlockSpec((rblk, hps * dk), rowmap(off_k)),
                  pl.BlockSpec((rblk, hps * dv), rowmap(off_v)),
                  pl.BlockSpec((rblk, hps * dv), rowmap(off_r)),
                  pl.BlockSpec((rblk, GATE_PAD), lambda b, h, i: (b * steps + i, off_g)),
                  pl.BlockSpec((GATE_PAD, hps * dk), lambda b, h, i: (0, h)),
                  pl.BlockSpec((1, hps * dk), lambda b, h, i: (0, h)),
                  pl.BlockSpec((1, hps * dv), lambda b, h, i: (0, h))],
        out_specs=pl.BlockSpec((rblk, hps * dv), lambda b, h, i: (b * steps + i, h)),
        out_shape=jax.ShapeDtypeStruct((n_rows, H * dv), BF16),
        scratch_shapes=[pltpu.VMEM((hps, dv, dk), F32)],
        compiler_params=_cparams(("parallel", "parallel", "arbitrary")),
        name="gla_chunks",
    )(qkvrg, qkvrg, qkvrg, qkvrg, qkvrg, wg, bg, ng)


def _conv_kernel(x_ref, wb_ref, wc_ref, wh_ref, cw_ref, o_ref, ubuf_ref, carry_ref,
                 *, seq_rows, pad_rows, tm):
    i = pl.program_id(0)
    j = pl.program_id(1)
    x = x_ref[...]
    bg = jnp.dot(x, wb_ref[...], preferred_element_type=F32)
    cg = jnp.dot(x, wc_ref[...], preferred_element_type=F32)
    hh = jnp.dot(x, wh_ref[...], preferred_element_type=F32)
    row = i * tm + lax.broadcasted_iota(jnp.int32, (tm, 1), 0)
    u = jnp.where(row % seq_rows >= pad_rows, cg * hh, 0.0)

    @pl.when(i == 0)
    def _():
        carry_ref[j] = jnp.zeros(carry_ref.shape[1:], F32)

    ubuf_ref[pl.ds(0, SUBLANES), :] = carry_ref[j]
    ubuf_ref[pl.ds(SUBLANES, tm), :] = u
    carry_ref[j] = u[tm - SUBLANES:]
    cw = cw_ref[...]
    conv = u * cw[CONV_WIDTH - 1:CONV_WIDTH]
    for s in range(1, CONV_WIDTH):
        conv = conv + ubuf_ref[pl.ds(SUBLANES - s, tm), :] * cw[CONV_WIDTH - 1 - s:CONV_WIDTH - s]
    o_ref[...] = (bg * conv).astype(o_ref.dtype)


def _conv_mix(xb, w_in, conv_w, *, seq_rows, pad_rows):
    n_rows, d = xb.shape
    tm = _pick_tile(n_rows, CONV_ROWS, SUBLANES)
    tn = _pick_tile(d, CONV_COLS, LANES)
    nj = d // tn
    return pl.pallas_call(
        functools.partial(_conv_kernel, seq_rows=seq_rows, pad_rows=pad_rows, tm=tm),
        grid=(n_rows // tm, nj),
        in_specs=[pl.BlockSpec((tm, d), lambda i, j: (i, 0)),
                  pl.BlockSpec((d, tn), lambda i, j: (0, j)),
                  pl.BlockSpec((d, tn), lambda i, j: (0, nj + j)),
                  pl.BlockSpec((d, tn), lambda i, j: (0, 2 * nj + j)),
                  pl.BlockSpec((CONV_WIDTH, tn), lambda i, j: (0, j))],
        out_specs=pl.BlockSpec((tm, tn), lambda i, j: (i, j)),
        out_shape=jax.ShapeDtypeStruct((n_rows, d), BF16),
        scratch_shapes=[pltpu.VMEM((tm + SUBLANES, tn), F32),
                        pltpu.VMEM((nj, SUBLANES, tn), F32)],
        compiler_params=_cparams(("arbitrary", "arbitrary")),
        name="conv_mix",
    )(xb, w_in, w_in, w_in, conv_w)


def _layer_norm(y, g, b):
    mu = jnp.mean(y, axis=-1, keepdims=True)
    yc = y - mu
    var = jnp.mean(yc * yc, axis=-1, keepdims=True)
    return yc * lax.rsqrt(var + LN_EPS) * g + b


def _proj_ln_kernel(a_ref, w_ref, h_ref, g_ref, b_ref, rw_ref, o_ref, lg_ref, hp_ref):
    y = jnp.dot(a_ref[...], w_ref[...], preferred_element_type=F32) + DN_ALPHA * h_ref[...]
    h = _layer_norm(y, g_ref[...], b_ref[...])
    o_ref[...] = h
    _store_rows_as_tiles(hp_ref, _pack_bf16_pairs(h), h.shape[0])
    lg_ref[...] = lax.dot_general(rw_ref[...], h.astype(BF16), (((1,), (1,)), ((), ())),
                                  preferred_element_type=F32)


def _proj_ln_router(a, w, h, g, b, rw_t):
    n_rows, kin = a.shape
    d = w.shape[1]
    e = rw_t.shape[0]
    tm = _pick_tile(n_rows, OUT_PROJ_ROWS, LANES)
    return pl.pallas_call(
        _proj_ln_kernel,
        grid=(n_rows // tm,),
        in_specs=[pl.BlockSpec((tm, kin), lambda i: (i, 0)),
                  pl.BlockSpec((kin, d), lambda i: (0, 0)),
                  pl.BlockSpec((tm, d), lambda i: (i, 0)),
                  pl.BlockSpec((1, d), lambda i: (0, 0)),
                  pl.BlockSpec((1, d), lambda i: (0, 0)),
                  pl.BlockSpec((e, d), lambda i: (0, 0))],
        out_specs=[pl.BlockSpec((tm, d), lambda i: (i, 0)),
                   pl.BlockSpec((e, tm), lambda i: (0, i)),
                   pl.BlockSpec((tm * SUBLANES, LANES), lambda i: (i, 0))],
        out_shape=[jax.ShapeDtypeStruct((n_rows, d), F32),
                   jax.ShapeDtypeStruct((e, n_rows), F32),
                   jax.ShapeDtypeStruct((n_rows * SUBLANES, LANES), jnp.uint32)],
        compiler_params=_cparams(("parallel",)),
        name="proj_ln_router",
    )(a, w, h, g, b, rw_t)


def _beats(cand, cand_idx, ref, ref_idx):
    return (cand > ref) | ((cand == ref) & (cand_idx < ref_idx))


def _route_kernel(lg_ref, bias_ref, w_ref, pos_ref, cnt_ref, run_ref, *, t):
    E, G = N_EXPERTS, N_GROUPS
    gs = E // G

    @pl.when(pl.program_id(0) == 0)
    def _():
        run_ref[...] = jnp.zeros_like(run_ref)

    s = jax.nn.sigmoid(lg_ref[...])
    c = s + bias_ref[...][:, :1]
    sub = lax.broadcasted_iota(jnp.int32, (gs, t), 0)

    grp_rows = []
    for g in range(G):
        cg = c[g * gs:(g + 1) * gs]
        rank = jnp.zeros((gs, t), jnp.int32)
        for m in range(gs):
            rank = rank + _beats(cg[m:m + 1], m, cg, sub).astype(jnp.int32)
        grp_rows.append(jnp.sum(jnp.where(rank < 2, cg, 0.0), axis=0, keepdims=True))
    gidx = lax.broadcasted_iota(jnp.int32, (G, t), 0)
    gscore = jnp.zeros((G, t), F32)
    for g in range(G):
        gscore = jnp.where(gidx == g, grp_rows[g], gscore)
    grank = jnp.zeros((G, t), jnp.int32)
    for m in range(G):
        grank = grank + _beats(gscore[m:m + 1], m, gscore, gidx).astype(jnp.int32)
    gsel = grank < TOPK_GROUPS

    masked = jnp.concatenate(
        [jnp.where(gsel[g:g + 1], c[g * gs:(g + 1) * gs], -jnp.inf) for g in range(G)], axis=0)
    eidx = lax.broadcasted_iota(jnp.int32, (E, t), 0).astype(F32)
    sel = jnp.zeros((E, t), jnp.bool_)
    for _ in range(TOP_K):
        best = jnp.max(masked, axis=0, keepdims=True)
        pick = jnp.min(jnp.where((masked == best) & jnp.logical_not(sel), eidx, float(E)),
                       axis=0, keepdims=True)
        hit = eidx == pick
        sel = sel | hit
        masked = jnp.where(hit, -jnp.inf, masked)
    gate = jnp.where(sel, s, 0.0)
    w_ref[...] = gate / jnp.sum(gate, axis=0, keepdims=True) * ROUTED_SCALE

    li = lax.broadcasted_iota(jnp.int32, (t, t), 0)
    lj = lax.broadcasted_iota(jnp.int32, (t, t), 1)
    upper = (li <= lj).astype(BF16)
    self_ = sel.astype(F32)
    incl = jnp.dot(sel.astype(BF16), upper, preferred_element_type=F32)
    run = run_ref[...][:, :1]
    pos_ref[...] = jnp.where(sel, run + incl - self_, -1.0)
    run_new = run + jnp.sum(self_, axis=1, keepdims=True)
    run_ref[...] = jnp.broadcast_to(run_new, run_ref.shape)
    cnt_ref[...] = jnp.broadcast_to(run_new, cnt_ref.shape)


def _route(logits_t, bias_col):
    e, n_rows = logits_t.shape
    t = _pick_tile(n_rows, ROUTE_LANES, LANES)
    return pl.pallas_call(
        functools.partial(_route_kernel, t=t),
        grid=(n_rows // t,),
        in_specs=[pl.BlockSpec((e, t), lambda i: (0, i)),
                  pl.BlockSpec((e, LANES), lambda i: (0, 0))],
        out_specs=[pl.BlockSpec((e, t), lambda i: (0, i)),
                   pl.BlockSpec((e, t), lambda i: (0, i)),
                   pl.BlockSpec((e, LANES), lambda i: (0, 0))],
        out_shape=[jax.ShapeDtypeStruct((e, n_rows), F32),
                   jax.ShapeDtypeStruct((e, n_rows), F32),
                   jax.ShapeDtypeStruct((e, LANES), F32)],
        scratch_shapes=[pltpu.VMEM((e, LANES), F32)],
        compiler_params=_cparams(("arbitrary",)),
        name="route",
    )(logits_t, bias_col)


def _compact_kernel(w_ref, pos_ref, pst_ref, slot_ref, w8_ref, *, t):
    E = N_EXPERTS
    pos = pos_ref[...]
    sel = pos >= 0.0
    ri = lax.broadcasted_iota(jnp.int32, (E, E), 0)
    ci = lax.broadcasted_iota(jnp.int32, (E, E), 1)
    below = (ci < ri).astype(BF16)
    order = jnp.dot(below, sel.astype(BF16), preferred_element_type=F32)
    slot = pst_ref[...][:, :1] + pos
    wd = w_ref[...]
    kidx = lax.broadcasted_iota(jnp.int32, (TOP_K, t), 0)
    slots = jnp.zeros((TOP_K, t), F32)
    w8 = jnp.zeros((TOP_K, t), F32)
    for k in range(TOP_K):
        m = sel & (order == float(k))
        slots = jnp.where(kidx == k, jnp.sum(jnp.where(m, slot, 0.0), axis=0, keepdims=True), slots)
        w8 = jnp.where(kidx == k, jnp.sum(jnp.where(m, wd, 0.0), axis=0, keepdims=True), w8)
    slot_ref[...] = (slots * float(SUBLANES)).astype(jnp.int32)
    w8_ref[...] = w8


def _compact(w_dense, pos_dense, pstart_col):
    e, n_rows = w_dense.shape
    t = _pick_tile(n_rows, ROUTE_LANES, LANES)
    return pl.pallas_call(
        functools.partial(_compact_kernel, t=t),
        grid=(n_rows // t,),
        in_specs=[pl.BlockSpec((e, t), lambda i: (0, i)),
                  pl.BlockSpec((e, t), lambda i: (0, i)),
                  pl.BlockSpec((e, LANES), lambda i: (0, 0))],
        out_specs=[pl.BlockSpec((TOP_K, t), lambda i: (0, i)),
                   pl.BlockSpec((TOP_K, t), lambda i: (0, i))],
        out_shape=[jax.ShapeDtypeStruct((TOP_K, n_rows), jnp.int32),
                   jax.ShapeDtypeStruct((TOP_K, n_rows), F32)],
        compiler_params=_cparams(("parallel",)),
        name="compact",
    )(w_dense, pos_dense, pstart_col)


def _pack_bf16_pairs(x):
    half = x.shape[1] // 2
    lo = lax.bitcast_convert_type(x[:, :half].astype(BF16).astype(F32), jnp.uint32)
    hi = lax.bitcast_convert_type(x[:, half:].astype(BF16).astype(F32), jnp.uint32)
    return (lo >> 16) | hi


def _unpack_bf16_pairs(w):
    lo = lax.bitcast_convert_type(w << 16, F32)
    hi = lax.bitcast_convert_type(w & jnp.uint32(0xFFFF0000), F32)
    return lo, hi


def _store_rows_as_tiles(ref_at, x, rows):
    for c in range(SUBLANES):
        ref_at[pl.ds(c, rows, stride=SUBLANES), :] = x[:, c * LANES:(c + 1) * LANES]


def _load_tile_chunk(ref_at, c, rows):
    return ref_at[pl.ds(c, rows, stride=SUBLANES), :]


def _dispatch_kernel(pad_lo_ref, pad_hi_ref, slot_ref, h_ref, xs_ref, hp_ref, zero_ref, sem, zsem,
                     *, t):
    i = pl.program_id(0)
    cur = i % 2
    _store_rows_as_tiles(hp_ref.at[cur], _pack_bf16_pairs(h_ref[...]), t)

    def issue(tok, carry):
        src = hp_ref.at[cur, pl.ds(pl.multiple_of(tok * SUBLANES, SUBLANES), SUBLANES), :]
        for k in range(TOP_K):
            dst = xs_ref.at[pl.ds(pl.multiple_of(slot_ref[k * t + tok], SUBLANES), SUBLANES), :]
            pltpu.make_async_copy(src, dst, sem.at[cur]).start(priority=k % 2)
        return carry

    lax.fori_loop(0, t, issue, 0, unroll=2)

    @pl.when(i == 0)
    def _():
        zero_ref[...] = jnp.zeros_like(zero_ref)

        def zero_copy(p):
            return pltpu.make_async_copy(
                zero_ref, xs_ref.at[pl.ds(pl.multiple_of(p * SUBLANES, SUBLANES), SUBLANES), :], zsem)

        def per_expert(step):
            def body(e, carry):
                lax.fori_loop(pad_lo_ref[e], pad_hi_ref[e], step, 0)
                return carry
            return body

        def zissue(p, c):
            zero_copy(p).start()
            return c

        def zwait(p, c):
            zero_copy(p).wait()
            return c

        lax.fori_loop(0, N_EXPERTS, per_expert(zissue), 0)
        lax.fori_loop(0, N_EXPERTS, per_expert(zwait), 0)

    def wait_tile(buf):
        for k in range(TOP_K):
            pltpu.make_async_copy(hp_ref.at[buf], xs_ref.at[pl.ds(0, t * SUBLANES), :],
                                  sem.at[buf]).wait()

    @pl.when(i > 0)
    def _():
        wait_tile(1 - cur)

    @pl.when(i == pl.num_programs(0) - 1)
    def _():
        wait_tile(cur)


def _dispatch(h, slots_tiled, pad_lo, pad_hi, n_slots, t):
    n_rows, d = h.shape
    n_tiles = n_rows // t
    return pl.pallas_call(
        functools.partial(_dispatch_kernel, t=t),
        grid_spec=pltpu.PrefetchScalarGridSpec(
            num_scalar_prefetch=2,
            grid=(n_tiles,),
            in_specs=[pl.BlockSpec((None, None, TOP_K * t), lambda i, lo, hi: (i, 0, 0),
                                   memory_space=pltpu.SMEM),
                      pl.BlockSpec((t, d), lambda i, lo, hi: (i, 0))],
            out_specs=pl.BlockSpec(memory_space=pl.ANY),
            scratch_shapes=[pltpu.VMEM((2, t * SUBLANES, LANES), jnp.uint32),
                            pltpu.VMEM((SUBLANES, LANES), jnp.uint32),
                            pltpu.SemaphoreType.DMA((2,)),
                            pltpu.SemaphoreType.DMA(())]),
        out_shape=jax.ShapeDtypeStruct((n_slots * SUBLANES, LANES), jnp.uint32),
        compiler_params=_cparams(("arbitrary",)),
        name="dispatch",
    )(pad_lo, pad_hi, slots_tiled, h)


def _swiglu(x, w1, w3, w2):
    a = jnp.dot(x, w1, preferred_element_type=F32)
    g = jnp.dot(x, w3, preferred_element_type=F32)
    hmid = (a * jax.nn.sigmoid(a) * g).astype(BF16)
    return jnp.dot(hmid, w2, preferred_element_type=F32)


def _expert_kernel(be_ref, nu_ref, ids_ref, nids_ref, w1_ref, w3_ref, w2_ref, hp_ref, y_ref,
                   w1b_ref, w3b_ref, w2b_ref, xbuf0_ref, xbuf1_ref, sem):
    b = pl.program_id(0)
    n_used = nu_ref[0]
    xbuf = (xbuf0_ref, xbuf1_ref)
    rows = MOE_BLOCK

    def row_copy(s_ref, buf, r):
        src = hp_ref.at[pl.ds(pl.multiple_of(s_ref[r], SUBLANES), SUBLANES), :]
        return pltpu.make_async_copy(src, xbuf[buf].at[pl.ds(r * SUBLANES, SUBLANES), :], sem.at[buf])

    def wait_block(buf):
        pltpu.make_async_copy(hp_ref.at[pl.ds(0, rows * SUBLANES), :], xbuf[buf], sem.at[buf]).wait()

    @pl.when(b == 0)
    def _():
        def issue(r2, carry):
            for p in range(2):
                r = r2 * 2 + p
                src = hp_ref.at[pl.ds(pl.multiple_of(ids_ref[r], SUBLANES), SUBLANES), :]
                dst = xbuf0_ref.at[pl.ds(pl.multiple_of(r * SUBLANES, SUBLANES), SUBLANES), :]
                pltpu.make_async_copy(src, dst, sem.at[0]).start(priority=p)
            return carry

        lax.fori_loop(0, rows // 2, issue, 0)

    def run_block(cur):
        nxt = 1 - cur
        wait_block(cur)

        @pl.when((b == 0) | (be_ref[b] != be_ref[jnp.maximum(b - 1, 0)]))
        def _():
            w1b_ref[...] = w1_ref[...].astype(BF16)
            w3b_ref[...] = w3_ref[...].astype(BF16)
            w2b_ref[...] = w2_ref[...].astype(BF16)

        group = rows // 4

        def issue_group(g):
            for r in range(g * group, (g + 1) * group):
                row_copy(nids_ref, nxt, r).start(priority=r % 2)

        issue_group(0)
        halves = [_unpack_bf16_pairs(_load_tile_chunk(xbuf[cur], c, rows)) for c in range(SUBLANES)]
        x = jnp.concatenate([lo.astype(BF16) for lo, _ in halves]
                            + [hi.astype(BF16) for _, hi in halves], axis=1)
        issue_group(1)
        a = jnp.dot(x, w1b_ref[...], preferred_element_type=F32)
        issue_group(2)
        g = jnp.dot(x, w3b_ref[...], preferred_element_type=F32)
        issue_group(3)
        hmid = (a * jax.nn.sigmoid(a) * g).astype(BF16)
        y = _pack_bf16_pairs(jnp.dot(hmid, w2b_ref[...], preferred_element_type=F32))
        _store_rows_as_tiles(y_ref, y, rows)

        @pl.when(b == n_used - 1)
        def _():
            wait_block(nxt)

    @pl.when(b < n_used)
    def _():
        for parity in range(2):
            @pl.when(b % 2 == parity)
            def _():
                run_block(parity)


def _experts(hp, ids_tiled, w1, w3, w2, layer, block_e, n_used):
    d, f = w1.shape[2], w1.shape[3]
    blk_rows = MOE_BLOCK * SUBLANES
    nb = ids_tiled.shape[0]

    def blk(b, be, nu):
        return jnp.minimum(b, nu[0] - 1)

    def wmap(b, be, nu):
        return (layer, be[blk(b, be, nu)], 0, 0)

    return pl.pallas_call(
        _expert_kernel,
        grid_spec=pltpu.PrefetchScalarGridSpec(
            num_scalar_prefetch=2,
            grid=(nb,),
            in_specs=[pl.BlockSpec((None, None, MOE_BLOCK), lambda b, be, nu: (blk(b, be, nu), 0, 0),
                                   memory_space=pltpu.SMEM),
                      pl.BlockSpec((None, None, MOE_BLOCK),
                                   lambda b, be, nu: (blk(b + 1, be, nu), 0, 0),
                                   memory_space=pltpu.SMEM),
                      pl.BlockSpec((None, None, d, f), wmap),
                      pl.BlockSpec((None, None, d, f), wmap),
                      pl.BlockSpec((None, None, f, d), wmap),
                      pl.BlockSpec(memory_space=pl.ANY)],
            out_specs=pl.BlockSpec((blk_rows, LANES), lambda b, be, nu: (blk(b, be, nu), 0)),
            scratch_shapes=[pltpu.VMEM((d, f), BF16), pltpu.VMEM((d, f), BF16),
                            pltpu.VMEM((f, d), BF16),
                            pltpu.VMEM((blk_rows, LANES), jnp.uint32),
                            pltpu.VMEM((blk_rows, LANES), jnp.uint32),
                            pltpu.SemaphoreType.DMA((2,))]),
        out_shape=jax.ShapeDtypeStruct((nb * blk_rows, LANES), jnp.uint32),
        compiler_params=_cparams(("arbitrary",)),
        name="experts",
    )(block_e, n_used, ids_tiled, ids_tiled, w1, w3, w2, hp)


def _inverse_rows(slot_flat, vals, n_slots):
    n_idx = slot_flat.shape[0]
    window = LANES
    mesh = plsc.VectorSubcoreMesh(core_axis_name="core", subcore_axis_name="subcore")

    @functools.partial(pl.kernel, out_type=jax.ShapeDtypeStruct((n_slots, LANES), jnp.int32),
                       mesh=mesh, scratch_types=[])
    def scatter_rows(x_hbm, i_hbm, o_hbm):
        def body(x_vmem, i_vmem):
            pltpu.sync_copy(x_vmem, o_hbm.at[i_vmem.at[0]])

        pltpu.emit_pipeline(
            body,
            grid=(n_idx // window,),
            in_specs=[pl.BlockSpec((window, LANES), lambda i: (i, 0)),
                      pl.BlockSpec((1, window), lambda i: (0, i))],
            out_specs=[],
            core_axis_name="subcore",
            dimension_semantics=(pltpu.PARALLEL,),
        )(x_hbm, i_hbm)

    return scatter_rows(vals, slot_flat.reshape(1, n_idx))


def _combine_kernel(slot_ref, nslot_ref, h_ref, w8_ref, ws1_ref, ws3_ref, ws2_ref, g_ref, b_ref,
                    ys_ref, o_ref, ob_ref, ybuf0_ref, ybuf1_ref, sem, *, t):
    i = pl.program_id(0)
    last = pl.num_programs(0) - 1
    ybuf = (ybuf0_ref, ybuf1_ref)

    def row_copy(s_ref, buf, k, tok):
        src = ys_ref.at[pl.ds(pl.multiple_of(s_ref[k * t + tok], SUBLANES), SUBLANES), :]
        return pltpu.make_async_copy(
            src, ybuf[buf].at[k, pl.ds(pl.multiple_of(tok * SUBLANES, SUBLANES), SUBLANES), :],
            sem.at[buf])

    def wait_tile(buf):
        for k in range(TOP_K):
            pltpu.make_async_copy(ys_ref.at[pl.ds(0, t * SUBLANES), :], ybuf[buf].at[k],
                                  sem.at[buf]).wait()

    @pl.when(i == 0)
    def _():
        def issue(tok, carry):
            for k in range(TOP_K):
                row_copy(slot_ref, 0, k, tok).start(priority=k % 2)
            return carry

        lax.fori_loop(0, t, issue, 0, unroll=2)

    group = t // TOP_K

    def reduce_tile(cur):
        nxt = 1 - cur

        def issue_group(g):
            for tok in range(g * group, (g + 1) * group):
                for k in range(TOP_K):
                    row_copy(nslot_ref, nxt, k, tok).start(priority=k % 2)

        wait_tile(cur)
        h = h_ref[...]
        issue_group(0)
        shared = _swiglu(h.astype(BF16), ws1_ref[...], ws3_ref[...], ws2_ref[...])
        w8 = w8_ref[...]
        wk = [w8[:, k:k + 1] for k in range(TOP_K)]
        acc_lo, acc_hi = [], []
        for c in range(SUBLANES):
            if c + 1 < TOP_K:
                issue_group(c + 1)
            lo_c, hi_c = None, None
            for k in range(TOP_K):
                lo, hi = _unpack_bf16_pairs(_load_tile_chunk(ybuf[cur].at[k], c, t))
                lo_c = lo * wk[k] if k == 0 else lo_c + lo * wk[k]
                hi_c = hi * wk[k] if k == 0 else hi_c + hi * wk[k]
            acc_lo.append(lo_c)
            acc_hi.append(hi_c)
        acc = jnp.concatenate(acc_lo + acc_hi, axis=1)
        y = DN_ALPHA * h + (acc + shared)
        out = _layer_norm(y, g_ref[...], b_ref[...])
        o_ref[...] = out
        ob_ref[...] = out.astype(BF16)

        @pl.when(i == last)
        def _():
            wait_tile(nxt)

    for parity in range(2):
        @pl.when(i % 2 == parity)
        def _():
            reduce_tile(parity)


def _combine(h, ys, slots_tiled, w8_t, ws1, ws3, ws2, g, b, t, out_rows, out_map):
    n_rows, d = h.shape
    f = ws1.shape[1]
    n_tiles = n_rows // t
    return pl.pallas_call(
        functools.partial(_combine_kernel, t=t),
        grid=(n_tiles,),
        in_specs=[pl.BlockSpec((None, None, TOP_K * t), lambda i: (i, 0, 0),
                               memory_space=pltpu.SMEM),
                  pl.BlockSpec((None, None, TOP_K * t),
                               lambda i: (jnp.minimum(i + 1, n_tiles - 1), 0, 0),
                               memory_space=pltpu.SMEM),
                  pl.BlockSpec((t, d), lambda i: (i, 0)),
                  pl.BlockSpec((t, TOP_K), lambda i: (i, 0)),
                  pl.BlockSpec((d, f), lambda i: (0, 0)),
                  pl.BlockSpec((d, f), lambda i: (0, 0)),
                  pl.BlockSpec((f, d), lambda i: (0, 0)),
                  pl.BlockSpec((1, d), lambda i: (0, 0)),
                  pl.BlockSpec((1, d), lambda i: (0, 0)),
                  pl.BlockSpec(memory_space=pl.ANY)],
        out_specs=[pl.BlockSpec((t, d), lambda i: (out_map(i), 0)),
                   pl.BlockSpec((t, d), lambda i: (i, 0))],
        out_shape=[jax.ShapeDtypeStruct((out_rows, d), F32),
                   jax.ShapeDtypeStruct((n_rows, d), BF16)],
        scratch_shapes=[pltpu.VMEM((TOP_K, t * SUBLANES, LANES), jnp.uint32),
                        pltpu.VMEM((TOP_K, t * SUBLANES, LANES), jnp.uint32),
                        pltpu.SemaphoreType.DMA((2,))],
        compiler_params=_cparams(("arbitrary",)),
        name="combine_ln",
    )(slots_tiled, slots_tiled, h, w8_t, ws1, ws3, ws2, g, b, ys)


def _moe_ln(h, hp, logits_t, router_bias, w1, w3, w2, layer, ws1, ws3, ws2, g, b, drop_rows,
            seq_rows):
    n_rows, d = h.shape
    E = N_EXPERTS
    assert d // 2 == SUBLANES * LANES, d
    bias_col = jnp.broadcast_to(router_bias.astype(F32)[:, None], (E, LANES))
    w_dense, pos_dense, cnt = _route(logits_t, bias_col)

    counts = cnt[:, 0].astype(jnp.int32)
    pcounts = (counts + MOE_BLOCK - 1) // MOE_BLOCK * MOE_BLOCK
    pends = jnp.cumsum(pcounts)
    pstarts = pends - pcounts
    n_blocks = -(-(n_rows * TOP_K) // MOE_BLOCK) + E
    n_slots = n_blocks * MOE_BLOCK
    block_lo = jnp.arange(n_blocks, dtype=jnp.int32) * MOE_BLOCK
    block_e = jnp.minimum(jnp.sum(pends[None, :] <= block_lo[:, None], axis=1), E - 1).astype(jnp.int32)
    n_used = (pends[-1:] // MOE_BLOCK).astype(jnp.int32)
    pstart_col = jnp.broadcast_to(pstarts.astype(F32)[:, None], (E, LANES))

    slots, w8 = _compact(w_dense, pos_dense, pstart_col)
    t = _pick_tile(n_rows, TOKEN_TILE, LANES)
    n_tiles = n_rows // t
    slots_tiled = slots.reshape(TOP_K, n_tiles, t).transpose(1, 0, 2).reshape(n_tiles, 1, TOP_K * t)
    tok_rows = jnp.tile(jnp.arange(n_rows, dtype=jnp.int32) * SUBLANES, TOP_K)
    inv = _inverse_rows((slots // SUBLANES).reshape(-1),
                        jnp.broadcast_to(tok_rows[:, None], (n_rows * TOP_K, LANES)), n_slots)
    ids = jnp.clip(inv[:, 0] // SUBLANES, 0, n_rows - 1) * SUBLANES
    ys = _experts(hp, ids.reshape(n_blocks, 1, MOE_BLOCK), w1, w3, w2, layer, block_e, n_used)
    if drop_rows == t:
        per_seq = seq_rows // t
        out_rows = n_rows - (n_rows // seq_rows) * t

        def out_map(i):
            return (i // per_seq) * (per_seq - 1) + jnp.maximum(i % per_seq - 1, 0)
    else:
        out_rows, out_map = n_rows, (lambda i: i)
    return _combine(h, ys, slots_tiled, w8.T, ws1, ws3, ws2, g, b, t, out_rows, out_map)


def kernel(x, meta_tokens, gla_w_in, gla_w_gate_up, gla_b_gate, gla_norm_g, gla_w_out,
           conv_w_in, conv_w, conv_w_out, ln1_g, ln1_b, router_w, router_bias,
           exp_w1, exp_w3, exp_w2, shared_w1, shared_w3, shared_w2, ln2_g, ln2_b):
    batch, seq, d = x.shape
    H = GLA_HEADS
    dk = d // 2 // H
    n_qkvr = 2 * H * dk + 2 * d
    rank = gla_w_in.shape[2] - n_qkvr
    seq_rows = -(-(N_META + seq) // ROW_ALIGN) * ROW_ALIGN
    pad_rows = seq_rows - N_META - seq
    n_rows = batch * seq_rows

    meta = jnp.broadcast_to(meta_tokens.astype(x.dtype)[None], (batch, N_META, d))
    h = jnp.concatenate([jnp.zeros((batch, pad_rows, d), x.dtype), meta, x], axis=1)
    h = h.reshape(n_rows, d)
    hb = h

    def row(v):
        return v.astype(F32)[None, :]

    for i in range(DEPTH):
        jm = i // 2
        if i % 2 == 0:
            w_in = gla_w_in[jm]
            w_all = jnp.pad(w_in, ((0, 0), (0, GATE_PAD - rank))).astype(BF16)
            wg_up = jnp.pad(gla_w_gate_up[jm], ((0, GATE_PAD - rank), (0, 0))).astype(BF16)
            qkvrg = _matmul(hb, w_all, BF16, IN_PROJ_ROWS, IN_PROJ_COLS, "gla_in_proj")
            mix = _gla(qkvrg, wg_up, row(gla_b_gate[jm]), row(gla_norm_g[jm]),
                       batch=batch, seq_rows=seq_rows, d_model=d)
            w_out = gla_w_out[jm].astype(BF16)
        else:
            mix = _conv_mix(hb, conv_w_in[jm].astype(BF16), conv_w[jm].astype(F32),
                            seq_rows=seq_rows, pad_rows=pad_rows)
            w_out = conv_w_out[jm].astype(BF16)
        h, logits_t, hp = _proj_ln_router(mix, w_out, h, row(ln1_g[i]), row(ln1_b[i]),
                                          router_w[i].T.astype(BF16))
        drop = pad_rows + N_META if i == DEPTH - 1 else 0
        h, hb = _moe_ln(h, hp, logits_t, router_bias[i], exp_w1, exp_w3, exp_w2, i,
                        shared_w1[i].astype(BF16), shared_w3[i].astype(BF16),
                        shared_w2[i].astype(BF16), row(ln2_g[i]), row(ln2_b[i]), drop, seq_rows)
    if h.shape[0] == batch * seq:
        return h.reshape(batch, seq, d)
    return h.reshape(batch, seq_rows, d)[:, pad_rows + N_META:]
```

```python
import functools

import jax
import jax.numpy as jnp
from jax import lax
from jax.experimental import pallas as pl
from jax.experimental.pallas import tpu as pltpu
from jax.experimental.pallas import tpu_sc as plsc

N_META = 16
GLA_HEADS = 4
GLA_GATE_TAU = 16.0
CONV_WIDTH = 3
N_EXPERTS = 64
TOP_K = 8
N_GROUPS = 8
TOPK_GROUPS = 4
ROUTED_SCALE = 2.5
LN_EPS = 1e-5
RMS_EPS = 1e-6
DEPTH = 2
DN_ALPHA = (2 * DEPTH) ** 0.25
LOG2_E = 1.4426950408889634

LANES = 128
SUBLANES = 8
ROW_ALIGN = 128
GLA_CHUNK = 64
GLA_SUB = 16
GLA_HEADS_PER_STEP = 4
MXU_WIDTH = 256
GATE_PAD = MXU_WIDTH
MOE_BLOCK = 512
TOKEN_TILE = 128
IN_PROJ_ROWS, IN_PROJ_COLS = 832, 1280
CONV_ROWS, CONV_COLS = 1024, 512
OUT_PROJ_ROWS = 512
GLA_ROWS = 640
ROUTE_LANES = 512
VMEM_LIMIT = 56 * 1024 * 1024

F32 = jnp.float32
BF16 = jnp.bfloat16


def _pick_tile(n, target, mult):
    best = None
    for t in range(mult, min(n, target) + 1, mult):
        if n % t == 0:
            best = t
    assert best is not None, (n, target, mult)
    return best


def _cparams(sem):
    return pltpu.CompilerParams(dimension_semantics=sem, vmem_limit_bytes=VMEM_LIMIT)


def _mm_kernel(x_ref, w_ref, o_ref):
    x = x_ref[...].astype(BF16)
    o_ref[...] = jnp.dot(x, w_ref[...], preferred_element_type=F32).astype(o_ref.dtype)


def _matmul(x, w, out_dtype, tm_target, tn_target, name):
    m, k = x.shape
    n = w.shape[1]
    tm = _pick_tile(m, tm_target, 16)
    tn = _pick_tile(n, tn_target, LANES)
    return pl.pallas_call(
        _mm_kernel,
        grid=(m // tm, n // tn),
        in_specs=[pl.BlockSpec((tm, k), lambda i, j: (i, 0)),
                  pl.BlockSpec((k, tn), lambda i, j: (0, j))],
        out_specs=pl.BlockSpec((tm, tn), lambda i, j: (i, j)),
        out_shape=jax.ShapeDtypeStruct((m, n), out_dtype),
        compiler_params=_cparams(("parallel", "arbitrary")),
        name=name,
    )(x, w)


def _gla_kernel(q_ref, k_ref, v_ref, r_ref, gl_ref, wg_ref, bg_ref, ng_ref, o_ref, st_ref,
                *, n_chunks, dk, dv, heads):
    C, S = GLA_CHUNK, GLA_SUB

    @pl.when(pl.program_id(2) == 0)
    def _():
        st_ref[...] = jnp.zeros_like(st_ref)

    ri = lax.broadcasted_iota(jnp.int32, (C, C), 0)
    ci = lax.broadcasted_iota(jnp.int32, (C, C), 1)
    tri = (ri >= ci).astype(BF16)
    rs = lax.broadcasted_iota(jnp.int32, (S, S), 0)
    cs = lax.broadcasted_iota(jnp.int32, (S, S), 1)
    nt = (((1,), (1,)), ((), ()))
    tn = (((0,), (0,)), ((), ()))
    half = S // 2

    def chunk(c, carry):
        r0 = pl.multiple_of(c * C, C)
        rows_c = pl.ds(r0, C)
        hs = range(heads)
        kcol = [pl.ds(hd * dk, dk) for hd in hs]
        vcol = [pl.ds(hd * dv, dv) for hd in hs]
        gl = gl_ref[rows_c, :]
        q = [q_ref[rows_c, kcol[hd]].astype(F32) * (dk ** -0.5) for hd in hs]
        kk = [k_ref[rows_c, kcol[hd]].astype(F32) for hd in hs]
        vv = [v_ref[rows_c, vcol[hd]] for hd in hs]
        st = [st_ref[hd] for hd in hs]
        z = [jnp.dot(gl, wg_ref[:, kcol[hd]], preferred_element_type=F32) + bg_ref[:, kcol[hd]]
             for hd in hs]
        b = []
        for hd in hs:
            la = ((jnp.minimum(z[hd], 0.0) - jnp.log(1.0 + jnp.exp(-jnp.abs(z[hd]))))
                  * (1.0 / GLA_GATE_TAU))
            h1 = la.astype(BF16)
            e1 = la - h1.astype(F32)
            h2 = e1.astype(BF16)
            h3 = (e1 - h2.astype(F32)).astype(BF16)
            b3 = jnp.dot(tri, jnp.concatenate([h1, h2, h3], axis=1), preferred_element_type=F32)
            b.append((b3[:, :dk] + b3[:, dk:2 * dk] + b3[:, 2 * dk:]) * LOG2_E)

        o_inter, st_new, off = [], [], []
        for hd in hs:
            qe = (q[hd] * jnp.exp2(b[hd])).astype(BF16)
            o_inter.append(lax.dot_general(qe, st[hd].astype(BF16), nt, preferred_element_type=F32))
            bl = b[hd][C - 1:C]
            khat = (kk[hd] * jnp.exp2(bl - b[hd])).astype(BF16)
            upd = lax.dot_general(vv[hd], khat, tn, preferred_element_type=F32)
            st_new.append(st[hd] * jnp.exp2(bl) + upd)
            offs = []
            for i in range(1, C // S):
                lo = i * S
                bref = b[hd][lo - 1:lo]
                qt = (q[hd][lo:lo + S] * jnp.exp2(b[hd][lo:lo + S] - bref)).astype(BF16)
                kt = (kk[hd][:lo] * jnp.exp2(bref - b[hd][:lo])).astype(BF16)
                offs.append(lax.dot_general(qt, kt, nt, preferred_element_type=F32))
            off.append(offs)

        dmats = []
        for hd in hs:
            blocks = []
            for i in range(C // S):
                lo = i * S
                qs = q[hd][lo:lo + S]
                bs = b[hd][lo:lo + S]
                dmat = jnp.zeros((S, S), F32)
                for j in range(S):
                    kj = kk[hd][lo + j:lo + j + 1]
                    bj = b[hd][lo + j:lo + j + 1]
                    if j < half:
                        p = qs * kj * jnp.exp2(bs - bj)
                        a = jnp.sum(p, axis=-1, keepdims=True)
                    else:
                        p = qs[half:] * kj * jnp.exp2(bs[half:] - bj)
                        a = jnp.concatenate([jnp.zeros((half, 1), F32),
                                             jnp.sum(p, axis=-1, keepdims=True)], axis=0)
                    dmat = jnp.where(cs == j, a, dmat)
                blocks.append(jnp.where(rs >= cs, dmat, 0.0).astype(BF16))
            dmats.append(blocks)

        for hd in hs:
            rows = []
            for i in range(C // S):
                lo = i * S
                o_i = jnp.dot(dmats[hd][i], vv[hd][lo:lo + S], preferred_element_type=F32)
                if i > 0:
                    o_i = o_i + jnp.dot(off[hd][i - 1].astype(BF16), vv[hd][:lo],
                                        preferred_element_type=F32)
                rows.append(o_i)
            o = o_inter[hd] + jnp.concatenate(rows, axis=0)
            ms = jnp.mean(o * o, axis=-1, keepdims=True)
            r = r_ref[rows_c, vcol[hd]].astype(F32)
            y = o * lax.rsqrt(ms + RMS_EPS) * ng_ref[:, vcol[hd]] * (r * jax.nn.sigmoid(r))
            st_ref[hd] = st_new[hd]
            o_ref[rows_c, vcol[hd]] = y.astype(o_ref.dtype)
        return carry

    lax.fori_loop(0, n_chunks, chunk, 0)


def _gla(qkvrg, wg, bg, ng, *, batch, seq_rows, d_model):
    H, hps = GLA_HEADS, GLA_HEADS_PER_STEP
    dk = d_model // 2 // H
    dv = d_model // H
    n_rows = batch * seq_rows
    rblk = _pick_tile(seq_rows, GLA_ROWS, GLA_CHUNK)
    steps = seq_rows // rblk
    groups = H // hps
    off_k, off_v, off_r = groups, (2 * H * dk) // (hps * dv), (2 * H * dk) // (hps * dv) + groups

    off_g = (2 * H * dk + 2 * H * dv) // GATE_PAD

    def rowmap(off):
        return lambda b, h, i: (b * steps + i, off + h)

    return pl.pallas_call(
        functools.partial(_gla_kernel, n_chunks=rblk // GLA_CHUNK, dk=dk, dv=dv, heads=hps),
        grid=(batch, groups, steps),
        in_specs=[pl.BlockSpec((rblk, hps * dk), rowmap(0)),
                  pl.BlockSpec((rblk, hps * dk), rowmap(off_k)),
                  pl.BlockSpec((rblk, hps * dv), rowmap(off_v)),
                  pl.BlockSpec((rblk, hps * dv), rowmap(off_r)),
                  pl.BlockSpec((rblk, GATE_PAD), lambda b, h, i: (b * steps + i, off_g)),
                  pl.BlockSpec((GATE_PAD, hps * dk), lambda b, h, i: (0, h)),
                  pl.BlockSpec((1, hps * dk), lambda b, h, i: (0, h)),
                  pl.BlockSpec((1, hps * dv), lambda b, h, i: (0, h))],
        out_specs=pl.BlockSpec((rblk, hps * dv), lambda b, h, i: (b * steps + i, h)),
        out_shape=jax.ShapeDtypeStruct((n_rows, H * dv), BF16),
        scratch_shapes=[pltpu.VMEM((hps, dv, dk), F32)],
        compiler_params=_cparams(("parallel", "parallel", "arbitrary")),
        name="gla_chunks",
    )(qkvrg, qkvrg, qkvrg, qkvrg, qkvrg, wg, bg, ng)


def _conv_kernel(x_ref, wb_ref, wc_ref, wh_ref, cw_ref, o_ref, ubuf_ref, carry_ref,
                 *, seq_rows, pad_rows, tm):
    i = pl.program_id(0)
    j = pl.program_id(1)
    x = x_ref[...]
    bg = jnp.dot(x, wb_ref[...], preferred_element_type=F32)
    cg = jnp.dot(x, wc_ref[...], preferred_element_type=F32)
    hh = jnp.dot(x, wh_ref[...], preferred_element_type=F32)
    row = i * tm + lax.broadcasted_iota(jnp.int32, (tm, 1), 0)
    u = jnp.where(row % seq_rows >= pad_rows, cg * hh, 0.0)

    @pl.when(i == 0)
    def _():
        carry_ref[j] = jnp.zeros(carry_ref.shape[1:], F32)

    ubuf_ref[pl.ds(0, SUBLANES), :] = carry_ref[j]
    ubuf_ref[pl.ds(SUBLANES, tm), :] = u
    carry_ref[j] = u[tm - SUBLANES:]
    cw = cw_ref[...]
    conv = u * cw[CONV_WIDTH - 1:CONV_WIDTH]
    for s in range(1, CONV_WIDTH):
        conv = conv + ubuf_ref[pl.ds(SUBLANES - s, tm), :] * cw[CONV_WIDTH - 1 - s:CONV_WIDTH - s]
    o_ref[...] = (bg * conv).astype(o_ref.dtype)


def _conv_mix(xb, w_in, conv_w, *, seq_rows, pad_rows):
    n_rows, d = xb.shape
    tm = _pick_tile(n_rows, CONV_ROWS, SUBLANES)
    tn = _pick_tile(d, CONV_COLS, LANES)
    nj = d // tn
    return pl.pallas_call(
        functools.partial(_conv_kernel, seq_rows=seq_rows, pad_rows=pad_rows, tm=tm),
        grid=(n_rows // tm, nj),
        in_specs=[pl.BlockSpec((tm, d), lambda i, j: (i, 0)),
                  pl.BlockSpec((d, tn), lambda i, j: (0, j)),
                  pl.BlockSpec((d, tn), lambda i, j: (0, nj + j)),
                  pl.BlockSpec((d, tn), lambda i, j: (0, 2 * nj + j)),
                  pl.BlockSpec((CONV_WIDTH, tn), lambda i, j: (0, j))],
        out_specs=pl.BlockSpec((tm, tn), lambda i, j: (i, j)),
        out_shape=jax.ShapeDtypeStruct((n_rows, d), BF16),
        scratch_shapes=[pltpu.VMEM((tm + SUBLANES, tn), F32),
                        pltpu.VMEM((nj, SUBLANES, tn), F32)],
        compiler_params=_cparams(("arbitrary", "arbitrary")),
        name="conv_mix",
    )(xb, w_in, w_in, w_in, conv_w)


def _layer_norm(y, g, b):
    mu = jnp.mean(y, axis=-1, keepdims=True)
    yc = y - mu
    var = jnp.mean(yc * yc, axis=-1, keepdims=True)
    return yc * lax.rsqrt(var + LN_EPS) * g + b


def _proj_ln_kernel(a_ref, w_ref, h_ref, g_ref, b_ref, rw_ref, o_ref, lg_ref, hp_ref):
    y = jnp.dot(a_ref[...], w_ref[...], preferred_element_type=F32) + DN_ALPHA * h_ref[...]
    h = _layer_norm(y, g_ref[...], b_ref[...])
    o_ref[...] = h
    _store_rows_as_tiles(hp_ref, _pack_bf16_pairs(h), h.shape[0])
    lg_ref[...] = lax.dot_general(rw_ref[...], h.astype(BF16), (((1,), (1,)), ((), ())),
                                  preferred_element_type=F32)


def _proj_ln_router(a, w, h, g, b, rw_t):
    n_rows, kin = a.shape
    d = w.shape[1]
    e = rw_t.shape[0]
    tm = _pick_tile(n_rows, OUT_PROJ_ROWS, LANES)
    return pl.pallas_call(
        _proj_ln_kernel,
        grid=(n_rows // tm,),
        in_specs=[pl.BlockSpec((tm, kin), lambda i: (i, 0)),
                  pl.BlockSpec((kin, d), lambda i: (0, 0)),
                  pl.BlockSpec((tm, d), lambda i: (i, 0)),
                  pl.BlockSpec((1, d), lambda i: (0, 0)),
                  pl.BlockSpec((1, d), lambda i: (0, 0)),
                  pl.BlockSpec((e, d), lambda i: (0, 0))],
        out_specs=[pl.BlockSpec((tm, d), lambda i: (i, 0)),
                   pl.BlockSpec((e, tm), lambda i: (0, i)),
                   pl.BlockSpec((tm * SUBLANES, LANES), lambda i: (i, 0))],
        out_shape=[jax.ShapeDtypeStruct((n_rows, d), F32),
                   jax.ShapeDtypeStruct((e, n_rows), F32),
                   jax.ShapeDtypeStruct((n_rows * SUBLANES, LANES), jnp.uint32)],
        compiler_params=_cparams(("parallel",)),
        name="proj_ln_router",
    )(a, w, h, g, b, rw_t)


def _beats(cand, cand_idx, ref, ref_idx):
    return (cand > ref) | ((cand == ref) & (cand_idx < ref_idx))


def _route_kernel(lg_ref, bias_ref, w_ref, pos_ref, cnt_ref, run_ref, *, t):
    E, G = N_EXPERTS, N_GROUPS
    gs = E // G

    @pl.when(pl.program_id(0) == 0)
    def _():
        run_ref[...] = jnp.zeros_like(run_ref)

    s = jax.nn.sigmoid(lg_ref[...])
    c = s + bias_ref[...][:, :1]
    sub = lax.broadcasted_iota(jnp.int32, (gs, t), 0)

    grp_rows = []
    for g in range(G):
        cg = c[g * gs:(g + 1) * gs]
        rank = jnp.zeros((gs, t), jnp.int32)
        for m in range(gs):
            rank = rank + _beats(cg[m:m + 1], m, cg, sub).astype(jnp.int32)
        grp_rows.append(jnp.sum(jnp.where(rank < 2, cg, 0.0), axis=0, keepdims=True))
    gidx = lax.broadcasted_iota(jnp.int32, (G, t), 0)
    gscore = jnp.zeros((G, t), F32)
    for g in range(G):
        gscore = jnp.where(gidx == g, grp_rows[g], gscore)
    grank = jnp.zeros((G, t), jnp.int32)
    for m in range(G):
        grank = grank + _beats(gscore[m:m + 1], m, gscore, gidx).astype(jnp.int32)
    gsel = grank < TOPK_GROUPS

    masked = jnp.concatenate(
        [jnp.where(gsel[g:g + 1], c[g * gs:(g + 1) * gs], -jnp.inf) for g in range(G)], axis=0)
    eidx = lax.broadcasted_iota(jnp.int32, (E, t), 0).astype(F32)
    sel = jnp.zeros((E, t), jnp.bool_)
    for _ in range(TOP_K):
        best = jnp.max(masked, axis=0, keepdims=True)
        pick = jnp.min(jnp.where((masked == best) & jnp.logical_not(sel), eidx, float(E)),
                       axis=0, keepdims=True)
        hit = eidx == pick
        sel = sel | hit
        masked = jnp.where(hit, -jnp.inf, masked)
    gate = jnp.where(sel, s, 0.0)
    w_ref[...] = gate / jnp.sum(gate, axis=0, keepdims=True) * ROUTED_SCALE

    li = lax.broadcasted_iota(jnp.int32, (t, t), 0)
    lj = lax.broadcasted_iota(jnp.int32, (t, t), 1)
    upper = (li <= lj).astype(BF16)
    self_ = sel.astype(F32)
    incl = jnp.dot(sel.astype(BF16), upper, preferred_element_type=F32)
    run = run_ref[...][:, :1]
    pos_ref[...] = jnp.where(sel, run + incl - self_, -1.0)
    run_new = run + jnp.sum(self_, axis=1, keepdims=True)
    run_ref[...] = jnp.broadcast_to(run_new, run_ref.shape)
    cnt_ref[...] = jnp.broadcast_to(run_new, cnt_ref.shape)


def _route(logits_t, bias_col):
    e, n_rows = logits_t.shape
    t = _pick_tile(n_rows, ROUTE_LANES, LANES)
    return pl.pallas_call(
        functools.partial(_route_kernel, t=t),
        grid=(n_rows // t,),
        in_specs=[pl.BlockSpec((e, t), lambda i: (0, i)),
                  pl.BlockSpec((e, LANES), lambda i: (0, 0))],
        out_specs=[pl.BlockSpec((e, t), lambda i: (0, i)),
                   pl.BlockSpec((e, t), lambda i: (0, i)),
                   pl.BlockSpec((e, LANES), lambda i: (0, 0))],
        out_shape=[jax.ShapeDtypeStruct((e, n_rows), F32),
                   jax.ShapeDtypeStruct((e, n_rows), F32),
                   jax.ShapeDtypeStruct((e, LANES), F32)],
        scratch_shapes=[pltpu.VMEM((e, LANES), F32)],
        compiler_params=_cparams(("arbitrary",)),
        name="route",
    )(logits_t, bias_col)


def _compact_kernel(w_ref, pos_ref, pst_ref, slot_ref, w8_ref, *, t):
    E = N_EXPERTS
    pos = pos_ref[...]
    sel = pos >= 0.0
    ri = lax.broadcasted_iota(jnp.int32, (E, E), 0)
    ci = lax.broadcasted_iota(jnp.int32, (E, E), 1)
    below = (ci < ri).astype(BF16)
    order = jnp.dot(below, sel.astype(BF16), preferred_element_type=F32)
    slot = pst_ref[...][:, :1] + pos
    wd = w_ref[...]
    kidx = lax.broadcasted_iota(jnp.int32, (TOP_K, t), 0)
    slots = jnp.zeros((TOP_K, t), F32)
    w8 = jnp.zeros((TOP_K, t), F32)
    for k in range(TOP_K):
        m = sel & (order == float(k))
        slots = jnp.where(kidx == k, jnp.sum(jnp.where(m, slot, 0.0), axis=0, keepdims=True), slots)
        w8 = jnp.where(kidx == k, jnp.sum(jnp.where(m, wd, 0.0), axis=0, keepdims=True), w8)
    slot_ref[...] = (slots * float(SUBLANES)).astype(jnp.int32)
    w8_ref[...] = w8


def _compact(w_dense, pos_dense, pstart_col):
    e, n_rows = w_dense.shape
    t = _pick_tile(n_rows, ROUTE_LANES, LANES)
    return pl.pallas_call(
        functools.partial(_compact_kernel, t=t),
        grid=(n_rows // t,),
        in_specs=[pl.BlockSpec((e, t), lambda i: (0, i)),
                  pl.BlockSpec((e, t), lambda i: (0, i)),
                  pl.BlockSpec((e, LANES), lambda i: (0, 0))],
        out_specs=[pl.BlockSpec((TOP_K, t), lambda i: (0, i)),
                   pl.BlockSpec((TOP_K, t), lambda i: (0, i))],
        out_shape=[jax.ShapeDtypeStruct((TOP_K, n_rows), jnp.int32),
                   jax.ShapeDtypeStruct((TOP_K, n_rows), F32)],
        compiler_params=_cparams(("parallel",)),
        name="compact",
    )(w_dense, pos_dense, pstart_col)


def _pack_bf16_pairs(x):
    half = x.shape[1] // 2
    lo = lax.bitcast_convert_type(x[:, :half].astype(BF16).astype(F32), jnp.uint32)
    hi = lax.bitcast_convert_type(x[:, half:].astype(BF16).astype(F32), jnp.uint32)
    return (lo >> 16) | hi


def _unpack_bf16_pairs(w):
    lo = lax.bitcast_convert_type(w << 16, F32)
    hi = lax.bitcast_convert_type(w & jnp.uint32(0xFFFF0000), F32)
    return lo, hi


def _store_rows_as_tiles(ref_at, x, rows):
    for c in range(SUBLANES):
        ref_at[pl.ds(c, rows, stride=SUBLANES), :] = x[:, c * LANES:(c + 1) * LANES]


def _load_tile_chunk(ref_at, c, rows):
    return ref_at[pl.ds(c, rows, stride=SUBLANES), :]


def _swiglu(x, w1, w3, w2):
    a = jnp.dot(x, w1, preferred_element_type=F32)
    g = jnp.dot(x, w3, preferred_element_type=F32)
    hmid = (a * jax.nn.sigmoid(a) * g).astype(BF16)
    return jnp.dot(hmid, w2, preferred_element_type=F32)


def _expert_kernel(be_ref, nu_ref, ids_ref, nids_ref, w1_ref, w3_ref, w2_ref, hp_ref, y_ref,
                   w1b_ref, w3b_ref, w2b_ref, xbuf0_ref, xbuf1_ref, sem):
    b = pl.program_id(0)
    n_used = nu_ref[0]
    xbuf = (xbuf0_ref, xbuf1_ref)
    rows = MOE_BLOCK

    def row_copy(s_ref, buf, r):
        src = hp_ref.at[pl.ds(pl.multiple_of(s_ref[r], SUBLANES), SUBLANES), :]
        return pltpu.make_async_copy(src, xbuf[buf].at[pl.ds(r * SUBLANES, SUBLANES), :], sem.at[buf])

    def wait_block(buf):
        pltpu.make_async_copy(hp_ref.at[pl.ds(0, rows * SUBLANES), :], xbuf[buf], sem.at[buf]).wait()

    @pl.when(b == 0)
    def _():
        def issue(r2, carry):
            for p in range(2):
                r = r2 * 2 + p
                src = hp_ref.at[pl.ds(pl.multiple_of(ids_ref[r], SUBLANES), SUBLANES), :]
                dst = xbuf0_ref.at[pl.ds(pl.multiple_of(r * SUBLANES, SUBLANES), SUBLANES), :]
                pltpu.make_async_copy(src, dst, sem.at[0]).start(priority=1)
            return carry

        lax.fori_loop(0, rows // 2, issue, 0)

    def run_block(cur):
        nxt = 1 - cur
        wait_block(cur)

        @pl.when((b == 0) | (be_ref[b] != be_ref[jnp.maximum(b - 1, 0)]))
        def _():
            w1b_ref[...] = w1_ref[...].astype(BF16)
            w3b_ref[...] = w3_ref[...].astype(BF16)
            w2b_ref[...] = w2_ref[...].astype(BF16)

        group = rows // 4

        def issue_group(g):
            for r in range(g * group, (g + 1) * group):
                row_copy(nids_ref, nxt, r).start(priority=1)

        issue_group(0)
        halves = [_unpack_bf16_pairs(_load_tile_chunk(xbuf[cur], c, rows)) for c in range(SUBLANES)]
        x = jnp.concatenate([lo.astype(BF16) for lo, _ in halves]
                            + [hi.astype(BF16) for _, hi in halves], axis=1)
        issue_group(1)
        a = jnp.dot(x, w1b_ref[...], preferred_element_type=F32)
        issue_group(2)
        g = jnp.dot(x, w3b_ref[...], preferred_element_type=F32)
        issue_group(3)
        hmid = (a * jax.nn.sigmoid(a) * g).astype(BF16)
        y = _pack_bf16_pairs(jnp.dot(hmid, w2b_ref[...], preferred_element_type=F32))
        _store_rows_as_tiles(y_ref, y, rows)

        @pl.when(b == n_used - 1)
        def _():
            wait_block(nxt)

    @pl.when(b < n_used)
    def _():
        for parity in range(2):
            @pl.when(b % 2 == parity)
            def _():
                run_block(parity)


def _experts(hp, ids_tiled, w1, w3, w2, layer, block_e, n_used):
    d, f = w1.shape[2], w1.shape[3]
    blk_rows = MOE_BLOCK * SUBLANES
    nb = ids_tiled.shape[0]

    def blk(b, be, nu):
        return jnp.minimum(b, nu[0] - 1)

    def wmap(b, be, nu):
        return (layer, be[blk(b, be, nu)], 0, 0)

    return pl.pallas_call(
        _expert_kernel,
        grid_spec=pltpu.PrefetchScalarGridSpec(
            num_scalar_prefetch=2,
            grid=(nb,),
            in_specs=[pl.BlockSpec((None, None, MOE_BLOCK), lambda b, be, nu: (blk(b, be, nu), 0, 0),
                                   memory_space=pltpu.SMEM),
                      pl.BlockSpec((None, None, MOE_BLOCK),
                                   lambda b, be, nu: (blk(b + 1, be, nu), 0, 0),
                                   memory_space=pltpu.SMEM),
                      pl.BlockSpec((None, None, d, f), wmap),
                      pl.BlockSpec((None, None, d, f), wmap),
                      pl.BlockSpec((None, None, f, d), wmap),
                      pl.BlockSpec(memory_space=pl.ANY)],
            out_specs=pl.BlockSpec((blk_rows, LANES), lambda b, be, nu: (blk(b, be, nu), 0)),
            scratch_shapes=[pltpu.VMEM((d, f), BF16), pltpu.VMEM((d, f), BF16),
                            pltpu.VMEM((f, d), BF16),
                            pltpu.VMEM((blk_rows, LANES), jnp.uint32),
                            pltpu.VMEM((blk_rows, LANES), jnp.uint32),
                            pltpu.SemaphoreType.DMA((2,))]),
        out_shape=jax.ShapeDtypeStruct((nb * blk_rows, LANES), jnp.uint32),
        compiler_params=_cparams(("arbitrary",)),
        name="experts",
    )(block_e, n_used, ids_tiled, ids_tiled, w1, w3, w2, hp)


def _inverse_rows(slot_flat, vals, n_slots):
    n_idx = slot_flat.shape[0]
    window = LANES
    mesh = plsc.VectorSubcoreMesh(core_axis_name="core", subcore_axis_name="subcore")

    @functools.partial(pl.kernel, out_type=jax.ShapeDtypeStruct((n_slots, LANES), jnp.int32),
                       mesh=mesh, scratch_types=[])
    def scatter_rows(x_hbm, i_hbm, o_hbm):
        def body(x_vmem, i_vmem):
            pltpu.sync_copy(x_vmem, o_hbm.at[i_vmem.at[0]])

        pltpu.emit_pipeline(
            body,
            grid=(n_idx // window,),
            in_specs=[pl.BlockSpec((window, LANES), lambda i: (i, 0)),
                      pl.BlockSpec((1, window), lambda i: (0, i))],
            out_specs=[],
            core_axis_name="subcore",
            dimension_semantics=(pltpu.PARALLEL,),
        )(x_hbm, i_hbm)

    return scatter_rows(vals, slot_flat.reshape(1, n_idx))


def _combine_kernel(slot_ref, nslot_ref, h_ref, w8_ref, ws1_ref, ws3_ref, ws2_ref, g_ref, b_ref,
                    ys_ref, o_ref, ob_ref, ybuf0_ref, ybuf1_ref, sem, *, t):
    i = pl.program_id(0)
    last = pl.num_programs(0) - 1
    ybuf = (ybuf0_ref, ybuf1_ref)

    def row_copy(s_ref, buf, k, tok):
        src = ys_ref.at[pl.ds(pl.multiple_of(s_ref[k * t + tok], SUBLANES), SUBLANES), :]
        return pltpu.make_async_copy(
            src, ybuf[buf].at[k, pl.ds(pl.multiple_of(tok * SUBLANES, SUBLANES), SUBLANES), :],
            sem.at[buf])

    def wait_tile(buf):
        for k in range(TOP_K):
            pltpu.make_async_copy(ys_ref.at[pl.ds(0, t * SUBLANES), :], ybuf[buf].at[k],
                                  sem.at[buf]).wait()

    @pl.when(i == 0)
    def _():
        def issue(tok, carry):
            for k in range(TOP_K):
                row_copy(slot_ref, 0, k, tok).start(priority=k % 2)
            return carry

        lax.fori_loop(0, t, issue, 0, unroll=2)

    group = t // TOP_K

    def reduce_tile(cur):
        nxt = 1 - cur

        def issue_group(g):
            for tok in range(g * group, (g + 1) * group):
                for k in range(TOP_K):
                    row_copy(nslot_ref, nxt, k, tok).start(priority=k % 2)

        wait_tile(cur)
        h = h_ref[...]
        issue_group(0)
        shared = _swiglu(h.astype(BF16), ws1_ref[...], ws3_ref[...], ws2_ref[...])
        w8 = w8_ref[...]
        wk = [w8[:, k:k + 1] for k in range(TOP_K)]
        acc_lo, acc_hi = [], []
        for c in range(SUBLANES):
            if c + 1 < TOP_K:
                issue_group(c + 1)
            lo_c, hi_c = None, None
            for k in range(TOP_K):
                lo, hi = _unpack_bf16_pairs(_load_tile_chunk(ybuf[cur].at[k], c, t))
                lo_c = lo * wk[k] if k == 0 else lo_c + lo * wk[k]
                hi_c = hi * wk[k] if k == 0 else hi_c + hi * wk[k]
            acc_lo.append(lo_c)
            acc_hi.append(hi_c)
        acc = jnp.concatenate(acc_lo + acc_hi, axis=1)
        y = DN_ALPHA * h + (acc + shared)
        out = _layer_norm(y, g_ref[...], b_ref[...])
        o_ref[...] = out
        ob_ref[...] = out.astype(BF16)

        @pl.when(i == last)
        def _():
            wait_tile(nxt)

    for parity in range(2):
        @pl.when(i % 2 == parity)
        def _():
            reduce_tile(parity)


def _combine(h, ys, slots_tiled, w8_t, ws1, ws3, ws2, g, b, t, out_rows, out_map):
    n_rows, d = h.shape
    f = ws1.shape[1]
    n_tiles = n_rows // t
    return pl.pallas_call(
        functools.partial(_combine_kernel, t=t),
        grid=(n_tiles,),
        in_specs=[pl.BlockSpec((None, None, TOP_K * t), lambda i: (i, 0, 0),
                               memory_space=pltpu.SMEM),
                  pl.BlockSpec((None, None, TOP_K * t),
                               lambda i: (jnp.minimum(i + 1, n_tiles - 1), 0, 0),
                               memory_space=pltpu.SMEM),
                  pl.BlockSpec((t, d), lambda i: (i, 0)),
                  pl.BlockSpec((t, TOP_K), lambda i: (i, 0)),
                  pl.BlockSpec((d, f), lambda i: (0, 0)),
                  pl.BlockSpec((d, f), lambda i: (0, 0)),
                  pl.BlockSpec((f, d), lambda i: (0, 0)),
                  pl.BlockSpec((1, d), lambda i: (0, 0)),
                  pl.BlockSpec((1, d), lambda i: (0, 0)),
                  pl.BlockSpec(memory_space=pl.ANY)],
        out_specs=[pl.BlockSpec((t, d), lambda i: (out_map(i), 0)),
                   pl.BlockSpec((t, d), lambda i: (i, 0))],
        out_shape=[jax.ShapeDtypeStruct((out_rows, d), F32),
                   jax.ShapeDtypeStruct((n_rows, d), BF16)],
        scratch_shapes=[pltpu.VMEM((TOP_K, t * SUBLANES, LANES), jnp.uint32),
                        pltpu.VMEM((TOP_K, t * SUBLANES, LANES), jnp.uint32),
                        pltpu.SemaphoreType.DMA((2,))],
        compiler_params=_cparams(("arbitrary",)),
        name="combine_ln",
    )(slots_tiled, slots_tiled, h, w8_t, ws1, ws3, ws2, g, b, ys)


def _moe_ln(h, hp, logits_t, router_bias, w1, w3, w2, layer, ws1, ws3, ws2, g, b, drop_rows,
            seq_rows):
    n_rows, d = h.shape
    E = N_EXPERTS
    assert d // 2 == SUBLANES * LANES, d
    bias_col = jnp.broadcast_to(router_bias.astype(F32)[:, None], (E, LANES))
    w_dense, pos_dense, cnt = _route(logits_t, bias_col)

    counts = cnt[:, 0].astype(jnp.int32)
    pcounts = (counts + MOE_BLOCK - 1) // MOE_BLOCK * MOE_BLOCK
    pends = jnp.cumsum(pcounts)
    pstarts = pends - pcounts
    n_blocks = -(-(n_rows * TOP_K) // MOE_BLOCK) + E
    n_slots = n_blocks * MOE_BLOCK
    block_lo = jnp.arange(n_blocks, dtype=jnp.int32) * MOE_BLOCK
    block_e = jnp.minimum(jnp.sum(pends[None, :] <= block_lo[:, None], axis=1), E - 1).astype(jnp.int32)
    n_used = (pends[-1:] // MOE_BLOCK).astype(jnp.int32)
    pstart_col = jnp.broadcast_to(pstarts.astype(F32)[:, None], (E, LANES))

    slots, w8 = _compact(w_dense, pos_dense, pstart_col)
    t = _pick_tile(n_rows, TOKEN_TILE, LANES)
    n_tiles = n_rows // t
    slots_tiled = slots.reshape(TOP_K, n_tiles, t).transpose(1, 0, 2).reshape(n_tiles, 1, TOP_K * t)
    tok_rows = jnp.tile(jnp.arange(n_rows, dtype=jnp.int32) * SUBLANES, TOP_K)
    inv = _inverse_rows((slots // SUBLANES).reshape(-1),
                        jnp.broadcast_to(tok_rows[:, None], (n_rows * TOP_K, LANES)), n_slots)
    ids = jnp.clip(inv[:, 0] // SUBLANES, 0, n_rows - 1) * SUBLANES
    ys = _experts(hp, ids.reshape(n_blocks, 1, MOE_BLOCK), w1, w3, w2, layer, block_e, n_used)
    if drop_rows == t:
        per_seq = seq_rows // t
        out_rows = n_rows - (n_rows // seq_rows) * t

        def out_map(i):
            return (i // per_seq) * (per_seq - 1) + jnp.maximum(i % per_seq - 1, 0)
    else:
        out_rows, out_map = n_rows, (lambda i: i)
    return _combine(h, ys, slots_tiled, w8.T, ws1, ws3, ws2, g, b, t, out_rows, out_map)


def kernel(x, meta_tokens, gla_w_in, gla_w_gate_up, gla_b_gate, gla_norm_g, gla_w_out,
           conv_w_in, conv_w, conv_w_out, ln1_g, ln1_b, router_w, router_bias,
           exp_w1, exp_w3, exp_w2, shared_w1, shared_w3, shared_w2, ln2_g, ln2_b):
    batch, seq, d = x.shape
    H = GLA_HEADS
    dk = d // 2 // H
    n_qkvr = 2 * H * dk + 2 * d
    rank = gla_w_in.shape[2] - n_qkvr
    seq_rows = -(-(N_META + seq) // ROW_ALIGN) * ROW_ALIGN
    pad_rows = seq_rows - N_META - seq
    n_rows = batch * seq_rows

    meta = jnp.broadcast_to(meta_tokens.astype(x.dtype)[None], (batch, N_META, d))
    h = jnp.concatenate([jnp.zeros((batch, pad_rows, d), x.dtype), meta, x], axis=1)
    h = h.reshape(n_rows, d)
    hb = h

    def row(v):
        return v.astype(F32)[None, :]

    for i in range(DEPTH):
        jm = i // 2
        if i % 2 == 0:
            w_in = gla_w_in[jm]
            w_all = jnp.pad(w_in, ((0, 0), (0, GATE_PAD - rank))).astype(BF16)
            wg_up = jnp.pad(gla_w_gate_up[jm], ((0, GATE_PAD - rank), (0, 0))).astype(BF16)
            qkvrg = _matmul(hb, w_all, BF16, IN_PROJ_ROWS, IN_PROJ_COLS, "gla_in_proj")
            mix = _gla(qkvrg, wg_up, row(gla_b_gate[jm]), row(gla_norm_g[jm]),
                       batch=batch, seq_rows=seq_rows, d_model=d)
            w_out = gla_w_out[jm].astype(BF16)
        else:
            mix = _conv_mix(hb, conv_w_in[jm].astype(BF16), conv_w[jm].astype(F32),
                            seq_rows=seq_rows, pad_rows=pad_rows)
            w_out = conv_w_out[jm].astype(BF16)
        h, logits_t, hp = _proj_ln_router(mix, w_out, h, row(ln1_g[i]), row(ln1_b[i]),
                                          router_w[i].T.astype(BF16))
        drop = pad_rows + N_META if i == DEPTH - 1 else 0
        h, hb = _moe_ln(h, hp, logits_t, router_bias[i], exp_w1, exp_w3, exp_w2, i,
                        shared_w1[i].astype(BF16), shared_w3[i].astype(BF16),
                        shared_w2[i].astype(BF16), row(ln2_g[i]), row(ln2_b[i]), drop, seq_rows)
    if h.shape[0] == batch * seq:
        return h.reshape(batch, seq, d)
    return h.reshape(batch, seq_rows, d)[:, pad_rows + N_META:]
```

```python
import functools

import jax
import jax.numpy as jnp
from jax import lax
from jax.experimental import pallas as pl
from jax.experimental.pallas import tpu as pltpu
from jax.experimental.pallas import tpu_sc as plsc

N_META = 16
GLA_HEADS = 4
GLA_GATE_TAU = 16.0
CONV_WIDTH = 3
N_EXPERTS = 64
TOP_K = 8
N_GROUPS = 8
TOPK_GROUPS = 4
ROUTED_SCALE = 2.5
LN_EPS = 1e-5
RMS_EPS = 1e-6
DEPTH = 2
DN_ALPHA = (2 * DEPTH) ** 0.25
LOG2_E = 1.4426950408889634

LANES = 128
SUBLANES = 8
ROW_ALIGN = 128
GLA_CHUNK = 64
GLA_SUB = 16
GLA_HEADS_PER_STEP = 4
MXU_WIDTH = 256
GATE_PAD = MXU_WIDTH
MOE_BLOCK = 512
TOKEN_TILE = 128
IN_PROJ_ROWS, IN_PROJ_COLS = 832, 1280
CONV_ROWS, CONV_COLS = 1024, 512
OUT_PROJ_ROWS = 512
GLA_ROWS = 640
ROUTE_LANES = 512
VMEM_LIMIT = 56 * 1024 * 1024

F32 = jnp.float32
BF16 = jnp.bfloat16


def _pick_tile(n, target, mult):
    best = None
    for t in range(mult, min(n, target) + 1, mult):
        if n % t == 0:
            best = t
    assert best is not None, (n, target, mult)
    return best


def _cparams(sem):
    return pltpu.CompilerParams(dimension_semantics=sem, vmem_limit_bytes=VMEM_LIMIT)


def _mm_kernel(x_ref, w_ref, o_ref):
    x = x_ref[...].astype(BF16)
    o_ref[...] = jnp.dot(x, w_ref[...], preferred_element_type=F32).astype(o_ref.dtype)


def _matmul(x, w, out_dtype, tm_target, tn_target, name):
    m, k = x.shape
    n = w.shape[1]
    tm = _pick_tile(m, tm_target, 16)
    tn = _pick_tile(n, tn_target, LANES)
    return pl.pallas_call(
        _mm_kernel,
        grid=(m // tm, n // tn),
        in_specs=[pl.BlockSpec((tm, k), lambda i, j: (i, 0)),
                  pl.BlockSpec((k, tn), lambda i, j: (0, j))],
        out_specs=pl.BlockSpec((tm, tn), lambda i, j: (i, j)),
        out_shape=jax.ShapeDtypeStruct((m, n), out_dtype),
        compiler_params=_cparams(("parallel", "arbitrary")),
        name=name,
    )(x, w)


def _gla_kernel(q_ref, k_ref, v_ref, r_ref, gl_ref, wg_ref, bg_ref, ng_ref, o_ref, st_ref,
                *, n_chunks, dk, dv, heads):
    C, S = GLA_CHUNK, GLA_SUB

    @pl.when(pl.program_id(2) == 0)
    def _():
        st_ref[...] = jnp.zeros_like(st_ref)

    ri = lax.broadcasted_iota(jnp.int32, (C, C), 0)
    ci = lax.broadcasted_iota(jnp.int32, (C, C), 1)
    tri = (ri >= ci).astype(BF16)
    rs = lax.broadcasted_iota(jnp.int32, (S, S), 0)
    cs = lax.broadcasted_iota(jnp.int32, (S, S), 1)
    nt = (((1,), (1,)), ((), ()))
    tn = (((0,), (0,)), ((), ()))
    half = S // 2

    def chunk(c, carry):
        r0 = pl.multiple_of(c * C, C)
        rows_c = pl.ds(r0, C)
        hs = range(heads)
        kcol = [pl.ds(hd * dk, dk) for hd in hs]
        vcol = [pl.ds(hd * dv, dv) for hd in hs]
        gl = gl_ref[rows_c, :]
        q = [q_ref[rows_c, kcol[hd]].astype(F32) * (dk ** -0.5) for hd in hs]
        kk = [k_ref[rows_c, kcol[hd]].astype(F32) for hd in hs]
        vv = [v_ref[rows_c, vcol[hd]] for hd in hs]
        st = [st_ref[hd] for hd in hs]
        z = [jnp.dot(gl, wg_ref[:, kcol[hd]], preferred_element_type=F32) + bg_ref[:, kcol[hd]]
             for hd in hs]
        b = []
        for hd in hs:
            la = ((jnp.minimum(z[hd], 0.0) - jnp.log(1.0 + jnp.exp(-jnp.abs(z[hd]))))
                  * (1.0 / GLA_GATE_TAU))
            h1 = la.astype(BF16)
            e1 = la - h1.astype(F32)
            h2 = e1.astype(BF16)
            h3 = (e1 - h2.astype(F32)).astype(BF16)
            b3 = jnp.dot(tri, jnp.concatenate([h1, h2, h3], axis=1), preferred_element_type=F32)
            b.append((b3[:, :dk] + b3[:, dk:2 * dk] + b3[:, 2 * dk:]) * LOG2_E)

        o_inter, st_new, off = [], [], []
        for hd in hs:
            qe = (q[hd] * jnp.exp2(b[hd])).astype(BF16)
            o_inter.append(lax.dot_general(qe, st[hd].astype(BF16), nt, preferred_element_type=F32))
            bl = b[hd][C - 1:C]
            khat = (kk[hd] * jnp.exp2(bl - b[hd])).astype(BF16)
            upd = lax.dot_general(vv[hd], khat, tn, preferred_element_type=F32)
            st_new.append(st[hd] * jnp.exp2(bl) + upd)
            offs = []
            for i in range(1, C // S):
                lo = i * S
                bref = b[hd][lo - 1:lo]
                qt = (q[hd][lo:lo + S] * jnp.exp2(b[hd][lo:lo + S] - bref)).astype(BF16)
                kt = (kk[hd][:lo] * jnp.exp2(bref - b[hd][:lo])).astype(BF16)
                offs.append(lax.dot_general(qt, kt, nt, preferred_element_type=F32))
            off.append(offs)

        dmats = []
        for hd in hs:
            blocks = []
            for i in range(C // S):
                lo = i * S
                qs = q[hd][lo:lo + S]
                bs = b[hd][lo:lo + S]
                dmat = jnp.zeros((S, S), F32)
                for j in range(S):
                    kj = kk[hd][lo + j:lo + j + 1]
                    bj = b[hd][lo + j:lo + j + 1]
                    if j < half:
                        p = qs * kj * jnp.exp2(bs - bj)
                        a = jnp.sum(p, axis=-1, keepdims=True)
                    else:
                        p = qs[half:] * kj * jnp.exp2(bs[half:] - bj)
                        a = jnp.concatenate([jnp.zeros((half, 1), F32),
                                             jnp.sum(p, axis=-1, keepdims=True)], axis=0)
                    dmat = jnp.where(cs == j, a, dmat)
                blocks.append(jnp.where(rs >= cs, dmat, 0.0).astype(BF16))
            dmats.append(blocks)

        for hd in hs:
            rows = []
            for i in range(C // S):
                lo = i * S
                o_i = jnp.dot(dmats[hd][i], vv[hd][lo:lo + S], preferred_element_type=F32)
                if i > 0:
                    o_i = o_i + jnp.dot(off[hd][i - 1].astype(BF16), vv[hd][:lo],
                                        preferred_element_type=F32)
                rows.append(o_i)
            o = o_inter[hd] + jnp.concatenate(rows, axis=0)
            ms = jnp.mean(o * o, axis=-1, keepdims=True)
            r = r_ref[rows_c, vcol[hd]].astype(F32)
            y = o * lax.rsqrt(ms + RMS_EPS) * ng_ref[:, vcol[hd]] * (r * jax.nn.sigmoid(r))
            st_ref[hd] = st_new[hd]
            o_ref[rows_c, vcol[hd]] = y.astype(o_ref.dtype)
        return carry

    lax.fori_loop(0, n_chunks, chunk, 0)


def _gla(qkvrg, wg, bg, ng, *, batch, seq_rows, d_model):
    H, hps = GLA_HEADS, GLA_HEADS_PER_STEP
    dk = d_model // 2 // H
    dv = d_model // H
    n_rows = batch * seq_rows
    rblk = _pick_tile(seq_rows, GLA_ROWS, GLA_CHUNK)
    steps = seq_rows // rblk
    groups = H // hps
    off_k, off_v, off_r = groups, (2 * H * dk) // (hps * dv), (2 * H * dk) // (hps * dv) + groups

    off_g = (2 * H * dk + 2 * H * dv) // GATE_PAD

    def rowmap(off):
        return lambda b, h, i: (b * steps + i, off + h)

    return pl.pallas_call(
        functools.partial(_gla_kernel, n_chunks=rblk // GLA_CHUNK, dk=dk, dv=dv, heads=hps),
        grid=(batch, groups, steps),
        in_specs=[pl.BlockSpec((rblk, hps * dk), rowmap(0)),
                  pl.BlockSpec((rblk, hps * dk), rowmap(off_k)),
                  pl.BlockSpec((rblk, hps * dv), rowmap(off_v)),
                  pl.BlockSpec((rblk, hps * dv), rowmap(off_r)),
                  pl.BlockSpec((rblk, GATE_PAD), lambda b, h, i: (b * steps + i, off_g)),
                  pl.BlockSpec((GATE_PAD, hps * dk), lambda b, h, i: (0, h)),
                  pl.BlockSpec((1, hps * dk), lambda b, h, i: (0, h)),
                  pl.BlockSpec((1, hps * dv), lambda b, h, i: (0, h))],
        out_specs=pl.BlockSpec((rblk, hps * dv), lambda b, h, i: (b * steps + i, h)),
        out_shape=jax.ShapeDtypeStruct((n_rows, H * dv), BF16),
        scratch_shapes=[pltpu.VMEM((hps, dv, dk), F32)],
        compiler_params=_cparams(("parallel", "parallel", "arbitrary")),
        name="gla_chunks",
    )(qkvrg, qkvrg, qkvrg, qkvrg, qkvrg, wg, bg, ng)


def _conv_kernel(x_ref, wb_ref, wc_ref, wh_ref, cw_ref, o_ref, ubuf_ref, carry_ref,
                 *, seq_rows, pad_rows, tm):
    i = pl.program_id(0)
    j = pl.program_id(1)
    x = x_ref[...]
    bg = jnp.dot(x, wb_ref[...], preferred_element_type=F32)
    cg = jnp.dot(x, wc_ref[...], preferred_element_type=F32)
    hh = jnp.dot(x, wh_ref[...], preferred_element_type=F32)
    row = i * tm + lax.broadcasted_iota(jnp.int32, (tm, 1), 0)
    u = jnp.where(row % seq_rows >= pad_rows, cg * hh, 0.0)

    @pl.when(i == 0)
    def _():
        carry_ref[j] = jnp.zeros(carry_ref.shape[1:], F32)

    ubuf_ref[pl.ds(0, SUBLANES), :] = carry_ref[j]
    ubuf_ref[pl.ds(SUBLANES, tm), :] = u
    carry_ref[j] = u[tm - SUBLANES:]
    cw = cw_ref[...]
    conv = u * cw[CONV_WIDTH - 1:CONV_WIDTH]
    for s in range(1, CONV_WIDTH):
        conv = conv + ubuf_ref[pl.ds(SUBLANES - s, tm), :] * cw[CONV_WIDTH - 1 - s:CONV_WIDTH - s]
    o_ref[...] = (bg * conv).astype(o_ref.dtype)


def _conv_mix(xb, w_in, conv_w, *, seq_rows, pad_rows):
    n_rows, d = xb.shape
    tm = _pick_tile(n_rows, CONV_ROWS, SUBLANES)
    tn = _pick_tile(d, CONV_COLS, LANES)
    nj = d // tn
    return pl.pallas_call(
        functools.partial(_conv_kernel, seq_rows=seq_rows, pad_rows=pad_rows, tm=tm),
        grid=(n_rows // tm, nj),
        in_specs=[pl.BlockSpec((tm, d), lambda i, j: (i, 0)),
                  pl.BlockSpec((d, tn), lambda i, j: (0, j)),
                  pl.BlockSpec((d, tn), lambda i, j: (0, nj + j)),
                  pl.BlockSpec((d, tn), lambda i, j: (0, 2 * nj + j)),
                  pl.BlockSpec((CONV_WIDTH, tn), lambda i, j: (0, j))],
        out_specs=pl.BlockSpec((tm, tn), lambda i, j: (i, j)),
        out_shape=jax.ShapeDtypeStruct((n_rows, d), BF16),
        scratch_shapes=[pltpu.VMEM((tm + SUBLANES, tn), F32),
                        pltpu.VMEM((nj, SUBLANES, tn), F32)],
        compiler_params=_cparams(("arbitrary", "arbitrary")),
        name="conv_mix",
    )(xb, w_in, w_in, w_in, conv_w)


def _layer_norm(y, g, b):
    mu = jnp.mean(y, axis=-1, keepdims=True)
    yc = y - mu
    var = jnp.mean(yc * yc, axis=-1, keepdims=True)
    return yc * lax.rsqrt(var + LN_EPS) * g + b


def _proj_ln_kernel(a_ref, w_ref, h_ref, g_ref, b_ref, rw_ref, o_ref, lg_ref, hp_ref):
    y = jnp.dot(a_ref[...], w_ref[...], preferred_element_type=F32) + DN_ALPHA * h_ref[...]
    h = _layer_norm(y, g_ref[...], b_ref[...])
    o_ref[...] = h
    _store_rows_as_tiles(hp_ref, _pack_bf16_pairs(h), h.shape[0])
    lg_ref[...] = lax.dot_general(rw_ref[...], h.astype(BF16), (((1,), (1,)), ((), ())),
                                  preferred_element_type=F32)


def _proj_ln_router(a, w, h, g, b, rw_t):
    n_rows, kin = a.shape
    d = w.shape[1]
    e = rw_t.shape[0]
    tm = _pick_tile(n_rows, OUT_PROJ_ROWS, LANES)
    return pl.pallas_call(
        _proj_ln_kernel,
        grid=(n_rows // tm,),
        in_specs=[pl.BlockSpec((tm, kin), lambda i: (i, 0)),
                  pl.BlockSpec((kin, d), lambda i: (0, 0)),
                  pl.BlockSpec((tm, d), lambda i: (i, 0)),
                  pl.BlockSpec((1, d), lambda i: (0, 0)),
                  pl.BlockSpec((1, d), lambda i: (0, 0)),
                  pl.BlockSpec((e, d), lambda i: (0, 0))],
        out_specs=[pl.BlockSpec((tm, d), lambda i: (i, 0)),
                   pl.BlockSpec((e, tm), lambda i: (0, i)),
                   pl.BlockSpec((tm * SUBLANES, LANES), lambda i: (i, 0))],
        out_shape=[jax.ShapeDtypeStruct((n_rows, d), F32),
                   jax.ShapeDtypeStruct((e, n_rows), F32),
                   jax.ShapeDtypeStruct((n_rows * SUBLANES, LANES), jnp.uint32)],
        compiler_params=_cparams(("parallel",)),
        name="proj_ln_router",
    )(a, w, h, g, b, rw_t)


def _beats(cand, cand_idx, ref, ref_idx):
    return (cand > ref) | ((cand == ref) & (cand_idx < ref_idx))


def _route_kernel(lg_ref, bias_ref, w_ref, pos_ref, cnt_ref, run_ref, *, t):
    E, G = N_EXPERTS, N_GROUPS
    gs = E // G

    @pl.when(pl.program_id(0) == 0)
    def _():
        run_ref[...] = jnp.zeros_like(run_ref)

    s = jax.nn.sigmoid(lg_ref[...])
    c = s + bias_ref[...][:, :1]
    sub = lax.broadcasted_iota(jnp.int32, (gs, t), 0)

    grp_rows = []
    for g in range(G):
        cg = c[g * gs:(g + 1) * gs]
        rank = jnp.zeros((gs, t), jnp.int32)
        for m in range(gs):
            rank = rank + _beats(cg[m:m + 1], m, cg, sub).astype(jnp.int32)
        grp_rows.append(jnp.sum(jnp.where(rank < 2, cg, 0.0), axis=0, keepdims=True))
    gidx = lax.broadcasted_iota(jnp.int32, (G, t), 0)
    gscore = jnp.zeros((G, t), F32)
    for g in range(G):
        gscore = jnp.where(gidx == g, grp_rows[g], gscore)
    grank = jnp.zeros((G, t), jnp.int32)
    for m in range(G):
        grank = grank + _beats(gscore[m:m + 1], m, gscore, gidx).astype(jnp.int32)
    gsel = grank < TOPK_GROUPS

    masked = jnp.concatenate(
        [jnp.where(gsel[g:g + 1], c[g * gs:(g + 1) * gs], -jnp.inf) for g in range(G)], axis=0)
    eidx = lax.broadcasted_iota(jnp.int32, (E, t), 0).astype(F32)
    sel = jnp.zeros((E, t), jnp.bool_)
    for _ in range(TOP_K):
        best = jnp.max(masked, axis=0, keepdims=True)
        pick = jnp.min(jnp.where((masked == best) & jnp.logical_not(sel), eidx, float(E)),
                       axis=0, keepdims=True)
        hit = eidx == pick
        sel = sel | hit
        masked = jnp.where(hit, -jnp.inf, masked)
    gate = jnp.where(sel, s, 0.0)
    w_ref[...] = gate / jnp.sum(gate, axis=0, keepdims=True) * ROUTED_SCALE

    li = lax.broadcasted_iota(jnp.int32, (t, t), 0)
    lj = lax.broadcasted_iota(jnp.int32, (t, t), 1)
    upper = (li <= lj).astype(BF16)
    self_ = sel.astype(F32)
    incl = jnp.dot(sel.astype(BF16), upper, preferred_element_type=F32)
    run = run_ref[...][:, :1]
    pos_ref[...] = jnp.where(sel, run + incl - self_, -1.0)
    run_new = run + jnp.sum(self_, axis=1, keepdims=True)
    run_ref[...] = jnp.broadcast_to(run_new, run_ref.shape)
    cnt_ref[...] = jnp.broadcast_to(run_new, cnt_ref.shape)


def _route(logits_t, bias_col):
    e, n_rows = logits_t.shape
    t = _pick_tile(n_rows, ROUTE_LANES, LANES)
    return pl.pallas_call(
        functools.partial(_route_kernel, t=t),
        grid=(n_rows // t,),
        in_specs=[pl.BlockSpec((e, t), lambda i: (0, i)),
                  pl.BlockSpec((e, LANES), lambda i: (0, 0))],
        out_specs=[pl.BlockSpec((e, t), lambda i: (0, i)),
                   pl.BlockSpec((e, t), lambda i: (0, i)),
                   pl.BlockSpec((e, LANES), lambda i: (0, 0))],
        out_shape=[jax.ShapeDtypeStruct((e, n_rows), F32),
                   jax.ShapeDtypeStruct((e, n_rows), F32),
                   jax.ShapeDtypeStruct((e, LANES), F32)],
        scratch_shapes=[pltpu.VMEM((e, LANES), F32)],
        compiler_params=_cparams(("arbitrary",)),
        name="route",
    )(logits_t, bias_col)


def _compact_kernel(w_ref, pos_ref, pst_ref, slot_ref, w8_ref, *, t):
    E = N_EXPERTS
    pos = pos_ref[...]
    sel = pos >= 0.0
    ri = lax.broadcasted_iota(jnp.int32, (E, E), 0)
    ci = lax.broadcasted_iota(jnp.int32, (E, E), 1)
    below = (ci < ri).astype(BF16)
    order = jnp.dot(below, sel.astype(BF16), preferred_element_type=F32)
    slot = pst_ref[...][:, :1] + pos
    wd = w_ref[...]
    kidx = lax.broadcasted_iota(jnp.int32, (TOP_K, t), 0)
    slots = jnp.zeros((TOP_K, t), F32)
    w8 = jnp.zeros((TOP_K, t), F32)
    for k in range(TOP_K):
        m = sel & (order == float(k))
        slots = jnp.where(kidx == k, jnp.sum(jnp.where(m, slot, 0.0), axis=0, keepdims=True), slots)
        w8 = jnp.where(kidx == k, jnp.sum(jnp.where(m, wd, 0.0), axis=0, keepdims=True), w8)
    slot_ref[...] = (slots * float(SUBLANES)).astype(jnp.int32)
    w8_ref[...] = w8


def _compact(w_dense, pos_dense, pstart_col):
    e, n_rows = w_dense.shape
    t = _pick_tile(n_rows, ROUTE_LANES, LANES)
    return pl.pallas_call(
        functools.partial(_compact_kernel, t=t),
        grid=(n_rows // t,),
        in_specs=[pl.BlockSpec((e, t), lambda i: (0, i)),
                  pl.BlockSpec((e, t), lambda i: (0, i)),
                  pl.BlockSpec((e, LANES), lambda i: (0, 0))],
        out_specs=[pl.BlockSpec((TOP_K, t), lambda i: (0, i)),
                   pl.BlockSpec((TOP_K, t), lambda i: (0, i))],
        out_shape=[jax.ShapeDtypeStruct((TOP_K, n_rows), jnp.int32),
                   jax.ShapeDtypeStruct((TOP_K, n_rows), F32)],
        compiler_params=_cparams(("parallel",)),
        name="compact",
    )(w_dense, pos_dense, pstart_col)


def _pack_bf16_pairs(x):
    half = x.shape[1] // 2
    lo = lax.bitcast_convert_type(x[:, :half].astype(BF16).astype(F32), jnp.uint32)
    hi = lax.bitcast_convert_type(x[:, half:].astype(BF16).astype(F32), jnp.uint32)
    return (lo >> 16) | hi


def _unpack_bf16_pairs(w):
    lo = lax.bitcast_convert_type(w << 16, F32)
    hi = lax.bitcast_convert_type(w & jnp.uint32(0xFFFF0000), F32)
    return lo, hi


def _store_rows_as_tiles(ref_at, x, rows):
    for c in range(SUBLANES):
        ref_at[pl.ds(c, rows, stride=SUBLANES), :] = x[:, c * LANES:(c + 1) * LANES]


def _load_tile_chunk(ref_at, c, rows):
    return ref_at[pl.ds(c, rows, stride=SUBLANES), :]


def _swiglu(x, w1, w3, w2):
    a = jnp.dot(x, w1, preferred_element_type=F32)
    g = jnp.dot(x, w3, preferred_element_type=F32)
    hmid = (a * jax.nn.sigmoid(a) * g).astype(BF16)
    return jnp.dot(hmid, w2, preferred_element_type=F32)


def _expert_kernel(be_ref, nu_ref, cur_ref, nxt_ref, prv_ref, w1_ref, w3_ref, w2_ref, hp_ref, yt_ref,
                   w1b_ref, w3b_ref, w2b_ref, xbuf0_ref, xbuf1_ref, ybuf0_ref, ybuf1_ref, gsem, ssem):
    b = pl.program_id(0)
    n_used = nu_ref[0]
    rows = MOE_BLOCK
    xbuf = (xbuf0_ref, xbuf1_ref)
    ybuf = (ybuf0_ref, ybuf1_ref)

    def tile(row):
        return pl.ds(pl.multiple_of(row, SUBLANES), SUBLANES)

    def gather(s_ref, buf, r):
        return pltpu.make_async_copy(hp_ref.at[tile(s_ref[r]), :], xbuf[buf].at[tile(r * SUBLANES), :],
                                     gsem.at[buf])

    def scatter(s_ref, buf, r):
        return pltpu.make_async_copy(ybuf[buf].at[tile(r * SUBLANES), :],
                                     yt_ref.at[tile(s_ref[rows + r]), :], ssem.at[buf])

    def wait_gather(buf):
        pltpu.make_async_copy(hp_ref.at[pl.ds(0, rows * SUBLANES), :], xbuf[buf], gsem.at[buf]).wait()

    def wait_scatter(buf):
        pltpu.make_async_copy(ybuf[buf], yt_ref.at[pl.ds(0, rows * SUBLANES), :], ssem.at[buf]).wait()

    @pl.when(b == 0)
    def _():
        def issue(r, carry):
            gather(cur_ref, 0, r).start(priority=1)
            return carry

        lax.fori_loop(0, rows, issue, 0, unroll=2)

    def run_block(cur):
        nxt = 1 - cur
        wait_gather(cur)

        @pl.when(b >= 1)
        def _():
            wait_scatter(cur)

        @pl.when((b == 0) | (be_ref[b] != be_ref[jnp.maximum(b - 1, 0)]))
        def _():
            w1b_ref[...] = w1_ref[...].astype(BF16)
            w3b_ref[...] = w3_ref[...].astype(BF16)
            w2b_ref[...] = w2_ref[...].astype(BF16)

        group = rows // 4

        def issue_group(g):
            for r in range(g * group, (g + 1) * group):
                gather(nxt_ref, nxt, r).start(priority=1)
                scatter(prv_ref, nxt, r).start(priority=0)

        issue_group(0)
        halves = [_unpack_bf16_pairs(_load_tile_chunk(xbuf[cur], c, rows)) for c in range(SUBLANES)]
        x = jnp.concatenate([lo.astype(BF16) for lo, _ in halves]
                            + [hi.astype(BF16) for _, hi in halves], axis=1)
        issue_group(1)
        a = jnp.dot(x, w1b_ref[...], preferred_element_type=F32)
        issue_group(2)
        g = jnp.dot(x, w3b_ref[...], preferred_element_type=F32)
        issue_group(3)
        hmid = (a * jax.nn.sigmoid(a) * g).astype(BF16)
        y = _pack_bf16_pairs(jnp.dot(hmid, w2b_ref[...], preferred_element_type=F32))
        _store_rows_as_tiles(ybuf[cur], y, rows)

        @pl.when(b == n_used - 1)
        def _():
            def issue(r, carry):
                scatter(cur_ref, cur, r).start(priority=0)
                return carry

            lax.fori_loop(0, rows, issue, 0, unroll=2)
            wait_gather(nxt)
            wait_scatter(nxt)
            wait_scatter(cur)

    @pl.when(b < n_used)
    def _():
        for parity in range(2):
            @pl.when(b % 2 == parity)
            def _():
                run_block(parity)


def _experts(hp, ids, w1, w3, w2, layer, block_e, n_used, out_rows):
    d, f = w1.shape[2], w1.shape[3]
    blk_rows = MOE_BLOCK * SUBLANES
    nb = ids.shape[0] - 1

    def blk(b, be, nu):
        return jnp.minimum(b, nu[0] - 1)

    def wmap(b, be, nu):
        return (layer, be[blk(b, be, nu)], 0, 0)

    def prev(b, be, nu):
        return jnp.where(b == 0, nb, jnp.maximum(blk(b, be, nu) - 1, 0))

    def ispec(which):
        return pl.BlockSpec((None, None, 2 * MOE_BLOCK), lambda b, be, nu: (which(b, be, nu), 0, 0),
                            memory_space=pltpu.SMEM)

    return pl.pallas_call(
        _expert_kernel,
        grid_spec=pltpu.PrefetchScalarGridSpec(
            num_scalar_prefetch=2,
            grid=(nb,),
            in_specs=[ispec(blk), ispec(lambda b, be, nu: blk(b + 1, be, nu)), ispec(prev),
                      pl.BlockSpec((None, None, d, f), wmap),
                      pl.BlockSpec((None, None, d, f), wmap),
                      pl.BlockSpec((None, None, f, d), wmap),
                      pl.BlockSpec(memory_space=pl.ANY)],
            out_specs=pl.BlockSpec(memory_space=pl.ANY),
            scratch_shapes=[pltpu.VMEM((d, f), BF16), pltpu.VMEM((d, f), BF16),
                            pltpu.VMEM((f, d), BF16)]
            + [pltpu.VMEM((blk_rows, LANES), jnp.uint32)] * 4
            + [pltpu.SemaphoreType.DMA((2,)), pltpu.SemaphoreType.DMA((2,))]),
        out_shape=jax.ShapeDtypeStruct((out_rows, LANES), jnp.uint32),
        compiler_params=_cparams(("arbitrary",)),
        name="experts",
    )(block_e, n_used, ids, ids, ids, w1, w3, w2, hp)


def _inverse_rows(slot_flat, vals, n_slots):
    n_idx = slot_flat.shape[0]
    window = LANES
    mesh = plsc.VectorSubcoreMesh(core_axis_name="core", subcore_axis_name="subcore")

    @functools.partial(pl.kernel, out_type=jax.ShapeDtypeStruct((n_slots, LANES), jnp.int32),
                       mesh=mesh, scratch_types=[])
    def scatter_rows(x_hbm, i_hbm, o_hbm):
        def body(x_vmem, i_vmem):
            pltpu.sync_copy(x_vmem, o_hbm.at[i_vmem.at[0]])

        pltpu.emit_pipeline(
            body,
            grid=(n_idx // window,),
            in_specs=[pl.BlockSpec((window, LANES), lambda i: (i, 0)),
                      pl.BlockSpec((1, window), lambda i: (0, i))],
            out_specs=[],
            core_axis_name="subcore",
            dimension_semantics=(pltpu.PARALLEL,),
        )(x_hbm, i_hbm)

    return scatter_rows(vals, slot_flat.reshape(1, n_idx))


def _combine_kernel(h_ref, w8_ref, ws1_ref, ws3_ref, ws2_ref, g_ref, b_ref, yt_ref, o_ref, ob_ref,
                    *, t):
    h = h_ref[...]
    shared = _swiglu(h.astype(BF16), ws1_ref[...], ws3_ref[...], ws2_ref[...])
    w8 = w8_ref[...]
    wk = [w8[:, k:k + 1] for k in range(TOP_K)]
    acc_lo, acc_hi = [], []
    for c in range(SUBLANES):
        lo_c, hi_c = None, None
        for k in range(TOP_K):
            words = yt_ref[pl.ds(k * SUBLANES + c, t, stride=TOP_K * SUBLANES), :]
            lo, hi = _unpack_bf16_pairs(words)
            lo_c = lo * wk[k] if k == 0 else lo_c + lo * wk[k]
            hi_c = hi * wk[k] if k == 0 else hi_c + hi * wk[k]
        acc_lo.append(lo_c)
        acc_hi.append(hi_c)
    acc = jnp.concatenate(acc_lo + acc_hi, axis=1)
    y = DN_ALPHA * h + (acc + shared)
    out = _layer_norm(y, g_ref[...], b_ref[...])
    o_ref[...] = out
    ob_ref[...] = out.astype(BF16)


def _combine(h, yt, w8_t, ws1, ws3, ws2, g, b, t, out_rows, out_map):
    n_rows, d = h.shape
    f = ws1.shape[1]
    return pl.pallas_call(
        functools.partial(_combine_kernel, t=t),
        grid=(n_rows // t,),
        in_specs=[pl.BlockSpec((t, d), lambda i: (i, 0)),
                  pl.BlockSpec((t, TOP_K), lambda i: (i, 0)),
                  pl.BlockSpec((d, f), lambda i: (0, 0)),
                  pl.BlockSpec((d, f), lambda i: (0, 0)),
                  pl.BlockSpec((f, d), lambda i: (0, 0)),
                  pl.BlockSpec((1, d), lambda i: (0, 0)),
                  pl.BlockSpec((1, d), lambda i: (0, 0)),
                  pl.BlockSpec((t * TOP_K * SUBLANES, LANES), lambda i: (i, 0))],
        out_specs=[pl.BlockSpec((t, d), lambda i: (out_map(i), 0)),
                   pl.BlockSpec((t, d), lambda i: (i, 0))],
        out_shape=[jax.ShapeDtypeStruct((out_rows, d), F32),
                   jax.ShapeDtypeStruct((n_rows, d), BF16)],
        compiler_params=_cparams(("parallel",)),
        name="combine_ln",
    )(h, w8_t, ws1, ws3, ws2, g, b, yt)


def _moe_ln(h, hp, logits_t, router_bias, w1, w3, w2, layer, ws1, ws3, ws2, g, b, drop_rows,
            seq_rows):
    n_rows, d = h.shape
    E = N_EXPERTS
    assert d // 2 == SUBLANES * LANES, d
    bias_col = jnp.broadcast_to(router_bias.astype(F32)[:, None], (E, LANES))
    w_dense, pos_dense, cnt = _route(logits_t, bias_col)

    counts = cnt[:, 0].astype(jnp.int32)
    pcounts = (counts + MOE_BLOCK - 1) // MOE_BLOCK * MOE_BLOCK
    pends = jnp.cumsum(pcounts)
    pstarts = pends - pcounts
    n_blocks = -(-(n_rows * TOP_K) // MOE_BLOCK) + E
    n_slots = n_blocks * MOE_BLOCK
    block_lo = jnp.arange(n_blocks, dtype=jnp.int32) * MOE_BLOCK
    block_e = jnp.minimum(jnp.sum(pends[None, :] <= block_lo[:, None], axis=1), E - 1).astype(jnp.int32)
    n_used = (pends[-1:] // MOE_BLOCK).astype(jnp.int32)
    pstart_col = jnp.broadcast_to(pstarts.astype(F32)[:, None], (E, LANES))

    slots, w8 = _compact(w_dense, pos_dense, pstart_col)
    t = _pick_tile(n_rows, TOKEN_TILE, LANES)

    n_assign = n_rows * TOP_K
    flat = jnp.arange(n_assign, dtype=jnp.int32)
    vals = (flat % n_rows) * TOP_K + flat // n_rows
    inv = _inverse_rows((slots // SUBLANES).reshape(-1),
                        jnp.broadcast_to(vals[:, None], (n_assign, LANES)), n_slots)[:, 0]
    slot_id = jnp.arange(n_slots, dtype=jnp.int32)
    e_of = block_e[slot_id // MOE_BLOCK]
    valid = slot_id - pstarts[e_of] < counts[e_of]
    inv = jnp.clip(jnp.where(valid, inv, 0), 0, n_assign - 1)
    spare = n_assign * SUBLANES
    src_rows = (inv // TOP_K) * SUBLANES
    dst_rows = jnp.where(valid, inv * SUBLANES, spare)
    ids = jnp.concatenate([src_rows.reshape(n_blocks, 1, MOE_BLOCK),
                           dst_rows.reshape(n_blocks, 1, MOE_BLOCK)], axis=2)
    first_prev = jnp.concatenate([jnp.zeros((1, 1, MOE_BLOCK), jnp.int32),
                                  jnp.full((1, 1, MOE_BLOCK), spare, jnp.int32)], axis=2)
    yt = _experts(hp, jnp.concatenate([ids, first_prev], axis=0), w1, w3, w2, layer, block_e,
                  n_used, spare + SUBLANES)
    if drop_rows == t:
        per_seq = seq_rows // t
        out_rows = n_rows - (n_rows // seq_rows) * t

        def out_map(i):
            return (i // per_seq) * (per_seq - 1) + jnp.maximum(i % per_seq - 1, 0)
    else:
        out_rows, out_map = n_rows, (lambda i: i)
    return _combine(h, yt, w8.T, ws1, ws3, ws2, g, b, t, out_rows, out_map)


def kernel(x, meta_tokens, gla_w_in, gla_w_gate_up, gla_b_gate, gla_norm_g, gla_w_out,
           conv_w_in, conv_w, conv_w_out, ln1_g, ln1_b, router_w, router_bias,
           exp_w1, exp_w3, exp_w2, shared_w1, shared_w3, shared_w2, ln2_g, ln2_b):
    batch, seq, d = x.shape
    H = GLA_HEADS
    dk = d // 2 // H
    n_qkvr = 2 * H * dk + 2 * d
    rank = gla_w_in.shape[2] - n_qkvr
    seq_rows = -(-(N_META + seq) // ROW_ALIGN) * ROW_ALIGN
    pad_rows = seq_rows - N_META - seq
    n_rows = batch * seq_rows

    meta = jnp.broadcast_to(meta_tokens.astype(x.dtype)[None], (batch, N_META, d))
    h = jnp.concatenate([jnp.zeros((batch, pad_rows, d), x.dtype), meta, x], axis=1)
    h = h.reshape(n_rows, d)
    hb = h

    def row(v):
        return v.astype(F32)[None, :]

    for i in range(DEPTH):
        jm = i // 2
        if i % 2 == 0:
            w_in = gla_w_in[jm]
            w_all = jnp.pad(w_in, ((0, 0), (0, GATE_PAD - rank))).astype(BF16)
            wg_up = jnp.pad(gla_w_gate_up[jm], ((0, GATE_PAD - rank), (0, 0))).astype(BF16)
            qkvrg = _matmul(hb, w_all, BF16, IN_PROJ_ROWS, IN_PROJ_COLS, "gla_in_proj")
            mix = _gla(qkvrg, wg_up, row(gla_b_gate[jm]), row(gla_norm_g[jm]),
                       batch=batch, seq_rows=seq_rows, d_model=d)
            w_out = gla_w_out[jm].astype(BF16)
        else:
            mix = _conv_mix(hb, conv_w_in[jm].astype(BF16), conv_w[jm].astype(F32),
                            seq_rows=seq_rows, pad_rows=pad_rows)
            w_out = conv_w_out[jm].astype(BF16)
        h, logits_t, hp = _proj_ln_router(mix, w_out, h, row(ln1_g[i]), row(ln1_b[i]),
                                          router_w[i].T.astype(BF16))
        drop = pad_rows + N_META if i == DEPTH - 1 else 0
        h, hb = _moe_ln(h, hp, logits_t, router_bias[i], exp_w1, exp_w3, exp_w2, i,
                        shared_w1[i].astype(BF16), shared_w3[i].astype(BF16),
                        shared_w2[i].astype(BF16), row(ln2_g[i]), row(ln2_b[i]), drop, seq_rows)
    if h.shape[0] == batch * seq:
        return h.reshape(batch, seq, d)
    return h.reshape(batch, seq_rows, d)[:, pad_rows + N_META:]
```

```python
import functools

import jax
import jax.numpy as jnp
from jax import lax
from jax.experimental import pallas as pl
from jax.experimental.pallas import tpu as pltpu
from jax.experimental.pallas import tpu_sc as plsc

N_META = 16
GLA_HEADS = 4
GLA_GATE_TAU = 16.0
CONV_WIDTH = 3
N_EXPERTS = 64
TOP_K = 8
N_GROUPS = 8
TOPK_GROUPS = 4
ROUTED_SCALE = 2.5
LN_EPS = 1e-5
RMS_EPS = 1e-6
DEPTH = 2
DN_ALPHA = (2 * DEPTH) ** 0.25
LOG2_E = 1.4426950408889634

LANES = 128
SUBLANES = 8
ROW_ALIGN = 128
GLA_CHUNK = 64
GLA_SUB = 16
GLA_HEADS_PER_STEP = 4
MXU_WIDTH = 256
GATE_PAD = MXU_WIDTH
MOE_BLOCK = 512
TOKEN_TILE = 128
IN_PROJ_ROWS, IN_PROJ_COLS = 832, 1280
CONV_ROWS, CONV_COLS = 1024, 512
OUT_PROJ_ROWS = 512
GLA_ROWS = 640
ROUTE_LANES = 512
VMEM_LIMIT = 56 * 1024 * 1024

F32 = jnp.float32
BF16 = jnp.bfloat16


def _pick_tile(n, target, mult):
    best = None
    for t in range(mult, min(n, target) + 1, mult):
        if n % t == 0:
            best = t
    assert best is not None, (n, target, mult)
    return best


def _cparams(sem):
    return pltpu.CompilerParams(dimension_semantics=sem, vmem_limit_bytes=VMEM_LIMIT)


def _mm_kernel(x_ref, w_ref, o_ref):
    x = x_ref[...].astype(BF16)
    o_ref[...] = jnp.dot(x, w_ref[...], preferred_element_type=F32).astype(o_ref.dtype)


def _matmul(x, w, out_dtype, tm_target, tn_target, name):
    m, k = x.shape
    n = w.shape[1]
    tm = _pick_tile(m, tm_target, 16)
    tn = _pick_tile(n, tn_target, LANES)
    return pl.pallas_call(
        _mm_kernel,
        grid=(m // tm, n // tn),
        in_specs=[pl.BlockSpec((tm, k), lambda i, j: (i, 0)),
                  pl.BlockSpec((k, tn), lambda i, j: (0, j))],
        out_specs=pl.BlockSpec((tm, tn), lambda i, j: (i, j)),
        out_shape=jax.ShapeDtypeStruct((m, n), out_dtype),
        compiler_params=_cparams(("parallel", "arbitrary")),
        name=name,
    )(x, w)


def _gla_kernel(q_ref, k_ref, v_ref, r_ref, gl_ref, wg_ref, bg_ref, ng_ref, o_ref, st_ref,
                *, n_chunks, dk, dv, heads):
    C, S = GLA_CHUNK, GLA_SUB

    @pl.when(pl.program_id(2) == 0)
    def _():
        st_ref[...] = jnp.zeros_like(st_ref)

    ri = lax.broadcasted_iota(jnp.int32, (C, C), 0)
    ci = lax.broadcasted_iota(jnp.int32, (C, C), 1)
    tri = (ri >= ci).astype(BF16)
    rs = lax.broadcasted_iota(jnp.int32, (S, S), 0)
    cs = lax.broadcasted_iota(jnp.int32, (S, S), 1)
    nt = (((1,), (1,)), ((), ()))
    tn = (((0,), (0,)), ((), ()))
    half = S // 2

    def chunk(c, carry):
        r0 = pl.multiple_of(c * C, C)
        rows_c = pl.ds(r0, C)
        hs = range(heads)
        kcol = [pl.ds(hd * dk, dk) for hd in hs]
        vcol = [pl.ds(hd * dv, dv) for hd in hs]
        gl = gl_ref[rows_c, :]
        q = [q_ref[rows_c, kcol[hd]].astype(F32) * (dk ** -0.5) for hd in hs]
        kk = [k_ref[rows_c, kcol[hd]].astype(F32) for hd in hs]
        vv = [v_ref[rows_c, vcol[hd]] for hd in hs]
        st = [st_ref[hd] for hd in hs]
        z = [jnp.dot(gl, wg_ref[:, kcol[hd]], preferred_element_type=F32) + bg_ref[:, kcol[hd]]
             for hd in hs]
        b = []
        for hd in hs:
            la = ((jnp.minimum(z[hd], 0.0) - jnp.log(1.0 + jnp.exp(-jnp.abs(z[hd]))))
                  * (1.0 / GLA_GATE_TAU))
            h1 = la.astype(BF16)
            e1 = la - h1.astype(F32)
            h2 = e1.astype(BF16)
            h3 = (e1 - h2.astype(F32)).astype(BF16)
            b3 = jnp.dot(tri, jnp.concatenate([h1, h2, h3], axis=1), preferred_element_type=F32)
            b.append((b3[:, :dk] + b3[:, dk:2 * dk] + b3[:, 2 * dk:]) * LOG2_E)

        o_inter, st_new, off = [], [], []
        for hd in hs:
            qe = (q[hd] * jnp.exp2(b[hd])).astype(BF16)
            o_inter.append(lax.dot_general(qe, st[hd].astype(BF16), nt, preferred_element_type=F32))
            bl = b[hd][C - 1:C]
            khat = (kk[hd] * jnp.exp2(bl - b[hd])).astype(BF16)
            upd = lax.dot_general(vv[hd], khat, tn, preferred_element_type=F32)
            st_new.append(st[hd] * jnp.exp2(bl) + upd)
            offs = []
            for i in range(1, C // S):
                lo = i * S
                bref = b[hd][lo - 1:lo]
                qt = (q[hd][lo:lo + S] * jnp.exp2(b[hd][lo:lo + S] - bref)).astype(BF16)
                kt = (kk[hd][:lo] * jnp.exp2(bref - b[hd][:lo])).astype(BF16)
                offs.append(lax.dot_general(qt, kt, nt, preferred_element_type=F32))
            off.append(offs)

        dmats = []
        for hd in hs:
            blocks = []
            for i in range(C // S):
                lo = i * S
                qs = q[hd][lo:lo + S]
                bs = b[hd][lo:lo + S]
                dmat = jnp.zeros((S, S), F32)
                for j in range(S):
                    kj = kk[hd][lo + j:lo + j + 1]
                    bj = b[hd][lo + j:lo + j + 1]
                    if j < half:
                        p = qs * kj * jnp.exp2(bs - bj)
                        a = jnp.sum(p, axis=-1, keepdims=True)
                    else:
                        p = qs[half:] * kj * jnp.exp2(bs[half:] - bj)
                        a = jnp.concatenate([jnp.zeros((half, 1), F32),
                                             jnp.sum(p, axis=-1, keepdims=True)], axis=0)
                    dmat = jnp.where(cs == j, a, dmat)
                blocks.append(jnp.where(rs >= cs, dmat, 0.0).astype(BF16))
            dmats.append(blocks)

        for hd in hs:
            rows = []
            for i in range(C // S):
                lo = i * S
                o_i = jnp.dot(dmats[hd][i], vv[hd][lo:lo + S], preferred_element_type=F32)
                if i > 0:
                    o_i = o_i + jnp.dot(off[hd][i - 1].astype(BF16), vv[hd][:lo],
                                        preferred_element_type=F32)
                rows.append(o_i)
            o = o_inter[hd] + jnp.concatenate(rows, axis=0)
            ms = jnp.mean(o * o, axis=-1, keepdims=True)
            r = r_ref[rows_c, vcol[hd]].astype(F32)
            y = o * lax.rsqrt(ms + RMS_EPS) * ng_ref[:, vcol[hd]] * (r * jax.nn.sigmoid(r))
            st_ref[hd] = st_new[hd]
            o_ref[rows_c, vcol[hd]] = y.astype(o_ref.dtype)
        return carry

    lax.fori_loop(0, n_chunks, chunk, 0)


def _gla(qkvrg, wg, bg, ng, *, batch, seq_rows, d_model):
    H, hps = GLA_HEADS, GLA_HEADS_PER_STEP
    dk = d_model // 2 // H
    dv = d_model // H
    n_rows = batch * seq_rows
    rblk = _pick_tile(seq_rows, GLA_ROWS, GLA_CHUNK)
    steps = seq_rows // rblk
    groups = H // hps
    off_k, off_v, off_r = groups, (2 * H * dk) // (hps * dv), (2 * H * dk) // (hps * dv) + groups

    off_g = (2 * H * dk + 2 * H * dv) // GATE_PAD

    def rowmap(off):
        return lambda b, h, i: (b * steps + i, off + h)

    return pl.pallas_call(
        functools.partial(_gla_kernel, n_chunks=rblk // GLA_CHUNK, dk=dk, dv=dv, heads=hps),
        grid=(batch, groups, steps),
        in_specs=[pl.BlockSpec((rblk, hps * dk), rowmap(0)),
                  pl.BlockSpec((rblk, hps * dk), rowmap(off_k)),
                  pl.BlockSpec((rblk, hps * dv), rowmap(off_v)),
                  pl.BlockSpec((rblk, hps * dv), rowmap(off_r)),
                  pl.BlockSpec((rblk, GATE_PAD), lambda b, h, i: (b * steps + i, off_g)),
                  pl.BlockSpec((GATE_PAD, hps * dk), lambda b, h, i: (0, h)),
                  pl.BlockSpec((1, hps * dk), lambda b, h, i: (0, h)),
                  pl.BlockSpec((1, hps * dv), lambda b, h, i: (0, h))],
        out_specs=pl.BlockSpec((rblk, hps * dv), lambda b, h, i: (b * steps + i, h)),
        out_shape=jax.ShapeDtypeStruct((n_rows, H * dv), BF16),
        scratch_shapes=[pltpu.VMEM((hps, dv, dk), F32)],
        compiler_params=_cparams(("parallel", "parallel", "arbitrary")),
        name="gla_chunks",
    )(qkvrg, qkvrg, qkvrg, qkvrg, qkvrg, wg, bg, ng)


def _conv_kernel(x_ref, wb_ref, wc_ref, wh_ref, cw_ref, o_ref, ubuf_ref, carry_ref,
                 *, seq_rows, pad_rows, tm):
    i = pl.program_id(0)
    j = pl.program_id(1)
    x = x_ref[...]
    bg = jnp.dot(x, wb_ref[...], preferred_element_type=F32)
    cg = jnp.dot(x, wc_ref[...], preferred_element_type=F32)
    hh = jnp.dot(x, wh_ref[...], preferred_element_type=F32)
    row = i * tm + lax.broadcasted_iota(jnp.int32, (tm, 1), 0)
    u = jnp.where(row % seq_rows >= pad_rows, cg * hh, 0.0)

    @pl.when(i == 0)
    def _():
        carry_ref[j] = jnp.zeros(carry_ref.shape[1:], F32)

    ubuf_ref[pl.ds(0, SUBLANES), :] = carry_ref[j]
    ubuf_ref[pl.ds(SUBLANES, tm), :] = u
    carry_ref[j] = u[tm - SUBLANES:]
    cw = cw_ref[...]
    conv = u * cw[CONV_WIDTH - 1:CONV_WIDTH]
    for s in range(1, CONV_WIDTH):
        conv = conv + ubuf_ref[pl.ds(SUBLANES - s, tm), :] * cw[CONV_WIDTH - 1 - s:CONV_WIDTH - s]
    o_ref[...] = (bg * conv).astype(o_ref.dtype)


def _conv_mix(xb, w_in, conv_w, *, seq_rows, pad_rows):
    n_rows, d = xb.shape
    tm = _pick_tile(n_rows, CONV_ROWS, SUBLANES)
    tn = _pick_tile(d, CONV_COLS, LANES)
    nj = d // tn
    return pl.pallas_call(
        functools.partial(_conv_kernel, seq_rows=seq_rows, pad_rows=pad_rows, tm=tm),
        grid=(n_rows // tm, nj),
        in_specs=[pl.BlockSpec((tm, d), lambda i, j: (i, 0)),
                  pl.BlockSpec((d, tn), lambda i, j: (0, j)),
                  pl.BlockSpec((d, tn), lambda i, j: (0, nj + j)),
                  pl.BlockSpec((d, tn), lambda i, j: (0, 2 * nj + j)),
                  pl.BlockSpec((CONV_WIDTH, tn), lambda i, j: (0, j))],
        out_specs=pl.BlockSpec((tm, tn), lambda i, j: (i, j)),
        out_shape=jax.ShapeDtypeStruct((n_rows, d), BF16),
        scratch_shapes=[pltpu.VMEM((tm + SUBLANES, tn), F32),
                        pltpu.VMEM((nj, SUBLANES, tn), F32)],
        compiler_params=_cparams(("arbitrary", "arbitrary")),
        name="conv_mix",
    )(xb, w_in, w_in, w_in, conv_w)


def _layer_norm(y, g, b):
    mu = jnp.mean(y, axis=-1, keepdims=True)
    yc = y - mu
    var = jnp.mean(yc * yc, axis=-1, keepdims=True)
    return yc * lax.rsqrt(var + LN_EPS) * g + b


def _proj_ln_kernel(a_ref, w_ref, h_ref, g_ref, b_ref, rw_ref, o_ref, lg_ref, hp_ref):
    y = jnp.dot(a_ref[...], w_ref[...], preferred_element_type=F32) + DN_ALPHA * h_ref[...]
    h = _layer_norm(y, g_ref[...], b_ref[...])
    o_ref[...] = h
    _store_rows_as_tiles(hp_ref, _pack_bf16_pairs(h), h.shape[0])
    lg_ref[...] = lax.dot_general(rw_ref[...], h.astype(BF16), (((1,), (1,)), ((), ())),
                                  preferred_element_type=F32)


def _proj_ln_router(a, w, h, g, b, rw_t):
    n_rows, kin = a.shape
    d = w.shape[1]
    e = rw_t.shape[0]
    tm = _pick_tile(n_rows, OUT_PROJ_ROWS, LANES)
    return pl.pallas_call(
        _proj_ln_kernel,
        grid=(n_rows // tm,),
        in_specs=[pl.BlockSpec((tm, kin), lambda i: (i, 0)),
                  pl.BlockSpec((kin, d), lambda i: (0, 0)),
                  pl.BlockSpec((tm, d), lambda i: (i, 0)),
                  pl.BlockSpec((1, d), lambda i: (0, 0)),
                  pl.BlockSpec((1, d), lambda i: (0, 0)),
                  pl.BlockSpec((e, d), lambda i: (0, 0))],
        out_specs=[pl.BlockSpec((tm, d), lambda i: (i, 0)),
                   pl.BlockSpec((e, tm), lambda i: (0, i)),
                   pl.BlockSpec((tm * SUBLANES, LANES), lambda i: (i, 0))],
        out_shape=[jax.ShapeDtypeStruct((n_rows, d), F32),
                   jax.ShapeDtypeStruct((e, n_rows), F32),
                   jax.ShapeDtypeStruct((n_rows * SUBLANES, LANES), jnp.uint32)],
        compiler_params=_cparams(("parallel",)),
        name="proj_ln_router",
    )(a, w, h, g, b, rw_t)


def _beats(cand, cand_idx, ref, ref_idx):
    return (cand > ref) | ((cand == ref) & (cand_idx < ref_idx))


def _route_kernel(lg_ref, bias_ref, w_ref, pos_ref, cnt_ref, run_ref, *, t, seq_rows, pad_rows):
    E, G = N_EXPERTS, N_GROUPS
    gs = E // G

    @pl.when(pl.program_id(0) == 0)
    def _():
        run_ref[...] = jnp.zeros_like(run_ref)

    s = jax.nn.sigmoid(lg_ref[...])
    c = s + bias_ref[...][:, :1]
    sub = lax.broadcasted_iota(jnp.int32, (gs, t), 0)

    grp_rows = []
    for g in range(G):
        cg = c[g * gs:(g + 1) * gs]
        rank = jnp.zeros((gs, t), jnp.int32)
        for m in range(gs):
            rank = rank + _beats(cg[m:m + 1], m, cg, sub).astype(jnp.int32)
        grp_rows.append(jnp.sum(jnp.where(rank < 2, cg, 0.0), axis=0, keepdims=True))
    gidx = lax.broadcasted_iota(jnp.int32, (G, t), 0)
    gscore = jnp.zeros((G, t), F32)
    for g in range(G):
        gscore = jnp.where(gidx == g, grp_rows[g], gscore)
    grank = jnp.zeros((G, t), jnp.int32)
    for m in range(G):
        grank = grank + _beats(gscore[m:m + 1], m, gscore, gidx).astype(jnp.int32)
    gsel = grank < TOPK_GROUPS

    masked = jnp.concatenate(
        [jnp.where(gsel[g:g + 1], c[g * gs:(g + 1) * gs], -jnp.inf) for g in range(G)], axis=0)
    eidx = lax.broadcasted_iota(jnp.int32, (E, t), 0).astype(F32)
    sel = jnp.zeros((E, t), jnp.bool_)
    for _ in range(TOP_K):
        best = jnp.max(masked, axis=0, keepdims=True)
        pick = jnp.min(jnp.where((masked == best) & jnp.logical_not(sel), eidx, float(E)),
                       axis=0, keepdims=True)
        hit = eidx == pick
        sel = sel | hit
        masked = jnp.where(hit, -jnp.inf, masked)
    tok = pl.program_id(0) * t + lax.broadcasted_iota(jnp.int32, (1, t), 1)
    real = tok % seq_rows >= pad_rows
    sel = sel & real
    gate = jnp.where(sel, s, 0.0)
    total = jnp.where(real, jnp.sum(gate, axis=0, keepdims=True), 1.0)
    w_ref[...] = gate / total * ROUTED_SCALE

    li = lax.broadcasted_iota(jnp.int32, (t, t), 0)
    lj = lax.broadcasted_iota(jnp.int32, (t, t), 1)
    upper = (li <= lj).astype(BF16)
    self_ = sel.astype(F32)
    incl = jnp.dot(sel.astype(BF16), upper, preferred_element_type=F32)
    run = run_ref[...][:, :1]
    pos_ref[...] = jnp.where(sel, run + incl - self_, -1.0)
    run_new = run + jnp.sum(self_, axis=1, keepdims=True)
    run_ref[...] = jnp.broadcast_to(run_new, run_ref.shape)
    cnt_ref[...] = jnp.broadcast_to(run_new, cnt_ref.shape)


def _route(logits_t, bias_col, seq_rows, pad_rows):
    e, n_rows = logits_t.shape
    t = _pick_tile(n_rows, ROUTE_LANES, LANES)
    return pl.pallas_call(
        functools.partial(_route_kernel, t=t, seq_rows=seq_rows, pad_rows=pad_rows),
        grid=(n_rows // t,),
        in_specs=[pl.BlockSpec((e, t), lambda i: (0, i)),
                  pl.BlockSpec((e, LANES), lambda i: (0, 0))],
        out_specs=[pl.BlockSpec((e, t), lambda i: (0, i)),
                   pl.BlockSpec((e, t), lambda i: (0, i)),
                   pl.BlockSpec((e, LANES), lambda i: (0, 0))],
        out_shape=[jax.ShapeDtypeStruct((e, n_rows), F32),
                   jax.ShapeDtypeStruct((e, n_rows), F32),
                   jax.ShapeDtypeStruct((e, LANES), F32)],
        scratch_shapes=[pltpu.VMEM((e, LANES), F32)],
        compiler_params=_cparams(("arbitrary",)),
        name="route",
    )(logits_t, bias_col)


def _compact_kernel(w_ref, pos_ref, pst_ref, slot_ref, w8_ref, *, t):
    E = N_EXPERTS
    pos = pos_ref[...]
    sel = pos >= 0.0
    ri = lax.broadcasted_iota(jnp.int32, (E, E), 0)
    ci = lax.broadcasted_iota(jnp.int32, (E, E), 1)
    below = (ci < ri).astype(BF16)
    order = jnp.dot(below, sel.astype(BF16), preferred_element_type=F32)
    slot = pst_ref[...][:, :1] + pos
    wd = w_ref[...]
    kidx = lax.broadcasted_iota(jnp.int32, (TOP_K, t), 0)
    slots = jnp.zeros((TOP_K, t), F32)
    w8 = jnp.zeros((TOP_K, t), F32)
    for k in range(TOP_K):
        m = sel & (order == float(k))
        slots = jnp.where(kidx == k, jnp.sum(jnp.where(m, slot, 0.0), axis=0, keepdims=True), slots)
        w8 = jnp.where(kidx == k, jnp.sum(jnp.where(m, wd, 0.0), axis=0, keepdims=True), w8)
    slot_ref[...] = (slots * float(SUBLANES)).astype(jnp.int32)
    w8_ref[...] = w8


def _compact(w_dense, pos_dense, pstart_col):
    e, n_rows = w_dense.shape
    t = _pick_tile(n_rows, ROUTE_LANES, LANES)
    return pl.pallas_call(
        functools.partial(_compact_kernel, t=t),
        grid=(n_rows // t,),
        in_specs=[pl.BlockSpec((e, t), lambda i: (0, i)),
                  pl.BlockSpec((e, t), lambda i: (0, i)),
                  pl.BlockSpec((e, LANES), lambda i: (0, 0))],
        out_specs=[pl.BlockSpec((TOP_K, t), lambda i: (0, i)),
                   pl.BlockSpec((TOP_K, t), lambda i: (0, i))],
        out_shape=[jax.ShapeDtypeStruct((TOP_K, n_rows), jnp.int32),
                   jax.ShapeDtypeStruct((TOP_K, n_rows), F32)],
        compiler_params=_cparams(("parallel",)),
        name="compact",
    )(w_dense, pos_dense, pstart_col)


def _pack_bf16_pairs(x):
    half = x.shape[1] // 2
    lo = lax.bitcast_convert_type(x[:, :half].astype(BF16).astype(F32), jnp.uint32)
    hi = lax.bitcast_convert_type(x[:, half:].astype(BF16).astype(F32), jnp.uint32)
    return (lo >> 16) | hi


def _unpack_bf16_pairs(w):
    lo = lax.bitcast_convert_type(w << 16, F32)
    hi = lax.bitcast_convert_type(w & jnp.uint32(0xFFFF0000), F32)
    return lo, hi


def _store_rows_as_tiles(ref_at, x, rows):
    for c in range(SUBLANES):
        ref_at[pl.ds(c, rows, stride=SUBLANES), :] = x[:, c * LANES:(c + 1) * LANES]


def _load_tile_chunk(ref_at, c, rows):
    return ref_at[pl.ds(c, rows, stride=SUBLANES), :]


def _swiglu(x, w1, w3, w2):
    a = jnp.dot(x, w1, preferred_element_type=F32)
    g = jnp.dot(x, w3, preferred_element_type=F32)
    hmid = (a * jax.nn.sigmoid(a) * g).astype(BF16)
    return jnp.dot(hmid, w2, preferred_element_type=F32)


def _expert_kernel(be_ref, nu_ref, ids_ref, nids_ref, w1_ref, w3_ref, w2_ref, hp_ref, y_ref,
                   w1b_ref, w3b_ref, w2b_ref, xbuf0_ref, xbuf1_ref, sem):
    b = pl.program_id(0)
    n_used = nu_ref[0]
    xbuf = (xbuf0_ref, xbuf1_ref)
    rows = MOE_BLOCK

    def row_copy(s_ref, buf, r):
        src = hp_ref.at[pl.ds(pl.multiple_of(s_ref[r], SUBLANES), SUBLANES), :]
        return pltpu.make_async_copy(src, xbuf[buf].at[pl.ds(r * SUBLANES, SUBLANES), :], sem.at[buf])

    def wait_block(buf):
        pltpu.make_async_copy(hp_ref.at[pl.ds(0, rows * SUBLANES), :], xbuf[buf], sem.at[buf]).wait()

    @pl.when(b == 0)
    def _():
        def issue(r2, carry):
            for p in range(2):
                r = r2 * 2 + p
                src = hp_ref.at[pl.ds(pl.multiple_of(ids_ref[r], SUBLANES), SUBLANES), :]
                dst = xbuf0_ref.at[pl.ds(pl.multiple_of(r * SUBLANES, SUBLANES), SUBLANES), :]
                pltpu.make_async_copy(src, dst, sem.at[0]).start(priority=1)
            return carry

        lax.fori_loop(0, rows // 2, issue, 0)

    def run_block(cur):
        nxt = 1 - cur
        wait_block(cur)

        @pl.when((b == 0) | (be_ref[b] != be_ref[jnp.maximum(b - 1, 0)]))
        def _():
            w1b_ref[...] = w1_ref[...].astype(BF16)
            w3b_ref[...] = w3_ref[...].astype(BF16)
            w2b_ref[...] = w2_ref[...].astype(BF16)

        group = rows // 4

        def issue_group(g):
            for r in range(g * group, (g + 1) * group):
                row_copy(nids_ref, nxt, r).start(priority=1)

        issue_group(0)
        halves = [_unpack_bf16_pairs(_load_tile_chunk(xbuf[cur], c, rows)) for c in range(SUBLANES)]
        x = jnp.concatenate([lo.astype(BF16) for lo, _ in halves]
                            + [hi.astype(BF16) for _, hi in halves], axis=1)
        issue_group(1)
        a = jnp.dot(x, w1b_ref[...], preferred_element_type=F32)
        issue_group(2)
        g = jnp.dot(x, w3b_ref[...], preferred_element_type=F32)
        issue_group(3)
        hmid = (a * jax.nn.sigmoid(a) * g).astype(BF16)
        y = _pack_bf16_pairs(jnp.dot(hmid, w2b_ref[...], preferred_element_type=F32))
        _store_rows_as_tiles(y_ref, y, rows)

        @pl.when(b == n_used - 1)
        def _():
            wait_block(nxt)

    @pl.when(b < n_used)
    def _():
        for parity in range(2):
            @pl.when(b % 2 == parity)
            def _():
                run_block(parity)


def _experts(hp, ids_tiled, w1, w3, w2, layer, block_e, n_used):
    d, f = w1.shape[2], w1.shape[3]
    blk_rows = MOE_BLOCK * SUBLANES
    nb = ids_tiled.shape[0]

    def blk(b, be, nu):
        return jnp.minimum(b, nu[0] - 1)

    def wmap(b, be, nu):
        return (layer, be[blk(b, be, nu)], 0, 0)

    return pl.pallas_call(
        _expert_kernel,
        grid_spec=pltpu.PrefetchScalarGridSpec(
            num_scalar_prefetch=2,
            grid=(nb,),
            in_specs=[pl.BlockSpec((None, None, MOE_BLOCK), lambda b, be, nu: (blk(b, be, nu), 0, 0),
                                   memory_space=pltpu.SMEM),
                      pl.BlockSpec((None, None, MOE_BLOCK),
                                   lambda b, be, nu: (blk(b + 1, be, nu), 0, 0),
                                   memory_space=pltpu.SMEM),
                      pl.BlockSpec((None, None, d, f), wmap),
                      pl.BlockSpec((None, None, d, f), wmap),
                      pl.BlockSpec((None, None, f, d), wmap),
                      pl.BlockSpec(memory_space=pl.ANY)],
            out_specs=pl.BlockSpec((blk_rows, LANES), lambda b, be, nu: (blk(b, be, nu), 0)),
            scratch_shapes=[pltpu.VMEM((d, f), BF16), pltpu.VMEM((d, f), BF16),
                            pltpu.VMEM((f, d), BF16),
                            pltpu.VMEM((blk_rows, LANES), jnp.uint32),
                            pltpu.VMEM((blk_rows, LANES), jnp.uint32),
                            pltpu.SemaphoreType.DMA((2,))]),
        out_shape=jax.ShapeDtypeStruct((nb * blk_rows, LANES), jnp.uint32),
        compiler_params=_cparams(("arbitrary",)),
        name="experts",
    )(block_e, n_used, ids_tiled, ids_tiled, w1, w3, w2, hp)


def _inverse_rows(slot_flat, vals, n_slots):
    n_idx = slot_flat.shape[0]
    window = LANES
    mesh = plsc.VectorSubcoreMesh(core_axis_name="core", subcore_axis_name="subcore")

    @functools.partial(pl.kernel, out_type=jax.ShapeDtypeStruct((n_slots, LANES), jnp.int32),
                       mesh=mesh, scratch_types=[])
    def scatter_rows(x_hbm, i_hbm, o_hbm):
        def body(x_vmem, i_vmem):
            pltpu.sync_copy(x_vmem, o_hbm.at[i_vmem.at[0]])

        pltpu.emit_pipeline(
            body,
            grid=(n_idx // window,),
            in_specs=[pl.BlockSpec((window, LANES), lambda i: (i, 0)),
                      pl.BlockSpec((1, window), lambda i: (0, i))],
            out_specs=[],
            core_axis_name="subcore",
            dimension_semantics=(pltpu.PARALLEL,),
        )(x_hbm, i_hbm)

    return scatter_rows(vals, slot_flat.reshape(1, n_idx))


def _combine_kernel(slot_ref, nslot_ref, h_ref, w8_ref, ws1_ref, ws3_ref, ws2_ref, g_ref, b_ref,
                    ys_ref, o_ref, ob_ref, ybuf0_ref, ybuf1_ref, sem, *, t):
    i = pl.program_id(0)
    last = pl.num_programs(0) - 1
    ybuf = (ybuf0_ref, ybuf1_ref)

    def row_copy(s_ref, buf, k, tok):
        src = ys_ref.at[pl.ds(pl.multiple_of(s_ref[k * t + tok], SUBLANES), SUBLANES), :]
        return pltpu.make_async_copy(
            src, ybuf[buf].at[k, pl.ds(pl.multiple_of(tok * SUBLANES, SUBLANES), SUBLANES), :],
            sem.at[buf])

    def wait_tile(buf):
        for k in range(TOP_K):
            pltpu.make_async_copy(ys_ref.at[pl.ds(0, t * SUBLANES), :], ybuf[buf].at[k],
                                  sem.at[buf]).wait()

    @pl.when(i == 0)
    def _():
        def issue(tok, carry):
            for k in range(TOP_K):
                row_copy(slot_ref, 0, k, tok).start(priority=k % 2)
            return carry

        lax.fori_loop(0, t, issue, 0, unroll=2)

    group = t // TOP_K

    def reduce_tile(cur):
        nxt = 1 - cur

        def issue_group(g):
            for tok in range(g * group, (g + 1) * group):
                for k in range(TOP_K):
                    row_copy(nslot_ref, nxt, k, tok).start(priority=k % 2)

        wait_tile(cur)
        h = h_ref[...]
        issue_group(0)
        shared = _swiglu(h.astype(BF16), ws1_ref[...], ws3_ref[...], ws2_ref[...])
        w8 = w8_ref[...]
        wk = [w8[:, k:k + 1] for k in range(TOP_K)]
        acc_lo, acc_hi = [], []
        for c in range(SUBLANES):
            if c + 1 < TOP_K:
                issue_group(c + 1)
            lo_c, hi_c = None, None
            for k in range(TOP_K):
                lo, hi = _unpack_bf16_pairs(_load_tile_chunk(ybuf[cur].at[k], c, t))
                lo_c = lo * wk[k] if k == 0 else lo_c + lo * wk[k]
                hi_c = hi * wk[k] if k == 0 else hi_c + hi * wk[k]
            acc_lo.append(lo_c)
            acc_hi.append(hi_c)
        acc = jnp.concatenate(acc_lo + acc_hi, axis=1)
        y = DN_ALPHA * h + (acc + shared)
        out = _layer_norm(y, g_ref[...], b_ref[...])
        o_ref[...] = out
        ob_ref[...] = out.astype(BF16)

        @pl.when(i == last)
        def _():
            wait_tile(nxt)

    for parity in range(2):
        @pl.when(i % 2 == parity)
        def _():
            reduce_tile(parity)


def _combine(h, ys, slots_tiled, w8_t, ws1, ws3, ws2, g, b, t, out_rows, out_map):
    n_rows, d = h.shape
    f = ws1.shape[1]
    n_tiles = n_rows // t
    return pl.pallas_call(
        functools.partial(_combine_kernel, t=t),
        grid=(n_tiles,),
        in_specs=[pl.BlockSpec((None, None, TOP_K * t), lambda i: (i, 0, 0),
                               memory_space=pltpu.SMEM),
                  pl.BlockSpec((None, None, TOP_K * t),
                               lambda i: (jnp.minimum(i + 1, n_tiles - 1), 0, 0),
                               memory_space=pltpu.SMEM),
                  pl.BlockSpec((t, d), lambda i: (i, 0)),
                  pl.BlockSpec((t, TOP_K), lambda i: (i, 0)),
                  pl.BlockSpec((d, f), lambda i: (0, 0)),
                  pl.BlockSpec((d, f), lambda i: (0, 0)),
                  pl.BlockSpec((f, d), lambda i: (0, 0)),
                  pl.BlockSpec((1, d), lambda i: (0, 0)),
                  pl.BlockSpec((1, d), lambda i: (0, 0)),
                  pl.BlockSpec(memory_space=pl.ANY)],
        out_specs=[pl.BlockSpec((t, d), lambda i: (out_map(i), 0)),
                   pl.BlockSpec((t, d), lambda i: (i, 0))],
        out_shape=[jax.ShapeDtypeStruct((out_rows, d), F32),
                   jax.ShapeDtypeStruct((n_rows, d), BF16)],
        scratch_shapes=[pltpu.VMEM((TOP_K, t * SUBLANES, LANES), jnp.uint32),
                        pltpu.VMEM((TOP_K, t * SUBLANES, LANES), jnp.uint32),
                        pltpu.SemaphoreType.DMA((2,))],
        compiler_params=_cparams(("arbitrary",)),
        name="combine_ln",
    )(slots_tiled, slots_tiled, h, w8_t, ws1, ws3, ws2, g, b, ys)


def _moe_ln(h, hp, logits_t, router_bias, w1, w3, w2, layer, ws1, ws3, ws2, g, b, drop_rows,
            seq_rows, pad_rows):
    n_rows, d = h.shape
    E = N_EXPERTS
    assert d // 2 == SUBLANES * LANES, d
    bias_col = jnp.broadcast_to(router_bias.astype(F32)[:, None], (E, LANES))
    w_dense, pos_dense, cnt = _route(logits_t, bias_col, seq_rows, pad_rows)

    counts = cnt[:, 0].astype(jnp.int32)
    pcounts = (counts + MOE_BLOCK - 1) // MOE_BLOCK * MOE_BLOCK
    pends = jnp.cumsum(pcounts)
    pstarts = pends - pcounts
    n_blocks = -(-(n_rows * TOP_K) // MOE_BLOCK) + E
    n_slots = n_blocks * MOE_BLOCK
    block_lo = jnp.arange(n_blocks, dtype=jnp.int32) * MOE_BLOCK
    block_e = jnp.minimum(jnp.sum(pends[None, :] <= block_lo[:, None], axis=1), E - 1).astype(jnp.int32)
    n_used = (pends[-1:] // MOE_BLOCK).astype(jnp.int32)
    pstart_col = jnp.broadcast_to(pstarts.astype(F32)[:, None], (E, LANES))

    slots, w8 = _compact(w_dense, pos_dense, pstart_col)
    t = _pick_tile(n_rows, TOKEN_TILE, LANES)
    n_tiles = n_rows // t
    slots_tiled = slots.reshape(TOP_K, n_tiles, t).transpose(1, 0, 2).reshape(n_tiles, 1, TOP_K * t)
    tok = jnp.arange(n_rows, dtype=jnp.int32)
    tok_rows = jnp.tile(tok * SUBLANES, TOP_K)
    real = jnp.tile(tok % seq_rows >= pad_rows, TOP_K)
    inv = _inverse_rows(jnp.where(real, (slots // SUBLANES).reshape(-1), n_slots),
                        jnp.broadcast_to(tok_rows[:, None], (n_rows * TOP_K, LANES)), n_slots + 1)
    ids = jnp.clip(inv[:n_slots, 0] // SUBLANES, 0, n_rows - 1) * SUBLANES
    ys = _experts(hp, ids.reshape(n_blocks, 1, MOE_BLOCK), w1, w3, w2, layer, block_e, n_used)
    if drop_rows == t:
        per_seq = seq_rows // t
        out_rows = n_rows - (n_rows // seq_rows) * t

        def out_map(i):
            return (i // per_seq) * (per_seq - 1) + jnp.maximum(i % per_seq - 1, 0)
    else:
        out_rows, out_map = n_rows, (lambda i: i)
    return _combine(h, ys, slots_tiled, w8.T, ws1, ws3, ws2, g, b, t, out_rows, out_map)


def kernel(x, meta_tokens, gla_w_in, gla_w_gate_up, gla_b_gate, gla_norm_g, gla_w_out,
           conv_w_in, conv_w, conv_w_out, ln1_g, ln1_b, router_w, router_bias,
           exp_w1, exp_w3, exp_w2, shared_w1, shared_w3, shared_w2, ln2_g, ln2_b):
    batch, seq, d = x.shape
    H = GLA_HEADS
    dk = d // 2 // H
    n_qkvr = 2 * H * dk + 2 * d
    rank = gla_w_in.shape[2] - n_qkvr
    seq_rows = -(-(N_META + seq) // ROW_ALIGN) * ROW_ALIGN
    pad_rows = seq_rows - N_META - seq
    n_rows = batch * seq_rows

    meta = jnp.broadcast_to(meta_tokens.astype(x.dtype)[None], (batch, N_META, d))
    h = jnp.concatenate([jnp.zeros((batch, pad_rows, d), x.dtype), meta, x], axis=1)
    h = h.reshape(n_rows, d)
    hb = h

    def row(v):
        return v.astype(F32)[None, :]

    for i in range(DEPTH):
        jm = i // 2
        if i % 2 == 0:
            w_in = gla_w_in[jm]
            w_all = jnp.pad(w_in, ((0, 0), (0, GATE_PAD - rank))).astype(BF16)
            wg_up = jnp.pad(gla_w_gate_up[jm], ((0, GATE_PAD - rank), (0, 0))).astype(BF16)
            qkvrg = _matmul(hb, w_all, BF16, IN_PROJ_ROWS, IN_PROJ_COLS, "gla_in_proj")
            mix = _gla(qkvrg, wg_up, row(gla_b_gate[jm]), row(gla_norm_g[jm]),
                       batch=batch, seq_rows=seq_rows, d_model=d)
            w_out = gla_w_out[jm].astype(BF16)
        else:
            mix = _conv_mix(hb, conv_w_in[jm].astype(BF16), conv_w[jm].astype(F32),
                            seq_rows=seq_rows, pad_rows=pad_rows)
            w_out = conv_w_out[jm].astype(BF16)
        h, logits_t, hp = _proj_ln_router(mix, w_out, h, row(ln1_g[i]), row(ln1_b[i]),
                                          router_w[i].T.astype(BF16))
        drop = pad_rows + N_META if i == DEPTH - 1 else 0
        h, hb = _moe_ln(h, hp, logits_t, router_bias[i], exp_w1, exp_w3, exp_w2, i,
                        shared_w1[i].astype(BF16), shared_w3[i].astype(BF16),
                        shared_w2[i].astype(BF16), row(ln2_g[i]), row(ln2_b[i]), drop, seq_rows,
                        pad_rows)
    if h.shape[0] == batch * seq:
        return h.reshape(batch, seq, d)
    return h.reshape(batch, seq_rows, d)[:, pad_rows + N_META:]
```

```python
import functools

import jax
import jax.numpy as jnp
from jax import lax
from jax.experimental import pallas as pl
from jax.experimental.pallas import tpu as pltpu
from jax.experimental.pallas import tpu_sc as plsc

N_META = 16
GLA_HEADS = 4
GLA_GATE_TAU = 16.0
CONV_WIDTH = 3
N_EXPERTS = 64
TOP_K = 8
N_GROUPS = 8
TOPK_GROUPS = 4
ROUTED_SCALE = 2.5
LN_EPS = 1e-5
RMS_EPS = 1e-6
DEPTH = 2
DN_ALPHA = (2 * DEPTH) ** 0.25
LOG2_E = 1.4426950408889634

LANES = 128
SUBLANES = 8
ROW_ALIGN = 128
GLA_CHUNK = 64
GLA_SUB = 16
GLA_HEADS_PER_STEP = 4
MXU_WIDTH = 256
GATE_PAD = MXU_WIDTH
MOE_BLOCK = 512
TOKEN_TILE = 128
IN_PROJ_ROWS, IN_PROJ_COLS = 832, 1280
CONV_ROWS, CONV_COLS = 1024, 512
OUT_PROJ_ROWS = 512
GLA_ROWS = 640
ROUTE_LANES = 512
VMEM_LIMIT = 56 * 1024 * 1024

F32 = jnp.float32
BF16 = jnp.bfloat16


def _pick_tile(n, target, mult):
    best = None
    for t in range(mult, min(n, target) + 1, mult):
        if n % t == 0:
            best = t
    assert best is not None, (n, target, mult)
    return best


def _cparams(sem):
    return pltpu.CompilerParams(dimension_semantics=sem, vmem_limit_bytes=VMEM_LIMIT)


def _mm_kernel(x_ref, w_ref, o_ref):
    x = x_ref[...].astype(BF16)
    o_ref[...] = jnp.dot(x, w_ref[...], preferred_element_type=F32).astype(o_ref.dtype)


def _matmul(x, w, out_dtype, tm_target, tn_target, name):
    m, k = x.shape
    n = w.shape[1]
    tm = _pick_tile(m, tm_target, 16)
    tn = _pick_tile(n, tn_target, LANES)
    return pl.pallas_call(
        _mm_kernel,
        grid=(m // tm, n // tn),
        in_specs=[pl.BlockSpec((tm, k), lambda i, j: (i, 0)),
                  pl.BlockSpec((k, tn), lambda i, j: (0, j))],
        out_specs=pl.BlockSpec((tm, tn), lambda i, j: (i, j)),
        out_shape=jax.ShapeDtypeStruct((m, n), out_dtype),
        compiler_params=_cparams(("parallel", "arbitrary")),
        name=name,
    )(x, w)


def _gla_kernel(q_ref, k_ref, v_ref, r_ref, gl_ref, wg_ref, bg_ref, ng_ref, o_ref, st_ref,
                *, n_chunks, dk, dv, heads):
    C, S = GLA_CHUNK, GLA_SUB

    @pl.when(pl.program_id(2) == 0)
    def _():
        st_ref[...] = jnp.zeros_like(st_ref)

    ri = lax.broadcasted_iota(jnp.int32, (C, C), 0)
    ci = lax.broadcasted_iota(jnp.int32, (C, C), 1)
    tri = (ri >= ci).astype(BF16)
    rs = lax.broadcasted_iota(jnp.int32, (S, S), 0)
    cs = lax.broadcasted_iota(jnp.int32, (S, S), 1)
    nt = (((1,), (1,)), ((), ()))
    tn = (((0,), (0,)), ((), ()))
    half = S // 2

    def chunk(c, carry):
        r0 = pl.multiple_of(c * C, C)
        rows_c = pl.ds(r0, C)
        hs = range(heads)
        kcol = [pl.ds(hd * dk, dk) for hd in hs]
        vcol = [pl.ds(hd * dv, dv) for hd in hs]
        gl = gl_ref[rows_c, :]
        q = [q_ref[rows_c, kcol[hd]].astype(F32) * (dk ** -0.5) for hd in hs]
        kk = [k_ref[rows_c, kcol[hd]].astype(F32) for hd in hs]
        vv = [v_ref[rows_c, vcol[hd]] for hd in hs]
        st = [st_ref[hd] for hd in hs]
        z = [jnp.dot(gl, wg_ref[:, kcol[hd]], preferred_element_type=F32) + bg_ref[:, kcol[hd]]
             for hd in hs]
        b = []
        for hd in hs:
            la = ((jnp.minimum(z[hd], 0.0) - jnp.log(1.0 + jnp.exp(-jnp.abs(z[hd]))))
                  * (1.0 / GLA_GATE_TAU))
            h1 = la.astype(BF16)
            e1 = la - h1.astype(F32)
            h2 = e1.astype(BF16)
            h3 = (e1 - h2.astype(F32)).astype(BF16)
            b3 = jnp.dot(tri, jnp.concatenate([h1, h2, h3], axis=1), preferred_element_type=F32)
            b.append((b3[:, :dk] + b3[:, dk:2 * dk] + b3[:, 2 * dk:]) * LOG2_E)

        o_inter, st_new, off = [], [], []
        for hd in hs:
            qe = (q[hd] * jnp.exp2(b[hd])).astype(BF16)
            o_inter.append(lax.dot_general(qe, st[hd].astype(BF16), nt, preferred_element_type=F32))
            bl = b[hd][C - 1:C]
            khat = (kk[hd] * jnp.exp2(bl - b[hd])).astype(BF16)
            upd = lax.dot_general(vv[hd], khat, tn, preferred_element_type=F32)
            st_new.append(st[hd] * jnp.exp2(bl) + upd)
            offs = []
            for i in range(1, C // S):
                lo = i * S
                bref = b[hd][lo - 1:lo]
                qt = (q[hd][lo:lo + S] * jnp.exp2(b[hd][lo:lo + S] - bref)).astype(BF16)
                kt = (kk[hd][:lo] * jnp.exp2(bref - b[hd][:lo])).astype(BF16)
                offs.append(lax.dot_general(qt, kt, nt, preferred_element_type=F32))
            off.append(offs)

        dmats = []
        for hd in hs:
            blocks = []
            for i in range(C // S):
                lo = i * S
                qs = q[hd][lo:lo + S]
                bs = b[hd][lo:lo + S]
                dmat = jnp.zeros((S, S), F32)
                for j in range(S):
                    kj = kk[hd][lo + j:lo + j + 1]
                    bj = b[hd][lo + j:lo + j + 1]
                    if j < half:
                        p = qs * kj * jnp.exp2(bs - bj)
                        a = jnp.sum(p, axis=-1, keepdims=True)
                    else:
                        p = qs[half:] * kj * jnp.exp2(bs[half:] - bj)
                        a = jnp.concatenate([jnp.zeros((half, 1), F32),
                                             jnp.sum(p, axis=-1, keepdims=True)], axis=0)
                    dmat = jnp.where(cs == j, a, dmat)
                blocks.append(jnp.where(rs >= cs, dmat, 0.0).astype(BF16))
            dmats.append(blocks)

        for hd in hs:
            rows = []
            for i in range(C // S):
                lo = i * S
                o_i = jnp.dot(dmats[hd][i], vv[hd][lo:lo + S], preferred_element_type=F32)
                if i > 0:
                    o_i = o_i + jnp.dot(off[hd][i - 1].astype(BF16), vv[hd][:lo],
                                        preferred_element_type=F32)
                rows.append(o_i)
            o = o_inter[hd] + jnp.concatenate(rows, axis=0)
            ms = jnp.mean(o * o, axis=-1, keepdims=True)
            r = r_ref[rows_c, vcol[hd]].astype(F32)
            y = o * lax.rsqrt(ms + RMS_EPS) * ng_ref[:, vcol[hd]] * (r * jax.nn.sigmoid(r))
            st_ref[hd] = st_new[hd]
            o_ref[rows_c, vcol[hd]] = y.astype(o_ref.dtype)
        return carry

    lax.fori_loop(0, n_chunks, chunk, 0)


def _gla(qkvrg, wg, bg, ng, *, batch, seq_rows, d_model):
    H, hps = GLA_HEADS, GLA_HEADS_PER_STEP
    dk = d_model // 2 // H
    dv = d_model // H
    n_rows = batch * seq_rows
    rblk = _pick_tile(seq_rows, GLA_ROWS, GLA_CHUNK)
    steps = seq_rows // rblk
    groups = H // hps
    off_k, off_v, off_r = groups, (2 * H * dk) // (hps * dv), (2 * H * dk) // (hps * dv) + groups

    off_g = (2 * H * dk + 2 * H * dv) // GATE_PAD

    def rowmap(off):
        return lambda b, h, i: (b * steps + i, off + h)

    return pl.pallas_call(
        functools.partial(_gla_kernel, n_chunks=rblk // GLA_CHUNK, dk=dk, dv=dv, heads=hps),
        grid=(batch, groups, steps),
        in_specs=[pl.BlockSpec((rblk, hps * dk), rowmap(0)),
                  pl.BlockSpec((rblk, hps * dk), rowmap(off_k)),
                  pl.BlockSpec((rblk, hps * dv), rowmap(off_v)),
                  pl.BlockSpec((rblk, hps * dv), rowmap(off_r)),
                  pl.BlockSpec((rblk, GATE_PAD), lambda b, h, i: (b * steps + i, off_g)),
                  pl.BlockSpec((GATE_PAD, hps * dk), lambda b, h, i: (0, h)),
                  pl.BlockSpec((1, hps * dk), lambda b, h, i: (0, h)),
                  pl.BlockSpec((1, hps * dv), lambda b, h, i: (0, h))],
        out_specs=pl.BlockSpec((rblk, hps * dv), lambda b, h, i: (b * steps + i, h)),
        out_shape=jax.ShapeDtypeStruct((n_rows, H * dv), BF16),
        scratch_shapes=[pltpu.VMEM((hps, dv, dk), F32)],
        compiler_params=_cparams(("parallel", "parallel", "arbitrary")),
        name="gla_chunks",
    )(qkvrg, qkvrg, qkvrg, qkvrg, qkvrg, wg, bg, ng)


def _conv_kernel(x_ref, wb_ref, wc_ref, wh_ref, cw_ref, o_ref, ubuf_ref, carry_ref,
                 *, seq_rows, pad_rows, tm):
    i = pl.program_id(0)
    j = pl.program_id(1)
    x = x_ref[...]
    bg = jnp.dot(x, wb_ref[...], preferred_element_type=F32)
    cg = jnp.dot(x, wc_ref[...], preferred_element_type=F32)
    hh = jnp.dot(x, wh_ref[...], preferred_element_type=F32)
    row = i * tm + lax.broadcasted_iota(jnp.int32, (tm, 1), 0)
    u = jnp.where(row % seq_rows >= pad_rows, cg * hh, 0.0)

    @pl.when(i == 0)
    def _():
        carry_ref[j] = jnp.zeros(carry_ref.shape[1:], F32)

    ubuf_ref[pl.ds(0, SUBLANES), :] = carry_ref[j]
    ubuf_ref[pl.ds(SUBLANES, tm), :] = u
    carry_ref[j] = u[tm - SUBLANES:]
    cw = cw_ref[...]
    conv = u * cw[CONV_WIDTH - 1:CONV_WIDTH]
    for s in range(1, CONV_WIDTH):
        conv = conv + ubuf_ref[pl.ds(SUBLANES - s, tm), :] * cw[CONV_WIDTH - 1 - s:CONV_WIDTH - s]
    o_ref[...] = (bg * conv).astype(o_ref.dtype)


def _conv_mix(xb, w_in, conv_w, *, seq_rows, pad_rows):
    n_rows, d = xb.shape
    tm = _pick_tile(n_rows, CONV_ROWS, SUBLANES)
    tn = _pick_tile(d, CONV_COLS, LANES)
    nj = d // tn
    return pl.pallas_call(
        functools.partial(_conv_kernel, seq_rows=seq_rows, pad_rows=pad_rows, tm=tm),
        grid=(n_rows // tm, nj),
        in_specs=[pl.BlockSpec((tm, d), lambda i, j: (i, 0)),
                  pl.BlockSpec((d, tn), lambda i, j: (0, j)),
                  pl.BlockSpec((d, tn), lambda i, j: (0, nj + j)),
                  pl.BlockSpec((d, tn), lambda i, j: (0, 2 * nj + j)),
                  pl.BlockSpec((CONV_WIDTH, tn), lambda i, j: (0, j))],
        out_specs=pl.BlockSpec((tm, tn), lambda i, j: (i, j)),
        out_shape=jax.ShapeDtypeStruct((n_rows, d), BF16),
        scratch_shapes=[pltpu.VMEM((tm + SUBLANES, tn), F32),
                        pltpu.VMEM((nj, SUBLANES, tn), F32)],
        compiler_params=_cparams(("arbitrary", "arbitrary")),
        name="conv_mix",
    )(xb, w_in, w_in, w_in, conv_w)


def _layer_norm(y, g, b):
    mu = jnp.mean(y, axis=-1, keepdims=True)
    yc = y - mu
    var = jnp.mean(yc * yc, axis=-1, keepdims=True)
    return yc * lax.rsqrt(var + LN_EPS) * g + b


def _proj_ln_kernel(a_ref, w_ref, h_ref, g_ref, b_ref, rw_ref, o_ref, lg_ref, hp_ref):
    y = jnp.dot(a_ref[...], w_ref[...], preferred_element_type=F32) + DN_ALPHA * h_ref[...]
    h = _layer_norm(y, g_ref[...], b_ref[...])
    o_ref[...] = h
    _store_rows_as_tiles(hp_ref, _pack_bf16_pairs(h), h.shape[0])
    lg_ref[...] = lax.dot_general(rw_ref[...], h.astype(BF16), (((1,), (1,)), ((), ())),
                                  preferred_element_type=F32)


def _proj_ln_router(a, w, h, g, b, rw_t):
    n_rows, kin = a.shape
    d = w.shape[1]
    e = rw_t.shape[0]
    tm = _pick_tile(n_rows, OUT_PROJ_ROWS, LANES)
    return pl.pallas_call(
        _proj_ln_kernel,
        grid=(n_rows // tm,),
        in_specs=[pl.BlockSpec((tm, kin), lambda i: (i, 0)),
                  pl.BlockSpec((kin, d), lambda i: (0, 0)),
                  pl.BlockSpec((tm, d), lambda i: (i, 0)),
                  pl.BlockSpec((1, d), lambda i: (0, 0)),
                  pl.BlockSpec((1, d), lambda i: (0, 0)),
                  pl.BlockSpec((e, d), lambda i: (0, 0))],
        out_specs=[pl.BlockSpec((tm, d), lambda i: (i, 0)),
                   pl.BlockSpec((e, tm), lambda i: (0, i)),
                   pl.BlockSpec((tm * SUBLANES, LANES), lambda i: (i, 0))],
        out_shape=[jax.ShapeDtypeStruct((n_rows, d), F32),
                   jax.ShapeDtypeStruct((e, n_rows), F32),
                   jax.ShapeDtypeStruct((n_rows * SUBLANES, LANES), jnp.uint32)],
        compiler_params=_cparams(("parallel",)),
        name="proj_ln_router",
    )(a, w, h, g, b, rw_t)


def _beats(cand, cand_idx, ref, ref_idx):
    return (cand > ref) | ((cand == ref) & (cand_idx < ref_idx))


def _route_kernel(lg_ref, bias_ref, w_ref, pos_ref, cnt_ref, run_ref, *, t):
    E, G = N_EXPERTS, N_GROUPS
    gs = E // G

    @pl.when(pl.program_id(0) == 0)
    def _():
        run_ref[...] = jnp.zeros_like(run_ref)

    s = jax.nn.sigmoid(lg_ref[...])
    c = s + bias_ref[...][:, :1]
    sub = lax.broadcasted_iota(jnp.int32, (gs, t), 0)

    grp_rows = []
    for g in range(G):
        cg = c[g * gs:(g + 1) * gs]
        rank = jnp.zeros((gs, t), jnp.int32)
        for m in range(gs):
            rank = rank + _beats(cg[m:m + 1], m, cg, sub).astype(jnp.int32)
        grp_rows.append(jnp.sum(jnp.where(rank < 2, cg, 0.0), axis=0, keepdims=True))
    gidx = lax.broadcasted_iota(jnp.int32, (G, t), 0)
    gscore = jnp.zeros((G, t), F32)
    for g in range(G):
        gscore = jnp.where(gidx == g, grp_rows[g], gscore)
    grank = jnp.zeros((G, t), jnp.int32)
    for m in range(G):
        grank = grank + _beats(gscore[m:m + 1], m, gscore, gidx).astype(jnp.int32)
    gsel = grank < TOPK_GROUPS

    masked = jnp.concatenate(
        [jnp.where(gsel[g:g + 1], c[g * gs:(g + 1) * gs], -jnp.inf) for g in range(G)], axis=0)
    eidx = lax.broadcasted_iota(jnp.int32, (E, t), 0).astype(F32)
    sel = jnp.zeros((E, t), jnp.bool_)
    for _ in range(TOP_K):
        best = jnp.max(masked, axis=0, keepdims=True)
        pick = jnp.min(jnp.where((masked == best) & jnp.logical_not(sel), eidx, float(E)),
                       axis=0, keepdims=True)
        hit = eidx == pick
        sel = sel | hit
        masked = jnp.where(hit, -jnp.inf, masked)
    gate = jnp.where(sel, s, 0.0)
    w_ref[...] = gate / jnp.sum(gate, axis=0, keepdims=True) * ROUTED_SCALE

    li = lax.broadcasted_iota(jnp.int32, (t, t), 0)
    lj = lax.broadcasted_iota(jnp.int32, (t, t), 1)
    upper = (li <= lj).astype(BF16)
    self_ = sel.astype(F32)
    incl = jnp.dot(sel.astype(BF16), upper, preferred_element_type=F32)
    run = run_ref[...][:, :1]
    pos_ref[...] = jnp.where(sel, run + incl - self_, -1.0)
    run_new = run + jnp.sum(self_, axis=1, keepdims=True)
    run_ref[...] = jnp.broadcast_to(run_new, run_ref.shape)
    cnt_ref[...] = jnp.broadcast_to(run_new, cnt_ref.shape)


def _route(logits_t, bias_col):
    e, n_rows = logits_t.shape
    t = _pick_tile(n_rows, ROUTE_LANES, LANES)
    return pl.pallas_call(
        functools.partial(_route_kernel, t=t),
        grid=(n_rows // t,),
        in_specs=[pl.BlockSpec((e, t), lambda i: (0, i)),
                  pl.BlockSpec((e, LANES), lambda i: (0, 0))],
        out_specs=[pl.BlockSpec((e, t), lambda i: (0, i)),
                   pl.BlockSpec((e, t), lambda i: (0, i)),
                   pl.BlockSpec((e, LANES), lambda i: (0, 0))],
        out_shape=[jax.ShapeDtypeStruct((e, n_rows), F32),
                   jax.ShapeDtypeStruct((e, n_rows), F32),
                   jax.ShapeDtypeStruct((e, LANES), F32)],
        scratch_shapes=[pltpu.VMEM((e, LANES), F32)],
        compiler_params=_cparams(("arbitrary",)),
        name="route",
    )(logits_t, bias_col)


def _compact_kernel(w_ref, pos_ref, pst_ref, slot_ref, w8_ref, *, t):
    E = N_EXPERTS
    pos = pos_ref[...]
    sel = pos >= 0.0
    ri = lax.broadcasted_iota(jnp.int32, (E, E), 0)
    ci = lax.broadcasted_iota(jnp.int32, (E, E), 1)
    below = (ci < ri).astype(BF16)
    order = jnp.dot(below, sel.astype(BF16), preferred_element_type=F32)
    slot = pst_ref[...][:, :1] + pos
    wd = w_ref[...]
    kidx = lax.broadcasted_iota(jnp.int32, (TOP_K, t), 0)
    slots = jnp.zeros((TOP_K, t), F32)
    w8 = jnp.zeros((TOP_K, t), F32)
    for k in range(TOP_K):
        m = sel & (order == float(k))
        slots = jnp.where(kidx == k, jnp.sum(jnp.where(m, slot, 0.0), axis=0, keepdims=True), slots)
        w8 = jnp.where(kidx == k, jnp.sum(jnp.where(m, wd, 0.0), axis=0, keepdims=True), w8)
    slot_ref[...] = (slots * float(SUBLANES)).astype(jnp.int32)
    w8_ref[...] = w8


def _compact(w_dense, pos_dense, pstart_col):
    e, n_rows = w_dense.shape
    t = _pick_tile(n_rows, ROUTE_LANES, LANES)
    return pl.pallas_call(
        functools.partial(_compact_kernel, t=t),
        grid=(n_rows // t,),
        in_specs=[pl.BlockSpec((e, t), lambda i: (0, i)),
                  pl.BlockSpec((e, t), lambda i: (0, i)),
                  pl.BlockSpec((e, LANES), lambda i: (0, 0))],
        out_specs=[pl.BlockSpec((TOP_K, t), lambda i: (0, i)),
                   pl.BlockSpec((TOP_K, t), lambda i: (0, i))],
        out_shape=[jax.ShapeDtypeStruct((TOP_K, n_rows), jnp.int32),
                   jax.ShapeDtypeStruct((TOP_K, n_rows), F32)],
        compiler_params=_cparams(("parallel",)),
        name="compact",
    )(w_dense, pos_dense, pstart_col)


def _pack_bf16_pairs(x):
    half = x.shape[1] // 2
    lo = lax.bitcast_convert_type(x[:, :half].astype(BF16).astype(F32), jnp.uint32)
    hi = lax.bitcast_convert_type(x[:, half:].astype(BF16).astype(F32), jnp.uint32)
    return (lo >> 16) | hi


def _unpack_bf16_pairs(w):
    lo = lax.bitcast_convert_type(w << 16, F32)
    hi = lax.bitcast_convert_type(w & jnp.uint32(0xFFFF0000), F32)
    return lo, hi


def _store_rows_as_tiles(ref_at, x, rows):
    for c in range(SUBLANES):
        ref_at[pl.ds(c, rows, stride=SUBLANES), :] = x[:, c * LANES:(c + 1) * LANES]


def _load_tile_chunk(ref_at, c, rows):
    return ref_at[pl.ds(c, rows, stride=SUBLANES), :]


def _swiglu(x, w1, w3, w2):
    a = jnp.dot(x, w1, preferred_element_type=F32)
    g = jnp.dot(x, w3, preferred_element_type=F32)
    hmid = (a * jax.nn.sigmoid(a) * g).astype(BF16)
    return jnp.dot(hmid, w2, preferred_element_type=F32)


def _expert_kernel(be_ref, nu_ref, ids_ref, nids_ref, w1_ref, w3_ref, w2_ref, hp_ref, y_ref,
                   w1b_ref, w3b_ref, w2b_ref, xbuf0_ref, xbuf1_ref, sem):
    b = pl.program_id(0)
    n_used = nu_ref[0]
    xbuf = (xbuf0_ref, xbuf1_ref)
    rows = MOE_BLOCK

    def row_copy(s_ref, buf, r):
        src = hp_ref.at[pl.ds(pl.multiple_of(s_ref[r], SUBLANES), SUBLANES), :]
        return pltpu.make_async_copy(src, xbuf[buf].at[pl.ds(r * SUBLANES, SUBLANES), :], sem.at[buf])

    def wait_block(buf):
        pltpu.make_async_copy(hp_ref.at[pl.ds(0, rows * SUBLANES), :], xbuf[buf], sem.at[buf]).wait()

    @pl.when(b == 0)
    def _():
        def issue(r2, carry):
            for p in range(2):
                r = r2 * 2 + p
                src = hp_ref.at[pl.ds(pl.multiple_of(ids_ref[r], SUBLANES), SUBLANES), :]
                dst = xbuf0_ref.at[pl.ds(pl.multiple_of(r * SUBLANES, SUBLANES), SUBLANES), :]
                pltpu.make_async_copy(src, dst, sem.at[0]).start(priority=1)
            return carry

        lax.fori_loop(0, rows // 2, issue, 0)

    def run_block(cur):
        nxt = 1 - cur
        wait_block(cur)

        @pl.when((b == 0) | (be_ref[b] != be_ref[jnp.maximum(b - 1, 0)]))
        def _():
            w1b_ref[...] = w1_ref[...].astype(BF16)
            w3b_ref[...] = w3_ref[...].astype(BF16)
            w2b_ref[...] = w2_ref[...].astype(BF16)

        group = rows // 4

        def issue_group(g):
            for r in range(g * group, (g + 1) * group):
                row_copy(nids_ref, nxt, r).start(priority=1)

        issue_group(0)
        halves = [_unpack_bf16_pairs(_load_tile_chunk(xbuf[cur], c, rows)) for c in range(SUBLANES)]
        x = jnp.concatenate([lo.astype(BF16) for lo, _ in halves]
                            + [hi.astype(BF16) for _, hi in halves], axis=1)
        issue_group(1)
        a = jnp.dot(x, w1b_ref[...], preferred_element_type=F32)
        issue_group(2)
        g = jnp.dot(x, w3b_ref[...], preferred_element_type=F32)
        issue_group(3)
        hmid = (a * jax.nn.sigmoid(a) * g).astype(BF16)
        y = _pack_bf16_pairs(jnp.dot(hmid, w2b_ref[...], preferred_element_type=F32))
        _store_rows_as_tiles(y_ref, y, rows)

        @pl.when(b == n_used - 1)
        def _():
            wait_block(nxt)

    @pl.when(b < n_used)
    def _():
        for parity in range(2):
            @pl.when(b % 2 == parity)
            def _():
                run_block(parity)


def _experts(hp, ids_tiled, w1, w3, w2, layer, block_e, n_used):
    d, f = w1.shape[2], w1.shape[3]
    blk_rows = MOE_BLOCK * SUBLANES
    nb = ids_tiled.shape[0]

    def blk(b, be, nu):
        return jnp.minimum(b, nu[0] - 1)

    def wmap(b, be, nu):
        return (layer, be[blk(b, be, nu)], 0, 0)

    return pl.pallas_call(
        _expert_kernel,
        grid_spec=pltpu.PrefetchScalarGridSpec(
            num_scalar_prefetch=2,
            grid=(nb,),
            in_specs=[pl.BlockSpec((None, None, MOE_BLOCK), lambda b, be, nu: (blk(b, be, nu), 0, 0),
                                   memory_space=pltpu.SMEM),
                      pl.BlockSpec((None, None, MOE_BLOCK),
                                   lambda b, be, nu: (blk(b + 1, be, nu), 0, 0),
                                   memory_space=pltpu.SMEM),
                      pl.BlockSpec((None, None, d, f), wmap),
                      pl.BlockSpec((None, None, d, f), wmap),
                      pl.BlockSpec((None, None, f, d), wmap),
                      pl.BlockSpec(memory_space=pl.ANY)],
            out_specs=pl.BlockSpec((blk_rows, LANES), lambda b, be, nu: (blk(b, be, nu), 0)),
            scratch_shapes=[pltpu.VMEM((d, f), BF16), pltpu.VMEM((d, f), BF16),
                            pltpu.VMEM((f, d), BF16),
                            pltpu.VMEM((blk_rows, LANES), jnp.uint32),
                            pltpu.VMEM((blk_rows, LANES), jnp.uint32),
                            pltpu.SemaphoreType.DMA((2,))]),
        out_shape=jax.ShapeDtypeStruct((nb * blk_rows, LANES), jnp.uint32),
        compiler_params=_cparams(("arbitrary",)),
        name="experts",
    )(block_e, n_used, ids_tiled, ids_tiled, w1, w3, w2, hp)


def _inverse_rows(slot_flat, vals, n_slots):
    n_idx = slot_flat.shape[0]
    window = LANES
    mesh = plsc.VectorSubcoreMesh(core_axis_name="core", subcore_axis_name="subcore")

    @functools.partial(pl.kernel, out_type=jax.ShapeDtypeStruct((n_slots, LANES), jnp.int32),
                       mesh=mesh, scratch_types=[])
    def scatter_rows(x_hbm, i_hbm, o_hbm):
        def body(x_vmem, i_vmem):
            pltpu.sync_copy(x_vmem, o_hbm.at[i_vmem.at[0]])

        pltpu.emit_pipeline(
            body,
            grid=(n_idx // window,),
            in_specs=[pl.BlockSpec((window, LANES), lambda i: (i, 0)),
                      pl.BlockSpec((1, window), lambda i: (0, i))],
            out_specs=[],
            core_axis_name="subcore",
            dimension_semantics=(pltpu.PARALLEL,),
        )(x_hbm, i_hbm)

    return scatter_rows(vals, slot_flat.reshape(1, n_idx))


def _combine_kernel(slot_ref, nslot_ref, h_ref, w8_ref, ws1_ref, ws3_ref, ws2_ref, g_ref, b_ref,
                    ys_ref, o_ref, ob_ref, ybuf0_ref, ybuf1_ref, sem, *, t):
    i = pl.program_id(0)
    last = pl.num_programs(0) - 1
    ybuf = (ybuf0_ref, ybuf1_ref)

    def row_copy(s_ref, buf, k, tok):
        src = ys_ref.at[pl.ds(pl.multiple_of(s_ref[k * t + tok], SUBLANES), SUBLANES), :]
        return pltpu.make_async_copy(
            src, ybuf[buf].at[k, pl.ds(pl.multiple_of(tok * SUBLANES, SUBLANES), SUBLANES), :],
            sem.at[buf])

    def wait_tile(buf):
        for k in range(TOP_K):
            pltpu.make_async_copy(ys_ref.at[pl.ds(0, t * SUBLANES), :], ybuf[buf].at[k],
                                  sem.at[buf]).wait()

    @pl.when(i == 0)
    def _():
        def issue(tok, carry):
            for k in range(TOP_K):
                row_copy(slot_ref, 0, k, tok).start(priority=k % 2)
            return carry

        lax.fori_loop(0, t, issue, 0, unroll=2)

    group = t // TOP_K

    def reduce_tile(cur):
        nxt = 1 - cur

        def issue_group(g):
            for tok in range(g * group, (g + 1) * group):
                for k in range(TOP_K):
                    row_copy(nslot_ref, nxt, k, tok).start(priority=k % 2)

        wait_tile(cur)
        h = h_ref[...]
        issue_group(0)
        shared = _swiglu(h.astype(BF16), ws1_ref[...], ws3_ref[...], ws2_ref[...])
        w8 = w8_ref[...]
        wk = [w8[:, k:k + 1] for k in range(TOP_K)]
        acc_lo, acc_hi = [], []
        for c in range(SUBLANES):
            if c + 1 < TOP_K:
                issue_group(c + 1)
            lo_c, hi_c = None, None
            for k in range(TOP_K):
                lo, hi = _unpack_bf16_pairs(_load_tile_chunk(ybuf[cur].at[k], c, t))
                lo_c = lo * wk[k] if k == 0 else lo_c + lo * wk[k]
                hi_c = hi * wk[k] if k == 0 else hi_c + hi * wk[k]
            acc_lo.append(lo_c)
            acc_hi.append(hi_c)
        acc = jnp.concatenate(acc_lo + acc_hi, axis=1)
        y = DN_ALPHA * h + (acc + shared)
        out = _layer_norm(y, g_ref[...], b_ref[...])
        o_ref[...] = out
        ob_ref[...] = out.astype(BF16)

        @pl.when(i == last)
        def _():
            wait_tile(nxt)

    for parity in range(2):
        @pl.when(i % 2 == parity)
        def _():
            reduce_tile(parity)


def _combine(h, ys, slots_tiled, w8_t, ws1, ws3, ws2, g, b, t, out_rows, out_map):
    n_rows, d = h.shape
    f = ws1.shape[1]
    n_tiles = n_rows // t
    return pl.pallas_call(
        functools.partial(_combine_kernel, t=t),
        grid=(n_tiles,),
        in_specs=[pl.BlockSpec((None, None, TOP_K * t), lambda i: (i, 0, 0),
                               memory_space=pltpu.SMEM),
                  pl.BlockSpec((None, None, TOP_K * t),
                               lambda i: (jnp.minimum(i + 1, n_tiles - 1), 0, 0),
                               memory_space=pltpu.SMEM),
                  pl.BlockSpec((t, d), lambda i: (i, 0)),
                  pl.BlockSpec((t, TOP_K), lambda i: (i, 0)),
                  pl.BlockSpec((d, f), lambda i: (0, 0)),
                  pl.BlockSpec((d, f), lambda i: (0, 0)),
                  pl.BlockSpec((f, d), lambda i: (0, 0)),
                  pl.BlockSpec((1, d), lambda i: (0, 0)),
                  pl.BlockSpec((1, d), lambda i: (0, 0)),
                  pl.BlockSpec(memory_space=pl.ANY)],
        out_specs=[pl.BlockSpec((t, d), lambda i: (out_map(i), 0)),
                   pl.BlockSpec((t, d), lambda i: (i, 0))],
        out_shape=[jax.ShapeDtypeStruct((out_rows, d), F32),
                   jax.ShapeDtypeStruct((n_rows, d), BF16)],
        scratch_shapes=[pltpu.VMEM((TOP_K, t * SUBLANES, LANES), jnp.uint32),
                        pltpu.VMEM((TOP_K, t * SUBLANES, LANES), jnp.uint32),
                        pltpu.SemaphoreType.DMA((2,))],
        compiler_params=_cparams(("arbitrary",)),
        name="combine_ln",
    )(slots_tiled, slots_tiled, h, w8_t, ws1, ws3, ws2, g, b, ys)


def _moe_ln(h, hp, logits_t, router_bias, w1, w3, w2, layer, ws1, ws3, ws2, g, b, drop_rows,
            seq_rows):
    n_rows, d = h.shape
    E = N_EXPERTS
    assert d // 2 == SUBLANES * LANES, d
    bias_col = jnp.broadcast_to(router_bias.astype(F32)[:, None], (E, LANES))
    w_dense, pos_dense, cnt = _route(logits_t, bias_col)

    counts = cnt[:, 0].astype(jnp.int32)
    pcounts = (counts + MOE_BLOCK - 1) // MOE_BLOCK * MOE_BLOCK
    pends = jnp.cumsum(pcounts)
    pstarts = pends - pcounts
    n_blocks = -(-(n_rows * TOP_K) // MOE_BLOCK) + E
    n_slots = n_blocks * MOE_BLOCK
    block_lo = jnp.arange(n_blocks, dtype=jnp.int32) * MOE_BLOCK
    block_e = jnp.minimum(jnp.sum(pends[None, :] <= block_lo[:, None], axis=1), E - 1).astype(jnp.int32)
    n_used = (pends[-1:] // MOE_BLOCK).astype(jnp.int32)
    pstart_col = jnp.broadcast_to(pstarts.astype(F32)[:, None], (E, LANES))

    slots, w8 = _compact(w_dense, pos_dense, pstart_col)
    t = _pick_tile(n_rows, TOKEN_TILE, LANES)
    n_tiles = n_rows // t
    slots_tiled = slots.reshape(TOP_K, n_tiles, t).transpose(1, 0, 2).reshape(n_tiles, 1, TOP_K * t)
    tok_rows = jnp.tile(jnp.arange(n_rows, dtype=jnp.int32) * SUBLANES, TOP_K)
    inv = _inverse_rows((slots // SUBLANES).reshape(-1),
                        jnp.broadcast_to(tok_rows[:, None], (n_rows * TOP_K, LANES)), n_slots)
    slot_id = jnp.arange(n_slots, dtype=jnp.int32).reshape(n_blocks, MOE_BLOCK)
    used = slot_id < (pstarts + counts)[block_e][:, None]
    ids = jnp.where(used, jnp.clip(inv[:, 0].reshape(n_blocks, MOE_BLOCK) // SUBLANES, 0, n_rows - 1),
                    slot_id % n_rows) * SUBLANES
    ys = _experts(hp, ids.reshape(n_blocks, 1, MOE_BLOCK), w1, w3, w2, layer, block_e, n_used)
    if drop_rows == t:
        per_seq = seq_rows // t
        out_rows = n_rows - (n_rows // seq_rows) * t

        def out_map(i):
            return (i // per_seq) * (per_seq - 1) + jnp.maximum(i % per_seq - 1, 0)
    else:
        out_rows, out_map = n_rows, (lambda i: i)
    return _combine(h, ys, slots_tiled, w8.T, ws1, ws3, ws2, g, b, t, out_rows, out_map)


def kernel(x, meta_tokens, gla_w_in, gla_w_gate_up, gla_b_gate, gla_norm_g, gla_w_out,
           conv_w_in, conv_w, conv_w_out, ln1_g, ln1_b, router_w, router_bias,
           exp_w1, exp_w3, exp_w2, shared_w1, shared_w3, shared_w2, ln2_g, ln2_b):
    batch, seq, d = x.shape
    H = GLA_HEADS
    dk = d // 2 // H
    n_qkvr = 2 * H * dk + 2 * d
    rank = gla_w_in.shape[2] - n_qkvr
    seq_rows = -(-(N_META + seq) // ROW_ALIGN) * ROW_ALIGN
    pad_rows = seq_rows - N_META - seq
    n_rows = batch * seq_rows

    meta = jnp.broadcast_to(meta_tokens.astype(x.dtype)[None], (batch, N_META, d))
    h = jnp.concatenate([jnp.zeros((batch, pad_rows, d), x.dtype), meta, x], axis=1)
    h = h.reshape(n_rows, d)
    hb = h

    def row(v):
        return v.astype(F32)[None, :]

    for i in range(DEPTH):
        jm = i // 2
        if i % 2 == 0:
            w_in = gla_w_in[jm]
            w_all = jnp.pad(w_in, ((0, 0), (0, GATE_PAD - rank))).astype(BF16)
            wg_up = jnp.pad(gla_w_gate_up[jm], ((0, GATE_PAD - rank), (0, 0))).astype(BF16)
            qkvrg = _matmul(hb, w_all, BF16, IN_PROJ_ROWS, IN_PROJ_COLS, "gla_in_proj")
            mix = _gla(qkvrg, wg_up, row(gla_b_gate[jm]), row(gla_norm_g[jm]),
                       batch=batch, seq_rows=seq_rows, d_model=d)
            w_out = gla_w_out[jm].astype(BF16)
        else:
            mix = _conv_mix(hb, conv_w_in[jm].astype(BF16), conv_w[jm].astype(F32),
                            seq_rows=seq_rows, pad_rows=pad_rows)
            w_out = conv_w_out[jm].astype(BF16)
        h, logits_t, hp = _proj_ln_router(mix, w_out, h, row(ln1_g[i]), row(ln1_b[i]),
                                          router_w[i].T.astype(BF16))
        drop = pad_rows + N_META if i == DEPTH - 1 else 0
        h, hb = _moe_ln(h, hp, logits_t, router_bias[i], exp_w1, exp_w3, exp_w2, i,
                        shared_w1[i].astype(BF16), shared_w3[i].astype(BF16),
                        shared_w2[i].astype(BF16), row(ln2_g[i]), row(ln2_b[i]), drop, seq_rows)
    if h.shape[0] == batch * seq:
        return h.reshape(batch, seq, d)
    return h.reshape(batch, seq_rows, d)[:, pad_rows + N_META:]
```

```python
import functools

import jax
import jax.numpy as jnp
from jax import lax
from jax.experimental import pallas as pl
from jax.experimental.pallas import tpu as pltpu
from jax.experimental.pallas import tpu_sc as plsc

N_META = 16
GLA_HEADS = 4
GLA_GATE_TAU = 16.0
CONV_WIDTH = 3
N_EXPERTS = 64
TOP_K = 8
N_GROUPS = 8
TOPK_GROUPS = 4
ROUTED_SCALE = 2.5
LN_EPS = 1e-5
RMS_EPS = 1e-6
DEPTH = 2
DN_ALPHA = (2 * DEPTH) ** 0.25
LOG2_E = 1.4426950408889634

LANES = 128
SUBLANES = 8
ROW_ALIGN = 128
GLA_CHUNK = 64
GLA_SUB = 16
GLA_HEADS_PER_STEP = 4
MXU_WIDTH = 256
GATE_PAD = MXU_WIDTH
MOE_BLOCK = 512
TOKEN_TILE = 128
IN_PROJ_ROWS, IN_PROJ_COLS = 832, 1280
CONV_ROWS, CONV_COLS = 1024, 512
OUT_PROJ_ROWS = 512
GLA_ROWS = 640
ROUTE_LANES = 512
VMEM_LIMIT = 56 * 1024 * 1024

F32 = jnp.float32
BF16 = jnp.bfloat16


def _pick_tile(n, target, mult):
    best = None
    for t in range(mult, min(n, target) + 1, mult):
        if n % t == 0:
            best = t
    assert best is not None, (n, target, mult)
    return best


def _cparams(sem):
    return pltpu.CompilerParams(dimension_semantics=sem, vmem_limit_bytes=VMEM_LIMIT)


def _mm_kernel(x_ref, w_ref, o_ref):
    x = x_ref[...].astype(BF16)
    o_ref[...] = jnp.dot(x, w_ref[...], preferred_element_type=F32).astype(o_ref.dtype)


def _matmul(x, w, out_dtype, tm_target, tn_target, name):
    m, k = x.shape
    n = w.shape[1]
    tm = _pick_tile(m, tm_target, 16)
    tn = _pick_tile(n, tn_target, LANES)
    return pl.pallas_call(
        _mm_kernel,
        grid=(m // tm, n // tn),
        in_specs=[pl.BlockSpec((tm, k), lambda i, j: (i, 0)),
                  pl.BlockSpec((k, tn), lambda i, j: (0, j))],
        out_specs=pl.BlockSpec((tm, tn), lambda i, j: (i, j)),
        out_shape=jax.ShapeDtypeStruct((m, n), out_dtype),
        compiler_params=_cparams(("parallel", "arbitrary")),
        name=name,
    )(x, w)


def _gla_kernel(q_ref, k_ref, v_ref, r_ref, gl_ref, wg_ref, bg_ref, ng_ref, o_ref, st_ref,
                *, n_chunks, dk, dv, heads):
    C, S = GLA_CHUNK, GLA_SUB

    @pl.when(pl.program_id(2) == 0)
    def _():
        st_ref[...] = jnp.zeros_like(st_ref)

    ri = lax.broadcasted_iota(jnp.int32, (C, C), 0)
    ci = lax.broadcasted_iota(jnp.int32, (C, C), 1)
    tri = (ri >= ci).astype(BF16)
    rs = lax.broadcasted_iota(jnp.int32, (S, S), 0)
    cs = lax.broadcasted_iota(jnp.int32, (S, S), 1)
    nt = (((1,), (1,)), ((), ()))
    tn = (((0,), (0,)), ((), ()))
    half = S // 2

    def chunk(c, carry):
        r0 = pl.multiple_of(c * C, C)
        rows_c = pl.ds(r0, C)
        hs = range(heads)
        kcol = [pl.ds(hd * dk, dk) for hd in hs]
        vcol = [pl.ds(hd * dv, dv) for hd in hs]
        gl = gl_ref[rows_c, :]
        q = [q_ref[rows_c, kcol[hd]].astype(F32) * (dk ** -0.5) for hd in hs]
        kk = [k_ref[rows_c, kcol[hd]].astype(F32) for hd in hs]
        vv = [v_ref[rows_c, vcol[hd]] for hd in hs]
        st = [st_ref[hd] for hd in hs]
        z = [jnp.dot(gl, wg_ref[:, kcol[hd]], preferred_element_type=F32) + bg_ref[:, kcol[hd]]
             for hd in hs]
        b = []
        for hd in hs:
            la = ((jnp.minimum(z[hd], 0.0) - jnp.log(1.0 + jnp.exp(-jnp.abs(z[hd]))))
                  * (1.0 / GLA_GATE_TAU))
            h1 = la.astype(BF16)
            e1 = la - h1.astype(F32)
            h2 = e1.astype(BF16)
            h3 = (e1 - h2.astype(F32)).astype(BF16)
            b3 = jnp.dot(tri, jnp.concatenate([h1, h2, h3], axis=1), preferred_element_type=F32)
            b.append((b3[:, :dk] + b3[:, dk:2 * dk] + b3[:, 2 * dk:]) * LOG2_E)

        o_inter, st_new, off = [], [], []
        for hd in hs:
            qe = (q[hd] * jnp.exp2(b[hd])).astype(BF16)
            o_inter.append(lax.dot_general(qe, st[hd].astype(BF16), nt, preferred_element_type=F32))
            bl = b[hd][C - 1:C]
            khat = (kk[hd] * jnp.exp2(bl - b[hd])).astype(BF16)
            upd = lax.dot_general(vv[hd], khat, tn, preferred_element_type=F32)
            st_new.append(st[hd] * jnp.exp2(bl) + upd)
            offs = []
            for i in range(1, C // S):
                lo = i * S
                bref = b[hd][lo - 1:lo]
                qt = (q[hd][lo:lo + S] * jnp.exp2(b[hd][lo:lo + S] - bref)).astype(BF16)
                kt = (kk[hd][:lo] * jnp.exp2(bref - b[hd][:lo])).astype(BF16)
                offs.append(lax.dot_general(qt, kt, nt, preferred_element_type=F32))
            off.append(offs)

        dmats = []
        for hd in hs:
            blocks = []
            for i in range(C // S):
                lo = i * S
                qs = q[hd][lo:lo + S]
                bs = b[hd][lo:lo + S]
                dmat = jnp.zeros((S, S), F32)
                for j in range(S):
                    kj = kk[hd][lo + j:lo + j + 1]
                    bj = b[hd][lo + j:lo + j + 1]
                    if j < half:
                        p = qs * kj * jnp.exp2(bs - bj)
                        a = jnp.sum(p, axis=-1, keepdims=True)
                    else:
                        p = qs[half:] * kj * jnp.exp2(bs[half:] - bj)
                        a = jnp.concatenate([jnp.zeros((half, 1), F32),
                                             jnp.sum(p, axis=-1, keepdims=True)], axis=0)
                    dmat = jnp.where(cs == j, a, dmat)
                blocks.append(jnp.where(rs >= cs, dmat, 0.0).astype(BF16))
            dmats.append(blocks)

        for hd in hs:
            rows = []
            for i in range(C // S):
                lo = i * S
                o_i = jnp.dot(dmats[hd][i], vv[hd][lo:lo + S], preferred_element_type=F32)
                if i > 0:
                    o_i = o_i + jnp.dot(off[hd][i - 1].astype(BF16), vv[hd][:lo],
                                        preferred_element_type=F32)
                rows.append(o_i)
            o = o_inter[hd] + jnp.concatenate(rows, axis=0)
            ms = jnp.mean(o * o, axis=-1, keepdims=True)
            r = r_ref[rows_c, vcol[hd]].astype(F32)
            y = o * lax.rsqrt(ms + RMS_EPS) * ng_ref[:, vcol[hd]] * (r * jax.nn.sigmoid(r))
            st_ref[hd] = st_new[hd]
            o_ref[rows_c, vcol[hd]] = y.astype(o_ref.dtype)
        return carry

    lax.fori_loop(0, n_chunks, chunk, 0)


def _gla(qkvrg, wg, bg, ng, *, batch, seq_rows, d_model):
    H, hps = GLA_HEADS, GLA_HEADS_PER_STEP
    dk = d_model // 2 // H
    dv = d_model // H
    n_rows = batch * seq_rows
    rblk = _pick_tile(seq_rows, GLA_ROWS, GLA_CHUNK)
    steps = seq_rows // rblk
    groups = H // hps
    off_k, off_v, off_r = groups, (2 * H * dk) // (hps * dv), (2 * H * dk) // (hps * dv) + groups

    off_g = (2 * H * dk + 2 * H * dv) // GATE_PAD

    def rowmap(off):
        return lambda b, h, i: (b * steps + i, off + h)

    return pl.pallas_call(
        functools.partial(_gla_kernel, n_chunks=rblk // GLA_CHUNK, dk=dk, dv=dv, heads=hps),
        grid=(batch, groups, steps),
        in_specs=[pl.BlockSpec((rblk, hps * dk), rowmap(0)),
                  pl.BlockSpec((rblk, hps * dk), rowmap(off_k)),
                  pl.BlockSpec((rblk, hps * dv), rowmap(off_v)),
                  pl.BlockSpec((rblk, hps * dv), rowmap(off_r)),
                  pl.BlockSpec((rblk, GATE_PAD), lambda b, h, i: (b * steps + i, off_g)),
                  pl.BlockSpec((GATE_PAD, hps * dk), lambda b, h, i: (0, h)),
                  pl.BlockSpec((1, hps * dk), lambda b, h, i: (0, h)),
                  pl.BlockSpec((1, hps * dv), lambda b, h, i: (0, h))],
        out_specs=pl.BlockSpec((rblk, hps * dv), lambda b, h, i: (b * steps + i, h)),
        out_shape=jax.ShapeDtypeStruct((n_rows, H * dv), BF16),
        scratch_shapes=[pltpu.VMEM((hps, dv, dk), F32)],
        compiler_params=_cparams(("parallel", "parallel", "arbitrary")),
        name="gla_chunks",
    )(qkvrg, qkvrg, qkvrg, qkvrg, qkvrg, wg, bg, ng)


def _conv_kernel(x_ref, wb_ref, wc_ref, wh_ref, cw_ref, o_ref, ubuf_ref, carry_ref,
                 *, seq_rows, pad_rows, tm):
    i = pl.program_id(0)
    j = pl.program_id(1)
    x = x_ref[...]
    bg = jnp.dot(x, wb_ref[...], preferred_element_type=F32)
    cg = jnp.dot(x, wc_ref[...], preferred_element_type=F32)
    hh = jnp.dot(x, wh_ref[...], preferred_element_type=F32)
    row = i * tm + lax.broadcasted_iota(jnp.int32, (tm, 1), 0)
    u = jnp.where(row % seq_rows >= pad_rows, cg * hh, 0.0)

    @pl.when(i == 0)
    def _():
        carry_ref[j] = jnp.zeros(carry_ref.shape[1:], F32)

    ubuf_ref[pl.ds(0, SUBLANES), :] = carry_ref[j]
    ubuf_ref[pl.ds(SUBLANES, tm), :] = u
    carry_ref[j] = u[tm - SUBLANES:]
    cw = cw_ref[...]
    conv = u * cw[CONV_WIDTH - 1:CONV_WIDTH]
    for s in range(1, CONV_WIDTH):
        conv = conv + ubuf_ref[pl.ds(SUBLANES - s, tm), :] * cw[CONV_WIDTH - 1 - s:CONV_WIDTH - s]
    o_ref[...] = (bg * conv).astype(o_ref.dtype)


def _conv_mix(xb, w_in, conv_w, *, seq_rows, pad_rows):
    n_rows, d = xb.shape
    tm = _pick_tile(n_rows, CONV_ROWS, SUBLANES)
    tn = _pick_tile(d, CONV_COLS, LANES)
    nj = d // tn
    return pl.pallas_call(
        functools.partial(_conv_kernel, seq_rows=seq_rows, pad_rows=pad_rows, tm=tm),
        grid=(n_rows // tm, nj),
        in_specs=[pl.BlockSpec((tm, d), lambda i, j: (i, 0)),
                  pl.BlockSpec((d, tn), lambda i, j: (0, j)),
                  pl.BlockSpec((d, tn), lambda i, j: (0, nj + j)),
                  pl.BlockSpec((d, tn), lambda i, j: (0, 2 * nj + j)),
                  pl.BlockSpec((CONV_WIDTH, tn), lambda i, j: (0, j))],
        out_specs=pl.BlockSpec((tm, tn), lambda i, j: (i, j)),
        out_shape=jax.ShapeDtypeStruct((n_rows, d), BF16),
        scratch_shapes=[pltpu.VMEM((tm + SUBLANES, tn), F32),
                        pltpu.VMEM((nj, SUBLANES, tn), F32)],
        compiler_params=_cparams(("arbitrary", "arbitrary")),
        name="conv_mix",
    )(xb, w_in, w_in, w_in, conv_w)


def _layer_norm(y, g, b):
    mu = jnp.mean(y, axis=-1, keepdims=True)
    yc = y - mu
    var = jnp.mean(yc * yc, axis=-1, keepdims=True)
    return yc * lax.rsqrt(var + LN_EPS) * g + b


def _proj_ln_kernel(a_ref, w_ref, h_ref, g_ref, b_ref, rw_ref, o_ref, lg_ref, hp_ref):
    y = jnp.dot(a_ref[...], w_ref[...], preferred_element_type=F32) + DN_ALPHA * h_ref[...]
    h = _layer_norm(y, g_ref[...], b_ref[...])
    o_ref[...] = h
    _store_rows_as_tiles(hp_ref, _pack_bf16_pairs(h), h.shape[0])
    lg_ref[...] = lax.dot_general(rw_ref[...], h.astype(BF16), (((1,), (1,)), ((), ())),
                                  preferred_element_type=F32)


def _proj_ln_router(a, w, h, g, b, rw_t):
    n_rows, kin = a.shape
    d = w.shape[1]
    e = rw_t.shape[0]
    tm = _pick_tile(n_rows, OUT_PROJ_ROWS, LANES)
    return pl.pallas_call(
        _proj_ln_kernel,
        grid=(n_rows // tm,),
        in_specs=[pl.BlockSpec((tm, kin), lambda i: (i, 0)),
                  pl.BlockSpec((kin, d), lambda i: (0, 0)),
                  pl.BlockSpec((tm, d), lambda i: (i, 0)),
                  pl.BlockSpec((1, d), lambda i: (0, 0)),
                  pl.BlockSpec((1, d), lambda i: (0, 0)),
                  pl.BlockSpec((e, d), lambda i: (0, 0))],
        out_specs=[pl.BlockSpec((tm, d), lambda i: (i, 0)),
                   pl.BlockSpec((e, tm), lambda i: (0, i)),
                   pl.BlockSpec((tm * SUBLANES, LANES), lambda i: (i, 0))],
        out_shape=[jax.ShapeDtypeStruct((n_rows, d), F32),
                   jax.ShapeDtypeStruct((e, n_rows), F32),
                   jax.ShapeDtypeStruct((n_rows * SUBLANES, LANES), jnp.uint32)],
        compiler_params=_cparams(("parallel",)),
        name="proj_ln_router",
    )(a, w, h, g, b, rw_t)


def _beats(cand, cand_idx, ref, ref_idx):
    return (cand > ref) | ((cand == ref) & (cand_idx < ref_idx))


def _route_kernel(lg_ref, bias_ref, w_ref, pos_ref, cnt_ref, run_ref, *, t, seq_rows, pad_rows):
    E, G = N_EXPERTS, N_GROUPS
    gs = E // G

    @pl.when(pl.program_id(0) == 0)
    def _():
        run_ref[...] = jnp.zeros_like(run_ref)

    s = jax.nn.sigmoid(lg_ref[...])
    c = s + bias_ref[...][:, :1]
    sub = lax.broadcasted_iota(jnp.int32, (gs, t), 0)

    grp_rows = []
    for g in range(G):
        cg = c[g * gs:(g + 1) * gs]
        rank = jnp.zeros((gs, t), jnp.int32)
        for m in range(gs):
            rank = rank + _beats(cg[m:m + 1], m, cg, sub).astype(jnp.int32)
        grp_rows.append(jnp.sum(jnp.where(rank < 2, cg, 0.0), axis=0, keepdims=True))
    gidx = lax.broadcasted_iota(jnp.int32, (G, t), 0)
    gscore = jnp.zeros((G, t), F32)
    for g in range(G):
        gscore = jnp.where(gidx == g, grp_rows[g], gscore)
    grank = jnp.zeros((G, t), jnp.int32)
    for m in range(G):
        grank = grank + _beats(gscore[m:m + 1], m, gscore, gidx).astype(jnp.int32)
    gsel = grank < TOPK_GROUPS

    masked = jnp.concatenate(
        [jnp.where(gsel[g:g + 1], c[g * gs:(g + 1) * gs], -jnp.inf) for g in range(G)], axis=0)
    eidx = lax.broadcasted_iota(jnp.int32, (E, t), 0).astype(F32)
    sel = jnp.zeros((E, t), jnp.bool_)
    for _ in range(TOP_K):
        best = jnp.max(masked, axis=0, keepdims=True)
        pick = jnp.min(jnp.where((masked == best) & jnp.logical_not(sel), eidx, float(E)),
                       axis=0, keepdims=True)
        hit = eidx == pick
        sel = sel | hit
        masked = jnp.where(hit, -jnp.inf, masked)
    tok = pl.program_id(0) * t + lax.broadcasted_iota(jnp.int32, (1, t), 1)
    real = tok % seq_rows >= pad_rows
    sel = sel & real
    gate = jnp.where(sel, s, 0.0)
    total = jnp.where(real, jnp.sum(gate, axis=0, keepdims=True), 1.0)
    w_ref[...] = gate / total * ROUTED_SCALE

    li = lax.broadcasted_iota(jnp.int32, (t, t), 0)
    lj = lax.broadcasted_iota(jnp.int32, (t, t), 1)
    upper = (li <= lj).astype(BF16)
    self_ = sel.astype(F32)
    incl = jnp.dot(sel.astype(BF16), upper, preferred_element_type=F32)
    run = run_ref[...][:, :1]
    pos_ref[...] = jnp.where(sel, run + incl - self_, -1.0)
    run_new = run + jnp.sum(self_, axis=1, keepdims=True)
    run_ref[...] = jnp.broadcast_to(run_new, run_ref.shape)
    cnt_ref[...] = jnp.broadcast_to(run_new, cnt_ref.shape)


def _route(logits_t, bias_col, seq_rows, pad_rows):
    e, n_rows = logits_t.shape
    t = _pick_tile(n_rows, ROUTE_LANES, LANES)
    return pl.pallas_call(
        functools.partial(_route_kernel, t=t, seq_rows=seq_rows, pad_rows=pad_rows),
        grid=(n_rows // t,),
        in_specs=[pl.BlockSpec((e, t), lambda i: (0, i)),
                  pl.BlockSpec((e, LANES), lambda i: (0, 0))],
        out_specs=[pl.BlockSpec((e, t), lambda i: (0, i)),
                   pl.BlockSpec((e, t), lambda i: (0, i)),
                   pl.BlockSpec((e, LANES), lambda i: (0, 0))],
        out_shape=[jax.ShapeDtypeStruct((e, n_rows), F32),
                   jax.ShapeDtypeStruct((e, n_rows), F32),
                   jax.ShapeDtypeStruct((e, LANES), F32)],
        scratch_shapes=[pltpu.VMEM((e, LANES), F32)],
        compiler_params=_cparams(("arbitrary",)),
        name="route",
    )(logits_t, bias_col)


def _compact_kernel(w_ref, pos_ref, pst_ref, slot_ref, w8_ref, *, t):
    E = N_EXPERTS
    pos = pos_ref[...]
    sel = pos >= 0.0
    ri = lax.broadcasted_iota(jnp.int32, (E, E), 0)
    ci = lax.broadcasted_iota(jnp.int32, (E, E), 1)
    below = (ci < ri).astype(BF16)
    order = jnp.dot(below, sel.astype(BF16), preferred_element_type=F32)
    slot = pst_ref[...][:, :1] + pos
    wd = w_ref[...]
    kidx = lax.broadcasted_iota(jnp.int32, (TOP_K, t), 0)
    slots = jnp.zeros((TOP_K, t), F32)
    w8 = jnp.zeros((TOP_K, t), F32)
    for k in range(TOP_K):
        m = sel & (order == float(k))
        slots = jnp.where(kidx == k, jnp.sum(jnp.where(m, slot, 0.0), axis=0, keepdims=True), slots)
        w8 = jnp.where(kidx == k, jnp.sum(jnp.where(m, wd, 0.0), axis=0, keepdims=True), w8)
    slot_ref[...] = (slots * float(SUBLANES)).astype(jnp.int32)
    w8_ref[...] = w8


def _compact(w_dense, pos_dense, pstart_col):
    e, n_rows = w_dense.shape
    t = _pick_tile(n_rows, ROUTE_LANES, LANES)
    return pl.pallas_call(
        functools.partial(_compact_kernel, t=t),
        grid=(n_rows // t,),
        in_specs=[pl.BlockSpec((e, t), lambda i: (0, i)),
                  pl.BlockSpec((e, t), lambda i: (0, i)),
                  pl.BlockSpec((e, LANES), lambda i: (0, 0))],
        out_specs=[pl.BlockSpec((TOP_K, t), lambda i: (0, i)),
                   pl.BlockSpec((TOP_K, t), lambda i: (0, i))],
        out_shape=[jax.ShapeDtypeStruct((TOP_K, n_rows), jnp.int32),
                   jax.ShapeDtypeStruct((TOP_K, n_rows), F32)],
        compiler_params=_cparams(("parallel",)),
        name="compact",
    )(w_dense, pos_dense, pstart_col)


def _pack_bf16_pairs(x):
    half = x.shape[1] // 2
    lo = lax.bitcast_convert_type(x[:, :half].astype(BF16).astype(F32), jnp.uint32)
    hi = lax.bitcast_convert_type(x[:, half:].astype(BF16).astype(F32), jnp.uint32)
    return (lo >> 16) | hi


def _unpack_bf16_pairs(w):
    lo = lax.bitcast_convert_type(w << 16, F32)
    hi = lax.bitcast_convert_type(w & jnp.uint32(0xFFFF0000), F32)
    return lo, hi


def _store_rows_as_tiles(ref_at, x, rows):
    for c in range(SUBLANES):
        ref_at[pl.ds(c, rows, stride=SUBLANES), :] = x[:, c * LANES:(c + 1) * LANES]


def _load_tile_chunk(ref_at, c, rows):
    return ref_at[pl.ds(c, rows, stride=SUBLANES), :]


def _swiglu(x, w1, w3, w2):
    a = jnp.dot(x, w1, preferred_element_type=F32)
    g = jnp.dot(x, w3, preferred_element_type=F32)
    hmid = (a * jax.nn.sigmoid(a) * g).astype(BF16)
    return jnp.dot(hmid, w2, preferred_element_type=F32)


def _expert_kernel(be_ref, nu_ref, ids_ref, nids_ref, w1_ref, w3_ref, w2_ref, hp_ref, y_ref,
                   w1b_ref, w3b_ref, w2b_ref, xbuf0_ref, xbuf1_ref, sem):
    b = pl.program_id(0)
    n_used = nu_ref[0]
    xbuf = (xbuf0_ref, xbuf1_ref)
    rows = MOE_BLOCK

    def row_copy(s_ref, buf, r):
        src = hp_ref.at[pl.ds(pl.multiple_of(s_ref[r], SUBLANES), SUBLANES), :]
        return pltpu.make_async_copy(src, xbuf[buf].at[pl.ds(r * SUBLANES, SUBLANES), :], sem.at[buf])

    def wait_block(buf):
        pltpu.make_async_copy(hp_ref.at[pl.ds(0, rows * SUBLANES), :], xbuf[buf], sem.at[buf]).wait()

    @pl.when(b == 0)
    def _():
        def issue(r2, carry):
            for p in range(2):
                r = r2 * 2 + p
                src = hp_ref.at[pl.ds(pl.multiple_of(ids_ref[r], SUBLANES), SUBLANES), :]
                dst = xbuf0_ref.at[pl.ds(pl.multiple_of(r * SUBLANES, SUBLANES), SUBLANES), :]
                pltpu.make_async_copy(src, dst, sem.at[0]).start(priority=1)
            return carry

        lax.fori_loop(0, rows // 2, issue, 0)

    def run_block(cur):
        nxt = 1 - cur
        wait_block(cur)

        @pl.when((b == 0) | (be_ref[b] != be_ref[jnp.maximum(b - 1, 0)]))
        def _():
            w1b_ref[...] = w1_ref[...].astype(BF16)
            w3b_ref[...] = w3_ref[...].astype(BF16)
            w2b_ref[...] = w2_ref[...].astype(BF16)

        group = rows // 4

        def issue_group(g):
            for r in range(g * group, (g + 1) * group):
                row_copy(nids_ref, nxt, r).start(priority=1)

        issue_group(0)
        halves = [_unpack_bf16_pairs(_load_tile_chunk(xbuf[cur], c, rows)) for c in range(SUBLANES)]
        x = jnp.concatenate([lo.astype(BF16) for lo, _ in halves]
                            + [hi.astype(BF16) for _, hi in halves], axis=1)
        issue_group(1)
        a = jnp.dot(x, w1b_ref[...], preferred_element_type=F32)
        issue_group(2)
        g = jnp.dot(x, w3b_ref[...], preferred_element_type=F32)
        issue_group(3)
        hmid = (a * jax.nn.sigmoid(a) * g).astype(BF16)
        y = _pack_bf16_pairs(jnp.dot(hmid, w2b_ref[...], preferred_element_type=F32))
        _store_rows_as_tiles(y_ref, y, rows)

        @pl.when(b == n_used - 1)
        def _():
            wait_block(nxt)

    @pl.when(b < n_used)
    def _():
        for parity in range(2):
            @pl.when(b % 2 == parity)
            def _():
                run_block(parity)


def _experts(hp, ids_tiled, w1, w3, w2, layer, block_e, n_used):
    d, f = w1.shape[2], w1.shape[3]
    blk_rows = MOE_BLOCK * SUBLANES
    nb = ids_tiled.shape[0]

    def blk(b, be, nu):
        return jnp.minimum(b, nu[0] - 1)

    def wmap(b, be, nu):
        return (layer, be[blk(b, be, nu)], 0, 0)

    return pl.pallas_call(
        _expert_kernel,
        grid_spec=pltpu.PrefetchScalarGridSpec(
            num_scalar_prefetch=2,
            grid=(nb,),
            in_specs=[pl.BlockSpec((None, None, MOE_BLOCK), lambda b, be, nu: (blk(b, be, nu), 0, 0),
                                   memory_space=pltpu.SMEM),
                      pl.BlockSpec((None, None, MOE_BLOCK),
                                   lambda b, be, nu: (blk(b + 1, be, nu), 0, 0),
                                   memory_space=pltpu.SMEM),
                      pl.BlockSpec((None, None, d, f), wmap),
                      pl.BlockSpec((None, None, d, f), wmap),
                      pl.BlockSpec((None, None, f, d), wmap),
                      pl.BlockSpec(memory_space=pl.ANY)],
            out_specs=pl.BlockSpec((blk_rows, LANES), lambda b, be, nu: (blk(b, be, nu), 0)),
            scratch_shapes=[pltpu.VMEM((d, f), BF16), pltpu.VMEM((d, f), BF16),
                            pltpu.VMEM((f, d), BF16),
                            pltpu.VMEM((blk_rows, LANES), jnp.uint32),
                            pltpu.VMEM((blk_rows, LANES), jnp.uint32),
                            pltpu.SemaphoreType.DMA((2,))]),
        out_shape=jax.ShapeDtypeStruct((nb * blk_rows, LANES), jnp.uint32),
        compiler_params=_cparams(("arbitrary",)),
        name="experts",
    )(block_e, n_used, ids_tiled, ids_tiled, w1, w3, w2, hp)


def _inverse_rows(slot_flat, vals, n_slots):
    n_idx = slot_flat.shape[0]
    window = LANES
    mesh = plsc.VectorSubcoreMesh(core_axis_name="core", subcore_axis_name="subcore")

    @functools.partial(pl.kernel, out_type=jax.ShapeDtypeStruct((n_slots, LANES), jnp.int32),
                       mesh=mesh, scratch_types=[])
    def scatter_rows(x_hbm, i_hbm, o_hbm):
        def body(x_vmem, i_vmem):
            pltpu.sync_copy(x_vmem, o_hbm.at[i_vmem.at[0]])

        pltpu.emit_pipeline(
            body,
            grid=(n_idx // window,),
            in_specs=[pl.BlockSpec((window, LANES), lambda i: (i, 0)),
                      pl.BlockSpec((1, window), lambda i: (0, i))],
            out_specs=[],
            core_axis_name="subcore",
            dimension_semantics=(pltpu.PARALLEL,),
        )(x_hbm, i_hbm)

    return scatter_rows(vals, slot_flat.reshape(1, n_idx))


def _combine_kernel(slot_ref, nslot_ref, h_ref, w8_ref, ws1_ref, ws3_ref, ws2_ref, g_ref, b_ref,
                    ys_ref, o_ref, ob_ref, ybuf0_ref, ybuf1_ref, sem, *, t):
    i = pl.program_id(0)
    last = pl.num_programs(0) - 1
    ybuf = (ybuf0_ref, ybuf1_ref)

    def row_copy(s_ref, buf, k, tok):
        src = ys_ref.at[pl.ds(pl.multiple_of(s_ref[k * t + tok], SUBLANES), SUBLANES), :]
        return pltpu.make_async_copy(
            src, ybuf[buf].at[k, pl.ds(pl.multiple_of(tok * SUBLANES, SUBLANES), SUBLANES), :],
            sem.at[buf])

    def wait_tile(buf):
        for k in range(TOP_K):
            pltpu.make_async_copy(ys_ref.at[pl.ds(0, t * SUBLANES), :], ybuf[buf].at[k],
                                  sem.at[buf]).wait()

    @pl.when(i == 0)
    def _():
        def issue(tok, carry):
            for k in range(TOP_K):
                row_copy(slot_ref, 0, k, tok).start(priority=k % 2)
            return carry

        lax.fori_loop(0, t, issue, 0, unroll=2)

    group = t // TOP_K

    def reduce_tile(cur):
        nxt = 1 - cur

        def issue_group(g):
            for tok in range(g * group, (g + 1) * group):
                for k in range(TOP_K):
                    row_copy(nslot_ref, nxt, k, tok).start(priority=k % 2)

        wait_tile(cur)
        h = h_ref[...]
        issue_group(0)
        shared = _swiglu(h.astype(BF16), ws1_ref[...], ws3_ref[...], ws2_ref[...])
        w8 = w8_ref[...]
        wk = [w8[:, k:k + 1] for k in range(TOP_K)]
        acc_lo, acc_hi = [], []
        for c in range(SUBLANES):
            if c + 1 < TOP_K:
                issue_group(c + 1)
            lo_c, hi_c = None, None
            for k in range(TOP_K):
                lo, hi = _unpack_bf16_pairs(_load_tile_chunk(ybuf[cur].at[k], c, t))
                lo_c = lo * wk[k] if k == 0 else lo_c + lo * wk[k]
                hi_c = hi * wk[k] if k == 0 else hi_c + hi * wk[k]
            acc_lo.append(lo_c)
            acc_hi.append(hi_c)
        acc = jnp.concatenate(acc_lo + acc_hi, axis=1)
        y = DN_ALPHA * h + (acc + shared)
        out = _layer_norm(y, g_ref[...], b_ref[...])
        o_ref[...] = out
        ob_ref[...] = out.astype(BF16)

        @pl.when(i == last)
        def _():
            wait_tile(nxt)

    for parity in range(2):
        @pl.when(i % 2 == parity)
        def _():
            reduce_tile(parity)


def _combine(h, ys, slots_tiled, w8_t, ws1, ws3, ws2, g, b, t, out_rows, out_map):
    n_rows, d = h.shape
    f = ws1.shape[1]
    n_tiles = n_rows // t
    return pl.pallas_call(
        functools.partial(_combine_kernel, t=t),
        grid=(n_tiles,),
        in_specs=[pl.BlockSpec((None, None, TOP_K * t), lambda i: (i, 0, 0),
                               memory_space=pltpu.SMEM),
                  pl.BlockSpec((None, None, TOP_K * t),
                               lambda i: (jnp.minimum(i + 1, n_tiles - 1), 0, 0),
                               memory_space=pltpu.SMEM),
                  pl.BlockSpec((t, d), lambda i: (i, 0)),
                  pl.BlockSpec((t, TOP_K), lambda i: (i, 0)),
                  pl.BlockSpec((d, f), lambda i: (0, 0)),
                  pl.BlockSpec((d, f), lambda i: (0, 0)),
                  pl.BlockSpec((f, d), lambda i: (0, 0)),
                  pl.BlockSpec((1, d), lambda i: (0, 0)),
                  pl.BlockSpec((1, d), lambda i: (0, 0)),
                  pl.BlockSpec(memory_space=pl.ANY)],
        out_specs=[pl.BlockSpec((t, d), lambda i: (out_map(i), 0)),
                   pl.BlockSpec((t, d), lambda i: (i, 0))],
        out_shape=[jax.ShapeDtypeStruct((out_rows, d), F32),
                   jax.ShapeDtypeStruct((n_rows, d), BF16)],
        scratch_shapes=[pltpu.VMEM((TOP_K, t * SUBLANES, LANES), jnp.uint32),
                        pltpu.VMEM((TOP_K, t * SUBLANES, LANES), jnp.uint32),
                        pltpu.SemaphoreType.DMA((2,))],
        compiler_params=_cparams(("arbitrary",)),
        name="combine_ln",
    )(slots_tiled, slots_tiled, h, w8_t, ws1, ws3, ws2, g, b, ys)


def _moe_ln(h, hp, logits_t, router_bias, w1, w3, w2, layer, ws1, ws3, ws2, g, b, drop_rows,
            seq_rows, pad_rows):
    n_rows, d = h.shape
    E = N_EXPERTS
    assert d // 2 == SUBLANES * LANES, d
    bias_col = jnp.broadcast_to(router_bias.astype(F32)[:, None], (E, LANES))
    w_dense, pos_dense, cnt = _route(logits_t, bias_col, seq_rows, pad_rows)

    counts = cnt[:, 0].astype(jnp.int32)
    pcounts = (counts + MOE_BLOCK - 1) // MOE_BLOCK * MOE_BLOCK
    pends = jnp.cumsum(pcounts)
    pstarts = pends - pcounts
    n_blocks = -(-(n_rows * TOP_K) // MOE_BLOCK) + E
    n_slots = n_blocks * MOE_BLOCK
    block_lo = jnp.arange(n_blocks, dtype=jnp.int32) * MOE_BLOCK
    block_e = jnp.minimum(jnp.sum(pends[None, :] <= block_lo[:, None], axis=1), E - 1).astype(jnp.int32)
    n_used = (pends[-1:] // MOE_BLOCK).astype(jnp.int32)
    pstart_col = jnp.broadcast_to(pstarts.astype(F32)[:, None], (E, LANES))

    slots, w8 = _compact(w_dense, pos_dense, pstart_col)
    t = _pick_tile(n_rows, TOKEN_TILE, LANES)
    n_tiles = n_rows // t
    tok = jnp.arange(n_rows, dtype=jnp.int32)
    real = tok % seq_rows >= pad_rows
    n_pad = (n_rows // seq_rows) * pad_rows
    pad_id = (tok // seq_rows) * pad_rows + tok % seq_rows
    k_id = jnp.arange(TOP_K, dtype=jnp.int32)[:, None]
    slots = jnp.where(real[None, :], slots, ((pad_id[None, :] * TOP_K + k_id) % MOE_BLOCK) * SUBLANES)
    slots_tiled = slots.reshape(TOP_K, n_tiles, t).transpose(1, 0, 2).reshape(n_tiles, 1, TOP_K * t)
    tok_rows = jnp.tile(tok * SUBLANES, TOP_K)
    scatter_to = jnp.where(real[None, :], slots // SUBLANES, n_slots + k_id * max(n_pad, 1) + pad_id[None, :])
    inv = _inverse_rows(scatter_to.reshape(-1),
                        jnp.broadcast_to(tok_rows[:, None], (n_rows * TOP_K, LANES)),
                        n_slots + TOP_K * max(n_pad, 1))[:n_slots]
    slot_id = jnp.arange(n_slots, dtype=jnp.int32).reshape(n_blocks, MOE_BLOCK)
    used = slot_id < (pstarts + counts)[block_e][:, None]
    ids = jnp.where(used, jnp.clip(inv[:, 0].reshape(n_blocks, MOE_BLOCK) // SUBLANES, 0, n_rows - 1),
                    slot_id % n_rows) * SUBLANES
    ys = _experts(hp, ids.reshape(n_blocks, 1, MOE_BLOCK), w1, w3, w2, layer, block_e, n_used)
    if drop_rows == t:
        per_seq = seq_rows // t
        out_rows = n_rows - (n_rows // seq_rows) * t

        def out_map(i):
            return (i // per_seq) * (per_seq - 1) + jnp.maximum(i % per_seq - 1, 0)
    else:
        out_rows, out_map = n_rows, (lambda i: i)
    return _combine(h, ys, slots_tiled, w8.T, ws1, ws3, ws2, g, b, t, out_rows, out_map)


def kernel(x, meta_tokens, gla_w_in, gla_w_gate_up, gla_b_gate, gla_norm_g, gla_w_out,
           conv_w_in, conv_w, conv_w_out, ln1_g, ln1_b, router_w, router_bias,
           exp_w1, exp_w3, exp_w2, shared_w1, shared_w3, shared_w2, ln2_g, ln2_b):
    batch, seq, d = x.shape
    H = GLA_HEADS
    dk = d // 2 // H
    n_qkvr = 2 * H * dk + 2 * d
    rank = gla_w_in.shape[2] - n_qkvr
    seq_rows = -(-(N_META + seq) // ROW_ALIGN) * ROW_ALIGN
    pad_rows = seq_rows - N_META - seq
    n_rows = batch * seq_rows

    meta = jnp.broadcast_to(meta_tokens.astype(x.dtype)[None], (batch, N_META, d))
    h = jnp.concatenate([jnp.zeros((batch, pad_rows, d), x.dtype), meta, x], axis=1)
    h = h.reshape(n_rows, d)
    hb = h

    def row(v):
        return v.astype(F32)[None, :]

    for i in range(DEPTH):
        jm = i // 2
        if i % 2 == 0:
            w_in = gla_w_in[jm]
            w_all = jnp.pad(w_in, ((0, 0), (0, GATE_PAD - rank))).astype(BF16)
            wg_up = jnp.pad(gla_w_gate_up[jm], ((0, GATE_PAD - rank), (0, 0))).astype(BF16)
            qkvrg = _matmul(hb, w_all, BF16, IN_PROJ_ROWS, IN_PROJ_COLS, "gla_in_proj")
            mix = _gla(qkvrg, wg_up, row(gla_b_gate[jm]), row(gla_norm_g[jm]),
                       batch=batch, seq_rows=seq_rows, d_model=d)
            w_out = gla_w_out[jm].astype(BF16)
        else:
            mix = _conv_mix(hb, conv_w_in[jm].astype(BF16), conv_w[jm].astype(F32),
                            seq_rows=seq_rows, pad_rows=pad_rows)
            w_out = conv_w_out[jm].astype(BF16)
        h, logits_t, hp = _proj_ln_router(mix, w_out, h, row(ln1_g[i]), row(ln1_b[i]),
                                          router_w[i].T.astype(BF16))
        drop = pad_rows + N_META if i == DEPTH - 1 else 0
        h, hb = _moe_ln(h, hp, logits_t, router_bias[i], exp_w1, exp_w3, exp_w2, i,
                        shared_w1[i].astype(BF16), shared_w3[i].astype(BF16),
                        shared_w2[i].astype(BF16), row(ln2_g[i]), row(ln2_b[i]), drop, seq_rows,
                        pad_rows)
    if h.shape[0] == batch * seq:
        return h.reshape(batch, seq, d)
    return h.reshape(batch, seq_rows, d)[:, pad_rows + N_META:]
```

```python
import functools

import jax
import jax.numpy as jnp
from jax import lax
from jax.experimental import pallas as pl
from jax.experimental.pallas import tpu as pltpu
from jax.experimental.pallas import tpu_sc as plsc

N_META = 16
GLA_HEADS = 4
GLA_GATE_TAU = 16.0
CONV_WIDTH = 3
N_EXPERTS = 64
TOP_K = 8
N_GROUPS = 8
TOPK_GROUPS = 4
ROUTED_SCALE = 2.5
LN_EPS = 1e-5
RMS_EPS = 1e-6
DEPTH = 2
DN_ALPHA = (2 * DEPTH) ** 0.25
LOG2_E = 1.4426950408889634

LANES = 128
SUBLANES = 8
ROW_ALIGN = 128
GLA_CHUNK = 64
GLA_SUB = 16
GLA_HEADS_PER_STEP = 4
MXU_WIDTH = 256
GATE_PAD = MXU_WIDTH
MOE_BLOCK = 512
TOKEN_TILE = 128
IN_PROJ_ROWS, IN_PROJ_COLS = 832, 1280
CONV_ROWS, CONV_COLS = 1024, 512
OUT_PROJ_ROWS = 512
GLA_ROWS = 640
ROUTE_LANES = 512
VMEM_LIMIT = 56 * 1024 * 1024

F32 = jnp.float32
BF16 = jnp.bfloat16


def _pick_tile(n, target, mult):
    best = None
    for t in range(mult, min(n, target) + 1, mult):
        if n % t == 0:
            best = t
    assert best is not None, (n, target, mult)
    return best


def _cparams(sem):
    return pltpu.CompilerParams(dimension_semantics=sem, vmem_limit_bytes=VMEM_LIMIT)


def _mm_kernel(x_ref, w_ref, o_ref):
    x = x_ref[...].astype(BF16)
    o_ref[...] = jnp.dot(x, w_ref[...], preferred_element_type=F32).astype(o_ref.dtype)


def _matmul(x, w, out_dtype, tm_target, tn_target, name):
    m, k = x.shape
    n = w.shape[1]
    tm = _pick_tile(m, tm_target, 16)
    tn = _pick_tile(n, tn_target, LANES)
    return pl.pallas_call(
        _mm_kernel,
        grid=(m // tm, n // tn),
        in_specs=[pl.BlockSpec((tm, k), lambda i, j: (i, 0)),
                  pl.BlockSpec((k, tn), lambda i, j: (0, j))],
        out_specs=pl.BlockSpec((tm, tn), lambda i, j: (i, j)),
        out_shape=jax.ShapeDtypeStruct((m, n), out_dtype),
        compiler_params=_cparams(("parallel", "arbitrary")),
        name=name,
    )(x, w)


def _gla_kernel(q_ref, k_ref, v_ref, r_ref, gl_ref, wg_ref, bg_ref, ng_ref, o_ref, st_ref,
                *, n_chunks, dk, dv, heads):
    C, S = GLA_CHUNK, GLA_SUB

    @pl.when(pl.program_id(2) == 0)
    def _():
        st_ref[...] = jnp.zeros_like(st_ref)

    ri = lax.broadcasted_iota(jnp.int32, (C, C), 0)
    ci = lax.broadcasted_iota(jnp.int32, (C, C), 1)
    tri = (ri >= ci).astype(BF16)
    rs = lax.broadcasted_iota(jnp.int32, (S, S), 0)
    cs = lax.broadcasted_iota(jnp.int32, (S, S), 1)
    nt = (((1,), (1,)), ((), ()))
    tn = (((0,), (0,)), ((), ()))
    half = S // 2

    def chunk(c, carry):
        r0 = pl.multiple_of(c * C, C)
        rows_c = pl.ds(r0, C)
        hs = range(heads)
        kcol = [pl.ds(hd * dk, dk) for hd in hs]
        vcol = [pl.ds(hd * dv, dv) for hd in hs]
        gl = gl_ref[rows_c, :]
        q = [q_ref[rows_c, kcol[hd]].astype(F32) * (dk ** -0.5) for hd in hs]
        kk = [k_ref[rows_c, kcol[hd]].astype(F32) for hd in hs]
        vv = [v_ref[rows_c, vcol[hd]] for hd in hs]
        st = [st_ref[hd] for hd in hs]
        z = [jnp.dot(gl, wg_ref[:, kcol[hd]], preferred_element_type=F32) + bg_ref[:, kcol[hd]]
             for hd in hs]
        b = []
        for hd in hs:
            la = ((jnp.minimum(z[hd], 0.0) - jnp.log(1.0 + jnp.exp(-jnp.abs(z[hd]))))
                  * (1.0 / GLA_GATE_TAU))
            h1 = la.astype(BF16)
            e1 = la - h1.astype(F32)
            h2 = e1.astype(BF16)
            h3 = (e1 - h2.astype(F32)).astype(BF16)
            b3 = jnp.dot(tri, jnp.concatenate([h1, h2, h3], axis=1), preferred_element_type=F32)
            b.append((b3[:, :dk] + b3[:, dk:2 * dk] + b3[:, 2 * dk:]) * LOG2_E)

        o_inter, st_new, off = [], [], []
        for hd in hs:
            qe = (q[hd] * jnp.exp2(b[hd])).astype(BF16)
            o_inter.append(lax.dot_general(qe, st[hd].astype(BF16), nt, preferred_element_type=F32))
            bl = b[hd][C - 1:C]
            khat = (kk[hd] * jnp.exp2(bl - b[hd])).astype(BF16)
            upd = lax.dot_general(vv[hd], khat, tn, preferred_element_type=F32)
            st_new.append(st[hd] * jnp.exp2(bl) + upd)
            offs = []
            for i in range(1, C // S):
                lo = i * S
                bref = b[hd][lo - 1:lo]
                qt = (q[hd][lo:lo + S] * jnp.exp2(b[hd][lo:lo + S] - bref)).astype(BF16)
                kt = (kk[hd][:lo] * jnp.exp2(bref - b[hd][:lo])).astype(BF16)
                offs.append(lax.dot_general(qt, kt, nt, preferred_element_type=F32))
            off.append(offs)

        dmats = []
        for hd in hs:
            blocks = []
            for i in range(C // S):
                lo = i * S
                qs = q[hd][lo:lo + S]
                bs = b[hd][lo:lo + S]
                dmat = jnp.zeros((S, S), F32)
                for j in range(S):
                    kj = kk[hd][lo + j:lo + j + 1]
                    bj = b[hd][lo + j:lo + j + 1]
                    if j < half:
                        p = qs * kj * jnp.exp2(bs - bj)
                        a = jnp.sum(p, axis=-1, keepdims=True)
                    else:
                        p = qs[half:] * kj * jnp.exp2(bs[half:] - bj)
                        a = jnp.concatenate([jnp.zeros((half, 1), F32),
                                             jnp.sum(p, axis=-1, keepdims=True)], axis=0)
                    dmat = jnp.where(cs == j, a, dmat)
                blocks.append(jnp.where(rs >= cs, dmat, 0.0).astype(BF16))
            dmats.append(blocks)

        for hd in hs:
            rows = []
            for i in range(C // S):
                lo = i * S
                o_i = jnp.dot(dmats[hd][i], vv[hd][lo:lo + S], preferred_element_type=F32)
                if i > 0:
                    o_i = o_i + jnp.dot(off[hd][i - 1].astype(BF16), vv[hd][:lo],
                                        preferred_element_type=F32)
                rows.append(o_i)
            o = o_inter[hd] + jnp.concatenate(rows, axis=0)
            ms = jnp.mean(o * o, axis=-1, keepdims=True)
            r = r_ref[rows_c, vcol[hd]].astype(F32)
            y = o * lax.rsqrt(ms + RMS_EPS) * ng_ref[:, vcol[hd]] * (r * jax.nn.sigmoid(r))
            st_ref[hd] = st_new[hd]
            o_ref[rows_c, vcol[hd]] = y.astype(o_ref.dtype)
        return carry

    lax.fori_loop(0, n_chunks, chunk, 0)


def _gla(qkvrg, wg, bg, ng, *, batch, seq_rows, d_model):
    H, hps = GLA_HEADS, GLA_HEADS_PER_STEP
    dk = d_model // 2 // H
    dv = d_model // H
    n_rows = batch * seq_rows
    rblk = _pick_tile(seq_rows, GLA_ROWS, GLA_CHUNK)
    steps = seq_rows // rblk
    groups = H // hps
    off_k, off_v, off_r = groups, (2 * H * dk) // (hps * dv), (2 * H * dk) // (hps * dv) + groups

    off_g = (2 * H * dk + 2 * H * dv) // GATE_PAD

    def rowmap(off):
        return lambda b, h, i: (b * steps + i, off + h)

    return pl.pallas_call(
        functools.partial(_gla_kernel, n_chunks=rblk // GLA_CHUNK, dk=dk, dv=dv, heads=hps),
        grid=(batch, groups, steps),
        in_specs=[pl.BlockSpec((rblk, hps * dk), rowmap(0)),
                  pl.BlockSpec((rblk, hps * dk), rowmap(off_k)),
                  pl.BlockSpec((rblk, hps * dv), rowmap(off_v)),
                  pl.BlockSpec((rblk, hps * dv), rowmap(off_r)),
                  pl.BlockSpec((rblk, GATE_PAD), lambda b, h, i: (b * steps + i, off_g)),
                  pl.BlockSpec((GATE_PAD, hps * dk), lambda b, h, i: (0, h)),
                  pl.BlockSpec((1, hps * dk), lambda b, h, i: (0, h)),
                  pl.BlockSpec((1, hps * dv), lambda b, h, i: (0, h))],
        out_specs=pl.BlockSpec((rblk, hps * dv), lambda b, h, i: (b * steps + i, h)),
        out_shape=jax.ShapeDtypeStruct((n_rows, H * dv), BF16),
        scratch_shapes=[pltpu.VMEM((hps, dv, dk), F32)],
        compiler_params=_cparams(("parallel", "parallel", "arbitrary")),
        name="gla_chunks",
    )(qkvrg, qkvrg, qkvrg, qkvrg, qkvrg, wg, bg, ng)


def _conv_kernel(x_ref, wb_ref, wc_ref, wh_ref, cw_ref, o_ref, ubuf_ref, carry_ref,
                 *, seq_rows, pad_rows, tm):
    i = pl.program_id(0)
    j = pl.program_id(1)
    x = x_ref[...]
    bg = jnp.dot(x, wb_ref[...], preferred_element_type=F32)
    cg = jnp.dot(x, wc_ref[...], preferred_element_type=F32)
    hh = jnp.dot(x, wh_ref[...], preferred_element_type=F32)
    row = i * tm + lax.broadcasted_iota(jnp.int32, (tm, 1), 0)
    u = jnp.where(row % seq_rows >= pad_rows, cg * hh, 0.0)

    @pl.when(i == 0)
    def _():
        carry_ref[j] = jnp.zeros(carry_ref.shape[1:], F32)

    ubuf_ref[pl.ds(0, SUBLANES), :] = carry_ref[j]
    ubuf_ref[pl.ds(SUBLANES, tm), :] = u
    carry_ref[j] = u[tm - SUBLANES:]
    cw = cw_ref[...]
    conv = u * cw[CONV_WIDTH - 1:CONV_WIDTH]
    for s in range(1, CONV_WIDTH):
        conv = conv + ubuf_ref[pl.ds(SUBLANES - s, tm), :] * cw[CONV_WIDTH - 1 - s:CONV_WIDTH - s]
    o_ref[...] = (bg * conv).astype(o_ref.dtype)


def _conv_mix(xb, w_in, conv_w, *, seq_rows, pad_rows):
    n_rows, d = xb.shape
    tm = _pick_tile(n_rows, CONV_ROWS, SUBLANES)
    tn = _pick_tile(d, CONV_COLS, LANES)
    nj = d // tn
    return pl.pallas_call(
        functools.partial(_conv_kernel, seq_rows=seq_rows, pad_rows=pad_rows, tm=tm),
        grid=(n_rows // tm, nj),
        in_specs=[pl.BlockSpec((tm, d), lambda i, j: (i, 0)),
                  pl.BlockSpec((d, tn), lambda i, j: (0, j)),
                  pl.BlockSpec((d, tn), lambda i, j: (0, nj + j)),
                  pl.BlockSpec((d, tn), lambda i, j: (0, 2 * nj + j)),
                  pl.BlockSpec((CONV_WIDTH, tn), lambda i, j: (0, j))],
        out_specs=pl.BlockSpec((tm, tn), lambda i, j: (i, j)),
        out_shape=jax.ShapeDtypeStruct((n_rows, d), BF16),
        scratch_shapes=[pltpu.VMEM((tm + SUBLANES, tn), F32),
                        pltpu.VMEM((nj, SUBLANES, tn), F32)],
        compiler_params=_cparams(("arbitrary", "arbitrary")),
        name="conv_mix",
    )(xb, w_in, w_in, w_in, conv_w)


def _layer_norm(y, g, b):
    mu = jnp.mean(y, axis=-1, keepdims=True)
    yc = y - mu
    var = jnp.mean(yc * yc, axis=-1, keepdims=True)
    return yc * lax.rsqrt(var + LN_EPS) * g + b


def _proj_ln_kernel(a_ref, w_ref, h_ref, g_ref, b_ref, rw_ref, o_ref, lg_ref, hp_ref):
    y = jnp.dot(a_ref[...], w_ref[...], preferred_element_type=F32) + DN_ALPHA * h_ref[...]
    h = _layer_norm(y, g_ref[...], b_ref[...])
    o_ref[...] = h
    _store_rows_as_tiles(hp_ref, _pack_bf16_pairs(h), h.shape[0])
    lg_ref[...] = lax.dot_general(rw_ref[...], h.astype(BF16), (((1,), (1,)), ((), ())),
                                  preferred_element_type=F32)


def _proj_ln_router(a, w, h, g, b, rw_t):
    n_rows, kin = a.shape
    d = w.shape[1]
    e = rw_t.shape[0]
    tm = _pick_tile(n_rows, OUT_PROJ_ROWS, LANES)
    return pl.pallas_call(
        _proj_ln_kernel,
        grid=(n_rows // tm,),
        in_specs=[pl.BlockSpec((tm, kin), lambda i: (i, 0)),
                  pl.BlockSpec((kin, d), lambda i: (0, 0)),
                  pl.BlockSpec((tm, d), lambda i: (i, 0)),
                  pl.BlockSpec((1, d), lambda i: (0, 0)),
                  pl.BlockSpec((1, d), lambda i: (0, 0)),
                  pl.BlockSpec((e, d), lambda i: (0, 0))],
        out_specs=[pl.BlockSpec((tm, d), lambda i: (i, 0)),
                   pl.BlockSpec((e, tm), lambda i: (0, i)),
                   pl.BlockSpec((tm * SUBLANES, LANES), lambda i: (i, 0))],
        out_shape=[jax.ShapeDtypeStruct((n_rows, d), F32),
                   jax.ShapeDtypeStruct((e, n_rows), F32),
                   jax.ShapeDtypeStruct((n_rows * SUBLANES, LANES), jnp.uint32)],
        compiler_params=_cparams(("parallel",)),
        name="proj_ln_router",
    )(a, w, h, g, b, rw_t)


def _beats(cand, cand_idx, ref, ref_idx):
    return (cand > ref) | ((cand == ref) & (cand_idx < ref_idx))


def _route_kernel(lg_ref, bias_ref, w_ref, pos_ref, cnt_ref, run_ref, *, t, seq_rows, pad_rows):
    E, G = N_EXPERTS, N_GROUPS
    gs = E // G

    @pl.when(pl.program_id(0) == 0)
    def _():
        run_ref[...] = jnp.zeros_like(run_ref)

    s = jax.nn.sigmoid(lg_ref[...])
    c = s + bias_ref[...][:, :1]
    sub = lax.broadcasted_iota(jnp.int32, (gs, t), 0)

    grp_rows = []
    for g in range(G):
        cg = c[g * gs:(g + 1) * gs]
        rank = jnp.zeros((gs, t), jnp.int32)
        for m in range(gs):
            rank = rank + _beats(cg[m:m + 1], m, cg, sub).astype(jnp.int32)
        grp_rows.append(jnp.sum(jnp.where(rank < 2, cg, 0.0), axis=0, keepdims=True))
    gidx = lax.broadcasted_iota(jnp.int32, (G, t), 0)
    gscore = jnp.zeros((G, t), F32)
    for g in range(G):
        gscore = jnp.where(gidx == g, grp_rows[g], gscore)
    grank = jnp.zeros((G, t), jnp.int32)
    for m in range(G):
        grank = grank + _beats(gscore[m:m + 1], m, gscore, gidx).astype(jnp.int32)
    gsel = grank < TOPK_GROUPS

    masked = jnp.concatenate(
        [jnp.where(gsel[g:g + 1], c[g * gs:(g + 1) * gs], -jnp.inf) for g in range(G)], axis=0)
    eidx = lax.broadcasted_iota(jnp.int32, (E, t), 0).astype(F32)
    sel = jnp.zeros((E, t), jnp.bool_)
    for _ in range(TOP_K):
        best = jnp.max(masked, axis=0, keepdims=True)
        pick = jnp.min(jnp.where((masked == best) & jnp.logical_not(sel), eidx, float(E)),
                       axis=0, keepdims=True)
        hit = eidx == pick
        sel = sel | hit
        masked = jnp.where(hit, -jnp.inf, masked)
    tok = pl.program_id(0) * t + lax.broadcasted_iota(jnp.int32, (1, t), 1)
    real = tok % seq_rows >= pad_rows
    sel = sel & real
    gate = jnp.where(sel, s, 0.0)
    total = jnp.where(real, jnp.sum(gate, axis=0, keepdims=True), 1.0)
    w_ref[...] = gate / total * ROUTED_SCALE

    li = lax.broadcasted_iota(jnp.int32, (t, t), 0)
    lj = lax.broadcasted_iota(jnp.int32, (t, t), 1)
    upper = (li <= lj).astype(BF16)
    self_ = sel.astype(F32)
    incl = jnp.dot(sel.astype(BF16), upper, preferred_element_type=F32)
    run = run_ref[...][:, :1]
    pos_ref[...] = jnp.where(sel, run + incl - self_, -1.0)
    run_new = run + jnp.sum(self_, axis=1, keepdims=True)
    run_ref[...] = jnp.broadcast_to(run_new, run_ref.shape)
    cnt_ref[...] = jnp.broadcast_to(run_new, cnt_ref.shape)


def _route(logits_t, bias_col, seq_rows, pad_rows):
    e, n_rows = logits_t.shape
    t = _pick_tile(n_rows, ROUTE_LANES, LANES)
    return pl.pallas_call(
        functools.partial(_route_kernel, t=t, seq_rows=seq_rows, pad_rows=pad_rows),
        grid=(n_rows // t,),
        in_specs=[pl.BlockSpec((e, t), lambda i: (0, i)),
                  pl.BlockSpec((e, LANES), lambda i: (0, 0))],
        out_specs=[pl.BlockSpec((e, t), lambda i: (0, i)),
                   pl.BlockSpec((e, t), lambda i: (0, i)),
                   pl.BlockSpec((e, LANES), lambda i: (0, 0))],
        out_shape=[jax.ShapeDtypeStruct((e, n_rows), F32),
                   jax.ShapeDtypeStruct((e, n_rows), F32),
                   jax.ShapeDtypeStruct((e, LANES), F32)],
        scratch_shapes=[pltpu.VMEM((e, LANES), F32)],
        compiler_params=_cparams(("arbitrary",)),
        name="route",
    )(logits_t, bias_col)


def _compact_kernel(w_ref, pos_ref, pst_ref, slot_ref, w8_ref, *, t):
    E = N_EXPERTS
    pos = pos_ref[...]
    sel = pos >= 0.0
    ri = lax.broadcasted_iota(jnp.int32, (E, E), 0)
    ci = lax.broadcasted_iota(jnp.int32, (E, E), 1)
    below = (ci < ri).astype(BF16)
    order = jnp.dot(below, sel.astype(BF16), preferred_element_type=F32)
    slot = pst_ref[...][:, :1] + pos
    wd = w_ref[...]
    kidx = lax.broadcasted_iota(jnp.int32, (TOP_K, t), 0)
    slots = jnp.zeros((TOP_K, t), F32)
    w8 = jnp.zeros((TOP_K, t), F32)
    for k in range(TOP_K):
        m = sel & (order == float(k))
        slots = jnp.where(kidx == k, jnp.sum(jnp.where(m, slot, 0.0), axis=0, keepdims=True), slots)
        w8 = jnp.where(kidx == k, jnp.sum(jnp.where(m, wd, 0.0), axis=0, keepdims=True), w8)
    slot_ref[...] = (slots * float(SUBLANES)).astype(jnp.int32)
    w8_ref[...] = w8


def _compact(w_dense, pos_dense, pstart_col):
    e, n_rows = w_dense.shape
    t = _pick_tile(n_rows, ROUTE_LANES, LANES)
    return pl.pallas_call(
        functools.partial(_compact_kernel, t=t),
        grid=(n_rows // t,),
        in_specs=[pl.BlockSpec((e, t), lambda i: (0, i)),
                  pl.BlockSpec((e, t), lambda i: (0, i)),
                  pl.BlockSpec((e, LANES), lambda i: (0, 0))],
        out_specs=[pl.BlockSpec((TOP_K, t), lambda i: (0, i)),
                   pl.BlockSpec((TOP_K, t), lambda i: (0, i))],
        out_shape=[jax.ShapeDtypeStruct((TOP_K, n_rows), jnp.int32),
                   jax.ShapeDtypeStruct((TOP_K, n_rows), F32)],
        compiler_params=_cparams(("parallel",)),
        name="compact",
    )(w_dense, pos_dense, pstart_col)


def _pack_bf16_pairs(x):
    half = x.shape[1] // 2
    lo = lax.bitcast_convert_type(x[:, :half].astype(BF16).astype(F32), jnp.uint32)
    hi = lax.bitcast_convert_type(x[:, half:].astype(BF16).astype(F32), jnp.uint32)
    return (lo >> 16) | hi


def _unpack_bf16_pairs(w):
    lo = lax.bitcast_convert_type(w << 16, F32)
    hi = lax.bitcast_convert_type(w & jnp.uint32(0xFFFF0000), F32)
    return lo, hi


def _store_rows_as_tiles(ref_at, x, rows):
    for c in range(SUBLANES):
        ref_at[pl.ds(c, rows, stride=SUBLANES), :] = x[:, c * LANES:(c + 1) * LANES]


def _load_tile_chunk(ref_at, c, rows):
    return ref_at[pl.ds(c, rows, stride=SUBLANES), :]


def _swiglu(x, w1, w3, w2):
    a = jnp.dot(x, w1, preferred_element_type=F32)
    g = jnp.dot(x, w3, preferred_element_type=F32)
    hmid = (a * jax.nn.sigmoid(a) * g).astype(BF16)
    return jnp.dot(hmid, w2, preferred_element_type=F32)


def _expert_kernel(be_ref, nu_ref, sp_ref, nexp_ref, ids_ref, nids_ref, w1_ref, w3_ref, w2_ref,
                   hp_ref, y_ref, w1b_ref, w3b_ref, w2b_ref, wf1_ref, wf3_ref, wf2_ref,
                   xbuf0_ref, xbuf1_ref, sem, wsem, *, layer):
    b = pl.program_id(0)
    n_used = nu_ref[0]

    def weight_copies(e, buf):
        return [pltpu.make_async_copy(w_ref.at[layer, e], wf_ref.at[buf], wsem.at[buf])
                for w_ref, wf_ref in ((w1_ref, wf1_ref), (w3_ref, wf3_ref), (w2_ref, wf2_ref))]
    xbuf = (xbuf0_ref, xbuf1_ref)
    rows = MOE_BLOCK

    def row_copy(s_ref, buf, r):
        src = hp_ref.at[pl.ds(pl.multiple_of(s_ref[r], SUBLANES), SUBLANES), :]
        return pltpu.make_async_copy(src, xbuf[buf].at[pl.ds(r * SUBLANES, SUBLANES), :], sem.at[buf])

    def wait_block(buf):
        pltpu.make_async_copy(hp_ref.at[pl.ds(0, rows * SUBLANES), :], xbuf[buf], sem.at[buf]).wait()

    @pl.when(b == 0)
    def _():
        def issue(r2, carry):
            for p in range(2):
                r = r2 * 2 + p
                src = hp_ref.at[pl.ds(pl.multiple_of(ids_ref[r], SUBLANES), SUBLANES), :]
                dst = xbuf0_ref.at[pl.ds(pl.multiple_of(r * SUBLANES, SUBLANES), SUBLANES), :]
                pltpu.make_async_copy(src, dst, sem.at[0]).start(priority=1)
            return carry

        lax.fori_loop(0, rows // 2, issue, 0)

    def run_block(cur):
        nxt = 1 - cur
        wait_block(cur)

        @pl.when((b == 0) | (be_ref[b] != be_ref[jnp.maximum(b - 1, 0)]))
        def _():
            buf = sp_ref[b]

            @pl.when(b == 0)
            def _():
                for c in weight_copies(be_ref[0], buf):
                    c.start()

            for c in weight_copies(be_ref[b], buf):
                c.wait()
            w1b_ref[...] = wf1_ref[buf].astype(BF16)
            w3b_ref[...] = wf3_ref[buf].astype(BF16)
            w2b_ref[...] = wf2_ref[buf].astype(BF16)

            @pl.when(nexp_ref[b] >= 0)
            def _():
                for c in weight_copies(nexp_ref[b], 1 - buf):
                    c.start()

        group = rows // 4

        def issue_group(g):
            for r in range(g * group, (g + 1) * group):
                row_copy(nids_ref, nxt, r).start(priority=1)

        issue_group(0)
        halves = [_unpack_bf16_pairs(_load_tile_chunk(xbuf[cur], c, rows)) for c in range(SUBLANES)]
        x = jnp.concatenate([lo.astype(BF16) for lo, _ in halves]
                            + [hi.astype(BF16) for _, hi in halves], axis=1)
        issue_group(1)
        a = jnp.dot(x, w1b_ref[...], preferred_element_type=F32)
        issue_group(2)
        g = jnp.dot(x, w3b_ref[...], preferred_element_type=F32)
        issue_group(3)
        hmid = (a * jax.nn.sigmoid(a) * g).astype(BF16)
        y = _pack_bf16_pairs(jnp.dot(hmid, w2b_ref[...], preferred_element_type=F32))
        _store_rows_as_tiles(y_ref, y, rows)

        @pl.when(b == n_used - 1)
        def _():
            wait_block(nxt)

    @pl.when(b < n_used)
    def _():
        for parity in range(2):
            @pl.when(b % 2 == parity)
            def _():
                run_block(parity)


def _experts(hp, ids_tiled, w1, w3, w2, layer, block_e, n_used):
    d, f = w1.shape[2], w1.shape[3]
    blk_rows = MOE_BLOCK * SUBLANES
    nb = ids_tiled.shape[0]

    idx = jnp.arange(nb, dtype=jnp.int32)
    is_new = ((idx == 0) | (block_e != jnp.roll(block_e, 1))) & (idx < n_used[0])
    seg_parity = ((jnp.cumsum(is_new) - 1) % 2).astype(jnp.int32)
    nxt_new = lax.cummin(jnp.where(is_new, idx, nb)[::-1])[::-1]
    nxt_new = jnp.concatenate([nxt_new[1:], jnp.full((1,), nb, jnp.int32)])
    next_e = jnp.where(nxt_new < nb, block_e[jnp.minimum(nxt_new, nb - 1)], -1).astype(jnp.int32)

    def blk(b, be, nu, sp, ne):
        return jnp.minimum(b, nu[0] - 1)

    any_spec = pl.BlockSpec(memory_space=pl.ANY)
    return pl.pallas_call(
        functools.partial(_expert_kernel, layer=layer),
        grid_spec=pltpu.PrefetchScalarGridSpec(
            num_scalar_prefetch=4,
            grid=(nb,),
            in_specs=[pl.BlockSpec((None, None, MOE_BLOCK), lambda b, *s: (blk(b, *s), 0, 0),
                                   memory_space=pltpu.SMEM),
                      pl.BlockSpec((None, None, MOE_BLOCK), lambda b, *s: (blk(b + 1, *s), 0, 0),
                                   memory_space=pltpu.SMEM),
                      any_spec, any_spec, any_spec, any_spec],
            out_specs=pl.BlockSpec((blk_rows, LANES), lambda b, *s: (blk(b, *s), 0)),
            scratch_shapes=[pltpu.VMEM((d, f), BF16), pltpu.VMEM((d, f), BF16),
                            pltpu.VMEM((f, d), BF16),
                            pltpu.VMEM((2, d, f), F32), pltpu.VMEM((2, d, f), F32),
                            pltpu.VMEM((2, f, d), F32),
                            pltpu.VMEM((blk_rows, LANES), jnp.uint32),
                            pltpu.VMEM((blk_rows, LANES), jnp.uint32),
                            pltpu.SemaphoreType.DMA((2,)), pltpu.SemaphoreType.DMA((2,))]),
        out_shape=jax.ShapeDtypeStruct((nb * blk_rows, LANES), jnp.uint32),
        compiler_params=_cparams(("arbitrary",)),
        name="experts",
    )(block_e, n_used, seg_parity, next_e, ids_tiled, ids_tiled, w1, w3, w2, hp)


def _inverse_rows(slot_flat, vals, n_slots):
    n_idx = slot_flat.shape[0]
    window = LANES
    mesh = plsc.VectorSubcoreMesh(core_axis_name="core", subcore_axis_name="subcore")

    @functools.partial(pl.kernel, out_type=jax.ShapeDtypeStruct((n_slots, LANES), jnp.int32),
                       mesh=mesh, scratch_types=[])
    def scatter_rows(x_hbm, i_hbm, o_hbm):
        def body(x_vmem, i_vmem):
            pltpu.sync_copy(x_vmem, o_hbm.at[i_vmem.at[0]])

        pltpu.emit_pipeline(
            body,
            grid=(n_idx // window,),
            in_specs=[pl.BlockSpec((window, LANES), lambda i: (i, 0)),
                      pl.BlockSpec((1, window), lambda i: (0, i))],
            out_specs=[],
            core_axis_name="subcore",
            dimension_semantics=(pltpu.PARALLEL,),
        )(x_hbm, i_hbm)

    return scatter_rows(vals, slot_flat.reshape(1, n_idx))


def _combine_kernel(slot_ref, nslot_ref, h_ref, w8_ref, ws1_ref, ws3_ref, ws2_ref, g_ref, b_ref,
                    ys_ref, o_ref, ob_ref, ybuf0_ref, ybuf1_ref, sem, *, t):
    i = pl.program_id(0)
    last = pl.num_programs(0) - 1
    ybuf = (ybuf0_ref, ybuf1_ref)

    def row_copy(s_ref, buf, k, tok):
        src = ys_ref.at[pl.ds(pl.multiple_of(s_ref[k * t + tok], SUBLANES), SUBLANES), :]
        return pltpu.make_async_copy(
            src, ybuf[buf].at[k, pl.ds(pl.multiple_of(tok * SUBLANES, SUBLANES), SUBLANES), :],
            sem.at[buf])

    def wait_tile(buf):
        for k in range(TOP_K):
            pltpu.make_async_copy(ys_ref.at[pl.ds(0, t * SUBLANES), :], ybuf[buf].at[k],
                                  sem.at[buf]).wait()

    @pl.when(i == 0)
    def _():
        def issue(tok, carry):
            for k in range(TOP_K):
                row_copy(slot_ref, 0, k, tok).start(priority=k % 2)
            return carry

        lax.fori_loop(0, t, issue, 0, unroll=2)

    group = t // TOP_K

    def reduce_tile(cur):
        nxt = 1 - cur

        def issue_group(g):
            for tok in range(g * group, (g + 1) * group):
                for k in range(TOP_K):
                    row_copy(nslot_ref, nxt, k, tok).start(priority=k % 2)

        wait_tile(cur)
        h = h_ref[...]
        issue_group(0)
        shared = _swiglu(h.astype(BF16), ws1_ref[...], ws3_ref[...], ws2_ref[...])
        w8 = w8_ref[...]
        wk = [w8[:, k:k + 1] for k in range(TOP_K)]
        acc_lo, acc_hi = [], []
        for c in range(SUBLANES):
            if c + 1 < TOP_K:
                issue_group(c + 1)
            lo_c, hi_c = None, None
            for k in range(TOP_K):
                lo, hi = _unpack_bf16_pairs(_load_tile_chunk(ybuf[cur].at[k], c, t))
                lo_c = lo * wk[k] if k == 0 else lo_c + lo * wk[k]
                hi_c = hi * wk[k] if k == 0 else hi_c + hi * wk[k]
            acc_lo.append(lo_c)
            acc_hi.append(hi_c)
        acc = jnp.concatenate(acc_lo + acc_hi, axis=1)
        y = DN_ALPHA * h + (acc + shared)
        out = _layer_norm(y, g_ref[...], b_ref[...])
        o_ref[...] = out
        ob_ref[...] = out.astype(BF16)

        @pl.when(i == last)
        def _():
            wait_tile(nxt)

    for parity in range(2):
        @pl.when(i % 2 == parity)
        def _():
            reduce_tile(parity)


def _combine(h, ys, slots_tiled, w8_t, ws1, ws3, ws2, g, b, t, out_rows, out_map):
    n_rows, d = h.shape
    f = ws1.shape[1]
    n_tiles = n_rows // t
    return pl.pallas_call(
        functools.partial(_combine_kernel, t=t),
        grid=(n_tiles,),
        in_specs=[pl.BlockSpec((None, None, TOP_K * t), lambda i: (i, 0, 0),
                               memory_space=pltpu.SMEM),
                  pl.BlockSpec((None, None, TOP_K * t),
                               lambda i: (jnp.minimum(i + 1, n_tiles - 1), 0, 0),
                               memory_space=pltpu.SMEM),
                  pl.BlockSpec((t, d), lambda i: (i, 0)),
                  pl.BlockSpec((t, TOP_K), lambda i: (i, 0)),
                  pl.BlockSpec((d, f), lambda i: (0, 0)),
                  pl.BlockSpec((d, f), lambda i: (0, 0)),
                  pl.BlockSpec((f, d), lambda i: (0, 0)),
                  pl.BlockSpec((1, d), lambda i: (0, 0)),
                  pl.BlockSpec((1, d), lambda i: (0, 0)),
                  pl.BlockSpec(memory_space=pl.ANY)],
        out_specs=[pl.BlockSpec((t, d), lambda i: (out_map(i), 0)),
                   pl.BlockSpec((t, d), lambda i: (i, 0))],
        out_shape=[jax.ShapeDtypeStruct((out_rows, d), F32),
                   jax.ShapeDtypeStruct((n_rows, d), BF16)],
        scratch_shapes=[pltpu.VMEM((TOP_K, t * SUBLANES, LANES), jnp.uint32),
                        pltpu.VMEM((TOP_K, t * SUBLANES, LANES), jnp.uint32),
                        pltpu.SemaphoreType.DMA((2,))],
        compiler_params=_cparams(("arbitrary",)),
        name="combine_ln",
    )(slots_tiled, slots_tiled, h, w8_t, ws1, ws3, ws2, g, b, ys)


def _moe_ln(h, hp, logits_t, router_bias, w1, w3, w2, layer, ws1, ws3, ws2, g, b, drop_rows,
            seq_rows, pad_rows):
    n_rows, d = h.shape
    E = N_EXPERTS
    assert d // 2 == SUBLANES * LANES, d
    bias_col = jnp.broadcast_to(router_bias.astype(F32)[:, None], (E, LANES))
    w_dense, pos_dense, cnt = _route(logits_t, bias_col, seq_rows, pad_rows)

    counts = cnt[:, 0].astype(jnp.int32)
    pcounts = (counts + MOE_BLOCK - 1) // MOE_BLOCK * MOE_BLOCK
    pends = jnp.cumsum(pcounts)
    pstarts = pends - pcounts
    n_blocks = -(-(n_rows * TOP_K) // MOE_BLOCK) + E
    n_slots = n_blocks * MOE_BLOCK
    block_lo = jnp.arange(n_blocks, dtype=jnp.int32) * MOE_BLOCK
    block_e = jnp.minimum(jnp.sum(pends[None, :] <= block_lo[:, None], axis=1), E - 1).astype(jnp.int32)
    n_used = (pends[-1:] // MOE_BLOCK).astype(jnp.int32)
    pstart_col = jnp.broadcast_to(pstarts.astype(F32)[:, None], (E, LANES))

    slots, w8 = _compact(w_dense, pos_dense, pstart_col)
    t = _pick_tile(n_rows, TOKEN_TILE, LANES)
    n_tiles = n_rows // t
    tok = jnp.arange(n_rows, dtype=jnp.int32)
    real = tok % seq_rows >= pad_rows
    n_pad = (n_rows // seq_rows) * pad_rows
    pad_id = (tok // seq_rows) * pad_rows + tok % seq_rows
    k_id = jnp.arange(TOP_K, dtype=jnp.int32)[:, None]
    slots = jnp.where(real[None, :], slots, ((pad_id[None, :] * TOP_K + k_id) % MOE_BLOCK) * SUBLANES)
    slots_tiled = slots.reshape(TOP_K, n_tiles, t).transpose(1, 0, 2).reshape(n_tiles, 1, TOP_K * t)
    tok_rows = jnp.tile(tok * SUBLANES, TOP_K)
    scatter_to = jnp.where(real[None, :], slots // SUBLANES, n_slots + k_id * max(n_pad, 1) + pad_id[None, :])
    inv = _inverse_rows(scatter_to.reshape(-1),
                        jnp.broadcast_to(tok_rows[:, None], (n_rows * TOP_K, LANES)),
                        n_slots + TOP_K * max(n_pad, 1))[:n_slots]
    slot_id = jnp.arange(n_slots, dtype=jnp.int32).reshape(n_blocks, MOE_BLOCK)
    used = slot_id < (pstarts + counts)[block_e][:, None]
    ids = jnp.where(used, jnp.clip(inv[:, 0].reshape(n_blocks, MOE_BLOCK) // SUBLANES, 0, n_rows - 1),
                    slot_id % n_rows) * SUBLANES
    ys = _experts(hp, ids.reshape(n_blocks, 1, MOE_BLOCK), w1, w3, w2, layer, block_e, n_used)
    if drop_rows == t:
        per_seq = seq_rows // t
        out_rows = n_rows - (n_rows // seq_rows) * t

        def out_map(i):
            return (i // per_seq) * (per_seq - 1) + jnp.maximum(i % per_seq - 1, 0)
    else:
        out_rows, out_map = n_rows, (lambda i: i)
    return _combine(h, ys, slots_tiled, w8.T, ws1, ws3, ws2, g, b, t, out_rows, out_map)


def kernel(x, meta_tokens, gla_w_in, gla_w_gate_up, gla_b_gate, gla_norm_g, gla_w_out,
           conv_w_in, conv_w, conv_w_out, ln1_g, ln1_b, router_w, router_bias,
           exp_w1, exp_w3, exp_w2, shared_w1, shared_w3, shared_w2, ln2_g, ln2_b):
    batch, seq, d = x.shape
    H = GLA_HEADS
    dk = d // 2 // H
    n_qkvr = 2 * H * dk + 2 * d
    rank = gla_w_in.shape[2] - n_qkvr
    seq_rows = -(-(N_META + seq) // ROW_ALIGN) * ROW_ALIGN
    pad_rows = seq_rows - N_META - seq
    n_rows = batch * seq_rows

    meta = jnp.broadcast_to(meta_tokens.astype(x.dtype)[None], (batch, N_META, d))
    h = jnp.concatenate([jnp.zeros((batch, pad_rows, d), x.dtype), meta, x], axis=1)
    h = h.reshape(n_rows, d)
    hb = h

    def row(v):
        return v.astype(F32)[None, :]

    for i in range(DEPTH):
        jm = i // 2
        if i % 2 == 0:
            w_in = gla_w_in[jm]
            w_all = jnp.pad(w_in, ((0, 0), (0, GATE_PAD - rank))).astype(BF16)
            wg_up = jnp.pad(gla_w_gate_up[jm], ((0, GATE_PAD - rank), (0, 0))).astype(BF16)
            qkvrg = _matmul(hb, w_all, BF16, IN_PROJ_ROWS, IN_PROJ_COLS, "gla_in_proj")
            mix = _gla(qkvrg, wg_up, row(gla_b_gate[jm]), row(gla_norm_g[jm]),
                       batch=batch, seq_rows=seq_rows, d_model=d)
            w_out = gla_w_out[jm].astype(BF16)
        else:
            mix = _conv_mix(hb, conv_w_in[jm].astype(BF16), conv_w[jm].astype(F32),
                            seq_rows=seq_rows, pad_rows=pad_rows)
            w_out = conv_w_out[jm].astype(BF16)
        h, logits_t, hp = _proj_ln_router(mix, w_out, h, row(ln1_g[i]), row(ln1_b[i]),
                                          router_w[i].T.astype(BF16))
        drop = pad_rows + N_META if i == DEPTH - 1 else 0
        h, hb = _moe_ln(h, hp, logits_t, router_bias[i], exp_w1, exp_w3, exp_w2, i,
                        shared_w1[i].astype(BF16), shared_w3[i].astype(BF16),
                        shared_w2[i].astype(BF16), row(ln2_g[i]), row(ln2_b[i]), drop, seq_rows,
                        pad_rows)
    if h.shape[0] == batch * seq:
        return h.reshape(batch, seq, d)
    return h.reshape(batch, seq_rows, d)[:, pad_rows + N_META:]
```
